```python
import math
import jax, jax.numpy as jnp
from jax import lax
import numpy as np

D_MODEL = 4096
BATCH = 4
SEQ = 2048
DEPTH = 4
DEC_BATCH = 8
DEC_SEQ = 4
PAST_LEN = 8192
PAGE_SIZE = 128

D_MIX = D_MODEL
HEAD_GROUP_WIDTH = D_MIX // 4
GLA_HEADS = 8
GLA_DV = HEAD_GROUP_WIDTH // GLA_HEADS
GLA_DK = GLA_DV // 2
GLA_GATE_RANK = 16
GLA_GATE_NORM = 16.0
GLA_CHUNK = 16
ML_HEADS = 8
ML_DH = HEAD_GROUP_WIDTH // ML_HEADS
ML_CHUNK = 64
GM_GROUPS = 8
GM_CH = HEAD_GROUP_WIDTH // GM_GROUPS
GM_CHUNK = 128
MB_HEADS = 8
MB_DH = HEAD_GROUP_WIDTH // MB_HEADS
MB_BLOCK = 256
MB_TOPK = 3
MB_QCHUNK = 16
ROT_DIM = MB_DH // 4
ROPE_THETA = 500000.0
EPS = 1e-6

IN_SPLITS = (
    GLA_HEADS * GLA_DK, GLA_HEADS * GLA_DK, GLA_HEADS * GLA_DV, GLA_GATE_RANK, HEAD_GROUP_WIDTH,
    ML_HEADS * ML_DH, ML_HEADS * ML_DH, ML_HEADS * ML_DH, ML_HEADS, ML_HEADS, HEAD_GROUP_WIDTH, HEAD_GROUP_WIDTH,
    HEAD_GROUP_WIDTH, HEAD_GROUP_WIDTH, HEAD_GROUP_WIDTH,
    MB_HEADS * MB_DH, MB_HEADS * MB_DH, MB_HEADS * MB_DH, HEAD_GROUP_WIDTH,
)
D_IN = sum(IN_SPLITS)

kernel_name = 'hybrid_gla_mlstm_gmlp_moba_step'


def _rmsnorm(x, w):
    xf = x.astype(jnp.float32)
    return xf * lax.rsqrt(jnp.mean(xf * xf, axis=-1, keepdims=True) + EPS) * w.astype(jnp.float32)


def _layernorm(x, w, b):
    xf = x.astype(jnp.float32)
    xc = xf - jnp.mean(xf, axis=-1, keepdims=True)
    return xc * lax.rsqrt(jnp.mean(xc * xc, axis=-1, keepdims=True) + EPS) * w.astype(jnp.float32) + b.astype(jnp.float32)


def _rope_partial(x, pos):
    half = ROT_DIM // 2
    inv_freq = jnp.power(ROPE_THETA, -jnp.arange(0, ROT_DIM, 2, dtype=jnp.float32) / ROT_DIM)
    ang = pos.astype(jnp.float32)[:, None] * inv_freq[None, :]
    cos = jnp.cos(ang)[None, :, None, :]
    sin = jnp.sin(ang)[None, :, None, :]
    x1 = x[..., :half]
    x2 = x[..., half:ROT_DIM]
    return jnp.concatenate([x1 * cos - x2 * sin, x2 * cos + x1 * sin, x[..., ROT_DIM:]], axis=-1)


def _gla_chunked(q, k, v, log_a, s0):
    B, T, H, DK = q.shape
    DV = v.shape[-1]
    C = math.gcd(T, GLA_CHUNK)
    NC = T // C
    to_chunks = lambda a: a.reshape(B, NC, C, H, a.shape[-1]).transpose(1, 0, 3, 2, 4)
    causal = jnp.tril(jnp.ones((C, C), dtype=bool))

    def step(S, xs):
        qc, kc, vc, gc = xs
        b = jnp.cumsum(gc, axis=2)
        diff = b[:, :, :, None, :] - b[:, :, None, :, :]
        decay = jnp.exp(jnp.where(causal[:, :, None], diff, -jnp.inf))
        att = jnp.einsum('bhtd,bhsd,bhtsd->bhts', qc, kc, decay)
        o = jnp.einsum('bhtd,bhde->bhte', qc * jnp.exp(b), S) + jnp.einsum('bhts,bhse->bhte', att, vc)
        b_end = b[:, :, -1:, :]
        S_new = jnp.exp(b_end[:, :, 0, :])[..., None] * S + jnp.einsum('bhsd,bhse->bhde', kc * jnp.exp(b_end - b), vc)
        return S_new, o

    S, o = lax.scan(step, s0, (to_chunks(q), to_chunks(k), to_chunks(v), to_chunks(log_a)))
    return o.transpose(1, 0, 3, 2, 4).reshape(B, T, H, DV), S


def _mlstm_chunked(q, k, v, i_pre, log_f, c0, n0, m0):
    B, T, H, DH = q.shape
    C = math.gcd(T, ML_CHUNK)
    NC = T // C
    to_chunks = lambda a: a.reshape(B, NC, C, H, DH).transpose(1, 0, 3, 2, 4)
    gate_chunks = lambda a: a.reshape(B, NC, C, H).transpose(1, 0, 3, 2)
    causal = jnp.tril(jnp.ones((C, C), dtype=bool))

    def step(carry, xs):
        Cm, n, m = carry
        qc, kc, vc, ic, fc = xs
        F = jnp.cumsum(fc, axis=-1)
        D = jnp.where(causal, F[..., :, None] - F[..., None, :] + ic[..., None, :], -jnp.inf)
        inter = F + m[..., None]
        m_t = jnp.maximum(inter, jnp.max(D, axis=-1))
        w_intra = jnp.exp(D - m_t[..., None])
        w_state = jnp.exp(inter - m_t)
        qk = jnp.einsum('bhtd,bhsd->bhts', qc, kc) * w_intra
        num = w_state[..., None] * jnp.einsum('bhed,bhtd->bhte', Cm, qc) + jnp.einsum('bhts,bhse->bhte', qk, vc)
        den = w_state * jnp.einsum('bhd,bhtd->bht', n, qc) + jnp.sum(qk, axis=-1)
        h = num / jnp.maximum(jnp.abs(den), jnp.exp(-m_t))[..., None]
        m_new = m_t[..., -1]
        w_end = jnp.exp(F[..., -1:] - F + ic - m_new[..., None])
        dec = jnp.exp(F[..., -1] + m - m_new)
        C_new = dec[..., None, None] * Cm + jnp.einsum('bhs,bhse,bhsd->bhed', w_end, vc, kc)
        n_new = dec[..., None] * n + jnp.einsum('bhs,bhsd->bhd', w_end, kc)
        return (C_new, n_new, m_new), h

    (Cm, n, m), h = lax.scan(step, (c0, n0, m0), (to_chunks(q), to_chunks(k), to_chunks(v), gate_chunks(i_pre), gate_chunks(log_f)))
    return h.transpose(1, 0, 3, 2, 4).reshape(B, T, H, DH), Cm, n, m


def _gmlp_spatial(vn, w_s, b_s):
    B, T, W = vn.shape
    L = min(T, GM_CHUNK)
    NC = T // L
    vg = vn.reshape(B, NC, L, GM_GROUPS, GM_CH)
    ws = w_s[:, :L, :L] * jnp.tril(jnp.ones((L, L), dtype=w_s.dtype))
    s = jnp.einsum('gpr,bnrgc->bnpgc', ws, vg) + b_s[:, :L].T[None, None, :, :, None]
    return s.reshape(B, T, W)


def _moba_attention(q, pos, k_all, v_all):
    B, T, H, DH = q.shape
    L = k_all.shape[1]
    NB = -(-L // MB_BLOCK)
    pad = NB * MB_BLOCK - L
    to_blocks = lambda a: jnp.pad(a, ((0, 0), (0, pad), (0, 0), (0, 0))).reshape(B, NB, MB_BLOCK, H, DH).transpose(0, 3, 1, 2, 4)
    Kb = to_blocks(k_all)
    Vb = to_blocks(v_all)
    k_mean = jnp.mean(Kb, axis=3)
    own = pos // MB_BLOCK
    gate = jnp.einsum('bthd,bhnd->bhtn', q, k_mean)
    fully_past = jnp.arange(NB)[None, :] < own[:, None]
    gate = jnp.where(fully_past[None, None], gate, -jnp.inf)
    _, idx = lax.top_k(gate, min(MB_TOPK, NB))
    valid = idx < own[None, None, :, None]
    blocks = jnp.concatenate([idx, jnp.broadcast_to(own[None, None, :, None], (B, H, T, 1)).astype(idx.dtype)], axis=-1)
    block_ok = jnp.concatenate([valid, jnp.ones((B, H, T, 1), dtype=bool)], axis=-1)

    QC = math.gcd(T, MB_QCHUNK)
    NQ = T // QC
    chunk = lambda a: jnp.moveaxis(a.reshape((B, H, NQ, QC) + a.shape[3:]), 2, 0)
    qh = q.transpose(0, 2, 1, 3)
    bi = jnp.arange(B)[:, None, None, None]
    hi = jnp.arange(H)[None, :, None, None]
    scale = DH ** -0.5

    def attend(xs):
        qc, blk, ok, pc = xs
        Kg = Kb[bi, hi, blk]
        Vg = Vb[bi, hi, blk]
        s = jnp.einsum('bhqd,bhqnkd->bhqnk', qc, Kg).astype(jnp.float32) * scale
        key_pos = blk[..., None] * MB_BLOCK + jnp.arange(MB_BLOCK)
        allowed = ok[..., None] & (key_pos <= pc[None, None, :, None, None])
        s = jnp.where(allowed, s, -jnp.inf)
        p = jax.nn.softmax(s.reshape(B, H, QC, -1), axis=-1).reshape(s.shape)
        return jnp.einsum('bhqnk,bhqnkd->bhqd', p, Vg)

    o = lax.map(attend, (chunk(qh), chunk(blocks), chunk(block_ok), pos.reshape(NQ, QC)))
    return jnp.moveaxis(o, 0, 2).reshape(B, H, T, DH).transpose(0, 2, 1, 3)


def _mixer_layer(x, pos, k_past, v_past, gla_s0, ml_c0, ml_n0, ml_m0,
                 norm_w, w_in, gla_w_gate, gla_b_gate, gla_norm_w, ml_b_i, ml_b_f, ml_norm_w,
                 gm_ln_w, gm_ln_b, gm_w_s, gm_b_s, w_out):
    f32 = jnp.float32
    B, T, _ = x.shape
    h = _rmsnorm(x, norm_w)
    proj = jnp.einsum('btd,de->bte', h, w_in.astype(f32))
    (g_q, g_k, g_v, g_lr, g_z, m_q, m_k, m_v, m_i, m_f, m_o, m_z,
     c_u, c_v, c_z, a_q, a_k, a_v, a_z) = jnp.split(proj, np.cumsum(IN_SPLITS)[:-1].tolist(), axis=-1)

    q = g_q.reshape(B, T, GLA_HEADS, GLA_DK) * GLA_DK ** -0.5
    k = g_k.reshape(B, T, GLA_HEADS, GLA_DK)
    v = g_v.reshape(B, T, GLA_HEADS, GLA_DV)
    log_a = jax.nn.log_sigmoid(g_lr @ gla_w_gate.astype(f32) + gla_b_gate.astype(f32)) / GLA_GATE_NORM
    o, gla_s = _gla_chunked(q, k, v, log_a.reshape(B, T, GLA_HEADS, GLA_DK), gla_s0.astype(f32))
    out_a = _rmsnorm(o, gla_norm_w).reshape(B, T, HEAD_GROUP_WIDTH) * jax.nn.silu(g_z)

    q = m_q.reshape(B, T, ML_HEADS, ML_DH) * ML_DH ** -0.5
    k = m_k.reshape(B, T, ML_HEADS, ML_DH)
    v = m_v.reshape(B, T, ML_HEADS, ML_DH)
    i_pre = m_i + ml_b_i.astype(f32)
    log_f = jax.nn.log_sigmoid(m_f + ml_b_f.astype(f32))
    o, ml_c, ml_n, ml_m = _mlstm_chunked(q, k, v, i_pre, log_f, ml_c0.astype(f32), ml_n0.astype(f32), ml_m0.astype(f32))
    o = jax.nn.sigmoid(m_o).reshape(B, T, ML_HEADS, ML_DH) * o
    out_b = _rmsnorm(o, ml_norm_w).reshape(B, T, HEAD_GROUP_WIDTH) * jax.nn.silu(m_z)

    vn = _layernorm(jax.nn.gelu(c_v), gm_ln_w, gm_ln_b)
    out_c = jax.nn.gelu(c_u) * _gmlp_spatial(vn, gm_w_s.astype(f32), gm_b_s.astype(f32)) * jax.nn.silu(c_z)

    q = _rope_partial(a_q.reshape(B, T, MB_HEADS, MB_DH), pos)
    k = _rope_partial(a_k.reshape(B, T, MB_HEADS, MB_DH), pos)
    v = a_v.reshape(B, T, MB_HEADS, MB_DH)
    k_all = jnp.concatenate([k_past.astype(f32), k], axis=1)
    v_all = jnp.concatenate([v_past.astype(f32), v], axis=1)
    o = _moba_attention(q, pos, k_all, v_all)
    out_d = o.reshape(B, T, HEAD_GROUP_WIDTH) * jax.nn.silu(a_z)

    mix = jnp.concatenate([out_a, out_b, out_c, out_d], axis=-1)
    y = (x.astype(f32) + mix @ w_out.astype(f32)).astype(x.dtype)
    return y, k.astype(x.dtype), v.astype(x.dtype), gla_s, ml_c, ml_n, ml_m, vn


def setup_inputs(seed: int = 0) -> dict:
    key = jax.random.key(seed)
    ks = jax.random.split(key, 24)
    f32 = jnp.float32
    n_pages = PAST_LEN // PAGE_SIZE
    n_used = DEC_BATCH * n_pages
    n_pool = n_used + max(1, n_used // 4)
    page_table = jax.random.permutation(ks[4], n_pool)[:n_used].reshape(DEC_BATCH, n_pages).astype(jnp.int32)
    nrm = lambda k, shape, s=1.0: jax.random.normal(k, shape, f32) * s
    return {
        'x_prompt': nrm(ks[0], (BATCH, SEQ, D_MODEL)),
        'x_sample': nrm(ks[1], (DEC_BATCH, DEC_SEQ, D_MODEL)),
        'cache_k': nrm(ks[2], (DEPTH, n_pool, PAGE_SIZE, MB_HEADS, MB_DH)),
        'cache_v': nrm(ks[3], (DEPTH, n_pool, PAGE_SIZE, MB_HEADS, MB_DH)),
        'page_table': page_table,
        'state_gla': nrm(ks[5], (DEPTH, DEC_BATCH, GLA_HEADS, GLA_DK, GLA_DV)),
        'state_mlstm_C': nrm(ks[6], (DEPTH, DEC_BATCH, ML_HEADS, ML_DH, ML_DH)),
        'state_mlstm_n': nrm(ks[7], (DEPTH, DEC_BATCH, ML_HEADS, ML_DH)),
        'state_mlstm_m': nrm(ks[8], (DEPTH, DEC_BATCH, ML_HEADS)),
        'norm_w': 1.0 + nrm(ks[9], (DEPTH, D_MODEL), 0.02),
        'w_in': nrm(ks[10], (DEPTH, D_MODEL, D_IN), D_MODEL ** -0.5),
        'gla_w_gate': nrm(ks[11], (DEPTH, GLA_GATE_RANK, GLA_HEADS * GLA_DK), GLA_GATE_RANK ** -0.5),
        'gla_b_gate': nrm(ks[12], (DEPTH, GLA_HEADS * GLA_DK), 0.1),
        'gla_norm_w': 1.0 + nrm(ks[13], (DEPTH, GLA_DV), 0.02),
        'ml_b_i': nrm(ks[14], (DEPTH, ML_HEADS), 0.1),
        'ml_b_f': 3.0 + nrm(ks[15], (DEPTH, ML_HEADS), 0.1),
        'ml_norm_w': 1.0 + nrm(ks[16], (DEPTH, ML_DH), 0.02),
        'gm_ln_w': 1.0 + nrm(ks[17], (DEPTH, HEAD_GROUP_WIDTH), 0.02),
        'gm_ln_b': nrm(ks[18], (DEPTH, HEAD_GROUP_WIDTH), 0.02),
        'gm_w_s': nrm(ks[19], (DEPTH, GM_GROUPS, GM_CHUNK, GM_CHUNK), GM_CHUNK ** -0.5),
        'gm_b_s': 1.0 + nrm(ks[20], (DEPTH, GM_GROUPS, GM_CHUNK), 0.1),
        'w_out': nrm(ks[21], (DEPTH, D_MIX, D_MODEL), D_MIX ** -0.5),
        'final_norm_w': 1.0 + nrm(ks[22], (D_MODEL,), 0.02),
    }


def reference(x_prompt, x_sample, cache_k, cache_v, page_table, state_gla, state_mlstm_C, state_mlstm_n,
              state_mlstm_m, norm_w, w_in, gla_w_gate, gla_b_gate, gla_norm_w, ml_b_i, ml_b_f, ml_norm_w,
              gm_ln_w, gm_ln_b, gm_w_s, gm_b_s, w_out, final_norm_w):
    f32 = jnp.float32
    B, T_p, _ = x_prompt.shape
    DB, T_s, _ = x_sample.shape
    pos_p = jnp.arange(T_p, dtype=jnp.int32)
    pos_s = PAST_LEN + jnp.arange(T_s, dtype=jnp.int32)
    empty_kv = jnp.zeros((B, 0, MB_HEADS, MB_DH), f32)
    zeros_gla = jnp.zeros((B, GLA_HEADS, GLA_DK, GLA_DV), f32)
    zeros_c = jnp.zeros((B, ML_HEADS, ML_DH, ML_DH), f32)
    zeros_n = jnp.zeros((B, ML_HEADS, ML_DH), f32)
    zeros_m = jnp.zeros((B, ML_HEADS), f32)
    yp, ys = x_prompt, x_sample
    kp_l, vp_l, ks_l, vs_l = [], [], [], []
    gp_l, gs_l, cp_l, cs_l, np_l, ns_l, mp_l, ms_l, vv_l = [], [], [], [], [], [], [], [], []
    for l in range(DEPTH):
        w = (norm_w[l], w_in[l], gla_w_gate[l], gla_b_gate[l], gla_norm_w[l], ml_b_i[l], ml_b_f[l], ml_norm_w[l],
             gm_ln_w[l], gm_ln_b[l], gm_w_s[l], gm_b_s[l], w_out[l])
        yp, kp, vp, gp, cp, n_p, mp, _ = _mixer_layer(yp, pos_p, empty_kv, empty_kv, zeros_gla, zeros_c, zeros_n, zeros_m, *w)
        k_past = cache_k[l][page_table].reshape(DB, -1, MB_HEADS, MB_DH)
        v_past = cache_v[l][page_table].reshape(DB, -1, MB_HEADS, MB_DH)
        ys, k_s, v_s, gs, cs, n_s, ms, vv = _mixer_layer(ys, pos_s, k_past, v_past, state_gla[l], state_mlstm_C[l],
                                                        state_mlstm_n[l], state_mlstm_m[l], *w)
        kp_l.append(kp); vp_l.append(vp); ks_l.append(k_s); vs_l.append(v_s)
        gp_l.append(gp); gs_l.append(gs); cp_l.append(cp); cs_l.append(cs)
        np_l.append(n_p); ns_l.append(n_s); mp_l.append(mp); ms_l.append(ms); vv_l.append(vv)
    y_prompt = _rmsnorm(yp, final_norm_w).astype(x_prompt.dtype)
    y_sample = _rmsnorm(ys, final_norm_w).astype(x_sample.dtype)
    st = jnp.stack
    return (y_prompt, y_sample, st(kp_l), st(vp_l), st(ks_l), st(vs_l), st(gp_l), st(gs_l),
            st(cp_l), st(cs_l), st(np_l), st(ns_l), st(mp_l), st(ms_l), st(vv_l))
```

```python
import functools

import numpy as np
import jax
import jax.numpy as jnp
from jax import lax
from jax.experimental import pallas as pl
from jax.experimental.pallas import tpu as pltpu

F32 = jnp.float32
BF16 = jnp.bfloat16
HIGHEST = lax.Precision.HIGHEST

GROUP_W = 1024
GLA_HEADS, GLA_DK, GLA_DV = 8, 64, 128
GLA_GATE_RANK, GLA_GATE_NORM = 16, 16.0
ML_HEADS, ML_DH = 8, 128
GM_GROUPS, GM_CH, GM_CHUNK = 8, 128, 128
MB_HEADS, MB_DH, MB_BLOCK, MB_TOPK = 8, 128, 256, 3
ROT_DIM, ROPE_THETA = 32, 500000.0
EPS = 1e-6

LANES = 128
SUBLANES = 8
VMEM_LIMIT = 56 * 1024 * 1024

NEG = -1e30
SAMPLE_T = SUBLANES

_MAIN_ORDER = ("g_q", "g_k", "g_v", "g_z", "m_q", "m_k", "m_v", "m_o", "m_z",
               "c_u", "c_v", "c_z", "a_q", "a_k", "a_v", "a_z")
_SPLIT_NAMES = ("g_q", "g_k", "g_v", "g_lr", "g_z", "m_q", "m_k", "m_v", "m_i", "m_f", "m_o", "m_z",
                "c_u", "c_v", "c_z", "a_q", "a_k", "a_v", "a_z")
_SPLIT_W = (GLA_HEADS * GLA_DK, GLA_HEADS * GLA_DK, GLA_HEADS * GLA_DV, GLA_GATE_RANK, GROUP_W,
            GROUP_W, GROUP_W, GROUP_W, ML_HEADS, ML_HEADS, GROUP_W, GROUP_W,
            GROUP_W, GROUP_W, GROUP_W, GROUP_W, GROUP_W, GROUP_W, GROUP_W)
_SRC_OFF = dict(zip(_SPLIT_NAMES, np.concatenate([[0], np.cumsum(_SPLIT_W)[:-1]]).tolist()))
_SRC_W = dict(zip(_SPLIT_NAMES, _SPLIT_W))
_MAIN_OFF = {}
_off = 0
for _n in _MAIN_ORDER:
    _MAIN_OFF[_n] = _off
    _off += _SRC_W[_n]
N_MAIN = _off
SM_LR, SM_I, SM_F = 0, GLA_GATE_RANK, GLA_GATE_RANK + ML_HEADS


def _cparams(sem):
    return pltpu.CompilerParams(dimension_semantics=sem, vmem_limit_bytes=VMEM_LIMIT)


def _pick(n, cands):
    for c in cands:
        if n % c == 0:
            return c
    return n


def _silu(x):
    return x / (1.0 + jnp.exp(-x))


def _sigmoid(x):
    return 1.0 / (1.0 + jnp.exp(-x))


def _log_sigmoid(x):
    return jnp.minimum(x, 0.0) - jnp.log1p(jnp.exp(-jnp.abs(x)))


def _gelu(x):
    c = np.sqrt(2.0 / np.pi).astype(np.float32)
    return 0.5 * x * (1.0 + jnp.tanh(c * (x + 0.044715 * (x * x * x))))


def _dot_nt(a, b, precision=None):
    return lax.dot_general(a, b, (((1,), (1,)), ((), ())), preferred_element_type=F32, precision=precision)


def _dot_tn(a, b):
    return lax.dot_general(a, b, (((0,), (0,)), ((), ())), preferred_element_type=F32)


def _dot(a, b, precision=None):
    return jnp.dot(a, b, preferred_element_type=F32, precision=precision)


def _rms_small_kernel(x_ref, nw_ref, ws_ref, h_ref, sm_ref):
    x = x_ref[...]
    ms = jnp.mean(x * x, axis=-1, keepdims=True)
    hb = (x * lax.rsqrt(ms + EPS) * nw_ref[...]).astype(BF16)
    h_ref[...] = hb
    sm_ref[...] = _dot(hb, ws_ref[...])


def _rms_small(x, nw, w_small):
    m, d = x.shape
    bm = _pick(m, (256, 128, 64, 32, 16, 8))
    return pl.pallas_call(
        _rms_small_kernel,
        grid=(m // bm,),
        in_specs=[pl.BlockSpec((bm, d), lambda i: (i, 0)),
                  pl.BlockSpec((1, d), lambda i: (0, 0)),
                  pl.BlockSpec((d, LANES), lambda i: (0, 0))],
        out_specs=[pl.BlockSpec((bm, d), lambda i: (i, 0)),
                   pl.BlockSpec((bm, LANES), lambda i: (i, 0))],
        out_shape=[jax.ShapeDtypeStruct((m, d), BF16), jax.ShapeDtypeStruct((m, LANES), F32)],
        compiler_params=_cparams(("parallel",)),
        name="rms_small",
    )(x, nw.reshape(1, d), w_small)


def _rms_kernel(x_ref, nw_ref, o_ref):
    x = x_ref[...]
    ms = jnp.mean(x * x, axis=-1, keepdims=True)
    o_ref[...] = x * lax.rsqrt(ms + EPS) * nw_ref[...]


def _rms(x, nw):
    m, d = x.shape
    bm = _pick(m, (256, 128, 64, 32, 16, 8))
    return pl.pallas_call(
        _rms_kernel,
        grid=(m // bm,),
        in_specs=[pl.BlockSpec((bm, d), lambda i: (i, 0)), pl.BlockSpec((1, d), lambda i: (0, 0))],
        out_specs=pl.BlockSpec((bm, d), lambda i: (i, 0)),
        out_shape=jax.ShapeDtypeStruct((m, d), F32),
        compiler_params=_cparams(("parallel",)),
        name="rms_final",
    )(x, nw.reshape(1, d))


def _mm_kernel(a_ref, b_ref, o_ref):
    o_ref[...] = _dot(a_ref[...], b_ref[...])


def _matmul(a, b):
    m, k = a.shape
    n = b.shape[1]
    bm = _pick(m, (1024, 512, 256, 128, 64))
    bn = _pick(n, (1024, 512, 256, 128))
    return pl.pallas_call(
        _mm_kernel,
        grid=(n // bn, m // bm),
        in_specs=[pl.BlockSpec((bm, k), lambda j, i: (i, 0)),
                  pl.BlockSpec((k, bn), lambda j, i: (0, j))],
        out_specs=pl.BlockSpec((bm, bn), lambda j, i: (i, j)),
        out_shape=jax.ShapeDtypeStruct((m, n), F32),
        compiler_params=_cparams(("parallel", "parallel")),
        name="in_proj",
    )(a, b)


def _out_kernel(a0, a1, a2, a3, w_ref, x_ref, o_ref):
    acc = x_ref[...]
    for g, a in enumerate((a0, a1, a2, a3)):
        acc = acc + _dot(a[...].astype(BF16), w_ref[g])
    o_ref[...] = acc


def _out_proj(mix, w4, x):
    m, d = x.shape
    bm = _pick(m, (1024, 512, 256, 128, 64))
    bn = _pick(d, (512, 256, 128))
    a_spec = pl.BlockSpec((bm, GROUP_W), lambda i, j: (i, 0))
    return pl.pallas_call(
        _out_kernel,
        grid=(m // bm, d // bn),
        in_specs=[a_spec, a_spec, a_spec, a_spec,
                  pl.BlockSpec((4, GROUP_W, bn), lambda i, j: (0, 0, j)),
                  pl.BlockSpec((bm, bn), lambda i, j: (i, j))],
        out_specs=pl.BlockSpec((bm, bn), lambda i, j: (i, j)),
        out_shape=jax.ShapeDtypeStruct((m, d), F32),
        compiler_params=_cparams(("parallel", "parallel")),
        name="out_proj",
    )(*mix, w4, x)


def _rope_kernel(x_ref, a_ref, b_ref, c_ref, o_ref):
    a, b, c = a_ref[...], b_ref[...], c_ref[...]
    for h in range(MB_HEADS):
        sl = slice(h * MB_DH, (h + 1) * MB_DH)
        x = x_ref[0, :, sl]
        o_ref[0, :, sl] = (x * a + pltpu.roll(x, MB_DH - ROT_DIM // 2, 1) * b
                           + pltpu.roll(x, ROT_DIM // 2, 1) * c)


def _rope_tables(pos):
    half = ROT_DIM // 2
    inv_freq = jnp.power(ROPE_THETA, -jnp.arange(0, ROT_DIM, 2, dtype=F32) / ROT_DIM)
    ang = pos.astype(F32)[:, None] * inv_freq[None, :]
    cos, sin = jnp.cos(ang), jnp.sin(ang)
    t = pos.shape[0]
    a = jnp.concatenate([cos, cos, jnp.ones((t, MB_DH - ROT_DIM), F32)], axis=1)
    b = jnp.concatenate([-sin, jnp.zeros((t, MB_DH - half), F32)], axis=1)
    c = jnp.concatenate([jnp.zeros((t, half), F32), sin, jnp.zeros((t, MB_DH - ROT_DIM), F32)], axis=1)
    return a, b, c


def _rope(proj, col, tables):
    bsz, t, _ = proj.shape
    bt = _pick(t, (512, 256, 128, 64, 32, 16, 8))
    tab = pl.BlockSpec((bt, MB_DH), lambda b, i: (i, 0))
    return pl.pallas_call(
        _rope_kernel,
        grid=(bsz, t // bt),
        in_specs=[pl.BlockSpec((1, bt, GROUP_W), lambda b, i: (b, i, col // GROUP_W)), tab, tab, tab],
        out_specs=pl.BlockSpec((1, bt, GROUP_W), lambda b, i: (b, i, 0)),
        out_shape=jax.ShapeDtypeStruct((bsz, t, GROUP_W), F32),
        compiler_params=_cparams(("parallel", "parallel")),
        name="rope",
    )(proj, *tables)


def _gla_kernel(q_ref, k_ref, v_ref, z_ref, sm_ref, wg_ref, bg_ref, nw_ref, s0_ref,
                o_ref, s_ref, g_scr, *, t, c, t_valid):
    dk = GLA_DK
    x = _dot(sm_ref[0], wg_ref[...], precision=HIGHEST) + bg_ref[...]
    g_all = _log_sigmoid(x) * (1.0 / GLA_GATE_NORM)
    if t_valid < t:
        rows_t = lax.broadcasted_iota(jnp.int32, (t, LANES), 0)
        g_all = jnp.where(rows_t < t_valid, g_all, 0.0)
    g_scr[...] = g_all

    lane = lax.broadcasted_iota(jnp.int32, (c, LANES), 1)
    row = lax.broadcasted_iota(jnp.int32, (c, LANES), 0)
    head0 = lane < dk
    rr = lax.broadcasted_iota(jnp.int32, (c, c), 0)
    cc = lax.broadcasted_iota(jnp.int32, (c, c), 1)
    tril = jnp.where(rr >= cc, 1.0, 0.0).astype(F32)
    nw = nw_ref[...]

    def body(ci, s2t):
        r0 = pl.multiple_of(ci * c, c)
        q = q_ref[0, pl.ds(r0, c), :] * (dk ** -0.5)
        k = k_ref[0, pl.ds(r0, c), :]
        if t_valid < t:
            k = jnp.where(row < t_valid, k, 0.0)
        g = g_scr[pl.ds(r0, c), :]
        v = v_ref[0, pl.ds(r0, c), :]
        z = z_ref[0, pl.ds(r0, c), :]
        b = _dot(tril, g, precision=HIGHEST)
        qe = q * jnp.exp(b)
        lhs = jnp.concatenate([jnp.where(head0, qe, 0.0), jnp.where(head0, 0.0, qe)], axis=0)
        o_inter = _dot_nt(lhs.astype(BF16), s2t.astype(BF16))

        att0 = jnp.zeros((c, c), F32)
        att1 = jnp.zeros((c, c), F32)
        for s in range(c):
            dec = jnp.exp(jnp.where(row >= s, b - b[s:s + 1, :], NEG))
            tile = q * dec * k[s:s + 1, :]
            col0 = jnp.sum(jnp.where(head0, tile, 0.0), axis=-1, keepdims=True)
            col1 = jnp.sum(jnp.where(head0, 0.0, tile), axis=-1, keepdims=True)
            att0 = jnp.where(cc == s, col0, att0)
            att1 = jnp.where(cc == s, col1, att1)

        b_end = b[c - 1:c, :]
        kt = k * jnp.exp(b_end - b)
        rhs = jnp.concatenate([jnp.where(head0, kt, 0.0), jnp.where(head0, 0.0, kt)], axis=0)
        v01 = jnp.concatenate([v[:, :GLA_DV], v[:, GLA_DV:]], axis=0)
        upd = _dot_tn(v01.astype(BF16), rhs.astype(BF16))
        s2t_new = jnp.exp(b_end) * s2t + upd

        for h, att in enumerate((att0, att1)):
            vh = v[:, h * GLA_DV:(h + 1) * GLA_DV]
            o = o_inter[h * c:(h + 1) * c, :] + _dot(att.astype(BF16), vh.astype(BF16))
            on = o * lax.rsqrt(jnp.mean(o * o, axis=-1, keepdims=True) + EPS) * nw
            zh = z[:, h * GLA_DV:(h + 1) * GLA_DV]
            o_ref[0, pl.ds(r0, c), h * GLA_DV:(h + 1) * GLA_DV] = (on * _silu(zh)).astype(o_ref.dtype)
        return s2t_new

    s_ref[0, 0] = lax.fori_loop(0, t // c, body, s0_ref[0, 0])


def _gla(proj, small, wg_pad, bg, nw, s0t, t_valid, out_dtype):
    bsz, t, _ = proj.shape
    c = min(16, t)
    pairs = GLA_HEADS // 2
    qb, kb = _MAIN_OFF["g_q"] // LANES, _MAIN_OFF["g_k"] // LANES
    vb, zb = _MAIN_OFF["g_v"] // (2 * GLA_DV), _MAIN_OFF["g_z"] // (2 * GLA_DV)
    kern = functools.partial(_gla_kernel, t=t, c=c, t_valid=t_valid)
    return pl.pallas_call(
        kern,
        grid=(bsz, pairs),
        in_specs=[pl.BlockSpec((1, t, LANES), lambda b, p: (b, 0, qb + p)),
                  pl.BlockSpec((1, t, LANES), lambda b, p: (b, 0, kb + p)),
                  pl.BlockSpec((1, t, 2 * GLA_DV), lambda b, p: (b, 0, vb + p)),
                  pl.BlockSpec((1, t, 2 * GLA_DV), lambda b, p: (b, 0, zb + p)),
                  pl.BlockSpec((1, t, LANES), lambda b, p: (b, 0, 0)),
                  pl.BlockSpec((LANES, LANES), lambda b, p: (0, p)),
                  pl.BlockSpec((1, LANES), lambda b, p: (0, p)),
                  pl.BlockSpec((1, GLA_DV), lambda b, p: (0, 0)),
                  pl.BlockSpec((1, 1, GLA_DV, LANES), lambda b, p: (b, p, 0, 0))],
        out_specs=[pl.BlockSpec((1, t, 2 * GLA_DV), lambda b, p: (b, 0, p)),
                   pl.BlockSpec((1, 1, GLA_DV, LANES), lambda b, p: (b, p, 0, 0))],
        out_shape=[jax.ShapeDtypeStruct((bsz, t, GROUP_W), out_dtype),
                   jax.ShapeDtypeStruct((bsz, pairs, GLA_DV, LANES), F32)],
        scratch_shapes=[pltpu.VMEM((t, LANES), F32)],
        compiler_params=_cparams(("parallel", "parallel")),
        name="gla",
    )(proj, proj, proj, proj, small, wg_pad, bg.reshape(1, -1), nw.reshape(1, -1), s0t)


def _gla_state_to_pairs(s):
    bsz = s.shape[0]
    s = s.reshape(bsz, GLA_HEADS // 2, 2, GLA_DK, GLA_DV)
    return s.transpose(0, 1, 4, 2, 3).reshape(bsz, GLA_HEADS // 2, GLA_DV, 2 * GLA_DK)


def _gla_state_from_pairs(s):
    bsz = s.shape[0]
    s = s.reshape(bsz, GLA_HEADS // 2, GLA_DV, 2, GLA_DK)
    return s.transpose(0, 1, 3, 4, 2).reshape(bsz, GLA_HEADS, GLA_DK, GLA_DV)


def _gates_kernel(sm_ref, bias_ref, o_ref, *, t_valid):
    x = sm_ref[0] + bias_ref[...]
    lane = lax.broadcasted_iota(jnp.int32, x.shape, 1)
    is_f = (lane >= SM_F) & (lane < SM_F + ML_HEADS)
    out = jnp.where(is_f, _log_sigmoid(x), x)
    if t_valid is not None:
        row = lax.broadcasted_iota(jnp.int32, x.shape, 0)
        out = jnp.where(row < t_valid, out, jnp.where(is_f, 0.0, NEG))
    o_ref[0] = out


def _gates(small, bias_row, t_valid):
    bsz, t, _ = small.shape
    bt = t if t_valid < t else _pick(t, (1024, 512, 256, 128, 64, 32, 16, 8))
    return pl.pallas_call(
        functools.partial(_gates_kernel, t_valid=t_valid if t_valid < t else None),
        grid=(bsz, t // bt),
        in_specs=[pl.BlockSpec((1, bt, LANES), lambda b, i: (b, i, 0)),
                  pl.BlockSpec((1, LANES), lambda b, i: (0, 0))],
        out_specs=pl.BlockSpec((1, bt, LANES), lambda b, i: (b, i, 0)),
        out_shape=jax.ShapeDtypeStruct((bsz, t, LANES), F32),
        compiler_params=_cparams(("parallel", "parallel")),
        name="ml_gates",
    )(small, bias_row)


def _mlstm_kernel(q_ref, k_ref, v_ref, og_ref, z_ref, gt_ref, gi_ref, gf_ref, nw_ref,
                  c0_ref, n0_ref, m0_ref, o_ref, c_ref, n_ref, m_ref, *, t, c):
    h = pl.program_id(1)
    lane = lax.broadcasted_iota(jnp.int32, (c, LANES), 1)
    rr = lax.broadcasted_iota(jnp.int32, (c, c), 0)
    cc = lax.broadcasted_iota(jnp.int32, (c, c), 1)
    causal = rr >= cc
    nw = nw_ref[...]

    def body(ci, carry):
        cm, n, m_prev = carry
        r0 = pl.multiple_of(ci * c, c)
        q = q_ref[0, pl.ds(r0, c), :] * (ML_DH ** -0.5)
        k = k_ref[0, pl.ds(r0, c), :]
        v = v_ref[0, pl.ds(r0, c), :]
        gt = gt_ref[0, pl.ds(r0, c), :]
        i_col = jnp.sum(jnp.where(lane == SM_I + h, gt, 0.0), axis=-1, keepdims=True)
        f_col = jnp.sum(jnp.where(lane == SM_F + h, gt, 0.0), axis=-1, keepdims=True)
        i_row = gi_ref[0, h, pl.ds(ci, 1), :]
        f_row = gf_ref[0, h, pl.ds(ci, 1), :]
        fc_col = jnp.sum(jnp.where(causal, f_row, 0.0), axis=-1, keepdims=True)
        fc_row = jnp.sum(jnp.where(rr <= cc, f_col, 0.0), axis=0, keepdims=True)
        d = jnp.where(causal, fc_col - fc_row + i_row, NEG)
        inter = fc_col + m_prev
        m_t = jnp.maximum(inter, jnp.max(d, axis=-1, keepdims=True))
        w_intra = jnp.exp(d - m_t)
        w_state = jnp.exp(inter - m_t)
        qb, kb, vb = q.astype(BF16), k.astype(BF16), v.astype(BF16)
        qk = _dot_nt(qb, kb) * w_intra
        num = w_state * _dot_nt(qb, cm.astype(BF16)) + _dot(qk.astype(BF16), vb)
        den = w_state * jnp.sum(q * n, axis=-1, keepdims=True) + jnp.sum(qk, axis=-1, keepdims=True)
        hout = num / jnp.maximum(jnp.abs(den), jnp.exp(-m_t))
        m_new = m_t[c - 1:c, :]
        f_end = fc_col[c - 1:c, :]
        w_end = jnp.exp(f_end - fc_col + i_col - m_new)
        dec = jnp.exp(f_end + m_prev - m_new)
        c_new = dec * cm + _dot_tn((v * w_end).astype(BF16), kb)
        n_new = dec * n + jnp.sum(k * w_end, axis=0, keepdims=True)

        o = _sigmoid(og_ref[0, pl.ds(r0, c), :]) * hout
        on = o * lax.rsqrt(jnp.mean(o * o, axis=-1, keepdims=True) + EPS) * nw
        o_ref[0, pl.ds(r0, c), :] = (on * _silu(z_ref[0, pl.ds(r0, c), :])).astype(o_ref.dtype)
        return c_new, n_new, m_new

    cm, n, m = lax.fori_loop(0, t // c, body, (c0_ref[0, 0], n0_ref[0, 0], m0_ref[0, 0][:, :1]))
    c_ref[0, 0] = cm
    n_ref[0, 0] = n
    m_ref[0, 0] = jnp.broadcast_to(m, (1, LANES))


def _mlstm(proj, gates, gates_t, nw, c0, n0, m0, out_dtype):
    bsz, t, _ = proj.shape
    c = min(128, t)
    nc = t // c
    col = lambda name: _MAIN_OFF[name] // ML_DH
    spec = lambda name: pl.BlockSpec((1, t, ML_DH), lambda b, h, o=col(name): (b, 0, o + h))
    st = lambda rows: pl.BlockSpec((1, 1, rows, ML_DH), lambda b, h: (b, h, 0, 0))
    return pl.pallas_call(
        functools.partial(_mlstm_kernel, t=t, c=c),
        grid=(bsz, ML_HEADS),
        in_specs=[spec("m_q"), spec("m_k"), spec("m_v"), spec("m_o"), spec("m_z"),
                  pl.BlockSpec((1, t, LANES), lambda b, h: (b, 0, 0)),
                  pl.BlockSpec((1, ML_HEADS, nc, c), lambda b, h: (b, SM_I // ML_HEADS, 0, 0)),
                  pl.BlockSpec((1, ML_HEADS, nc, c), lambda b, h: (b, SM_F // ML_HEADS, 0, 0)),
                  pl.BlockSpec((1, ML_DH), lambda b, h: (0, 0)),
                  st(ML_DH), st(1), st(1)],
        out_specs=[pl.BlockSpec((1, t, ML_DH), lambda b, h: (b, 0, h)), st(ML_DH), st(1), st(1)],
        out_shape=[jax.ShapeDtypeStruct((bsz, t, GROUP_W), out_dtype),
                   jax.ShapeDtypeStruct((bsz, ML_HEADS, ML_DH, ML_DH), F32),
                   jax.ShapeDtypeStruct((bsz, ML_HEADS, 1, ML_DH), F32),
                   jax.ShapeDtypeStruct((bsz, ML_HEADS, 1, LANES), F32)],
        compiler_params=_cparams(("parallel", "parallel")),
        name="mlstm",
    )(proj, proj, proj, proj, proj, gates, gates_t, gates_t, nw.reshape(1, -1), c0, n0, m0)


def _gmlp_kernel(u_ref, v_ref, z_ref, lw_ref, lb_ref, ws_ref, bs_ref, o_ref, vn_ref, *, l):
    gv = _gelu(v_ref[0])
    mu = jnp.mean(gv, axis=-1, keepdims=True)
    xc = gv - mu
    vn = xc * lax.rsqrt(jnp.mean(xc * xc, axis=-1, keepdims=True) + EPS) * lw_ref[...] + lb_ref[...]
    vn_ref[0] = vn
    rr = lax.broadcasted_iota(jnp.int32, (l, l), 0)
    cc = lax.broadcasted_iota(jnp.int32, (l, l), 1)
    for g in range(GM_GROUPS):
        sl = slice(g * GM_CH, (g + 1) * GM_CH)
        w = jnp.where(rr >= cc, ws_ref[g], 0.0)
        vg = vn[:, sl]
        if l >= GM_CHUNK:
            s = _dot(w.astype(BF16), vg.astype(BF16))
        else:
            s = jnp.zeros((l, GM_CH), F32)
            for r in range(l):
                s = s + w[:, r:r + 1] * vg[r:r + 1, :]
        s = s + bs_ref[:, g:g + 1]
        o_ref[0, :, sl] = (_gelu(u_ref[0, :, sl]) * s * _silu(z_ref[0, :, sl])).astype(o_ref.dtype)


def _gmlp(proj, lw, lb, ws, bs_t, out_dtype):
    bsz, t, _ = proj.shape
    l = min(t, GM_CHUNK)
    col = lambda name: _MAIN_OFF[name] // GROUP_W
    spec = lambda name: pl.BlockSpec((1, l, GROUP_W), lambda b, i, o=col(name): (b, i, o))
    return pl.pallas_call(
        functools.partial(_gmlp_kernel, l=l),
        grid=(bsz, t // l),
        in_specs=[spec("c_u"), spec("c_v"), spec("c_z"),
                  pl.BlockSpec((1, GROUP_W), lambda b, i: (0, 0)),
                  pl.BlockSpec((1, GROUP_W), lambda b, i: (0, 0)),
                  pl.BlockSpec((GM_GROUPS, l, l), lambda b, i: (0, 0, 0)),
                  pl.BlockSpec((l, GM_GROUPS), lambda b, i: (0, 0))],
        out_specs=[pl.BlockSpec((1, l, GROUP_W), lambda b, i: (b, i, 0)),
                   pl.BlockSpec((1, l, GROUP_W), lambda b, i: (b, i, 0))],
        out_shape=[jax.ShapeDtypeStruct((bsz, t, GROUP_W), out_dtype),
                   jax.ShapeDtypeStruct((bsz, t, GROUP_W), F32)],
        compiler_params=_cparams(("parallel", "parallel")),
        name="gmlp",
    )(proj, proj, proj, lw.reshape(1, -1), lb.reshape(1, -1), ws, bs_t)


def _moba_prompt_kernel(q_ref, k_ref, v_ref, z_ref, o_ref, kb_scr, vb_scr, km_scr, *, t):
    blk = MB_BLOCK
    nb = t // blk
    scale = MB_DH ** -0.5
    kb_scr[...] = k_ref[0].astype(BF16)
    vb_scr[...] = v_ref[0].astype(BF16)
    km_scr[...] = jnp.zeros((LANES, MB_DH), F32)
    for n in range(nb):
        km_scr[n:n + 1, :] = jnp.mean(k_ref[0, n * blk:(n + 1) * blk, :], axis=0, keepdims=True)
    kmean = km_scr[...]
    lane = lax.broadcasted_iota(jnp.int32, (blk, LANES), 1)
    rr = lax.broadcasted_iota(jnp.int32, (blk, blk), 0)
    cc = lax.broadcasted_iota(jnp.int32, (blk, blk), 1)

    def qtile(qi, _):
        q0 = pl.multiple_of(qi * blk, blk)
        q = q_ref[0, pl.ds(q0, blk), :]
        gate = _dot_nt(q, kmean, precision=HIGHEST)
        cnt = jnp.zeros((blk, LANES), F32)
        for m in range(nb):
            gm = gate[:, m:m + 1]
            beats = (gm > gate) | ((gm == gate) & (lane > m))
            cnt = cnt + jnp.where(beats, jnp.where(qi > m, 1.0, 0.0), 0.0)
        bias = jnp.where((cnt < MB_TOPK) & (lane < qi), 0.0, NEG)
        qb = q.astype(BF16)

        s = _dot_nt(qb, kb_scr[pl.ds(q0, blk), :]) * scale
        s = jnp.where(cc <= rr, s, NEG)
        m_i = jnp.max(s, axis=-1, keepdims=True)
        p = jnp.exp(s - m_i)
        l_i = jnp.sum(p, axis=-1, keepdims=True)
        acc = _dot(p.astype(BF16), vb_scr[pl.ds(q0, blk), :])

        def kblock(j, carry):
            m_i, l_i, acc = carry
            k0 = pl.multiple_of(j * blk, blk)
            bias_j = jnp.sum(jnp.where(lane == j, bias, 0.0), axis=-1, keepdims=True)
            s = _dot_nt(qb, kb_scr[pl.ds(k0, blk), :]) * scale + bias_j
            m_new = jnp.maximum(m_i, jnp.max(s, axis=-1, keepdims=True))
            alpha = jnp.exp(m_i - m_new)
            p = jnp.exp(s - m_new)
            l_new = alpha * l_i + jnp.sum(p, axis=-1, keepdims=True)
            acc_new = alpha * acc + _dot(p.astype(BF16), vb_scr[pl.ds(k0, blk), :])
            return m_new, l_new, acc_new

        m_i, l_i, acc = lax.fori_loop(0, qi, kblock, (m_i, l_i, acc))
        o = acc / l_i
        o_ref[0, pl.ds(q0, blk), :] = (o * _silu(z_ref[0, pl.ds(q0, blk), :])).astype(o_ref.dtype)
        return 0

    lax.fori_loop(0, nb, qtile, 0)


def _moba_prompt(q_rope, k_rope, proj, out_dtype):
    bsz, t, _ = proj.shape
    assert t % MB_BLOCK == 0 and t // MB_BLOCK <= LANES
    vcol, zcol = _MAIN_OFF["a_v"] // MB_DH, _MAIN_OFF["a_z"] // MB_DH
    hd = pl.BlockSpec((1, t, MB_DH), lambda b, h: (b, 0, h))
    return pl.pallas_call(
        functools.partial(_moba_prompt_kernel, t=t),
        grid=(bsz, MB_HEADS),
        in_specs=[hd, hd,
                  pl.BlockSpec((1, t, MB_DH), lambda b, h: (b, 0, vcol + h)),
                  pl.BlockSpec((1, t, MB_DH), lambda b, h: (b, 0, zcol + h))],
        out_specs=hd,
        out_shape=jax.ShapeDtypeStruct((bsz, t, GROUP_W), out_dtype),
        scratch_shapes=[pltpu.VMEM((t, MB_DH), BF16), pltpu.VMEM((t, MB_DH), BF16),
                        pltpu.VMEM((LANES, MB_DH), F32)],
        compiler_params=_cparams(("parallel", "parallel")),
        name="moba_prompt",
    )(q_rope, k_rope, proj, proj)


def _moba_past_kernel(pt_ref, q_ref, k0_ref, k1_ref, v0_ref, v1_ref, o_ref, m_ref, l_ref, g_ref):
    del pt_ref
    scale = MB_DH ** -0.5
    lane = lax.broadcasted_iota(jnp.int32, (SAMPLE_T, LANES), 1)
    m_all = jnp.zeros((SAMPLE_T, LANES), F32)
    l_all = jnp.zeros((SAMPLE_T, LANES), F32)
    g_all = jnp.zeros((SAMPLE_T, LANES), F32)
    for h in range(MB_HEADS):
        sl = slice(h * MB_DH, (h + 1) * MB_DH)
        qh = q_ref[0, :, sl]
        k0, k1 = k0_ref[:, sl], k1_ref[:, sl]
        ksum = jnp.sum(k0, axis=0, keepdims=True) + jnp.sum(k1, axis=0, keepdims=True)
        gate = jnp.sum(qh * ksum, axis=-1, keepdims=True) * (1.0 / MB_BLOCK)
        qb = qh.astype(BF16)
        s0 = _dot_nt(qb, k0.astype(BF16)) * scale
        s1 = _dot_nt(qb, k1.astype(BF16)) * scale
        mh = jnp.maximum(jnp.max(s0, axis=-1, keepdims=True), jnp.max(s1, axis=-1, keepdims=True))
        p0, p1 = jnp.exp(s0 - mh), jnp.exp(s1 - mh)
        lh = jnp.sum(p0, axis=-1, keepdims=True) + jnp.sum(p1, axis=-1, keepdims=True)
        o_ref[h] = (_dot(p0.astype(BF16), v0_ref[:, sl].astype(BF16))
                    + _dot(p1.astype(BF16), v1_ref[:, sl].astype(BF16)))
        m_all = jnp.where(lane == h, mh, m_all)
        l_all = jnp.where(lane == h, lh, l_all)
        g_all = jnp.where(lane == h, gate, g_all)
    m_ref[...] = m_all
    l_ref[...] = l_all
    g_ref[...] = g_all


def _moba_past(layer, q_rope, cache_k4, cache_v4, page_table):
    db = q_rope.shape[0]
    page = cache_k4.shape[2]
    assert 2 * page == MB_BLOCK
    nb = page_table.shape[1] // 2
    pg = lambda half: pl.BlockSpec((None, None, page, GROUP_W),
                                   lambda b, n, pt, hf=half: (layer, pt[b, 2 * n + hf], 0, 0))
    stat = pl.BlockSpec((None, None, SAMPLE_T, LANES), lambda b, n, pt: (b, n, 0, 0))
    stat_shape = jax.ShapeDtypeStruct((db, nb, SAMPLE_T, LANES), F32)
    grid_spec = pltpu.PrefetchScalarGridSpec(
        num_scalar_prefetch=1,
        grid=(db, nb),
        in_specs=[pl.BlockSpec((1, SAMPLE_T, GROUP_W), lambda b, n, pt: (b, 0, 0)),
                  pg(0), pg(1), pg(0), pg(1)],
        out_specs=[pl.BlockSpec((None, None, MB_HEADS, SAMPLE_T, MB_DH), lambda b, n, pt: (b, n, 0, 0, 0)),
                   stat, stat, stat],
    )
    return pl.pallas_call(
        _moba_past_kernel,
        grid_spec=grid_spec,
        out_shape=[jax.ShapeDtypeStruct((db, nb, MB_HEADS, SAMPLE_T, MB_DH), F32),
                   stat_shape, stat_shape, stat_shape],
        compiler_params=_cparams(("parallel", "parallel")),
        name="moba_past",
    )(page_table, q_rope, cache_k4, cache_k4, cache_v4, cache_v4)


def _moba_merge_kernel(op_ref, m_ref, l_ref, g_ref, q_ref, k_ref, v_ref, z_ref, o_ref, *, nb, t_valid):
    scale = MB_DH ** -0.5
    g = g_ref[0]
    m = m_ref[0]
    l = l_ref[0]
    nidx = lax.broadcasted_iota(jnp.int32, g.shape, 0)
    sel = jnp.zeros(g.shape, jnp.bool_)
    gm = g
    for _ in range(min(MB_TOPK, nb)):
        mx = jnp.max(gm, axis=0, keepdims=True)
        first = jnp.min(jnp.where(gm == mx, nidx, nb), axis=0, keepdims=True)
        pick = nidx == first
        sel = sel | pick
        gm = jnp.where(pick, -jnp.inf, gm)

    lane = lax.broadcasted_iota(jnp.int32, (SAMPLE_T, LANES), 1)
    rr = lax.broadcasted_iota(jnp.int32, (SAMPLE_T, SAMPLE_T), 0)
    cc = lax.broadcasted_iota(jnp.int32, (SAMPLE_T, SAMPLE_T), 1)
    own = []
    m_own = jnp.zeros((SAMPLE_T, LANES), F32)
    for h in range(MB_HEADS):
        sl = slice(h * MB_DH, (h + 1) * MB_DH)
        s = _dot_nt(q_ref[0, :, sl].astype(BF16), k_ref[0, :, sl].astype(BF16)) * scale
        s = jnp.where((cc <= rr) & (cc < t_valid), s, NEG)
        mo = jnp.max(s, axis=-1, keepdims=True)
        p = jnp.exp(s - mo)
        own.append((mo, jnp.sum(p, axis=-1, keepdims=True),
                    _dot(p.astype(BF16), v_ref[0, :, sl].astype(BF16))))
        m_own = jnp.where(lane == h, mo, m_own)

    m_tot = jnp.maximum(jnp.max(jnp.where(sel, m, NEG), axis=0), m_own)
    w = jnp.where(sel, jnp.exp(m - m_tot[None]), 0.0)
    l_past = jnp.sum(w * l, axis=0)
    for h in range(MB_HEADS):
        sl = slice(h * MB_DH, (h + 1) * MB_DH)
        mo, lo, oo = own[h]
        w_own = jnp.exp(mo - m_tot[:, h:h + 1])
        acc = w_own * oo
        for n in range(nb):
            acc = acc + w[n][:, h:h + 1] * op_ref[0, n, h]
        den = l_past[:, h:h + 1] + w_own * lo
        o_ref[0, :, sl] = (acc / den) * _silu(z_ref[0, :, sl])


def _moba_merge(o_part, m_all, l_all, g_all, q_rope, k_rope, proj, t_valid):
    db, nb = o_part.shape[:2]
    vcol, zcol = _MAIN_OFF["a_v"] // GROUP_W, _MAIN_OFF["a_z"] // GROUP_W
    stat = pl.BlockSpec((1, nb, SAMPLE_T, LANES), lambda b: (b, 0, 0, 0))
    row = pl.BlockSpec((1, SAMPLE_T, GROUP_W), lambda b: (b, 0, 0))
    return pl.pallas_call(
        functools.partial(_moba_merge_kernel, nb=nb, t_valid=t_valid),
        grid=(db,),
        in_specs=[pl.BlockSpec((1, nb, MB_HEADS, SAMPLE_T, MB_DH), lambda b: (b, 0, 0, 0, 0)),
                  stat, stat, stat, row, row,
                  pl.BlockSpec((1, SAMPLE_T, GROUP_W), lambda b: (b, 0, vcol)),
                  pl.BlockSpec((1, SAMPLE_T, GROUP_W), lambda b: (b, 0, zcol))],
        out_specs=row,
        out_shape=jax.ShapeDtypeStruct((db, SAMPLE_T, GROUP_W), F32),
        compiler_params=_cparams(("parallel",)),
        name="moba_merge",
    )(o_part, m_all, l_all, g_all, q_rope, k_rope, proj, proj)


def _layer_weights(w_in_l, gla_w_gate_l, ml_b_i_l, ml_b_f_l, w_out_l):
    d = w_in_l.shape[0]
    cols = lambda n: w_in_l[:, _SRC_OFF[n]:_SRC_OFF[n] + _SRC_W[n]]
    w_main = jnp.concatenate([cols(n) for n in _MAIN_ORDER], axis=1).astype(BF16)
    pad = LANES - (GLA_GATE_RANK + 2 * ML_HEADS)
    w_small = jnp.concatenate([cols("g_lr"), cols("m_i"), cols("m_f"), jnp.zeros((d, pad), F32)],
                              axis=1).astype(BF16)
    wg_pad = jnp.concatenate([gla_w_gate_l, jnp.zeros((LANES - GLA_GATE_RANK, gla_w_gate_l.shape[1]), F32)],
                             axis=0)
    bias_row = jnp.concatenate([jnp.zeros((SM_I,), F32), ml_b_i_l, ml_b_f_l,
                                jnp.zeros((LANES - SM_F - ML_HEADS,), F32)]).reshape(1, LANES)
    w4 = w_out_l.astype(BF16).reshape(4, GROUP_W, -1)
    return w_main, w_small, wg_pad, bias_row, w4


def _mixers(x, bsz, t, t_valid, lw, rope_tables, gla_s0, ml_state, out_dtype):
    (norm_w, w_main, w_small, wg_pad, bias_row, gla_b_gate, gla_norm_w, ml_norm_w,
     gm_ln_w, gm_ln_b, gm_ws, gm_bs_t) = lw
    h, small = _rms_small(x, norm_w, w_small)
    proj = _matmul(h, w_main).reshape(bsz, t, N_MAIN)
    small = small.reshape(bsz, t, LANES)

    out_a, gla_s = _gla(proj, small, wg_pad, gla_b_gate, gla_norm_w, gla_s0, t_valid, out_dtype)

    gates = _gates(small, bias_row, t_valid)
    c = min(128, t)
    gates_t = gates.transpose(0, 2, 1).reshape(bsz, LANES, t // c, c)
    out_b, ml_c, ml_n, ml_m = _mlstm(proj, gates, gates_t, ml_norm_w, *ml_state, out_dtype)

    out_c, vn = _gmlp(proj, gm_ln_w, gm_ln_b, gm_ws, gm_bs_t, out_dtype)

    q_rope = _rope(proj, _MAIN_OFF["a_q"], rope_tables)
    k_rope = _rope(proj, _MAIN_OFF["a_k"], rope_tables)
    v_new = proj[:, :, _MAIN_OFF["a_v"]:_MAIN_OFF["a_v"] + GROUP_W]
    return proj, (out_a, out_b, out_c), q_rope, k_rope, v_new, gla_s, (ml_c, ml_n, ml_m[..., :1]), vn


def kernel(x_prompt, x_sample, cache_k, cache_v, page_table, state_gla, state_mlstm_C, state_mlstm_n,
           state_mlstm_m, norm_w, w_in, gla_w_gate, gla_b_gate, gla_norm_w, ml_b_i, ml_b_f, ml_norm_w,
           gm_ln_w, gm_ln_b, gm_w_s, gm_b_s, w_out, final_norm_w):
    bp, tp, d = x_prompt.shape
    db, ts, _ = x_sample.shape
    depth = w_in.shape[0]
    page = cache_k.shape[2]
    past_len = page_table.shape[1] * page
    assert w_out.shape[1] == 4 * GROUP_W and ts <= SAMPLE_T
    assert past_len % MB_BLOCK == 0 and tp % MB_BLOCK == 0

    tables_p = _rope_tables(jnp.arange(tp, dtype=jnp.int32))
    tables_s = _rope_tables(past_len + jnp.arange(SAMPLE_T, dtype=jnp.int32))
    cache_k4 = cache_k.reshape(depth, cache_k.shape[1], page, GROUP_W)
    cache_v4 = cache_v.reshape(depth, cache_v.shape[1], page, GROUP_W)

    yp = x_prompt.reshape(bp * tp, d)
    ys = jnp.pad(x_sample, ((0, 0), (0, SAMPLE_T - ts), (0, 0))).reshape(db * SAMPLE_T, d)
    dt_p = BF16
    dt_s = F32

    zero_gla = jnp.zeros((bp, GLA_HEADS // 2, GLA_DV, LANES), F32)
    zero_ml = (jnp.zeros((bp, ML_HEADS, ML_DH, ML_DH), F32), jnp.zeros((bp, ML_HEADS, 1, ML_DH), F32),
               jnp.zeros((bp, ML_HEADS, 1, LANES), F32))

    outs = {n: [] for n in ("kp", "vp", "ks", "vs", "gp", "gs", "cp", "cs", "np", "ns", "mp", "ms", "vv")}
    lp = min(tp, GM_CHUNK)
    for l in range(depth):
        w_main, w_small, wg_pad, bias_row, w4 = _layer_weights(w_in[l], gla_w_gate[l], ml_b_i[l], ml_b_f[l],
                                                               w_out[l])
        common = (norm_w[l], w_main, w_small, wg_pad, bias_row, gla_b_gate[l], gla_norm_w[l], ml_norm_w[l],
                  gm_ln_w[l], gm_ln_b[l])

        lw = common + (gm_w_s[l][:, :lp, :lp], gm_b_s[l][:, :lp].T)
        proj, mix, q_rope, k_rope, v_new, gla_s, ml_s, _ = _mixers(
            yp, bp, tp, tp, lw, tables_p, zero_gla, zero_ml, dt_p)
        out_d = _moba_prompt(q_rope, k_rope, proj, dt_p)
        yp = _out_proj([a.reshape(bp * tp, GROUP_W) for a in mix + (out_d,)], w4, yp)
        outs["kp"].append(k_rope.reshape(bp, tp, MB_HEADS, MB_DH))
        outs["vp"].append(v_new.reshape(bp, tp, MB_HEADS, MB_DH))
        outs["gp"].append(_gla_state_from_pairs(gla_s))
        outs["cp"].append(ml_s[0])
        outs["np"].append(ml_s[1][:, :, 0, :])
        outs["mp"].append(ml_s[2][:, :, 0, 0])

        lw = common + (gm_w_s[l][:, :SAMPLE_T, :SAMPLE_T], gm_b_s[l][:, :SAMPLE_T].T)
        ml_state = (state_mlstm_C[l], state_mlstm_n[l][:, :, None, :],
                    jnp.broadcast_to(state_mlstm_m[l][:, :, None, None], (db, ML_HEADS, 1, LANES)))
        proj, mix, q_rope, k_rope, v_new, gla_s, ml_s, vn = _mixers(
            ys, db, SAMPLE_T, ts, lw, tables_s, _gla_state_to_pairs(state_gla[l]), ml_state, dt_s)
        o_part, m_all, l_all, g_all = _moba_past(l, q_rope, cache_k4, cache_v4, page_table)
        out_d = _moba_merge(o_part, m_all, l_all, g_all, q_rope, k_rope, proj, ts)
        ys = _out_proj([a.reshape(db * SAMPLE_T, GROUP_W) for a in mix + (out_d,)], w4, ys)
        outs["ks"].append(k_rope[:, :ts].reshape(db, ts, MB_HEADS, MB_DH))
        outs["vs"].append(v_new[:, :ts].reshape(db, ts, MB_HEADS, MB_DH))
        outs["gs"].append(_gla_state_from_pairs(gla_s))
        outs["cs"].append(ml_s[0])
        outs["ns"].append(ml_s[1][:, :, 0, :])
        outs["ms"].append(ml_s[2][:, :, 0, 0])
        outs["vv"].append(vn[:, :ts])

    y_prompt = _rms(yp, final_norm_w).reshape(bp, tp, d)
    y_sample = _rms(ys, final_norm_w).reshape(db, SAMPLE_T, d)[:, :ts]
    st = jnp.stack
    return (y_prompt, y_sample, st(outs["kp"]), st(outs["vp"]), st(outs["ks"]), st(outs["vs"]),
            st(outs["gp"]), st(outs["gs"]), st(outs["cp"]), st(outs["cs"]), st(outs["np"]), st(outs["ns"]),
            st(outs["mp"]), st(outs["ms"]), st(outs["vv"]))
```

```python
import functools

import numpy as np
import jax
import jax.numpy as jnp
from jax import lax
from jax.experimental import pallas as pl
from jax.experimental.pallas import tpu as pltpu

F32 = jnp.float32
BF16 = jnp.bfloat16
HIGHEST = lax.Precision.HIGHEST

GROUP_W = 1024
GLA_HEADS, GLA_DK, GLA_DV = 8, 64, 128
GLA_GATE_RANK, GLA_GATE_NORM = 16, 16.0
ML_HEADS, ML_DH = 8, 128
GM_GROUPS, GM_CH, GM_CHUNK = 8, 128, 128
MB_HEADS, MB_DH, MB_BLOCK, MB_TOPK = 8, 128, 256, 3
ROT_DIM, ROPE_THETA = 32, 500000.0
EPS = 1e-6

LANES = 128
SUBLANES = 8
VMEM_LIMIT = 56 * 1024 * 1024

NEG = -1e30
SAMPLE_T = SUBLANES

_MAIN_ORDER = ("g_q", "g_k", "g_v", "g_z", "m_q", "m_k", "m_v", "m_o", "m_z",
               "c_u", "c_v", "c_z", "a_q", "a_k", "a_v", "a_z")
_SPLIT_NAMES = ("g_q", "g_k", "g_v", "g_lr", "g_z", "m_q", "m_k", "m_v", "m_i", "m_f", "m_o", "m_z",
                "c_u", "c_v", "c_z", "a_q", "a_k", "a_v", "a_z")
_SPLIT_W = (GLA_HEADS * GLA_DK, GLA_HEADS * GLA_DK, GLA_HEADS * GLA_DV, GLA_GATE_RANK, GROUP_W,
            GROUP_W, GROUP_W, GROUP_W, ML_HEADS, ML_HEADS, GROUP_W, GROUP_W,
            GROUP_W, GROUP_W, GROUP_W, GROUP_W, GROUP_W, GROUP_W, GROUP_W)
_SRC_OFF = dict(zip(_SPLIT_NAMES, np.concatenate([[0], np.cumsum(_SPLIT_W)[:-1]]).tolist()))
_SRC_W = dict(zip(_SPLIT_NAMES, _SPLIT_W))
_MAIN_OFF = {}
_off = 0
for _n in _MAIN_ORDER:
    _MAIN_OFF[_n] = _off
    _off += _SRC_W[_n]
N_MAIN = _off
SM_LR, SM_I, SM_F = 0, GLA_GATE_RANK, GLA_GATE_RANK + ML_HEADS


def _cparams(sem):
    return pltpu.CompilerParams(dimension_semantics=sem, vmem_limit_bytes=VMEM_LIMIT)


def _pick(n, cands):
    for c in cands:
        if n % c == 0:
            return c
    return n


def _silu(x):
    return x / (1.0 + jnp.exp(-x))


def _sigmoid(x):
    return 1.0 / (1.0 + jnp.exp(-x))


def _log_sigmoid(x):
    return jnp.minimum(x, 0.0) - jnp.log1p(jnp.exp(-jnp.abs(x)))


def _gelu(x):
    c = np.sqrt(2.0 / np.pi).astype(np.float32)
    return 0.5 * x * (1.0 + jnp.tanh(c * (x + 0.044715 * (x * x * x))))


def _dot_nt(a, b, precision=None):
    return lax.dot_general(a, b, (((1,), (1,)), ((), ())), preferred_element_type=F32, precision=precision)


def _dot_tn(a, b):
    return lax.dot_general(a, b, (((0,), (0,)), ((), ())), preferred_element_type=F32)


def _dot(a, b, precision=None):
    return jnp.dot(a, b, preferred_element_type=F32, precision=precision)


def _rms_small_kernel(x_ref, nw_ref, ws_ref, h_ref, sm_ref):
    x = x_ref[...]
    ms = jnp.mean(x * x, axis=-1, keepdims=True)
    hb = (x * lax.rsqrt(ms + EPS) * nw_ref[...]).astype(BF16)
    h_ref[...] = hb
    sm_ref[...] = _dot(hb, ws_ref[...])


def _rms_small(x, nw, w_small):
    m, d = x.shape
    bm = _pick(m, (256, 128, 64, 32, 16, 8))
    return pl.pallas_call(
        _rms_small_kernel,
        grid=(m // bm,),
        in_specs=[pl.BlockSpec((bm, d), lambda i: (i, 0)),
                  pl.BlockSpec((1, d), lambda i: (0, 0)),
                  pl.BlockSpec((d, LANES), lambda i: (0, 0))],
        out_specs=[pl.BlockSpec((bm, d), lambda i: (i, 0)),
                   pl.BlockSpec((bm, LANES), lambda i: (i, 0))],
        out_shape=[jax.ShapeDtypeStruct((m, d), BF16), jax.ShapeDtypeStruct((m, LANES), F32)],
        compiler_params=_cparams(("parallel",)),
        name="rms_small",
    )(x, nw.reshape(1, d), w_small)


def _rms_kernel(x_ref, nw_ref, o_ref):
    x = x_ref[...]
    ms = jnp.mean(x * x, axis=-1, keepdims=True)
    o_ref[...] = x * lax.rsqrt(ms + EPS) * nw_ref[...]


def _rms(x, nw):
    m, d = x.shape
    bm = _pick(m, (256, 128, 64, 32, 16, 8))
    return pl.pallas_call(
        _rms_kernel,
        grid=(m // bm,),
        in_specs=[pl.BlockSpec((bm, d), lambda i: (i, 0)), pl.BlockSpec((1, d), lambda i: (0, 0))],
        out_specs=pl.BlockSpec((bm, d), lambda i: (i, 0)),
        out_shape=jax.ShapeDtypeStruct((m, d), F32),
        compiler_params=_cparams(("parallel",)),
        name="rms_final",
    )(x, nw.reshape(1, d))


def _mm_kernel(a_ref, b_ref, o_ref):
    o_ref[...] = _dot(a_ref[...], b_ref[...])


def _matmul(a, b):
    m, k = a.shape
    n = b.shape[1]
    bm = _pick(m, (1024, 512, 256, 128, 64))
    bn = _pick(n, (1024, 512, 256, 128))
    return pl.pallas_call(
        _mm_kernel,
        grid=(n // bn, m // bm),
        in_specs=[pl.BlockSpec((bm, k), lambda j, i: (i, 0)),
                  pl.BlockSpec((k, bn), lambda j, i: (0, j))],
        out_specs=pl.BlockSpec((bm, bn), lambda j, i: (i, j)),
        out_shape=jax.ShapeDtypeStruct((m, n), F32),
        compiler_params=_cparams(("parallel", "parallel")),
        name="in_proj",
    )(a, b)


def _out_kernel(a0, a1, a2, a3, w_ref, x_ref, o_ref):
    acc = x_ref[...]
    for g, a in enumerate((a0, a1, a2, a3)):
        acc = acc + _dot(a[...].astype(BF16), w_ref[g])
    o_ref[...] = acc


def _out_proj(mix, w4, x):
    m, d = x.shape
    bm = _pick(m, (1024, 512, 256, 128, 64))
    bn = _pick(d, (512, 256, 128))
    a_spec = pl.BlockSpec((bm, GROUP_W), lambda i, j: (i, 0))
    return pl.pallas_call(
        _out_kernel,
        grid=(m // bm, d // bn),
        in_specs=[a_spec, a_spec, a_spec, a_spec,
                  pl.BlockSpec((4, GROUP_W, bn), lambda i, j: (0, 0, j)),
                  pl.BlockSpec((bm, bn), lambda i, j: (i, j))],
        out_specs=pl.BlockSpec((bm, bn), lambda i, j: (i, j)),
        out_shape=jax.ShapeDtypeStruct((m, d), F32),
        compiler_params=_cparams(("parallel", "parallel")),
        name="out_proj",
    )(*mix, w4, x)


def _rope_kernel(x_ref, a_ref, b_ref, c_ref, o_ref):
    a, b, c = a_ref[...], b_ref[...], c_ref[...]
    for h in range(MB_HEADS):
        sl = slice(h * MB_DH, (h + 1) * MB_DH)
        x = x_ref[0, :, sl]
        o_ref[0, :, sl] = (x * a + pltpu.roll(x, MB_DH - ROT_DIM // 2, 1) * b
                           + pltpu.roll(x, ROT_DIM // 2, 1) * c)


def _rope_tables(pos):
    half = ROT_DIM // 2
    inv_freq = jnp.power(ROPE_THETA, -jnp.arange(0, ROT_DIM, 2, dtype=F32) / ROT_DIM)
    ang = pos.astype(F32)[:, None] * inv_freq[None, :]
    cos, sin = jnp.cos(ang), jnp.sin(ang)
    t = pos.shape[0]
    a = jnp.concatenate([cos, cos, jnp.ones((t, MB_DH - ROT_DIM), F32)], axis=1)
    b = jnp.concatenate([-sin, jnp.zeros((t, MB_DH - half), F32)], axis=1)
    c = jnp.concatenate([jnp.zeros((t, half), F32), sin, jnp.zeros((t, MB_DH - ROT_DIM), F32)], axis=1)
    return a, b, c


def _rope(proj, col, tables):
    bsz, t, _ = proj.shape
    bt = _pick(t, (512, 256, 128, 64, 32, 16, 8))
    tab = pl.BlockSpec((bt, MB_DH), lambda b, i: (i, 0))
    return pl.pallas_call(
        _rope_kernel,
        grid=(bsz, t // bt),
        in_specs=[pl.BlockSpec((1, bt, GROUP_W), lambda b, i: (b, i, col // GROUP_W)), tab, tab, tab],
        out_specs=pl.BlockSpec((1, bt, GROUP_W), lambda b, i: (b, i, 0)),
        out_shape=jax.ShapeDtypeStruct((bsz, t, GROUP_W), F32),
        compiler_params=_cparams(("parallel", "parallel")),
        name="rope",
    )(proj, *tables)


def _kv_heads_kernel(k_ref, v_ref, a_ref, b_ref, c_ref, k4_ref, v4_ref):
    a, b, c = a_ref[...], b_ref[...], c_ref[...]
    bt = a.shape[0]
    for h in range(MB_HEADS):
        sl = slice(h * MB_DH, (h + 1) * MB_DH)
        x = k_ref[0, :, sl]
        rows = pl.ds(h, bt, stride=MB_HEADS)
        k4_ref[0, rows, :] = (x * a + pltpu.roll(x, MB_DH - ROT_DIM // 2, 1) * b
                              + pltpu.roll(x, ROT_DIM // 2, 1) * c)
        v4_ref[0, rows, :] = v_ref[0, :, sl]


def _kv_heads(proj, tables):
    bsz, t, _ = proj.shape
    bt = _pick(t, (256, 128, 64, 32, 16, 8))
    tab = pl.BlockSpec((bt, MB_DH), lambda b, i: (i, 0))
    kcol, vcol = _MAIN_OFF["a_k"] // GROUP_W, _MAIN_OFF["a_v"] // GROUP_W
    out = pl.BlockSpec((1, bt * MB_HEADS, MB_DH), lambda b, i: (b, i, 0))
    shape = jax.ShapeDtypeStruct((bsz, t * MB_HEADS, MB_DH), F32)
    return pl.pallas_call(
        _kv_heads_kernel,
        grid=(bsz, t // bt),
        in_specs=[pl.BlockSpec((1, bt, GROUP_W), lambda b, i: (b, i, kcol)),
                  pl.BlockSpec((1, bt, GROUP_W), lambda b, i: (b, i, vcol)), tab, tab, tab],
        out_specs=[out, out],
        out_shape=[shape, shape],
        compiler_params=_cparams(("parallel", "parallel")),
        name="kv_heads",
    )(proj, proj, *tables)


def _gla_kernel(q_ref, k_ref, v_ref, z_ref, sm_ref, wg_ref, bg_ref, nw_ref, s0_ref,
                o_ref, s_ref, g_scr, b_scr, k_scr, *, t, c, nch, t_valid):
    dk = GLA_DK
    sup = c * nch
    x = _dot(sm_ref[0], wg_ref[...], precision=HIGHEST) + bg_ref[...]
    g_all = _log_sigmoid(x) * (1.0 / GLA_GATE_NORM)
    if t_valid < t:
        rows_t = lax.broadcasted_iota(jnp.int32, (t, LANES), 0)
        g_all = jnp.where(rows_t < t_valid, g_all, 0.0)
    g_scr[...] = g_all

    lane = lax.broadcasted_iota(jnp.int32, (sup, LANES), 1)
    row = lax.broadcasted_iota(jnp.int32, (sup, LANES), 0)
    head0 = lane < dk
    rmod = row & (c - 1)
    rr = lax.broadcasted_iota(jnp.int32, (sup, sup), 0)
    cc = lax.broadcasted_iota(jnp.int32, (sup, sup), 1)
    cdiff = cc - (rr - (rr & (c - 1)))
    tril_bd = jnp.where((cdiff >= 0) & (cdiff <= (rr & (c - 1))), 1.0, 0.0).astype(F32)
    nw = nw_ref[...]

    def chunk_rows(ref, off):
        return jnp.concatenate(
            [jnp.broadcast_to(ref[ch * c + off:ch * c + off + 1, :], (c, LANES)) for ch in range(nch)], axis=0)

    def body(si, s2t):
        r0 = pl.multiple_of(si * sup, sup)
        q = q_ref[0, pl.ds(r0, sup), :] * (dk ** -0.5)
        k = k_ref[0, pl.ds(r0, sup), :]
        if t_valid < t:
            k = jnp.where(row < t_valid, k, 0.0)
        g = g_scr[pl.ds(r0, sup), :]
        v = v_ref[0, pl.ds(r0, sup), :]
        z = z_ref[0, pl.ds(r0, sup), :]
        b = _dot(tril_bd, g, precision=HIGHEST)
        b_scr[...] = b
        k_scr[...] = k

        att0 = jnp.zeros((sup, sup), F32)
        att1 = jnp.zeros((sup, sup), F32)
        for s in range(c):
            dec = jnp.exp(jnp.where(rmod >= s, b - chunk_rows(b_scr, s), NEG))
            tile = q * dec * chunk_rows(k_scr, s)
            col0 = jnp.sum(jnp.where(head0, tile, 0.0), axis=-1, keepdims=True)
            col1 = jnp.sum(jnp.where(head0, 0.0, tile), axis=-1, keepdims=True)
            hit = cdiff == s
            att0 = jnp.where(hit, col0, att0)
            att1 = jnp.where(hit, col1, att1)

        kt = k * jnp.exp(chunk_rows(b_scr, c - 1) - b)
        qe = q * jnp.exp(b)
        qe0, qe1 = jnp.where(head0, qe, 0.0), jnp.where(head0, 0.0, qe)
        kt0, kt1 = jnp.where(head0, kt, 0.0), jnp.where(head0, 0.0, kt)
        v0, v1 = v[:, :GLA_DV], v[:, GLA_DV:]
        rows = [slice(ch * c, (ch + 1) * c) for ch in range(nch)]
        upds = [_dot_tn(jnp.concatenate([v0[r], v1[r]], axis=0).astype(BF16),
                        jnp.concatenate([kt0[r], kt1[r]], axis=0).astype(BF16)) for r in rows]
        o_int = []
        for ch, r in enumerate(rows):
            lhs = jnp.concatenate([qe0[r], qe1[r]], axis=0).astype(BF16)
            o_int.append(_dot_nt(lhs, s2t.astype(BF16)))
            s2t = jnp.exp(b_scr[ch * c + c - 1:ch * c + c, :]) * s2t + upds[ch]

        for h, (att, vh) in enumerate(((att0, v0), (att1, v1))):
            o = (jnp.concatenate([oi[h * c:(h + 1) * c, :] for oi in o_int], axis=0)
                 + _dot(att.astype(BF16), vh.astype(BF16)))
            on = o * lax.rsqrt(jnp.mean(o * o, axis=-1, keepdims=True) + EPS) * nw
            zh = z[:, h * GLA_DV:(h + 1) * GLA_DV]
            o_ref[0, pl.ds(r0, sup), h * GLA_DV:(h + 1) * GLA_DV] = (on * _silu(zh)).astype(o_ref.dtype)
        return s2t

    s_ref[0, 0] = lax.fori_loop(0, t // sup, body, s0_ref[0, 0])


def _gla(proj, small, wg_pad, bg, nw, s0t, t_valid, out_dtype):
    bsz, t, _ = proj.shape
    c = min(16, t)
    nch = _pick(t // c, (8, 4, 2, 1))
    pairs = GLA_HEADS // 2
    qb, kb = _MAIN_OFF["g_q"] // LANES, _MAIN_OFF["g_k"] // LANES
    vb, zb = _MAIN_OFF["g_v"] // (2 * GLA_DV), _MAIN_OFF["g_z"] // (2 * GLA_DV)
    kern = functools.partial(_gla_kernel, t=t, c=c, nch=nch, t_valid=t_valid)
    return pl.pallas_call(
        kern,
        grid=(bsz, pairs),
        in_specs=[pl.BlockSpec((1, t, LANES), lambda b, p: (b, 0, qb + p)),
                  pl.BlockSpec((1, t, LANES), lambda b, p: (b, 0, kb + p)),
                  pl.BlockSpec((1, t, 2 * GLA_DV), lambda b, p: (b, 0, vb + p)),
                  pl.BlockSpec((1, t, 2 * GLA_DV), lambda b, p: (b, 0, zb + p)),
                  pl.BlockSpec((1, t, LANES), lambda b, p: (b, 0, 0)),
                  pl.BlockSpec((LANES, LANES), lambda b, p: (0, p)),
                  pl.BlockSpec((1, LANES), lambda b, p: (0, p)),
                  pl.BlockSpec((1, GLA_DV), lambda b, p: (0, 0)),
                  pl.BlockSpec((1, 1, GLA_DV, LANES), lambda b, p: (b, p, 0, 0))],
        out_specs=[pl.BlockSpec((1, t, 2 * GLA_DV), lambda b, p: (b, 0, p)),
                   pl.BlockSpec((1, 1, GLA_DV, LANES), lambda b, p: (b, p, 0, 0))],
        out_shape=[jax.ShapeDtypeStruct((bsz, t, GROUP_W), out_dtype),
                   jax.ShapeDtypeStruct((bsz, pairs, GLA_DV, LANES), F32)],
        scratch_shapes=[pltpu.VMEM((t, LANES), F32), pltpu.VMEM((c * nch, LANES), F32),
                        pltpu.VMEM((c * nch, LANES), F32)],
        compiler_params=_cparams(("parallel", "parallel")),
        name="gla",
    )(proj, proj, proj, proj, small, wg_pad, bg.reshape(1, -1), nw.reshape(1, -1), s0t)


def _gla_state_to_pairs(s):
    bsz = s.shape[0]
    s = s.reshape(bsz, GLA_HEADS // 2, 2, GLA_DK, GLA_DV)
    return s.transpose(0, 1, 4, 2, 3).reshape(bsz, GLA_HEADS // 2, GLA_DV, 2 * GLA_DK)


def _gla_state_from_pairs(s):
    bsz = s.shape[0]
    s = s.reshape(bsz, GLA_HEADS // 2, GLA_DV, 2, GLA_DK)
    return s.transpose(0, 1, 3, 4, 2).reshape(bsz, GLA_HEADS, GLA_DK, GLA_DV)


def _gates_kernel(sm_ref, bias_ref, o_ref, *, t_valid):
    x = sm_ref[0] + bias_ref[...]
    lane = lax.broadcasted_iota(jnp.int32, x.shape, 1)
    is_f = (lane >= SM_F) & (lane < SM_F + ML_HEADS)
    out = jnp.where(is_f, _log_sigmoid(x), x)
    if t_valid is not None:
        row = lax.broadcasted_iota(jnp.int32, x.shape, 0)
        out = jnp.where(row < t_valid, out, jnp.where(is_f, 0.0, NEG))
    o_ref[0] = out


def _gates(small, bias_row, t_valid):
    bsz, t, _ = small.shape
    bt = t if t_valid < t else _pick(t, (1024, 512, 256, 128, 64, 32, 16, 8))
    return pl.pallas_call(
        functools.partial(_gates_kernel, t_valid=t_valid if t_valid < t else None),
        grid=(bsz, t // bt),
        in_specs=[pl.BlockSpec((1, bt, LANES), lambda b, i: (b, i, 0)),
                  pl.BlockSpec((1, LANES), lambda b, i: (0, 0))],
        out_specs=pl.BlockSpec((1, bt, LANES), lambda b, i: (b, i, 0)),
        out_shape=jax.ShapeDtypeStruct((bsz, t, LANES), F32),
        compiler_params=_cparams(("parallel", "parallel")),
        name="ml_gates",
    )(small, bias_row)


ML_PAIR = 2


def _mlstm_kernel(q_ref, k_ref, v_ref, og_ref, z_ref, gt_ref, nw_ref,
                  c0_ref, n0_ref, m0_ref, o_ref, c_ref, n_ref, m_ref, t_scr, *, t, c):
    h0 = pl.program_id(1) * ML_PAIR
    hs = range(ML_PAIR)
    lane = lax.broadcasted_iota(jnp.int32, (c, LANES), 1)
    rr = lax.broadcasted_iota(jnp.int32, (c, c), 0)
    cc = lax.broadcasted_iota(jnp.int32, (c, c), 1)
    causal = rr >= cc
    nw = nw_ref[...]
    hsl = lambda j: slice(j * ML_DH, (j + 1) * ML_DH)

    def body(ci, carry):
        cms, ns, m_prevs = carry
        r0 = pl.multiple_of(ci * c, c)
        gt = gt_ref[0, pl.ds(r0, c), :]
        if c == LANES:
            t_scr[...] = gt.T
        qs = [q_ref[0, pl.ds(r0, c), hsl(j)] * (ML_DH ** -0.5) for j in hs]
        ks = [k_ref[0, pl.ds(r0, c), hsl(j)] for j in hs]
        vs = [v_ref[0, pl.ds(r0, c), hsl(j)] for j in hs]
        qbs = [q.astype(BF16) for q in qs]
        kbs = [k.astype(BF16) for k in ks]
        s_qk = [_dot_nt(qbs[j], kbs[j]) for j in hs]
        s_qc = [_dot_nt(qbs[j], cms[j].astype(BF16)) for j in hs]

        d, inter, m_t, i_cols, fc_cols = [], [], [], [], []
        for j in hs:
            i_col = jnp.sum(jnp.where(lane == SM_I + h0 + j, gt, 0.0), axis=-1, keepdims=True)
            f_col = jnp.sum(jnp.where(lane == SM_F + h0 + j, gt, 0.0), axis=-1, keepdims=True)
            if c == LANES:
                i_row = t_scr[pl.ds(SM_I + h0 + j, 1), :]
                f_row = t_scr[pl.ds(SM_F + h0 + j, 1), :]
            else:
                i_row = jnp.sum(jnp.where(rr == cc, i_col, 0.0), axis=0, keepdims=True)
                f_row = jnp.sum(jnp.where(rr == cc, f_col, 0.0), axis=0, keepdims=True)
            fc_col = jnp.sum(jnp.where(causal, f_row, 0.0), axis=-1, keepdims=True)
            fc_row = jnp.sum(jnp.where(rr <= cc, f_col, 0.0), axis=0, keepdims=True)
            dj = jnp.where(causal, fc_col - fc_row + i_row, NEG)
            d.append(dj)
            inter.append(fc_col + m_prevs[j])
            m_t.append(jnp.maximum(inter[j], jnp.max(dj, axis=-1, keepdims=True)))
            i_cols.append(i_col)
            fc_cols.append(fc_col)

        w_state = [jnp.exp(inter[j] - m_t[j]) for j in hs]
        qk = [s_qk[j] * jnp.exp(d[j] - m_t[j]) for j in hs]
        pv = [_dot(qk[j].astype(BF16), vs[j].astype(BF16)) for j in hs]
        w_end, dec, m_new = [], [], []
        for j in hs:
            m_new.append(m_t[j][c - 1:c, :])
            f_end = fc_cols[j][c - 1:c, :]
            w_end.append(jnp.exp(f_end - fc_cols[j] + i_cols[j] - m_new[j]))
            dec.append(jnp.exp(f_end + m_prevs[j] - m_new[j]))
        upd = [_dot_tn((vs[j] * w_end[j]).astype(BF16), kbs[j]) for j in hs]

        c_new, n_new = [], []
        for j in hs:
            num = w_state[j] * s_qc[j] + pv[j]
            den = (w_state[j] * jnp.sum(qs[j] * ns[j], axis=-1, keepdims=True)
                   + jnp.sum(qk[j], axis=-1, keepdims=True))
            hout = num / jnp.maximum(jnp.abs(den), jnp.exp(-m_t[j]))
            c_new.append(dec[j] * cms[j] + upd[j])
            n_new.append(dec[j] * ns[j] + jnp.sum(ks[j] * w_end[j], axis=0, keepdims=True))
            o = _sigmoid(og_ref[0, pl.ds(r0, c), hsl(j)]) * hout
            on = o * lax.rsqrt(jnp.mean(o * o, axis=-1, keepdims=True) + EPS) * nw
            o_ref[0, pl.ds(r0, c), hsl(j)] = (on * _silu(z_ref[0, pl.ds(r0, c), hsl(j)])).astype(o_ref.dtype)
        return tuple(c_new), tuple(n_new), tuple(m_new)

    init = (tuple(c0_ref[0, j] for j in hs), tuple(n0_ref[0, j] for j in hs),
            tuple(m0_ref[0, j][:, :1] for j in hs))
    cms, ns, ms = lax.fori_loop(0, t // c, body, init)
    for j in hs:
        c_ref[0, j] = cms[j]
        n_ref[0, j] = ns[j]
        m_ref[0, j] = jnp.broadcast_to(ms[j], (1, LANES))


def _mlstm(proj, gates, nw, c0, n0, m0, out_dtype):
    bsz, t, _ = proj.shape
    c = min(128, t)
    w = ML_PAIR * ML_DH
    col = lambda name: _MAIN_OFF[name] // w
    spec = lambda name: pl.BlockSpec((1, t, w), lambda b, p, o=col(name): (b, 0, o + p))
    st = lambda rows: pl.BlockSpec((1, ML_PAIR, rows, ML_DH), lambda b, p: (b, p, 0, 0))
    return pl.pallas_call(
        functools.partial(_mlstm_kernel, t=t, c=c),
        grid=(bsz, ML_HEADS // ML_PAIR),
        in_specs=[spec("m_q"), spec("m_k"), spec("m_v"), spec("m_o"), spec("m_z"),
                  pl.BlockSpec((1, t, LANES), lambda b, p: (b, 0, 0)),
                  pl.BlockSpec((1, ML_DH), lambda b, p: (0, 0)),
                  st(ML_DH), st(1), st(1)],
        out_specs=[pl.BlockSpec((1, t, w), lambda b, p: (b, 0, p)), st(ML_DH), st(1), st(1)],
        out_shape=[jax.ShapeDtypeStruct((bsz, t, GROUP_W), out_dtype),
                   jax.ShapeDtypeStruct((bsz, ML_HEADS, ML_DH, ML_DH), F32),
                   jax.ShapeDtypeStruct((bsz, ML_HEADS, 1, ML_DH), F32),
                   jax.ShapeDtypeStruct((bsz, ML_HEADS, 1, LANES), F32)],
        scratch_shapes=[pltpu.VMEM((LANES, LANES), F32)],
        compiler_params=_cparams(("parallel", "parallel")),
        name="mlstm",
    )(proj, proj, proj, proj, proj, gates, nw.reshape(1, -1), c0, n0, m0)


def _gmlp_kernel(u_ref, v_ref, z_ref, lw_ref, lb_ref, ws_ref, bs_ref, o_ref, vn_ref, *, l):
    gv = _gelu(v_ref[0])
    mu = jnp.mean(gv, axis=-1, keepdims=True)
    xc = gv - mu
    vn = xc * lax.rsqrt(jnp.mean(xc * xc, axis=-1, keepdims=True) + EPS) * lw_ref[...] + lb_ref[...]
    vn_ref[0] = vn
    rr = lax.broadcasted_iota(jnp.int32, (l, l), 0)
    cc = lax.broadcasted_iota(jnp.int32, (l, l), 1)
    for g in range(GM_GROUPS):
        sl = slice(g * GM_CH, (g + 1) * GM_CH)
        w = jnp.where(rr >= cc, ws_ref[g], 0.0)
        vg = vn[:, sl]
        if l >= GM_CHUNK:
            s = _dot(w.astype(BF16), vg.astype(BF16))
        else:
            s = jnp.zeros((l, GM_CH), F32)
            for r in range(l):
                s = s + w[:, r:r + 1] * vg[r:r + 1, :]
        s = s + bs_ref[:, g:g + 1]
        o_ref[0, :, sl] = (_gelu(u_ref[0, :, sl]) * s * _silu(z_ref[0, :, sl])).astype(o_ref.dtype)


def _gmlp(proj, lw, lb, ws, bs_t, out_dtype):
    bsz, t, _ = proj.shape
    l = min(t, GM_CHUNK)
    col = lambda name: _MAIN_OFF[name] // GROUP_W
    spec = lambda name: pl.BlockSpec((1, l, GROUP_W), lambda b, i, o=col(name): (b, i, o))
    return pl.pallas_call(
        functools.partial(_gmlp_kernel, l=l),
        grid=(bsz, t // l),
        in_specs=[spec("c_u"), spec("c_v"), spec("c_z"),
                  pl.BlockSpec((1, GROUP_W), lambda b, i: (0, 0)),
                  pl.BlockSpec((1, GROUP_W), lambda b, i: (0, 0)),
                  pl.BlockSpec((GM_GROUPS, l, l), lambda b, i: (0, 0, 0)),
                  pl.BlockSpec((l, GM_GROUPS), lambda b, i: (0, 0))],
        out_specs=[pl.BlockSpec((1, l, GROUP_W), lambda b, i: (b, i, 0)),
                   pl.BlockSpec((1, l, GROUP_W), lambda b, i: (b, i, 0))],
        out_shape=[jax.ShapeDtypeStruct((bsz, t, GROUP_W), out_dtype),
                   jax.ShapeDtypeStruct((bsz, t, GROUP_W), F32)],
        compiler_params=_cparams(("parallel", "parallel")),
        name="gmlp",
    )(proj, proj, proj, lw.reshape(1, -1), lb.reshape(1, -1), ws, bs_t)


def _moba_prompt_kernel(q_ref, k_ref, v_ref, z_ref, o_ref, kb_scr, vb_scr, km_scr, *, t):
    blk = MB_BLOCK
    nb = t // blk
    scale = MB_DH ** -0.5
    h = pl.program_id(1)
    km_scr[...] = jnp.zeros((LANES, MB_DH), F32)
    for n in range(nb):
        rows = pl.ds(n * blk * MB_HEADS + h, blk, stride=MB_HEADS)
        kn = k_ref[0, rows, :]
        kb_scr[n * blk:(n + 1) * blk, :] = kn.astype(BF16)
        vb_scr[n * blk:(n + 1) * blk, :] = v_ref[0, rows, :].astype(BF16)
        km_scr[n:n + 1, :] = jnp.mean(kn, axis=0, keepdims=True)
    kmean = km_scr[...]
    lane = lax.broadcasted_iota(jnp.int32, (blk, LANES), 1)
    rr = lax.broadcasted_iota(jnp.int32, (blk, blk), 0)
    cc = lax.broadcasted_iota(jnp.int32, (blk, blk), 1)

    for qi in range(nb):
        rows = slice(qi * blk, (qi + 1) * blk)
        q = q_ref[0, rows, :]
        qb = q.astype(BF16)
        bias = None
        if qi > MB_TOPK:
            gate = _dot_nt(q, kmean, precision=HIGHEST)
            cnt = jnp.zeros((blk, LANES), F32)
            for m in range(qi):
                gm = gate[:, m:m + 1]
                beats = (gm > gate) | ((gm == gate) & (lane > m))
                cnt = cnt + jnp.where(beats, 1.0, 0.0)
            bias = jnp.where(cnt < MB_TOPK, 0.0, NEG)
        ss = []
        for j in range(qi + 1):
            s = _dot_nt(qb, kb_scr[j * blk:(j + 1) * blk, :]) * scale
            if j == qi:
                s = jnp.where(cc <= rr, s, NEG)
            elif bias is not None:
                s = s + bias[:, j:j + 1]
            ss.append(s)
        m_i = jnp.max(ss[0], axis=-1, keepdims=True)
        for s in ss[1:]:
            m_i = jnp.maximum(m_i, jnp.max(s, axis=-1, keepdims=True))
        ps = [jnp.exp(s - m_i) for s in ss]
        l_i = jnp.sum(ps[0], axis=-1, keepdims=True)
        for p in ps[1:]:
            l_i = l_i + jnp.sum(p, axis=-1, keepdims=True)
        acc = _dot(ps[0].astype(BF16), vb_scr[0:blk, :])
        for j in range(1, qi + 1):
            acc = acc + _dot(ps[j].astype(BF16), vb_scr[j * blk:(j + 1) * blk, :])
        o_ref[0, rows, :] = ((acc / l_i) * _silu(z_ref[0, rows, :])).astype(o_ref.dtype)


def _moba_prompt(q_rope, k4, v4, proj, out_dtype):
    bsz, t, _ = proj.shape
    assert t % MB_BLOCK == 0 and t // MB_BLOCK <= LANES
    zcol = _MAIN_OFF["a_z"] // MB_DH
    hd = pl.BlockSpec((1, t, MB_DH), lambda b, h: (b, 0, h))
    kv = pl.BlockSpec((1, t * MB_HEADS, MB_DH), lambda b, h: (b, 0, 0))
    return pl.pallas_call(
        functools.partial(_moba_prompt_kernel, t=t),
        grid=(bsz, MB_HEADS),
        in_specs=[hd, kv, kv, pl.BlockSpec((1, t, MB_DH), lambda b, h: (b, 0, zcol + h))],
        out_specs=hd,
        out_shape=jax.ShapeDtypeStruct((bsz, t, GROUP_W), out_dtype),
        scratch_shapes=[pltpu.VMEM((t, MB_DH), BF16), pltpu.VMEM((t, MB_DH), BF16),
                        pltpu.VMEM((LANES, MB_DH), F32)],
        compiler_params=_cparams(("parallel", "arbitrary")),
        name="moba_prompt",
    )(q_rope, k4, v4, proj)


def _moba_past_kernel(pt_ref, q_ref, k0_ref, k1_ref, v0_ref, v1_ref, o_ref, m_ref, l_ref, g_ref):
    del pt_ref
    scale = MB_DH ** -0.5
    lane = lax.broadcasted_iota(jnp.int32, (SAMPLE_T, LANES), 1)
    heads = range(MB_HEADS)
    page = k0_ref.shape[0] // MB_HEADS
    head_rows = lambda h: pl.ds(h, page, stride=MB_HEADS)
    ksum = (jnp.sum(k0_ref[...].reshape(page, MB_HEADS, MB_DH), axis=0)
            + jnp.sum(k1_ref[...].reshape(page, MB_HEADS, MB_DH), axis=0))
    qs = [q_ref[0, :, h * MB_DH:(h + 1) * MB_DH] for h in heads]
    qbs = [q.astype(BF16) for q in qs]
    s0 = [_dot_nt(qbs[h], k0_ref[head_rows(h), :].astype(BF16)) * scale for h in heads]
    s1 = [_dot_nt(qbs[h], k1_ref[head_rows(h), :].astype(BF16)) * scale for h in heads]
    m_all = jnp.zeros((SAMPLE_T, LANES), F32)
    l_all = jnp.zeros((SAMPLE_T, LANES), F32)
    g_all = jnp.zeros((SAMPLE_T, LANES), F32)
    p0, p1 = [], []
    for h in heads:
        mh = jnp.maximum(jnp.max(s0[h], axis=-1, keepdims=True), jnp.max(s1[h], axis=-1, keepdims=True))
        p0.append(jnp.exp(s0[h] - mh))
        p1.append(jnp.exp(s1[h] - mh))
        lh = jnp.sum(p0[h], axis=-1, keepdims=True) + jnp.sum(p1[h], axis=-1, keepdims=True)
        gate = jnp.sum(qs[h] * ksum[h:h + 1, :], axis=-1, keepdims=True) * (1.0 / MB_BLOCK)
        m_all = jnp.where(lane == h, mh, m_all)
        l_all = jnp.where(lane == h, lh, l_all)
        g_all = jnp.where(lane == h, gate, g_all)
    for h in heads:
        o_ref[h] = (_dot(p0[h].astype(BF16), v0_ref[head_rows(h), :].astype(BF16))
                    + _dot(p1[h].astype(BF16), v1_ref[head_rows(h), :].astype(BF16)))
    m_ref[...] = m_all
    l_ref[...] = l_all
    g_ref[...] = g_all


def _moba_past(layer, q_rope, cache_k, cache_v, page_table):
    db = q_rope.shape[0]
    rows = cache_k.shape[2]
    assert 2 * rows == MB_BLOCK * MB_HEADS
    nb = page_table.shape[1] // 2
    pg = lambda half: pl.BlockSpec((None, None, rows, MB_DH),
                                   lambda b, n, pt, hf=half: (layer, pt[b, 2 * n + hf], 0, 0))
    stat = pl.BlockSpec((None, None, SAMPLE_T, LANES), lambda b, n, pt: (b, n, 0, 0))
    stat_shape = jax.ShapeDtypeStruct((db, nb, SAMPLE_T, LANES), F32)
    grid_spec = pltpu.PrefetchScalarGridSpec(
        num_scalar_prefetch=1,
        grid=(db, nb),
        in_specs=[pl.BlockSpec((1, SAMPLE_T, GROUP_W), lambda b, n, pt: (b, 0, 0)),
                  pg(0), pg(1), pg(0), pg(1)],
        out_specs=[pl.BlockSpec((None, None, MB_HEADS, SAMPLE_T, MB_DH), lambda b, n, pt: (b, n, 0, 0, 0)),
                   stat, stat, stat],
    )
    return pl.pallas_call(
        _moba_past_kernel,
        grid_spec=grid_spec,
        out_shape=[jax.ShapeDtypeStruct((db, nb, MB_HEADS, SAMPLE_T, MB_DH), F32),
                   stat_shape, stat_shape, stat_shape],
        compiler_params=_cparams(("parallel", "parallel")),
        name="moba_past",
    )(page_table, q_rope, cache_k, cache_k, cache_v, cache_v)


def _moba_merge_kernel(op_ref, m_ref, l_ref, g_ref, q_ref, k_ref, v_ref, z_ref, o_ref, *, nb, t_valid):
    scale = MB_DH ** -0.5
    g = g_ref[0]
    m = m_ref[0]
    l = l_ref[0]
    nidx = lax.broadcasted_iota(jnp.int32, g.shape, 0)
    sel = jnp.zeros(g.shape, jnp.bool_)
    gm = g
    for _ in range(min(MB_TOPK, nb)):
        mx = jnp.max(gm, axis=0, keepdims=True)
        first = jnp.min(jnp.where(gm == mx, nidx, nb), axis=0, keepdims=True)
        pick = nidx == first
        sel = sel | pick
        gm = jnp.where(pick, -jnp.inf, gm)

    lane = lax.broadcasted_iota(jnp.int32, (SAMPLE_T, LANES), 1)
    rr = lax.broadcasted_iota(jnp.int32, (SAMPLE_T, SAMPLE_T), 0)
    cc = lax.broadcasted_iota(jnp.int32, (SAMPLE_T, SAMPLE_T), 1)
    own = []
    m_own = jnp.zeros((SAMPLE_T, LANES), F32)
    for h in range(MB_HEADS):
        sl = slice(h * MB_DH, (h + 1) * MB_DH)
        s = _dot_nt(q_ref[0, :, sl].astype(BF16), k_ref[0, :, sl].astype(BF16)) * scale
        s = jnp.where((cc <= rr) & (cc < t_valid), s, NEG)
        mo = jnp.max(s, axis=-1, keepdims=True)
        p = jnp.exp(s - mo)
        own.append((mo, jnp.sum(p, axis=-1, keepdims=True),
                    _dot(p.astype(BF16), v_ref[0, :, sl].astype(BF16))))
        m_own = jnp.where(lane == h, mo, m_own)

    m_tot = jnp.maximum(jnp.max(jnp.where(sel, m, NEG), axis=0), m_own)
    w = jnp.where(sel, jnp.exp(m - m_tot[None]), 0.0)
    l_past = jnp.sum(w * l, axis=0)
    for h in range(MB_HEADS):
        sl = slice(h * MB_DH, (h + 1) * MB_DH)
        mo, lo, oo = own[h]
        w_own = jnp.exp(mo - m_tot[:, h:h + 1])
        acc = w_own * oo
        for n in range(nb):
            acc = acc + w[n][:, h:h + 1] * op_ref[0, n, h]
        den = l_past[:, h:h + 1] + w_own * lo
        o_ref[0, :, sl] = (acc / den) * _silu(z_ref[0, :, sl])


def _moba_merge(o_part, m_all, l_all, g_all, q_rope, k_rope, proj, t_valid):
    db, nb = o_part.shape[:2]
    vcol, zcol = _MAIN_OFF["a_v"] // GROUP_W, _MAIN_OFF["a_z"] // GROUP_W
    stat = pl.BlockSpec((1, nb, SAMPLE_T, LANES), lambda b: (b, 0, 0, 0))
    row = pl.BlockSpec((1, SAMPLE_T, GROUP_W), lambda b: (b, 0, 0))
    return pl.pallas_call(
        functools.partial(_moba_merge_kernel, nb=nb, t_valid=t_valid),
        grid=(db,),
        in_specs=[pl.BlockSpec((1, nb, MB_HEADS, SAMPLE_T, MB_DH), lambda b: (b, 0, 0, 0, 0)),
                  stat, stat, stat, row, row,
                  pl.BlockSpec((1, SAMPLE_T, GROUP_W), lambda b: (b, 0, vcol)),
                  pl.BlockSpec((1, SAMPLE_T, GROUP_W), lambda b: (b, 0, zcol))],
        out_specs=row,
        out_shape=jax.ShapeDtypeStruct((db, SAMPLE_T, GROUP_W), F32),
        compiler_params=_cparams(("parallel",)),
        name="moba_merge",
    )(o_part, m_all, l_all, g_all, q_rope, k_rope, proj, proj)


def _layer_weights(w_in_l, gla_w_gate_l, ml_b_i_l, ml_b_f_l, w_out_l):
    d = w_in_l.shape[0]
    cols = lambda n: w_in_l[:, _SRC_OFF[n]:_SRC_OFF[n] + _SRC_W[n]]
    w_main = jnp.concatenate([cols(n) for n in _MAIN_ORDER], axis=1).astype(BF16)
    pad = LANES - (GLA_GATE_RANK + 2 * ML_HEADS)
    w_small = jnp.concatenate([cols("g_lr"), cols("m_i"), cols("m_f"), jnp.zeros((d, pad), F32)],
                              axis=1).astype(BF16)
    wg_pad = jnp.concatenate([gla_w_gate_l, jnp.zeros((LANES - GLA_GATE_RANK, gla_w_gate_l.shape[1]), F32)],
                             axis=0)
    bias_row = jnp.concatenate([jnp.zeros((SM_I,), F32), ml_b_i_l, ml_b_f_l,
                                jnp.zeros((LANES - SM_F - ML_HEADS,), F32)]).reshape(1, LANES)
    w4 = w_out_l.astype(BF16).reshape(4, GROUP_W, -1)
    return w_main, w_small, wg_pad, bias_row, w4


def _mixers(x, bsz, t, t_valid, lw, rope_tables, gla_s0, ml_state, out_dtype):
    (norm_w, w_main, w_small, wg_pad, bias_row, gla_b_gate, gla_norm_w, ml_norm_w,
     gm_ln_w, gm_ln_b, gm_ws, gm_bs_t) = lw
    h, small = _rms_small(x, norm_w, w_small)
    proj = _matmul(h, w_main).reshape(bsz, t, N_MAIN)
    small = small.reshape(bsz, t, LANES)

    out_a, gla_s = _gla(proj, small, wg_pad, gla_b_gate, gla_norm_w, gla_s0, t_valid, out_dtype)

    gates = _gates(small, bias_row, t_valid)
    out_b, ml_c, ml_n, ml_m = _mlstm(proj, gates, ml_norm_w, *ml_state, out_dtype)

    out_c, vn = _gmlp(proj, gm_ln_w, gm_ln_b, gm_ws, gm_bs_t, out_dtype)

    q_rope = _rope(proj, _MAIN_OFF["a_q"], rope_tables)
    return proj, (out_a, out_b, out_c), q_rope, gla_s, (ml_c, ml_n, ml_m[..., :1]), vn


def kernel(x_prompt, x_sample, cache_k, cache_v, page_table, state_gla, state_mlstm_C, state_mlstm_n,
           state_mlstm_m, norm_w, w_in, gla_w_gate, gla_b_gate, gla_norm_w, ml_b_i, ml_b_f, ml_norm_w,
           gm_ln_w, gm_ln_b, gm_w_s, gm_b_s, w_out, final_norm_w):
    bp, tp, d = x_prompt.shape
    db, ts, _ = x_sample.shape
    depth = w_in.shape[0]
    page = cache_k.shape[2]
    past_len = page_table.shape[1] * page
    assert w_out.shape[1] == 4 * GROUP_W and ts <= SAMPLE_T
    assert past_len % MB_BLOCK == 0 and tp % MB_BLOCK == 0

    tables_p = _rope_tables(jnp.arange(tp, dtype=jnp.int32))
    tables_s = _rope_tables(past_len + jnp.arange(SAMPLE_T, dtype=jnp.int32))
    cache_k = cache_k.reshape(depth, cache_k.shape[1], page * MB_HEADS, MB_DH)
    cache_v = cache_v.reshape(depth, cache_v.shape[1], page * MB_HEADS, MB_DH)

    yp = x_prompt.reshape(bp * tp, d)
    ys = jnp.pad(x_sample, ((0, 0), (0, SAMPLE_T - ts), (0, 0))).reshape(db * SAMPLE_T, d)
    dt_p = BF16
    dt_s = F32

    zero_gla = jnp.zeros((bp, GLA_HEADS // 2, GLA_DV, LANES), F32)
    zero_ml = (jnp.zeros((bp, ML_HEADS, ML_DH, ML_DH), F32), jnp.zeros((bp, ML_HEADS, 1, ML_DH), F32),
               jnp.zeros((bp, ML_HEADS, 1, LANES), F32))

    outs = {n: [] for n in ("kp", "vp", "ks", "vs", "gp", "gs", "cp", "cs", "np", "ns", "mp", "ms", "vv")}
    lp = min(tp, GM_CHUNK)
    for l in range(depth):
        w_main, w_small, wg_pad, bias_row, w4 = _layer_weights(w_in[l], gla_w_gate[l], ml_b_i[l], ml_b_f[l],
                                                               w_out[l])
        common = (norm_w[l], w_main, w_small, wg_pad, bias_row, gla_b_gate[l], gla_norm_w[l], ml_norm_w[l],
                  gm_ln_w[l], gm_ln_b[l])

        lw = common + (gm_w_s[l][:, :lp, :lp], gm_b_s[l][:, :lp].T)
        proj, mix, q_rope, gla_s, ml_s, _ = _mixers(
            yp, bp, tp, tp, lw, tables_p, zero_gla, zero_ml, dt_p)
        k4, v4 = _kv_heads(proj, tables_p)
        out_d = _moba_prompt(q_rope, k4, v4, proj, dt_p)
        yp = _out_proj([a.reshape(bp * tp, GROUP_W) for a in mix + (out_d,)], w4, yp)
        outs["kp"].append(k4.reshape(bp, tp, MB_HEADS, MB_DH))
        outs["vp"].append(v4.reshape(bp, tp, MB_HEADS, MB_DH))
        outs["gp"].append(_gla_state_from_pairs(gla_s))
        outs["cp"].append(ml_s[0])
        outs["np"].append(ml_s[1][:, :, 0, :])
        outs["mp"].append(ml_s[2][:, :, 0, 0])

        lw = common + (gm_w_s[l][:, :SAMPLE_T, :SAMPLE_T], gm_b_s[l][:, :SAMPLE_T].T)
        ml_state = (state_mlstm_C[l], state_mlstm_n[l][:, :, None, :],
                    jnp.broadcast_to(state_mlstm_m[l][:, :, None, None], (db, ML_HEADS, 1, LANES)))
        proj, mix, q_rope, gla_s, ml_s, vn = _mixers(
            ys, db, SAMPLE_T, ts, lw, tables_s, _gla_state_to_pairs(state_gla[l]), ml_state, dt_s)
        k_rope = _rope(proj, _MAIN_OFF["a_k"], tables_s)
        v_new = proj[:, :, _MAIN_OFF["a_v"]:_MAIN_OFF["a_v"] + GROUP_W]
        o_part, m_all, l_all, g_all = _moba_past(l, q_rope, cache_k, cache_v, page_table)
        out_d = _moba_merge(o_part, m_all, l_all, g_all, q_rope, k_rope, proj, ts)
        ys = _out_proj([a.reshape(db * SAMPLE_T, GROUP_W) for a in mix + (out_d,)], w4, ys)
        outs["ks"].append(k_rope[:, :ts].reshape(db, ts, MB_HEADS, MB_DH))
        outs["vs"].append(v_new[:, :ts].reshape(db, ts, MB_HEADS, MB_DH))
        outs["gs"].append(_gla_state_from_pairs(gla_s))
        outs["cs"].append(ml_s[0])
        outs["ns"].append(ml_s[1][:, :, 0, :])
        outs["ms"].append(ml_s[2][:, :, 0, 0])
        outs["vv"].append(vn[:, :ts])

    y_prompt = _rms(yp, final_norm_w).reshape(bp, tp, d)
    y_sample = _rms(ys, final_norm_w).reshape(db, SAMPLE_T, d)[:, :ts]
    st = jnp.stack
    return (y_prompt, y_sample, st(outs["kp"]), st(outs["vp"]), st(outs["ks"]), st(outs["vs"]),
            st(outs["gp"]), st(outs["gs"]), st(outs["cp"]), st(outs["cs"]), st(outs["np"]), st(outs["ns"]),
            st(outs["mp"]), st(outs["ms"]), st(outs["vv"]))
```

```python
import functools

import numpy as np
import jax
import jax.numpy as jnp
from jax import lax
from jax.experimental import pallas as pl
from jax.experimental.pallas import tpu as pltpu

F32 = jnp.float32
BF16 = jnp.bfloat16
HIGHEST = lax.Precision.HIGHEST

GROUP_W = 1024
GLA_HEADS, GLA_DK, GLA_DV = 8, 64, 128
GLA_GATE_RANK, GLA_GATE_NORM = 16, 16.0
ML_HEADS, ML_DH = 8, 128
GM_GROUPS, GM_CH, GM_CHUNK = 8, 128, 128
MB_HEADS, MB_DH, MB_BLOCK, MB_TOPK = 8, 128, 256, 3
ROT_DIM, ROPE_THETA = 32, 500000.0
EPS = 1e-6

LANES = 128
SUBLANES = 8
VMEM_LIMIT = 56 * 1024 * 1024

NEG = -1e30
SAMPLE_T = SUBLANES

_MAIN_ORDER = ("g_q", "g_k", "g_v", "g_z", "m_q", "m_k", "m_v", "m_o", "m_z",
               "c_u", "c_v", "c_z", "a_q", "a_k", "a_v", "a_z")
_SPLIT_NAMES = ("g_q", "g_k", "g_v", "g_lr", "g_z", "m_q", "m_k", "m_v", "m_i", "m_f", "m_o", "m_z",
                "c_u", "c_v", "c_z", "a_q", "a_k", "a_v", "a_z")
_SPLIT_W = (GLA_HEADS * GLA_DK, GLA_HEADS * GLA_DK, GLA_HEADS * GLA_DV, GLA_GATE_RANK, GROUP_W,
            GROUP_W, GROUP_W, GROUP_W, ML_HEADS, ML_HEADS, GROUP_W, GROUP_W,
            GROUP_W, GROUP_W, GROUP_W, GROUP_W, GROUP_W, GROUP_W, GROUP_W)
_SRC_OFF = dict(zip(_SPLIT_NAMES, np.concatenate([[0], np.cumsum(_SPLIT_W)[:-1]]).tolist()))
_SRC_W = dict(zip(_SPLIT_NAMES, _SPLIT_W))
_MAIN_OFF = {}
_off = 0
for _n in _MAIN_ORDER:
    _MAIN_OFF[_n] = _off
    _off += _SRC_W[_n]
N_MAIN = _off
SM_LR, SM_I, SM_F = 0, GLA_GATE_RANK, GLA_GATE_RANK + ML_HEADS


def _cparams(sem):
    return pltpu.CompilerParams(dimension_semantics=sem, vmem_limit_bytes=VMEM_LIMIT)


def _pick(n, cands):
    for c in cands:
        if n % c == 0:
            return c
    return n


def _silu(x):
    return x / (1.0 + jnp.exp(-x))


def _sigmoid(x):
    return 1.0 / (1.0 + jnp.exp(-x))


def _log_sigmoid(x):
    return jnp.minimum(x, 0.0) - jnp.log1p(jnp.exp(-jnp.abs(x)))


def _gelu(x):
    c = np.sqrt(2.0 / np.pi).astype(np.float32)
    return 0.5 * x * (1.0 + jnp.tanh(c * (x + 0.044715 * (x * x * x))))


def _dot_nt(a, b, precision=None):
    return lax.dot_general(a, b, (((1,), (1,)), ((), ())), preferred_element_type=F32, precision=precision)


def _dot_tn(a, b):
    return lax.dot_general(a, b, (((0,), (0,)), ((), ())), preferred_element_type=F32)


def _dot(a, b, precision=None):
    return jnp.dot(a, b, preferred_element_type=F32, precision=precision)


def _rms_small_kernel(x_ref, nw_ref, ws_ref, h_ref, sm_ref):
    x = x_ref[...]
    ms = jnp.mean(x * x, axis=-1, keepdims=True)
    hb = (x * lax.rsqrt(ms + EPS) * nw_ref[...]).astype(BF16)
    h_ref[...] = hb
    sm_ref[...] = _dot(hb, ws_ref[...])


def _rms_small(x, nw, w_small):
    m, d = x.shape
    bm = _pick(m, (256, 128, 64, 32, 16, 8))
    return pl.pallas_call(
        _rms_small_kernel,
        grid=(m // bm,),
        in_specs=[pl.BlockSpec((bm, d), lambda i: (i, 0)),
                  pl.BlockSpec((1, d), lambda i: (0, 0)),
                  pl.BlockSpec((d, LANES), lambda i: (0, 0))],
        out_specs=[pl.BlockSpec((bm, d), lambda i: (i, 0)),
                   pl.BlockSpec((bm, LANES), lambda i: (i, 0))],
        out_shape=[jax.ShapeDtypeStruct((m, d), BF16), jax.ShapeDtypeStruct((m, LANES), F32)],
        compiler_params=_cparams(("parallel",)),
        name="rms_small",
    )(x, nw.reshape(1, d), w_small)


def _rms_kernel(x_ref, nw_ref, o_ref):
    x = x_ref[...]
    ms = jnp.mean(x * x, axis=-1, keepdims=True)
    o_ref[...] = x * lax.rsqrt(ms + EPS) * nw_ref[...]


def _rms(x, nw):
    m, d = x.shape
    bm = _pick(m, (256, 128, 64, 32, 16, 8))
    return pl.pallas_call(
        _rms_kernel,
        grid=(m // bm,),
        in_specs=[pl.BlockSpec((bm, d), lambda i: (i, 0)), pl.BlockSpec((1, d), lambda i: (0, 0))],
        out_specs=pl.BlockSpec((bm, d), lambda i: (i, 0)),
        out_shape=jax.ShapeDtypeStruct((m, d), F32),
        compiler_params=_cparams(("parallel",)),
        name="rms_final",
    )(x, nw.reshape(1, d))


WP_BN = 1024
WP_B1 = _MAIN_OFF["g_z"] // WP_BN
WP_B2 = _MAIN_OFF["m_o"] // WP_BN
WP_S1 = GLA_GATE_RANK
WP_S2 = GLA_GATE_RANK + 2 * ML_HEADS
assert _MAIN_OFF["g_z"] % WP_BN == 0 and _MAIN_OFF["m_o"] % WP_BN == 0 and N_MAIN % WP_BN == 0


def _wprep_kernel(a_ref, t_ref, o_ref):
    j = pl.program_id(2)

    def shifted(s):
        x = jnp.concatenate([a_ref[...], t_ref[...]], axis=1)
        return x[:, s:s + WP_BN].astype(BF16)

    @pl.when(j < WP_B1)
    def _():
        o_ref[...] = a_ref[...].astype(BF16)

    @pl.when((j >= WP_B1) & (j < WP_B2))
    def _():
        o_ref[...] = shifted(WP_S1)

    @pl.when(j >= WP_B2)
    def _():
        o_ref[...] = shifted(WP_S2)


def _wprep(w_in):
    depth, d, _ = w_in.shape
    rb = _pick(d, (512, 256, 128, 64, 32, 16))
    return pl.pallas_call(
        _wprep_kernel,
        grid=(depth, d // rb, N_MAIN // WP_BN),
        in_specs=[pl.BlockSpec((None, rb, WP_BN), lambda l, i, j: (l, i, j)),
                  pl.BlockSpec((None, rb, LANES), lambda l, i, j: (l, i, (j + 1) * (WP_BN // LANES)))],
        out_specs=pl.BlockSpec((None, rb, WP_BN), lambda l, i, j: (l, i, j)),
        out_shape=jax.ShapeDtypeStruct((depth, d, N_MAIN), BF16),
        compiler_params=_cparams(("parallel", "parallel", "parallel")),
        name="w_prep",
    )(w_in, w_in)


def _mm_kernel(a_ref, b_ref, o_ref):
    o_ref[...] = _dot(a_ref[...], b_ref[...])


def _matmul(a, w_all, layer):
    m, k = a.shape
    n = w_all.shape[2]
    bm = _pick(m, (1024, 512, 256, 128, 64))
    bn = _pick(n, (1024, 512, 256, 128))
    return pl.pallas_call(
        _mm_kernel,
        grid=(n // bn, m // bm),
        in_specs=[pl.BlockSpec((bm, k), lambda j, i: (i, 0)),
                  pl.BlockSpec((None, k, bn), lambda j, i: (layer, 0, j))],
        out_specs=pl.BlockSpec((bm, bn), lambda j, i: (i, j)),
        out_shape=jax.ShapeDtypeStruct((m, n), F32),
        compiler_params=_cparams(("parallel", "parallel")),
        name="in_proj",
    )(a, w_all)


def _out_kernel(a0, a1, a2, a3, w_ref, x_ref, o_ref):
    acc = x_ref[...]
    for g, a in enumerate((a0, a1, a2, a3)):
        acc = acc + _dot(a[...].astype(BF16), w_ref[g].astype(BF16))
    o_ref[...] = acc


def _out_proj(mix, w_all, layer, x):
    m, d = x.shape
    bm = _pick(m, (1024, 512, 256, 128, 64))
    bn = _pick(d, (512, 256, 128))
    a_spec = pl.BlockSpec((bm, GROUP_W), lambda i, j: (i, 0))
    return pl.pallas_call(
        _out_kernel,
        grid=(m // bm, d // bn),
        in_specs=[a_spec, a_spec, a_spec, a_spec,
                  pl.BlockSpec((None, 4, GROUP_W, bn), lambda i, j: (layer, 0, 0, j)),
                  pl.BlockSpec((bm, bn), lambda i, j: (i, j))],
        out_specs=pl.BlockSpec((bm, bn), lambda i, j: (i, j)),
        out_shape=jax.ShapeDtypeStruct((m, d), F32),
        compiler_params=_cparams(("parallel", "parallel")),
        name="out_proj",
    )(*mix, w_all, x)


def _rope_kernel(x_ref, a_ref, b_ref, c_ref, o_ref):
    a, b, c = a_ref[...], b_ref[...], c_ref[...]
    for h in range(MB_HEADS):
        sl = slice(h * MB_DH, (h + 1) * MB_DH)
        x = x_ref[0, :, sl]
        o_ref[0, :, sl] = (x * a + pltpu.roll(x, MB_DH - ROT_DIM // 2, 1) * b
                           + pltpu.roll(x, ROT_DIM // 2, 1) * c)


def _rope_tables(pos):
    half = ROT_DIM // 2
    inv_freq = jnp.power(ROPE_THETA, -jnp.arange(0, ROT_DIM, 2, dtype=F32) / ROT_DIM)
    ang = pos.astype(F32)[:, None] * inv_freq[None, :]
    cos, sin = jnp.cos(ang), jnp.sin(ang)
    t = pos.shape[0]
    a = jnp.concatenate([cos, cos, jnp.ones((t, MB_DH - ROT_DIM), F32)], axis=1)
    b = jnp.concatenate([-sin, jnp.zeros((t, MB_DH - half), F32)], axis=1)
    c = jnp.concatenate([jnp.zeros((t, half), F32), sin, jnp.zeros((t, MB_DH - ROT_DIM), F32)], axis=1)
    return a, b, c


def _rope(proj, col, tables):
    bsz, t, _ = proj.shape
    bt = _pick(t, (512, 256, 128, 64, 32, 16, 8))
    tab = pl.BlockSpec((bt, MB_DH), lambda b, i: (i, 0))
    return pl.pallas_call(
        _rope_kernel,
        grid=(bsz, t // bt),
        in_specs=[pl.BlockSpec((1, bt, GROUP_W), lambda b, i: (b, i, col // GROUP_W)), tab, tab, tab],
        out_specs=pl.BlockSpec((1, bt, GROUP_W), lambda b, i: (b, i, 0)),
        out_shape=jax.ShapeDtypeStruct((bsz, t, GROUP_W), F32),
        compiler_params=_cparams(("parallel", "parallel")),
        name="rope",
    )(proj, *tables)


def _kv_heads_kernel(k_ref, v_ref, a_ref, b_ref, c_ref, *refs):
    k4_ref, v4_ref = refs[-2:]
    a, b, c = a_ref[...], b_ref[...], c_ref[...]
    bt = a.shape[0]
    for h in range(MB_HEADS):
        sl = slice(h * MB_DH, (h + 1) * MB_DH)
        x = k_ref[0, :, sl]
        rows = pl.ds(h, bt, stride=MB_HEADS)
        k4_ref[rows, :] = (x * a + pltpu.roll(x, MB_DH - ROT_DIM // 2, 1) * b
                           + pltpu.roll(x, ROT_DIM // 2, 1) * c)
        v4_ref[rows, :] = v_ref[0, :, sl]


def _kv_heads(proj, tables, layer, depth, stacks):
    bsz, t, _ = proj.shape
    bt = _pick(t, (256, 128, 64, 32, 16, 8))
    tab = pl.BlockSpec((bt, MB_DH), lambda b, i: (i, 0))
    kcol, vcol = _MAIN_OFF["a_k"] // GROUP_W, _MAIN_OFF["a_v"] // GROUP_W
    out = pl.BlockSpec((None, None, bt * MB_HEADS, MB_DH), lambda b, i: (layer, b, i, 0))
    shape = jax.ShapeDtypeStruct((depth, bsz, t * MB_HEADS, MB_DH), F32)
    in_specs = [pl.BlockSpec((1, bt, GROUP_W), lambda b, i: (b, i, kcol)),
                pl.BlockSpec((1, bt, GROUP_W), lambda b, i: (b, i, vcol)), tab, tab, tab]
    aliases = {}
    if stacks is not None:
        in_specs += [pl.BlockSpec(memory_space=pl.ANY), pl.BlockSpec(memory_space=pl.ANY)]
        aliases = {5: 0, 6: 1}
    return pl.pallas_call(
        _kv_heads_kernel,
        grid=(bsz, t // bt),
        in_specs=in_specs,
        out_specs=[out, out],
        out_shape=[shape, shape],
        input_output_aliases=aliases,
        compiler_params=_cparams(("parallel", "parallel")),
        name="kv_heads",
    )(proj, proj, *tables, *(stacks or ()))


def _gla_kernel(q_ref, k_ref, v_ref, z_ref, sm_ref, wg_ref, bg_ref, nw_ref, s0_ref,
                o_ref, s_ref, g_scr, b_scr, k_scr, *, t, c, nch, t_valid):
    dk = GLA_DK
    sup = c * nch
    x = _dot(sm_ref[0], wg_ref[...], precision=HIGHEST) + bg_ref[...]
    g_all = _log_sigmoid(x) * (1.0 / GLA_GATE_NORM)
    if t_valid < t:
        rows_t = lax.broadcasted_iota(jnp.int32, (t, LANES), 0)
        g_all = jnp.where(rows_t < t_valid, g_all, 0.0)
    g_scr[...] = g_all

    lane = lax.broadcasted_iota(jnp.int32, (sup, LANES), 1)
    row = lax.broadcasted_iota(jnp.int32, (sup, LANES), 0)
    head0 = lane < dk
    rmod = row & (c - 1)
    rr = lax.broadcasted_iota(jnp.int32, (sup, sup), 0)
    cc = lax.broadcasted_iota(jnp.int32, (sup, sup), 1)
    cdiff = cc - (rr - (rr & (c - 1)))
    tril_bd = jnp.where((cdiff >= 0) & (cdiff <= (rr & (c - 1))), 1.0, 0.0).astype(F32)
    nw = nw_ref[...]

    def chunk_rows(ref, off):
        return jnp.concatenate(
            [jnp.broadcast_to(ref[ch * c + off:ch * c + off + 1, :], (c, LANES)) for ch in range(nch)], axis=0)

    def body(si, s2t):
        r0 = pl.multiple_of(si * sup, sup)
        q = q_ref[0, pl.ds(r0, sup), :] * (dk ** -0.5)
        k = k_ref[0, pl.ds(r0, sup), :]
        if t_valid < t:
            k = jnp.where(row < t_valid, k, 0.0)
        g = g_scr[pl.ds(r0, sup), :]
        v = v_ref[0, pl.ds(r0, sup), :]
        z = z_ref[0, pl.ds(r0, sup), :]
        b = _dot(tril_bd, g, precision=HIGHEST)
        b_scr[...] = b
        k_scr[...] = k

        att0 = jnp.zeros((sup, sup), F32)
        att1 = jnp.zeros((sup, sup), F32)
        for s in range(c):
            dec = jnp.exp(jnp.where(rmod >= s, b - chunk_rows(b_scr, s), NEG))
            tile = q * dec * chunk_rows(k_scr, s)
            col0 = jnp.sum(jnp.where(head0, tile, 0.0), axis=-1, keepdims=True)
            col1 = jnp.sum(jnp.where(head0, 0.0, tile), axis=-1, keepdims=True)
            hit = cdiff == s
            att0 = jnp.where(hit, col0, att0)
            att1 = jnp.where(hit, col1, att1)

        kt = k * jnp.exp(chunk_rows(b_scr, c - 1) - b)
        qe = q * jnp.exp(b)
        qe0, qe1 = jnp.where(head0, qe, 0.0), jnp.where(head0, 0.0, qe)
        kt0, kt1 = jnp.where(head0, kt, 0.0), jnp.where(head0, 0.0, kt)
        v0, v1 = v[:, :GLA_DV], v[:, GLA_DV:]
        rows = [slice(ch * c, (ch + 1) * c) for ch in range(nch)]
        upds = [_dot_tn(jnp.concatenate([v0[r], v1[r]], axis=0).astype(BF16),
                        jnp.concatenate([kt0[r], kt1[r]], axis=0).astype(BF16)) for r in rows]
        o_int = []
        for ch, r in enumerate(rows):
            lhs = jnp.concatenate([qe0[r], qe1[r]], axis=0).astype(BF16)
            o_int.append(_dot_nt(lhs, s2t.astype(BF16)))
            s2t = jnp.exp(b_scr[ch * c + c - 1:ch * c + c, :]) * s2t + upds[ch]

        for h, (att, vh) in enumerate(((att0, v0), (att1, v1))):
            o = (jnp.concatenate([oi[h * c:(h + 1) * c, :] for oi in o_int], axis=0)
                 + _dot(att.astype(BF16), vh.astype(BF16)))
            on = o * lax.rsqrt(jnp.mean(o * o, axis=-1, keepdims=True) + EPS) * nw
            zh = z[:, h * GLA_DV:(h + 1) * GLA_DV]
            o_ref[0, pl.ds(r0, sup), h * GLA_DV:(h + 1) * GLA_DV] = (on * _silu(zh)).astype(o_ref.dtype)
        return s2t

    s_ref[0, 0] = lax.fori_loop(0, t // sup, body, s0_ref[0, 0])


def _gla(proj, small, wg_pad, bg, nw, s0t, t_valid, out_dtype):
    bsz, t, _ = proj.shape
    c = min(16, t)
    nch = _pick(t // c, (8, 4, 2, 1))
    pairs = GLA_HEADS // 2
    qb, kb = _MAIN_OFF["g_q"] // LANES, _MAIN_OFF["g_k"] // LANES
    vb, zb = _MAIN_OFF["g_v"] // (2 * GLA_DV), _MAIN_OFF["g_z"] // (2 * GLA_DV)
    kern = functools.partial(_gla_kernel, t=t, c=c, nch=nch, t_valid=t_valid)
    return pl.pallas_call(
        kern,
        grid=(bsz, pairs),
        in_specs=[pl.BlockSpec((1, t, LANES), lambda b, p: (b, 0, qb + p)),
                  pl.BlockSpec((1, t, LANES), lambda b, p: (b, 0, kb + p)),
                  pl.BlockSpec((1, t, 2 * GLA_DV), lambda b, p: (b, 0, vb + p)),
                  pl.BlockSpec((1, t, 2 * GLA_DV), lambda b, p: (b, 0, zb + p)),
                  pl.BlockSpec((1, t, LANES), lambda b, p: (b, 0, 0)),
                  pl.BlockSpec((LANES, LANES), lambda b, p: (0, p)),
                  pl.BlockSpec((1, LANES), lambda b, p: (0, p)),
                  pl.BlockSpec((1, GLA_DV), lambda b, p: (0, 0)),
                  pl.BlockSpec((1, 1, GLA_DV, LANES), lambda b, p: (b, p, 0, 0))],
        out_specs=[pl.BlockSpec((1, t, 2 * GLA_DV), lambda b, p: (b, 0, p)),
                   pl.BlockSpec((1, 1, GLA_DV, LANES), lambda b, p: (b, p, 0, 0))],
        out_shape=[jax.ShapeDtypeStruct((bsz, t, GROUP_W), out_dtype),
                   jax.ShapeDtypeStruct((bsz, pairs, GLA_DV, LANES), F32)],
        scratch_shapes=[pltpu.VMEM((t, LANES), F32), pltpu.VMEM((c * nch, LANES), F32),
                        pltpu.VMEM((c * nch, LANES), F32)],
        compiler_params=_cparams(("parallel", "parallel")),
        name="gla",
    )(proj, proj, proj, proj, small, wg_pad, bg.reshape(1, -1), nw.reshape(1, -1), s0t)


def _gla_state_to_pairs(s):
    bsz = s.shape[0]
    s = s.reshape(bsz, GLA_HEADS // 2, 2, GLA_DK, GLA_DV)
    return s.transpose(0, 1, 4, 2, 3).reshape(bsz, GLA_HEADS // 2, GLA_DV, 2 * GLA_DK)


def _gla_state_from_pairs(s):
    bsz = s.shape[0]
    s = s.reshape(bsz, GLA_HEADS // 2, GLA_DV, 2, GLA_DK)
    return s.transpose(0, 1, 3, 4, 2).reshape(bsz, GLA_HEADS, GLA_DK, GLA_DV)


def _gates_kernel(sm_ref, bias_ref, o_ref, *, t_valid):
    x = sm_ref[0] + bias_ref[...]
    lane = lax.broadcasted_iota(jnp.int32, x.shape, 1)
    is_f = (lane >= SM_F) & (lane < SM_F + ML_HEADS)
    out = jnp.where(is_f, _log_sigmoid(x), x)
    if t_valid is not None:
        row = lax.broadcasted_iota(jnp.int32, x.shape, 0)
        out = jnp.where(row < t_valid, out, jnp.where(is_f, 0.0, NEG))
    o_ref[0] = out


def _gates(small, bias_row, t_valid):
    bsz, t, _ = small.shape
    bt = t if t_valid < t else _pick(t, (1024, 512, 256, 128, 64, 32, 16, 8))
    return pl.pallas_call(
        functools.partial(_gates_kernel, t_valid=t_valid if t_valid < t else None),
        grid=(bsz, t // bt),
        in_specs=[pl.BlockSpec((1, bt, LANES), lambda b, i: (b, i, 0)),
                  pl.BlockSpec((1, LANES), lambda b, i: (0, 0))],
        out_specs=pl.BlockSpec((1, bt, LANES), lambda b, i: (b, i, 0)),
        out_shape=jax.ShapeDtypeStruct((bsz, t, LANES), F32),
        compiler_params=_cparams(("parallel", "parallel")),
        name="ml_gates",
    )(small, bias_row)


ML_PAIR = 2


def _mlstm_kernel(q_ref, k_ref, v_ref, og_ref, z_ref, gt_ref, nw_ref,
                  c0_ref, n0_ref, m0_ref, o_ref, c_ref, n_ref, m_ref, t_scr, *, t, c):
    h0 = pl.program_id(1) * ML_PAIR
    hs = range(ML_PAIR)
    lane = lax.broadcasted_iota(jnp.int32, (c, LANES), 1)
    rr = lax.broadcasted_iota(jnp.int32, (c, c), 0)
    cc = lax.broadcasted_iota(jnp.int32, (c, c), 1)
    causal = rr >= cc
    nw = nw_ref[...]
    hsl = lambda j: slice(j * ML_DH, (j + 1) * ML_DH)

    def body(ci, carry):
        cms, ns, m_prevs = carry
        r0 = pl.multiple_of(ci * c, c)
        gt = gt_ref[0, pl.ds(r0, c), :]
        if c == LANES:
            t_scr[...] = gt.T
        qs = [q_ref[0, pl.ds(r0, c), hsl(j)] * (ML_DH ** -0.5) for j in hs]
        ks = [k_ref[0, pl.ds(r0, c), hsl(j)] for j in hs]
        vs = [v_ref[0, pl.ds(r0, c), hsl(j)] for j in hs]
        qbs = [q.astype(BF16) for q in qs]
        kbs = [k.astype(BF16) for k in ks]
        s_qk = [_dot_nt(qbs[j], kbs[j]) for j in hs]
        s_qc = [_dot_nt(qbs[j], cms[j].astype(BF16)) for j in hs]

        d, inter, m_t, i_cols, fc_cols = [], [], [], [], []
        for j in hs:
            i_col = jnp.sum(jnp.where(lane == SM_I + h0 + j, gt, 0.0), axis=-1, keepdims=True)
            f_col = jnp.sum(jnp.where(lane == SM_F + h0 + j, gt, 0.0), axis=-1, keepdims=True)
            if c == LANES:
                i_row = t_scr[pl.ds(SM_I + h0 + j, 1), :]
                f_row = t_scr[pl.ds(SM_F + h0 + j, 1), :]
            else:
                i_row = jnp.sum(jnp.where(rr == cc, i_col, 0.0), axis=0, keepdims=True)
                f_row = jnp.sum(jnp.where(rr == cc, f_col, 0.0), axis=0, keepdims=True)
            fc_col = jnp.sum(jnp.where(causal, f_row, 0.0), axis=-1, keepdims=True)
            fc_row = jnp.sum(jnp.where(rr <= cc, f_col, 0.0), axis=0, keepdims=True)
            dj = jnp.where(causal, fc_col - fc_row + i_row, NEG)
            d.append(dj)
            inter.append(fc_col + m_prevs[j])
            m_t.append(jnp.maximum(inter[j], jnp.max(dj, axis=-1, keepdims=True)))
            i_cols.append(i_col)
            fc_cols.append(fc_col)

        w_state = [jnp.exp(inter[j] - m_t[j]) for j in hs]
        qk = [s_qk[j] * jnp.exp(d[j] - m_t[j]) for j in hs]
        pv = [_dot(qk[j].astype(BF16), vs[j].astype(BF16)) for j in hs]
        w_end, dec, m_new = [], [], []
        for j in hs:
            m_new.append(m_t[j][c - 1:c, :])
            f_end = fc_cols[j][c - 1:c, :]
            w_end.append(jnp.exp(f_end - fc_cols[j] + i_cols[j] - m_new[j]))
            dec.append(jnp.exp(f_end + m_prevs[j] - m_new[j]))
        upd = [_dot_tn((vs[j] * w_end[j]).astype(BF16), kbs[j]) for j in hs]

        c_new, n_new = [], []
        for j in hs:
            num = w_state[j] * s_qc[j] + pv[j]
            den = (w_state[j] * jnp.sum(qs[j] * ns[j], axis=-1, keepdims=True)
                   + jnp.sum(qk[j], axis=-1, keepdims=True))
            hout = num / jnp.maximum(jnp.abs(den), jnp.exp(-m_t[j]))
            c_new.append(dec[j] * cms[j] + upd[j])
            n_new.append(dec[j] * ns[j] + jnp.sum(ks[j] * w_end[j], axis=0, keepdims=True))
            o = _sigmoid(og_ref[0, pl.ds(r0, c), hsl(j)]) * hout
            on = o * lax.rsqrt(jnp.mean(o * o, axis=-1, keepdims=True) + EPS) * nw
            o_ref[0, pl.ds(r0, c), hsl(j)] = (on * _silu(z_ref[0, pl.ds(r0, c), hsl(j)])).astype(o_ref.dtype)
        return tuple(c_new), tuple(n_new), tuple(m_new)

    init = (tuple(c0_ref[0, j] for j in hs), tuple(n0_ref[0, j] for j in hs),
            tuple(m0_ref[0, j][:, :1] for j in hs))
    cms, ns, ms = lax.fori_loop(0, t // c, body, init)
    for j in hs:
        c_ref[0, j] = cms[j]
        n_ref[0, j] = ns[j]
        m_ref[0, j] = jnp.broadcast_to(ms[j], (1, LANES))


def _mlstm(proj, gates, nw, c0, n0, m0, out_dtype):
    bsz, t, _ = proj.shape
    c = min(128, t)
    w = ML_PAIR * ML_DH
    col = lambda name: _MAIN_OFF[name] // w
    spec = lambda name: pl.BlockSpec((1, t, w), lambda b, p, o=col(name): (b, 0, o + p))
    st = lambda rows: pl.BlockSpec((1, ML_PAIR, rows, ML_DH), lambda b, p: (b, p, 0, 0))
    return pl.pallas_call(
        functools.partial(_mlstm_kernel, t=t, c=c),
        grid=(bsz, ML_HEADS // ML_PAIR),
        in_specs=[spec("m_q"), spec("m_k"), spec("m_v"), spec("m_o"), spec("m_z"),
                  pl.BlockSpec((1, t, LANES), lambda b, p: (b, 0, 0)),
                  pl.BlockSpec((1, ML_DH), lambda b, p: (0, 0)),
                  st(ML_DH), st(1), st(1)],
        out_specs=[pl.BlockSpec((1, t, w), lambda b, p: (b, 0, p)), st(ML_DH), st(1), st(1)],
        out_shape=[jax.ShapeDtypeStruct((bsz, t, GROUP_W), out_dtype),
                   jax.ShapeDtypeStruct((bsz, ML_HEADS, ML_DH, ML_DH), F32),
                   jax.ShapeDtypeStruct((bsz, ML_HEADS, 1, ML_DH), F32),
                   jax.ShapeDtypeStruct((bsz, ML_HEADS, 1, LANES), F32)],
        scratch_shapes=[pltpu.VMEM((LANES, LANES), F32)],
        compiler_params=_cparams(("parallel", "parallel")),
        name="mlstm",
    )(proj, proj, proj, proj, proj, gates, nw.reshape(1, -1), c0, n0, m0)


def _gmlp_kernel(u_ref, v_ref, z_ref, lw_ref, lb_ref, ws_ref, bs_ref, o_ref, vn_ref, *, l):
    gv = _gelu(v_ref[0])
    mu = jnp.mean(gv, axis=-1, keepdims=True)
    xc = gv - mu
    vn = xc * lax.rsqrt(jnp.mean(xc * xc, axis=-1, keepdims=True) + EPS) * lw_ref[...] + lb_ref[...]
    vn_ref[0] = vn
    rr = lax.broadcasted_iota(jnp.int32, (l, l), 0)
    cc = lax.broadcasted_iota(jnp.int32, (l, l), 1)
    for g in range(GM_GROUPS):
        sl = slice(g * GM_CH, (g + 1) * GM_CH)
        w = jnp.where(rr >= cc, ws_ref[g], 0.0)
        vg = vn[:, sl]
        if l >= GM_CHUNK:
            s = _dot(w.astype(BF16), vg.astype(BF16))
        else:
            s = jnp.zeros((l, GM_CH), F32)
            for r in range(l):
                s = s + w[:, r:r + 1] * vg[r:r + 1, :]
        s = s + bs_ref[:, g:g + 1]
        o_ref[0, :, sl] = (_gelu(u_ref[0, :, sl]) * s * _silu(z_ref[0, :, sl])).astype(o_ref.dtype)


def _gmlp(proj, lw, lb, ws, bs_t, out_dtype):
    bsz, t, _ = proj.shape
    l = min(t, GM_CHUNK)
    col = lambda name: _MAIN_OFF[name] // GROUP_W
    spec = lambda name: pl.BlockSpec((1, l, GROUP_W), lambda b, i, o=col(name): (b, i, o))
    return pl.pallas_call(
        functools.partial(_gmlp_kernel, l=l),
        grid=(bsz, t // l),
        in_specs=[spec("c_u"), spec("c_v"), spec("c_z"),
                  pl.BlockSpec((1, GROUP_W), lambda b, i: (0, 0)),
                  pl.BlockSpec((1, GROUP_W), lambda b, i: (0, 0)),
                  pl.BlockSpec((GM_GROUPS, l, l), lambda b, i: (0, 0, 0)),
                  pl.BlockSpec((l, GM_GROUPS), lambda b, i: (0, 0))],
        out_specs=[pl.BlockSpec((1, l, GROUP_W), lambda b, i: (b, i, 0)),
                   pl.BlockSpec((1, l, GROUP_W), lambda b, i: (b, i, 0))],
        out_shape=[jax.ShapeDtypeStruct((bsz, t, GROUP_W), out_dtype),
                   jax.ShapeDtypeStruct((bsz, t, GROUP_W), F32)],
        compiler_params=_cparams(("parallel", "parallel")),
        name="gmlp",
    )(proj, proj, proj, lw.reshape(1, -1), lb.reshape(1, -1), ws, bs_t)


def _moba_prompt_kernel(q_ref, k_ref, v_ref, z_ref, o_ref, kb_scr, vb_scr, km_scr, *, t):
    blk = MB_BLOCK
    nb = t // blk
    scale = MB_DH ** -0.5
    h = pl.program_id(1)
    km_scr[...] = jnp.zeros((LANES, MB_DH), F32)
    for n in range(nb):
        rows = pl.ds(n * blk * MB_HEADS + h, blk, stride=MB_HEADS)
        kn = k_ref[rows, :]
        kb_scr[n * blk:(n + 1) * blk, :] = kn.astype(BF16)
        vb_scr[n * blk:(n + 1) * blk, :] = v_ref[rows, :].astype(BF16)
        km_scr[n:n + 1, :] = jnp.mean(kn, axis=0, keepdims=True)
    kmean = km_scr[...]
    lane = lax.broadcasted_iota(jnp.int32, (blk, LANES), 1)
    rr = lax.broadcasted_iota(jnp.int32, (blk, blk), 0)
    cc = lax.broadcasted_iota(jnp.int32, (blk, blk), 1)

    for qi in range(nb):
        rows = slice(qi * blk, (qi + 1) * blk)
        q = q_ref[0, rows, :]
        qb = q.astype(BF16)
        bias = None
        if qi > MB_TOPK:
            gate = _dot_nt(q, kmean, precision=HIGHEST)
            cnt = jnp.zeros((blk, LANES), F32)
            for m in range(qi):
                gm = gate[:, m:m + 1]
                beats = (gm > gate) | ((gm == gate) & (lane > m))
                cnt = cnt + jnp.where(beats, 1.0, 0.0)
            bias = jnp.where(cnt < MB_TOPK, 0.0, NEG)
        ss = []
        for j in range(qi + 1):
            s = _dot_nt(qb, kb_scr[j * blk:(j + 1) * blk, :]) * scale
            if j == qi:
                s = jnp.where(cc <= rr, s, NEG)
            elif bias is not None:
                s = s + bias[:, j:j + 1]
            ss.append(s)
        lane_tiles = lambda xs: [x[:, i:i + LANES] for x in xs for i in range(0, blk, LANES)]
        m_i = jnp.max(functools.reduce(jnp.maximum, lane_tiles(ss)), axis=-1, keepdims=True)
        ps = [jnp.exp(s - m_i) for s in ss]
        l_i = jnp.sum(functools.reduce(jnp.add, lane_tiles(ps)), axis=-1, keepdims=True)
        acc = _dot(ps[0].astype(BF16), vb_scr[0:blk, :])
        for j in range(1, qi + 1):
            acc = acc + _dot(ps[j].astype(BF16), vb_scr[j * blk:(j + 1) * blk, :])
        o_ref[0, rows, :] = ((acc / l_i) * _silu(z_ref[0, rows, :])).astype(o_ref.dtype)


def _moba_prompt(q_rope, k4, v4, layer, proj, out_dtype):
    bsz, t, _ = proj.shape
    assert t % MB_BLOCK == 0 and t // MB_BLOCK <= LANES
    zcol = _MAIN_OFF["a_z"] // MB_DH
    hd = pl.BlockSpec((1, t, MB_DH), lambda b, h: (b, 0, h))
    kv = pl.BlockSpec((None, None, t * MB_HEADS, MB_DH), lambda b, h: (layer, b, 0, 0))
    return pl.pallas_call(
        functools.partial(_moba_prompt_kernel, t=t),
        grid=(bsz, MB_HEADS),
        in_specs=[hd, kv, kv, pl.BlockSpec((1, t, MB_DH), lambda b, h: (b, 0, zcol + h))],
        out_specs=hd,
        out_shape=jax.ShapeDtypeStruct((bsz, t, GROUP_W), out_dtype),
        scratch_shapes=[pltpu.VMEM((t, MB_DH), BF16), pltpu.VMEM((t, MB_DH), BF16),
                        pltpu.VMEM((LANES, MB_DH), F32)],
        compiler_params=_cparams(("parallel", "arbitrary")),
        name="moba_prompt",
    )(q_rope, k4, v4, proj)


QROWS = MB_HEADS * SAMPLE_T
ST_M, ST_L, ST_G = 0, 1, 2


def _moba_past_kernel(pt_ref, q_ref, bias_ref, *refs, nbs):
    del pt_ref
    npg = 2 * nbs
    k_refs, v_refs = refs[:npg], refs[npg:2 * npg]
    o_ref, st_ref = refs[2 * npg:]
    scale = MB_DH ** -0.5
    page = k_refs[0].shape[0] // MB_HEADS
    q = q_ref[0]
    qb = q.astype(BF16)
    bias = bias_ref[...]
    lane = lax.broadcasted_iota(jnp.int32, (QROWS, LANES), 1)
    ss = [_dot_nt(qb, r[...].astype(BF16)) * scale + bias for r in k_refs]
    ps, ms, ls = [], [], []
    for n in range(nbs):
        s0, s1 = ss[2 * n], ss[2 * n + 1]
        m = jnp.maximum(jnp.max(s0, axis=-1, keepdims=True), jnp.max(s1, axis=-1, keepdims=True))
        p0, p1 = jnp.exp(s0 - m), jnp.exp(s1 - m)
        ps += [p0.astype(BF16), p1.astype(BF16)]
        ms.append(m)
        ls.append(jnp.sum(p0, axis=-1, keepdims=True) + jnp.sum(p1, axis=-1, keepdims=True))
    for n in range(nbs):
        o_ref[n] = (_dot(ps[2 * n], v_refs[2 * n][...].astype(BF16))
                    + _dot(ps[2 * n + 1], v_refs[2 * n + 1][...].astype(BF16)))
        ksum = (jnp.sum(k_refs[2 * n][...].reshape(page, MB_HEADS, MB_DH), axis=0)
                + jnp.sum(k_refs[2 * n + 1][...].reshape(page, MB_HEADS, MB_DH), axis=0))
        ksum_rows = jnp.concatenate(
            [jnp.broadcast_to(ksum[h:h + 1, :], (SAMPLE_T, MB_DH)) for h in range(MB_HEADS)], axis=0)
        gate = jnp.sum(q * ksum_rows, axis=-1, keepdims=True) * (1.0 / MB_BLOCK)
        st_ref[n] = jnp.where(lane == ST_M, ms[n], jnp.where(lane == ST_L, ls[n], gate))


def _moba_past(layer, q_rows, cache_k, cache_v, page_table):
    db = q_rows.shape[0]
    rows = cache_k.shape[2]
    assert 2 * rows == MB_BLOCK * MB_HEADS
    nb = page_table.shape[1] // 2
    nbs = _pick(nb, (4, 2, 1))
    key_head = np.arange(rows) % MB_HEADS
    row_head = np.arange(QROWS) // SAMPLE_T
    bias = jnp.asarray(np.where(key_head[None, :] == row_head[:, None], 0.0, NEG).astype(np.float32))
    pg = lambda i: pl.BlockSpec((None, None, rows, MB_DH),
                                lambda b, n, pt, i=i: (layer, pt[b, 2 * nbs * n + i], 0, 0))
    pages = [pg(i) for i in range(2 * nbs)]
    part = pl.BlockSpec((None, nbs, QROWS, MB_DH), lambda b, n, pt: (b, n, 0, 0))
    shape = jax.ShapeDtypeStruct((db, nb, QROWS, MB_DH), F32)
    grid_spec = pltpu.PrefetchScalarGridSpec(
        num_scalar_prefetch=1,
        grid=(db, nb // nbs),
        in_specs=[pl.BlockSpec((1, QROWS, MB_DH), lambda b, n, pt: (b, 0, 0)),
                  pl.BlockSpec((QROWS, rows), lambda b, n, pt: (0, 0))] + pages + pages,
        out_specs=[part, part],
    )
    return pl.pallas_call(
        functools.partial(_moba_past_kernel, nbs=nbs),
        grid_spec=grid_spec,
        out_shape=[shape, shape],
        compiler_params=_cparams(("parallel", "parallel")),
        name="moba_past",
    )(page_table, q_rows, bias, *([cache_k] * (2 * nbs)), *([cache_v] * (2 * nbs)))


def _moba_merge_kernel(op_ref, st_ref, q_ref, k_ref, v_ref, z_ref, o_ref, *, nb, t_valid):
    scale = MB_DH ** -0.5
    g = st_ref[0, :, :, ST_G:ST_G + 1]
    m = st_ref[0, :, :, ST_M:ST_M + 1]
    l = st_ref[0, :, :, ST_L:ST_L + 1]
    nidx = lax.broadcasted_iota(jnp.int32, g.shape, 0)
    sel = jnp.zeros(g.shape, jnp.bool_)
    gm = g
    for _ in range(min(MB_TOPK, nb)):
        mx = jnp.max(gm, axis=0, keepdims=True)
        first = jnp.min(jnp.where(gm == mx, nidx, nb), axis=0, keepdims=True)
        pick = nidx == first
        sel = sel | pick
        gm = jnp.where(pick, -jnp.inf, gm)

    rr = lax.broadcasted_iota(jnp.int32, (SAMPLE_T, SAMPLE_T), 0)
    cc = lax.broadcasted_iota(jnp.int32, (SAMPLE_T, SAMPLE_T), 1)
    mo, lo, oo = [], [], []
    for h in range(MB_HEADS):
        sl = slice(h * MB_DH, (h + 1) * MB_DH)
        qh = q_ref[0, h * SAMPLE_T:(h + 1) * SAMPLE_T, :]
        s = _dot_nt(qh.astype(BF16), k_ref[0, :, sl].astype(BF16)) * scale
        s = jnp.where((cc <= rr) & (cc < t_valid), s, NEG)
        mh = jnp.max(s, axis=-1, keepdims=True)
        p = jnp.exp(s - mh)
        mo.append(mh)
        lo.append(jnp.sum(p, axis=-1, keepdims=True))
        oo.append(_dot(p.astype(BF16), v_ref[0, :, sl].astype(BF16)))
    m_own, l_own, o_own = (jnp.concatenate(x, axis=0) for x in (mo, lo, oo))

    m_tot = jnp.maximum(jnp.max(jnp.where(sel, m, NEG), axis=0), m_own)
    w = jnp.where(sel, jnp.exp(m - m_tot[None]), 0.0)
    w_own = jnp.exp(m_own - m_tot)
    den = jnp.sum(w * l, axis=0) + w_own * l_own
    acc = w_own * o_own
    for n in range(nb):
        acc = acc + w[n] * op_ref[0, n]
    o = acc / den
    for h in range(MB_HEADS):
        sl = slice(h * MB_DH, (h + 1) * MB_DH)
        o_ref[0, :, sl] = o[h * SAMPLE_T:(h + 1) * SAMPLE_T, :] * _silu(z_ref[0, :, sl])


def _moba_merge(o_part, stats, q_rows, k_rope, proj, t_valid):
    db, nb = o_part.shape[:2]
    vcol, zcol = _MAIN_OFF["a_v"] // GROUP_W, _MAIN_OFF["a_z"] // GROUP_W
    part = pl.BlockSpec((1, nb, QROWS, MB_DH), lambda b: (b, 0, 0, 0))
    row = pl.BlockSpec((1, SAMPLE_T, GROUP_W), lambda b: (b, 0, 0))
    return pl.pallas_call(
        functools.partial(_moba_merge_kernel, nb=nb, t_valid=t_valid),
        grid=(db,),
        in_specs=[part, part, pl.BlockSpec((1, QROWS, MB_DH), lambda b: (b, 0, 0)), row,
                  pl.BlockSpec((1, SAMPLE_T, GROUP_W), lambda b: (b, 0, vcol)),
                  pl.BlockSpec((1, SAMPLE_T, GROUP_W), lambda b: (b, 0, zcol))],
        out_specs=row,
        out_shape=jax.ShapeDtypeStruct((db, SAMPLE_T, GROUP_W), F32),
        compiler_params=_cparams(("parallel",)),
        name="moba_merge",
    )(o_part, stats, q_rows, k_rope, proj, proj)


def _layer_weights(w_in_l, gla_w_gate_l, ml_b_i_l, ml_b_f_l):
    d = w_in_l.shape[0]
    cols = lambda n: w_in_l[:, _SRC_OFF[n]:_SRC_OFF[n] + _SRC_W[n]]
    pad = LANES - (GLA_GATE_RANK + 2 * ML_HEADS)
    w_small = jnp.concatenate([cols("g_lr"), cols("m_i"), cols("m_f"), jnp.zeros((d, pad), F32)],
                              axis=1).astype(BF16)
    wg_pad = jnp.concatenate([gla_w_gate_l, jnp.zeros((LANES - GLA_GATE_RANK, gla_w_gate_l.shape[1]), F32)],
                             axis=0)
    bias_row = jnp.concatenate([jnp.zeros((SM_I,), F32), ml_b_i_l, ml_b_f_l,
                                jnp.zeros((LANES - SM_F - ML_HEADS,), F32)]).reshape(1, LANES)
    return w_small, wg_pad, bias_row


def _mixers(x, bsz, t, t_valid, layer, lw, rope_tables, gla_s0, ml_state, out_dtype):
    (norm_w, w_main, w_small, wg_pad, bias_row, gla_b_gate, gla_norm_w, ml_norm_w,
     gm_ln_w, gm_ln_b, gm_ws, gm_bs_t) = lw
    h, small = _rms_small(x, norm_w, w_small)
    proj = _matmul(h, w_main, layer).reshape(bsz, t, N_MAIN)
    small = small.reshape(bsz, t, LANES)

    out_a, gla_s = _gla(proj, small, wg_pad, gla_b_gate, gla_norm_w, gla_s0, t_valid, out_dtype)

    gates = _gates(small, bias_row, t_valid)
    out_b, ml_c, ml_n, ml_m = _mlstm(proj, gates, ml_norm_w, *ml_state, out_dtype)

    out_c, vn = _gmlp(proj, gm_ln_w, gm_ln_b, gm_ws, gm_bs_t, out_dtype)

    q_rope = _rope(proj, _MAIN_OFF["a_q"], rope_tables)
    return proj, (out_a, out_b, out_c), q_rope, gla_s, (ml_c, ml_n, ml_m[..., :1]), vn


def kernel(x_prompt, x_sample, cache_k, cache_v, page_table, state_gla, state_mlstm_C, state_mlstm_n,
           state_mlstm_m, norm_w, w_in, gla_w_gate, gla_b_gate, gla_norm_w, ml_b_i, ml_b_f, ml_norm_w,
           gm_ln_w, gm_ln_b, gm_w_s, gm_b_s, w_out, final_norm_w):
    bp, tp, d = x_prompt.shape
    db, ts, _ = x_sample.shape
    depth = w_in.shape[0]
    page = cache_k.shape[2]
    past_len = page_table.shape[1] * page
    assert w_out.shape[1] == 4 * GROUP_W and ts <= SAMPLE_T
    assert past_len % MB_BLOCK == 0 and tp % MB_BLOCK == 0

    tables_p = _rope_tables(jnp.arange(tp, dtype=jnp.int32))
    tables_s = _rope_tables(past_len + jnp.arange(SAMPLE_T, dtype=jnp.int32))
    cache_k = cache_k.reshape(depth, cache_k.shape[1], page * MB_HEADS, MB_DH)
    cache_v = cache_v.reshape(depth, cache_v.shape[1], page * MB_HEADS, MB_DH)

    yp = x_prompt.reshape(bp * tp, d)
    ys = jnp.pad(x_sample, ((0, 0), (0, SAMPLE_T - ts), (0, 0))).reshape(db * SAMPLE_T, d)
    dt_p = BF16
    dt_s = F32

    zero_gla = jnp.zeros((bp, GLA_HEADS // 2, GLA_DV, LANES), F32)
    zero_ml = (jnp.zeros((bp, ML_HEADS, ML_DH, ML_DH), F32), jnp.zeros((bp, ML_HEADS, 1, ML_DH), F32),
               jnp.zeros((bp, ML_HEADS, 1, LANES), F32))

    outs = {n: [] for n in ("ks", "vs", "gp", "gs", "cp", "cs", "np", "ns", "mp", "ms", "vv")}
    lp = min(tp, GM_CHUNK)
    w_main = _wprep(w_in)
    w_out4 = w_out.reshape(depth, 4, GROUP_W, d)
    kv_stacks = None
    for l in range(depth):
        w_small, wg_pad, bias_row = _layer_weights(w_in[l], gla_w_gate[l], ml_b_i[l], ml_b_f[l])
        common = (norm_w[l], w_main, w_small, wg_pad, bias_row, gla_b_gate[l], gla_norm_w[l], ml_norm_w[l],
                  gm_ln_w[l], gm_ln_b[l])

        lw = common + (gm_w_s[l][:, :lp, :lp], gm_b_s[l][:, :lp].T)
        proj, mix, q_rope, gla_s, ml_s, _ = _mixers(
            yp, bp, tp, tp, l, lw, tables_p, zero_gla, zero_ml, dt_p)
        kv_stacks = _kv_heads(proj, tables_p, l, depth, kv_stacks)
        out_d = _moba_prompt(q_rope, kv_stacks[0], kv_stacks[1], l, proj, dt_p)
        yp = _out_proj([a.reshape(bp * tp, GROUP_W) for a in mix + (out_d,)], w_out4, l, yp)
        outs["gp"].append(_gla_state_from_pairs(gla_s))
        outs["cp"].append(ml_s[0])
        outs["np"].append(ml_s[1][:, :, 0, :])
        outs["mp"].append(ml_s[2][:, :, 0, 0])

        lw = common + (gm_w_s[l][:, :SAMPLE_T, :SAMPLE_T], gm_b_s[l][:, :SAMPLE_T].T)
        ml_state = (state_mlstm_C[l], state_mlstm_n[l][:, :, None, :],
                    jnp.broadcast_to(state_mlstm_m[l][:, :, None, None], (db, ML_HEADS, 1, LANES)))
        proj, mix, q_rope, gla_s, ml_s, vn = _mixers(
            ys, db, SAMPLE_T, ts, l, lw, tables_s, _gla_state_to_pairs(state_gla[l]), ml_state, dt_s)
        k_rope = _rope(proj, _MAIN_OFF["a_k"], tables_s)
        v_new = proj[:, :, _MAIN_OFF["a_v"]:_MAIN_OFF["a_v"] + GROUP_W]
        q_rows = (q_rope.reshape(db, SAMPLE_T, MB_HEADS, MB_DH).transpose(0, 2, 1, 3)
                  .reshape(db, QROWS, MB_DH))
        o_part, stats = _moba_past(l, q_rows, cache_k, cache_v, page_table)
        out_d = _moba_merge(o_part, stats, q_rows, k_rope, proj, ts)
        ys = _out_proj([a.reshape(db * SAMPLE_T, GROUP_W) for a in mix + (out_d,)], w_out4, l, ys)
        outs["ks"].append(k_rope[:, :ts].reshape(db, ts, MB_HEADS, MB_DH))
        outs["vs"].append(v_new[:, :ts].reshape(db, ts, MB_HEADS, MB_DH))
        outs["gs"].append(_gla_state_from_pairs(gla_s))
        outs["cs"].append(ml_s[0])
        outs["ns"].append(ml_s[1][:, :, 0, :])
        outs["ms"].append(ml_s[2][:, :, 0, 0])
        outs["vv"].append(vn[:, :ts])

    y_prompt = _rms(yp, final_norm_w).reshape(bp, tp, d)
    y_sample = _rms(ys, final_norm_w).reshape(db, SAMPLE_T, d)[:, :ts]
    st = jnp.stack
    k_prompt = kv_stacks[0].reshape(depth, bp, tp, MB_HEADS, MB_DH)
    v_prompt = kv_stacks[1].reshape(depth, bp, tp, MB_HEADS, MB_DH)
    return (y_prompt, y_sample, k_prompt, v_prompt, st(outs["ks"]), st(outs["vs"]),
            st(outs["gp"]), st(outs["gs"]), st(outs["cp"]), st(outs["cs"]), st(outs["np"]), st(outs["ns"]),
            st(outs["mp"]), st(outs["ms"]), st(outs["vv"]))
```

```python
import functools

import numpy as np
import jax
import jax.numpy as jnp
from jax import lax
from jax.experimental import pallas as pl
from jax.experimental.pallas import tpu as pltpu

F32 = jnp.float32
BF16 = jnp.bfloat16
HIGHEST = lax.Precision.HIGHEST

GROUP_W = 1024
GLA_HEADS, GLA_DK, GLA_DV = 8, 64, 128
GLA_GATE_RANK, GLA_GATE_NORM = 16, 16.0
ML_HEADS, ML_DH = 8, 128
GM_GROUPS, GM_CH, GM_CHUNK = 8, 128, 128
MB_HEADS, MB_DH, MB_BLOCK, MB_TOPK = 8, 128, 256, 3
ROT_DIM, ROPE_THETA = 32, 500000.0
EPS = 1e-6

LANES = 128
SUBLANES = 8
VMEM_LIMIT = 56 * 1024 * 1024

NEG = -1e30
SAMPLE_T = SUBLANES

_MAIN_ORDER = ("g_q", "g_k", "g_v", "g_z", "m_q", "m_k", "m_v", "m_o", "m_z",
               "c_u", "c_v", "c_z", "a_q", "a_k", "a_v", "a_z")
_SPLIT_NAMES = ("g_q", "g_k", "g_v", "g_lr", "g_z", "m_q", "m_k", "m_v", "m_i", "m_f", "m_o", "m_z",
                "c_u", "c_v", "c_z", "a_q", "a_k", "a_v", "a_z")
_SPLIT_W = (GLA_HEADS * GLA_DK, GLA_HEADS * GLA_DK, GLA_HEADS * GLA_DV, GLA_GATE_RANK, GROUP_W,
            GROUP_W, GROUP_W, GROUP_W, ML_HEADS, ML_HEADS, GROUP_W, GROUP_W,
            GROUP_W, GROUP_W, GROUP_W, GROUP_W, GROUP_W, GROUP_W, GROUP_W)
_SRC_OFF = dict(zip(_SPLIT_NAMES, np.concatenate([[0], np.cumsum(_SPLIT_W)[:-1]]).tolist()))
_SRC_W = dict(zip(_SPLIT_NAMES, _SPLIT_W))
_MAIN_OFF = {}
_off = 0
for _n in _MAIN_ORDER:
    _MAIN_OFF[_n] = _off
    _off += _SRC_W[_n]
N_MAIN = _off
SM_LR, SM_I, SM_F = 0, GLA_GATE_RANK, GLA_GATE_RANK + ML_HEADS


def _cparams(sem):
    return pltpu.CompilerParams(dimension_semantics=sem, vmem_limit_bytes=VMEM_LIMIT)


def _pick(n, cands):
    for c in cands:
        if n % c == 0:
            return c
    return n


def _silu(x):
    return x / (1.0 + jnp.exp(-x))


def _sigmoid(x):
    return 1.0 / (1.0 + jnp.exp(-x))


def _log_sigmoid(x):
    return jnp.minimum(x, 0.0) - jnp.log1p(jnp.exp(-jnp.abs(x)))


def _gelu(x):
    c = np.sqrt(2.0 / np.pi).astype(np.float32)
    return 0.5 * x * (1.0 + jnp.tanh(c * (x + 0.044715 * (x * x * x))))


def _dot_nt(a, b, precision=None):
    return lax.dot_general(a, b, (((1,), (1,)), ((), ())), preferred_element_type=F32, precision=precision)


def _dot_tn(a, b):
    return lax.dot_general(a, b, (((0,), (0,)), ((), ())), preferred_element_type=F32)


def _dot(a, b, precision=None):
    return jnp.dot(a, b, preferred_element_type=F32, precision=precision)


def _rms_small_kernel(x_ref, nw_ref, ws_ref, h_ref, sm_ref):
    x = x_ref[...]
    ms = jnp.mean(x * x, axis=-1, keepdims=True)
    hb = (x * lax.rsqrt(ms + EPS) * nw_ref[...]).astype(BF16)
    h_ref[...] = hb
    sm_ref[...] = _dot(hb, ws_ref[...])


def _rms_small(x, nw, w_small, layer):
    m, d = x.shape
    bm = _pick(m, (256, 128, 64, 32, 16, 8))
    return pl.pallas_call(
        _rms_small_kernel,
        grid=(m // bm,),
        in_specs=[pl.BlockSpec((bm, d), lambda i: (i, 0)),
                  pl.BlockSpec((1, d), lambda i: (0, 0)),
                  pl.BlockSpec((None, d, LANES), lambda i: (layer, 0, 0))],
        out_specs=[pl.BlockSpec((bm, d), lambda i: (i, 0)),
                   pl.BlockSpec((bm, LANES), lambda i: (i, 0))],
        out_shape=[jax.ShapeDtypeStruct((m, d), BF16), jax.ShapeDtypeStruct((m, LANES), F32)],
        compiler_params=_cparams(("parallel",)),
        name="rms_small",
    )(x, nw.reshape(1, d), w_small)


def _rms_kernel(x_ref, nw_ref, o_ref):
    x = x_ref[...]
    ms = jnp.mean(x * x, axis=-1, keepdims=True)
    o_ref[...] = x * lax.rsqrt(ms + EPS) * nw_ref[...]


def _rms(x, nw):
    m, d = x.shape
    bm = _pick(m, (256, 128, 64, 32, 16, 8))
    return pl.pallas_call(
        _rms_kernel,
        grid=(m // bm,),
        in_specs=[pl.BlockSpec((bm, d), lambda i: (i, 0)), pl.BlockSpec((1, d), lambda i: (0, 0))],
        out_specs=pl.BlockSpec((bm, d), lambda i: (i, 0)),
        out_shape=jax.ShapeDtypeStruct((m, d), F32),
        compiler_params=_cparams(("parallel",)),
        name="rms_final",
    )(x, nw.reshape(1, d))


WP_BN = 1024
WP_B1 = _MAIN_OFF["g_z"] // WP_BN
WP_B2 = _MAIN_OFF["m_o"] // WP_BN
WP_S1 = GLA_GATE_RANK
WP_S2 = GLA_GATE_RANK + 2 * ML_HEADS
assert _MAIN_OFF["g_z"] % WP_BN == 0 and _MAIN_OFF["m_o"] % WP_BN == 0 and N_MAIN % WP_BN == 0
assert WP_S1 % SUBLANES == 0 and WP_S2 % SUBLANES == 0 and sum(_SPLIT_W) % WP_S2 == 0


def _wprep_kernel(a_ref, t_ref, o_ref):
    j = pl.program_id(2)

    def emit(s):
        x = a_ref[...] if s == 0 else jnp.concatenate([a_ref[...], t_ref[...]], axis=0)[s:s + WP_BN]
        o_ref[...] = x.T.astype(BF16)

    @pl.when(j < WP_B1)
    def _():
        emit(0)

    @pl.when((j >= WP_B1) & (j < WP_B2))
    def _():
        emit(WP_S1)

    @pl.when(j >= WP_B2)
    def _():
        emit(WP_S2)


def _wprep(w_t):
    depth, _, d = w_t.shape
    kb = _pick(d, (512, 256, 128))
    return pl.pallas_call(
        _wprep_kernel,
        grid=(depth, d // kb, N_MAIN // WP_BN),
        in_specs=[pl.BlockSpec((None, WP_BN, kb), lambda l, i, j: (l, j, i)),
                  pl.BlockSpec((None, WP_S2, kb), lambda l, i, j: (l, (j + 1) * (WP_BN // WP_S2), i))],
        out_specs=pl.BlockSpec((None, kb, WP_BN), lambda l, i, j: (l, i, j)),
        out_shape=jax.ShapeDtypeStruct((depth, d, N_MAIN), BF16),
        compiler_params=_cparams(("parallel", "parallel", "parallel")),
        name="w_prep",
    )(w_t, w_t)


def _wsmall_kernel(lr_ref, if_ref, o_ref):
    kb = lr_ref.shape[1]
    x = jnp.concatenate([lr_ref[...], if_ref[...],
                         jnp.zeros((LANES - GLA_GATE_RANK - 2 * ML_HEADS, kb), F32)], axis=0)
    o_ref[...] = x.T.astype(BF16)


def _wsmall(w_t):
    depth, _, d = w_t.shape
    kb = _pick(d, (512, 256, 128))
    lr, gi = _SRC_OFF["g_lr"], _SRC_OFF["m_i"]
    assert _SRC_OFF["m_f"] == gi + ML_HEADS and lr % GLA_GATE_RANK == 0 and gi % (2 * ML_HEADS) == 0
    return pl.pallas_call(
        _wsmall_kernel,
        grid=(depth, d // kb),
        in_specs=[pl.BlockSpec((None, GLA_GATE_RANK, kb), lambda l, i: (l, lr // GLA_GATE_RANK, i)),
                  pl.BlockSpec((None, 2 * ML_HEADS, kb), lambda l, i: (l, gi // (2 * ML_HEADS), i))],
        out_specs=pl.BlockSpec((None, kb, LANES), lambda l, i: (l, i, 0)),
        out_shape=jax.ShapeDtypeStruct((depth, d, LANES), BF16),
        compiler_params=_cparams(("parallel", "parallel")),
        name="w_small",
    )(w_t, w_t)


def _mm_kernel(a_ref, b_ref, o_ref):
    o_ref[...] = _dot(a_ref[...], b_ref[...])


def _matmul(a, w_all, layer):
    m, k = a.shape
    n = w_all.shape[2]
    bm = _pick(m, (1024, 512, 256, 128, 64))
    bn = _pick(n, (1024, 512, 256, 128))
    return pl.pallas_call(
        _mm_kernel,
        grid=(n // bn, m // bm),
        in_specs=[pl.BlockSpec((bm, k), lambda j, i: (i, 0)),
                  pl.BlockSpec((None, k, bn), lambda j, i: (layer, 0, j))],
        out_specs=pl.BlockSpec((bm, bn), lambda j, i: (i, j)),
        out_shape=jax.ShapeDtypeStruct((m, n), F32),
        compiler_params=_cparams(("parallel", "parallel")),
        name="in_proj",
    )(a, w_all)


def _out_kernel(a0, a1, a2, a3, w_ref, x_ref, o_ref):
    acc = x_ref[...]
    for g, a in enumerate((a0, a1, a2, a3)):
        acc = acc + _dot(a[...].astype(BF16), w_ref[g].astype(BF16))
    o_ref[...] = acc


def _out_proj(mix, w_all, layer, x):
    m, d = x.shape
    bm = _pick(m, (1024, 512, 256, 128, 64))
    bn = _pick(d, (512, 256, 128))
    a_spec = pl.BlockSpec((bm, GROUP_W), lambda i, j: (i, 0))
    return pl.pallas_call(
        _out_kernel,
        grid=(m // bm, d // bn),
        in_specs=[a_spec, a_spec, a_spec, a_spec,
                  pl.BlockSpec((None, 4, GROUP_W, bn), lambda i, j: (layer, 0, 0, j)),
                  pl.BlockSpec((bm, bn), lambda i, j: (i, j))],
        out_specs=pl.BlockSpec((bm, bn), lambda i, j: (i, j)),
        out_shape=jax.ShapeDtypeStruct((m, d), F32),
        compiler_params=_cparams(("parallel", "parallel")),
        name="out_proj",
    )(*mix, w_all, x)


def _rope_kernel(x_ref, a_ref, b_ref, c_ref, o_ref):
    a, b, c = a_ref[...], b_ref[...], c_ref[...]
    for h in range(MB_HEADS):
        sl = slice(h * MB_DH, (h + 1) * MB_DH)
        x = x_ref[0, :, sl]
        o_ref[0, :, sl] = (x * a + pltpu.roll(x, MB_DH - ROT_DIM // 2, 1) * b
                           + pltpu.roll(x, ROT_DIM // 2, 1) * c)


def _rope_tables(pos):
    half = ROT_DIM // 2
    inv_freq = jnp.power(ROPE_THETA, -jnp.arange(0, ROT_DIM, 2, dtype=F32) / ROT_DIM)
    ang = pos.astype(F32)[:, None] * inv_freq[None, :]
    cos, sin = jnp.cos(ang), jnp.sin(ang)
    t = pos.shape[0]
    a = jnp.concatenate([cos, cos, jnp.ones((t, MB_DH - ROT_DIM), F32)], axis=1)
    b = jnp.concatenate([-sin, jnp.zeros((t, MB_DH - half), F32)], axis=1)
    c = jnp.concatenate([jnp.zeros((t, half), F32), sin, jnp.zeros((t, MB_DH - ROT_DIM), F32)], axis=1)
    return a, b, c


def _rope(proj, col, tables):
    bsz, t, _ = proj.shape
    bt = _pick(t, (512, 256, 128, 64, 32, 16, 8))
    tab = pl.BlockSpec((bt, MB_DH), lambda b, i: (i, 0))
    return pl.pallas_call(
        _rope_kernel,
        grid=(bsz, t // bt),
        in_specs=[pl.BlockSpec((1, bt, GROUP_W), lambda b, i: (b, i, col // GROUP_W)), tab, tab, tab],
        out_specs=pl.BlockSpec((1, bt, GROUP_W), lambda b, i: (b, i, 0)),
        out_shape=jax.ShapeDtypeStruct((bsz, t, GROUP_W), F32),
        compiler_params=_cparams(("parallel", "parallel")),
        name="rope",
    )(proj, *tables)


def _kv_heads_kernel(k_ref, v_ref, a_ref, b_ref, c_ref, *refs):
    k4_ref, v4_ref = refs[-2:]
    a, b, c = a_ref[...], b_ref[...], c_ref[...]
    bt = a.shape[0]
    for h in range(MB_HEADS):
        sl = slice(h * MB_DH, (h + 1) * MB_DH)
        x = k_ref[0, :, sl]
        rows = pl.ds(h, bt, stride=MB_HEADS)
        k4_ref[rows, :] = (x * a + pltpu.roll(x, MB_DH - ROT_DIM // 2, 1) * b
                           + pltpu.roll(x, ROT_DIM // 2, 1) * c)
        v4_ref[rows, :] = v_ref[0, :, sl]


def _kv_heads(proj, tables, layer, depth, stacks):
    bsz, t, _ = proj.shape
    bt = _pick(t, (256, 128, 64, 32, 16, 8))
    tab = pl.BlockSpec((bt, MB_DH), lambda b, i: (i, 0))
    kcol, vcol = _MAIN_OFF["a_k"] // GROUP_W, _MAIN_OFF["a_v"] // GROUP_W
    out = pl.BlockSpec((None, None, bt * MB_HEADS, MB_DH), lambda b, i: (layer, b, i, 0))
    shape = jax.ShapeDtypeStruct((depth, bsz, t * MB_HEADS, MB_DH), F32)
    in_specs = [pl.BlockSpec((1, bt, GROUP_W), lambda b, i: (b, i, kcol)),
                pl.BlockSpec((1, bt, GROUP_W), lambda b, i: (b, i, vcol)), tab, tab, tab]
    aliases = {}
    if stacks is not None:
        in_specs += [pl.BlockSpec(memory_space=pl.ANY), pl.BlockSpec(memory_space=pl.ANY)]
        aliases = {5: 0, 6: 1}
    return pl.pallas_call(
        _kv_heads_kernel,
        grid=(bsz, t // bt),
        in_specs=in_specs,
        out_specs=[out, out],
        out_shape=[shape, shape],
        input_output_aliases=aliases,
        compiler_params=_cparams(("parallel", "parallel")),
        name="kv_heads",
    )(proj, proj, *tables, *(stacks or ()))


def _gla_kernel(q_ref, k_ref, v_ref, z_ref, sm_ref, wg_ref, bg_ref, nw_ref, s0_ref,
                o_ref, s_ref, g_scr, b_scr, k_scr, *, t, c, nch, t_valid):
    dk = GLA_DK
    sup = c * nch
    x = _dot(sm_ref[0], wg_ref[...], precision=HIGHEST) + bg_ref[...]
    g_all = _log_sigmoid(x) * (1.0 / GLA_GATE_NORM)
    if t_valid < t:
        rows_t = lax.broadcasted_iota(jnp.int32, (t, LANES), 0)
        g_all = jnp.where(rows_t < t_valid, g_all, 0.0)
    g_scr[...] = g_all

    lane = lax.broadcasted_iota(jnp.int32, (sup, LANES), 1)
    row = lax.broadcasted_iota(jnp.int32, (sup, LANES), 0)
    head0 = lane < dk
    rmod = row & (c - 1)
    rr = lax.broadcasted_iota(jnp.int32, (sup, sup), 0)
    cc = lax.broadcasted_iota(jnp.int32, (sup, sup), 1)
    cdiff = cc - (rr - (rr & (c - 1)))
    tril_bd = jnp.where((cdiff >= 0) & (cdiff <= (rr & (c - 1))), 1.0, 0.0).astype(F32)
    nw = nw_ref[...]

    def chunk_rows(ref, off):
        return jnp.concatenate(
            [jnp.broadcast_to(ref[ch * c + off:ch * c + off + 1, :], (c, LANES)) for ch in range(nch)], axis=0)

    def body(si, s2t):
        r0 = pl.multiple_of(si * sup, sup)
        q = q_ref[0, pl.ds(r0, sup), :] * (dk ** -0.5)
        k = k_ref[0, pl.ds(r0, sup), :]
        if t_valid < t:
            k = jnp.where(row < t_valid, k, 0.0)
        g = g_scr[pl.ds(r0, sup), :]
        v = v_ref[0, pl.ds(r0, sup), :]
        z = z_ref[0, pl.ds(r0, sup), :]
        b = _dot(tril_bd, g, precision=HIGHEST)
        b_scr[...] = b
        k_scr[...] = k

        att0 = jnp.zeros((sup, sup), F32)
        att1 = jnp.zeros((sup, sup), F32)
        for s in range(c):
            dec = jnp.exp(jnp.where(rmod >= s, b - chunk_rows(b_scr, s), NEG))
            tile = q * dec * chunk_rows(k_scr, s)
            col0 = jnp.sum(jnp.where(head0, tile, 0.0), axis=-1, keepdims=True)
            col1 = jnp.sum(jnp.where(head0, 0.0, tile), axis=-1, keepdims=True)
            hit = cdiff == s
            att0 = jnp.where(hit, col0, att0)
            att1 = jnp.where(hit, col1, att1)

        kt = k * jnp.exp(chunk_rows(b_scr, c - 1) - b)
        qe = q * jnp.exp(b)
        qe0, qe1 = jnp.where(head0, qe, 0.0), jnp.where(head0, 0.0, qe)
        kt0, kt1 = jnp.where(head0, kt, 0.0), jnp.where(head0, 0.0, kt)
        v0, v1 = v[:, :GLA_DV], v[:, GLA_DV:]
        rows = [slice(ch * c, (ch + 1) * c) for ch in range(nch)]
        upds = [_dot_tn(jnp.concatenate([v0[r], v1[r]], axis=0).astype(BF16),
                        jnp.concatenate([kt0[r], kt1[r]], axis=0).astype(BF16)) for r in rows]
        o_int = []
        for ch, r in enumerate(rows):
            lhs = jnp.concatenate([qe0[r], qe1[r]], axis=0).astype(BF16)
            o_int.append(_dot_nt(lhs, s2t.astype(BF16)))
            s2t = jnp.exp(b_scr[ch * c + c - 1:ch * c + c, :]) * s2t + upds[ch]

        for h, (att, vh) in enumerate(((att0, v0), (att1, v1))):
            o = (jnp.concatenate([oi[h * c:(h + 1) * c, :] for oi in o_int], axis=0)
                 + _dot(att.astype(BF16), vh.astype(BF16)))
            on = o * lax.rsqrt(jnp.mean(o * o, axis=-1, keepdims=True) + EPS) * nw
            zh = z[:, h * GLA_DV:(h + 1) * GLA_DV]
            o_ref[0, pl.ds(r0, sup), h * GLA_DV:(h + 1) * GLA_DV] = (on * _silu(zh)).astype(o_ref.dtype)
        return s2t

    s_ref[0, 0] = lax.fori_loop(0, t // sup, body, s0_ref[0, 0])


def _gla(proj, small, wg_pad, bg, nw, s0t, t_valid, out_dtype):
    bsz, t, _ = proj.shape
    c = min(16, t)
    nch = _pick(t // c, (8, 4, 2, 1))
    pairs = GLA_HEADS // 2
    qb, kb = _MAIN_OFF["g_q"] // LANES, _MAIN_OFF["g_k"] // LANES
    vb, zb = _MAIN_OFF["g_v"] // (2 * GLA_DV), _MAIN_OFF["g_z"] // (2 * GLA_DV)
    kern = functools.partial(_gla_kernel, t=t, c=c, nch=nch, t_valid=t_valid)
    return pl.pallas_call(
        kern,
        grid=(bsz, pairs),
        in_specs=[pl.BlockSpec((1, t, LANES), lambda b, p: (b, 0, qb + p)),
                  pl.BlockSpec((1, t, LANES), lambda b, p: (b, 0, kb + p)),
                  pl.BlockSpec((1, t, 2 * GLA_DV), lambda b, p: (b, 0, vb + p)),
                  pl.BlockSpec((1, t, 2 * GLA_DV), lambda b, p: (b, 0, zb + p)),
                  pl.BlockSpec((1, t, LANES), lambda b, p: (b, 0, 0)),
                  pl.BlockSpec((LANES, LANES), lambda b, p: (0, p)),
                  pl.BlockSpec((1, LANES), lambda b, p: (0, p)),
                  pl.BlockSpec((1, GLA_DV), lambda b, p: (0, 0)),
                  pl.BlockSpec((1, 1, GLA_DV, LANES), lambda b, p: (b, p, 0, 0))],
        out_specs=[pl.BlockSpec((1, t, 2 * GLA_DV), lambda b, p: (b, 0, p)),
                   pl.BlockSpec((1, 1, GLA_DV, LANES), lambda b, p: (b, p, 0, 0))],
        out_shape=[jax.ShapeDtypeStruct((bsz, t, GROUP_W), out_dtype),
                   jax.ShapeDtypeStruct((bsz, pairs, GLA_DV, LANES), F32)],
        scratch_shapes=[pltpu.VMEM((t, LANES), F32), pltpu.VMEM((c * nch, LANES), F32),
                        pltpu.VMEM((c * nch, LANES), F32)],
        compiler_params=_cparams(("parallel", "parallel")),
        name="gla",
    )(proj, proj, proj, proj, small, wg_pad, bg.reshape(1, -1), nw.reshape(1, -1), s0t)


def _gla_state_to_pairs(s):
    bsz = s.shape[0]
    s = s.reshape(bsz, GLA_HEADS // 2, 2, GLA_DK, GLA_DV)
    return s.transpose(0, 1, 4, 2, 3).reshape(bsz, GLA_HEADS // 2, GLA_DV, 2 * GLA_DK)


def _gla_state_from_pairs(s):
    bsz = s.shape[0]
    s = s.reshape(bsz, GLA_HEADS // 2, GLA_DV, 2, GLA_DK)
    return s.transpose(0, 1, 3, 4, 2).reshape(bsz, GLA_HEADS, GLA_DK, GLA_DV)


def _gates_kernel(sm_ref, bias_ref, o_ref, *, t_valid):
    x = sm_ref[0] + bias_ref[...]
    lane = lax.broadcasted_iota(jnp.int32, x.shape, 1)
    is_f = (lane >= SM_F) & (lane < SM_F + ML_HEADS)
    out = jnp.where(is_f, _log_sigmoid(x), x)
    if t_valid is not None:
        row = lax.broadcasted_iota(jnp.int32, x.shape, 0)
        out = jnp.where(row < t_valid, out, jnp.where(is_f, 0.0, NEG))
    o_ref[0] = out


def _gates(small, bias_row, t_valid):
    bsz, t, _ = small.shape
    bt = t if t_valid < t else _pick(t, (1024, 512, 256, 128, 64, 32, 16, 8))
    return pl.pallas_call(
        functools.partial(_gates_kernel, t_valid=t_valid if t_valid < t else None),
        grid=(bsz, t // bt),
        in_specs=[pl.BlockSpec((1, bt, LANES), lambda b, i: (b, i, 0)),
                  pl.BlockSpec((1, LANES), lambda b, i: (0, 0))],
        out_specs=pl.BlockSpec((1, bt, LANES), lambda b, i: (b, i, 0)),
        out_shape=jax.ShapeDtypeStruct((bsz, t, LANES), F32),
        compiler_params=_cparams(("parallel", "parallel")),
        name="ml_gates",
    )(small, bias_row)


ML_PAIR = 2


def _mlstm_kernel(q_ref, k_ref, v_ref, og_ref, z_ref, gt_ref, nw_ref,
                  c0_ref, n0_ref, m0_ref, o_ref, c_ref, n_ref, m_ref, t_scr, *, t, c):
    h0 = pl.program_id(1) * ML_PAIR
    hs = range(ML_PAIR)
    lane = lax.broadcasted_iota(jnp.int32, (c, LANES), 1)
    rr = lax.broadcasted_iota(jnp.int32, (c, c), 0)
    cc = lax.broadcasted_iota(jnp.int32, (c, c), 1)
    causal = rr >= cc
    nw = nw_ref[...]
    hsl = lambda j: slice(j * ML_DH, (j + 1) * ML_DH)

    def body(ci, carry):
        cms, ns, m_prevs = carry
        r0 = pl.multiple_of(ci * c, c)
        gt = gt_ref[0, pl.ds(r0, c), :]
        if c == LANES:
            t_scr[...] = gt.T
        qs = [q_ref[0, pl.ds(r0, c), hsl(j)] * (ML_DH ** -0.5) for j in hs]
        ks = [k_ref[0, pl.ds(r0, c), hsl(j)] for j in hs]
        vs = [v_ref[0, pl.ds(r0, c), hsl(j)] for j in hs]
        qbs = [q.astype(BF16) for q in qs]
        kbs = [k.astype(BF16) for k in ks]
        s_qk = [_dot_nt(qbs[j], kbs[j]) for j in hs]
        s_qc = [_dot_nt(qbs[j], cms[j].astype(BF16)) for j in hs]

        d, inter, m_t, i_cols, fc_cols = [], [], [], [], []
        for j in hs:
            i_col = jnp.sum(jnp.where(lane == SM_I + h0 + j, gt, 0.0), axis=-1, keepdims=True)
            f_col = jnp.sum(jnp.where(lane == SM_F + h0 + j, gt, 0.0), axis=-1, keepdims=True)
            if c == LANES:
                i_row = t_scr[pl.ds(SM_I + h0 + j, 1), :]
                f_row = t_scr[pl.ds(SM_F + h0 + j, 1), :]
            else:
                i_row = jnp.sum(jnp.where(rr == cc, i_col, 0.0), axis=0, keepdims=True)
                f_row = jnp.sum(jnp.where(rr == cc, f_col, 0.0), axis=0, keepdims=True)
            fc_col = jnp.sum(jnp.where(causal, f_row, 0.0), axis=-1, keepdims=True)
            fc_row = jnp.sum(jnp.where(rr <= cc, f_col, 0.0), axis=0, keepdims=True)
            dj = jnp.where(causal, fc_col - fc_row + i_row, NEG)
            d.append(dj)
            inter.append(fc_col + m_prevs[j])
            m_t.append(jnp.maximum(inter[j], jnp.max(dj, axis=-1, keepdims=True)))
            i_cols.append(i_col)
            fc_cols.append(fc_col)

        w_state = [jnp.exp(inter[j] - m_t[j]) for j in hs]
        qk = [s_qk[j] * jnp.exp(d[j] - m_t[j]) for j in hs]
        pv = [_dot(qk[j].astype(BF16), vs[j].astype(BF16)) for j in hs]
        w_end, dec, m_new = [], [], []
        for j in hs:
            m_new.append(m_t[j][c - 1:c, :])
            f_end = fc_cols[j][c - 1:c, :]
            w_end.append(jnp.exp(f_end - fc_cols[j] + i_cols[j] - m_new[j]))
            dec.append(jnp.exp(f_end + m_prevs[j] - m_new[j]))
        upd = [_dot_tn((vs[j] * w_end[j]).astype(BF16), kbs[j]) for j in hs]

        c_new, n_new = [], []
        for j in hs:
            num = w_state[j] * s_qc[j] + pv[j]
            den = (w_state[j] * jnp.sum(qs[j] * ns[j], axis=-1, keepdims=True)
                   + jnp.sum(qk[j], axis=-1, keepdims=True))
            hout = num / jnp.maximum(jnp.abs(den), jnp.exp(-m_t[j]))
            c_new.append(dec[j] * cms[j] + upd[j])
            n_new.append(dec[j] * ns[j] + jnp.sum(ks[j] * w_end[j], axis=0, keepdims=True))
            o = _sigmoid(og_ref[0, pl.ds(r0, c), hsl(j)]) * hout
            on = o * lax.rsqrt(jnp.mean(o * o, axis=-1, keepdims=True) + EPS) * nw
            o_ref[0, pl.ds(r0, c), hsl(j)] = (on * _silu(z_ref[0, pl.ds(r0, c), hsl(j)])).astype(o_ref.dtype)
        return tuple(c_new), tuple(n_new), tuple(m_new)

    init = (tuple(c0_ref[0, j] for j in hs), tuple(n0_ref[0, j] for j in hs),
            tuple(m0_ref[0, j][:, :1] for j in hs))
    cms, ns, ms = lax.fori_loop(0, t // c, body, init)
    for j in hs:
        c_ref[0, j] = cms[j]
        n_ref[0, j] = ns[j]
        m_ref[0, j] = jnp.broadcast_to(ms[j], (1, LANES))


def _mlstm(proj, gates, nw, c0, n0, m0, out_dtype):
    bsz, t, _ = proj.shape
    c = min(128, t)
    w = ML_PAIR * ML_DH
    col = lambda name: _MAIN_OFF[name] // w
    spec = lambda name: pl.BlockSpec((1, t, w), lambda b, p, o=col(name): (b, 0, o + p))
    st = lambda rows: pl.BlockSpec((1, ML_PAIR, rows, ML_DH), lambda b, p: (b, p, 0, 0))
    return pl.pallas_call(
        functools.partial(_mlstm_kernel, t=t, c=c),
        grid=(bsz, ML_HEADS // ML_PAIR),
        in_specs=[spec("m_q"), spec("m_k"), spec("m_v"), spec("m_o"), spec("m_z"),
                  pl.BlockSpec((1, t, LANES), lambda b, p: (b, 0, 0)),
                  pl.BlockSpec((1, ML_DH), lambda b, p: (0, 0)),
                  st(ML_DH), st(1), st(1)],
        out_specs=[pl.BlockSpec((1, t, w), lambda b, p: (b, 0, p)), st(ML_DH), st(1), st(1)],
        out_shape=[jax.ShapeDtypeStruct((bsz, t, GROUP_W), out_dtype),
                   jax.ShapeDtypeStruct((bsz, ML_HEADS, ML_DH, ML_DH), F32),
                   jax.ShapeDtypeStruct((bsz, ML_HEADS, 1, ML_DH), F32),
                   jax.ShapeDtypeStruct((bsz, ML_HEADS, 1, LANES), F32)],
        scratch_shapes=[pltpu.VMEM((LANES, LANES), F32)],
        compiler_params=_cparams(("parallel", "parallel")),
        name="mlstm",
    )(proj, proj, proj, proj, proj, gates, nw.reshape(1, -1), c0, n0, m0)


def _gmlp_kernel(u_ref, v_ref, z_ref, lw_ref, lb_ref, ws_ref, bs_ref, o_ref, vn_ref, *, l):
    gv = _gelu(v_ref[0])
    mu = jnp.mean(gv, axis=-1, keepdims=True)
    xc = gv - mu
    vn = xc * lax.rsqrt(jnp.mean(xc * xc, axis=-1, keepdims=True) + EPS) * lw_ref[...] + lb_ref[...]
    vn_ref[0] = vn
    rr = lax.broadcasted_iota(jnp.int32, (l, l), 0)
    cc = lax.broadcasted_iota(jnp.int32, (l, l), 1)
    for g in range(GM_GROUPS):
        sl = slice(g * GM_CH, (g + 1) * GM_CH)
        w = jnp.where(rr >= cc, ws_ref[g], 0.0)
        vg = vn[:, sl]
        if l >= GM_CHUNK:
            s = _dot(w.astype(BF16), vg.astype(BF16))
        else:
            s = jnp.zeros((l, GM_CH), F32)
            for r in range(l):
                s = s + w[:, r:r + 1] * vg[r:r + 1, :]
        s = s + bs_ref[:, g:g + 1]
        o_ref[0, :, sl] = (_gelu(u_ref[0, :, sl]) * s * _silu(z_ref[0, :, sl])).astype(o_ref.dtype)


def _gmlp(proj, lw, lb, ws, bs_t, out_dtype):
    bsz, t, _ = proj.shape
    l = min(t, GM_CHUNK)
    col = lambda name: _MAIN_OFF[name] // GROUP_W
    spec = lambda name: pl.BlockSpec((1, l, GROUP_W), lambda b, i, o=col(name): (b, i, o))
    return pl.pallas_call(
        functools.partial(_gmlp_kernel, l=l),
        grid=(bsz, t // l),
        in_specs=[spec("c_u"), spec("c_v"), spec("c_z"),
                  pl.BlockSpec((1, GROUP_W), lambda b, i: (0, 0)),
                  pl.BlockSpec((1, GROUP_W), lambda b, i: (0, 0)),
                  pl.BlockSpec((GM_GROUPS, l, l), lambda b, i: (0, 0, 0)),
                  pl.BlockSpec((l, GM_GROUPS), lambda b, i: (0, 0))],
        out_specs=[pl.BlockSpec((1, l, GROUP_W), lambda b, i: (b, i, 0)),
                   pl.BlockSpec((1, l, GROUP_W), lambda b, i: (b, i, 0))],
        out_shape=[jax.ShapeDtypeStruct((bsz, t, GROUP_W), out_dtype),
                   jax.ShapeDtypeStruct((bsz, t, GROUP_W), F32)],
        compiler_params=_cparams(("parallel", "parallel")),
        name="gmlp",
    )(proj, proj, proj, lw.reshape(1, -1), lb.reshape(1, -1), ws, bs_t)


def _moba_prompt_kernel(q_ref, k_ref, v_ref, z_ref, o_ref, kb_scr, vb_scr, km_scr, *, t):
    blk = MB_BLOCK
    nb = t // blk
    scale = MB_DH ** -0.5
    h = pl.program_id(1)
    km_scr[...] = jnp.zeros((LANES, MB_DH), F32)
    for n in range(nb):
        rows = pl.ds(n * blk * MB_HEADS + h, blk, stride=MB_HEADS)
        kn = k_ref[rows, :]
        kb_scr[n * blk:(n + 1) * blk, :] = kn.astype(BF16)
        vb_scr[n * blk:(n + 1) * blk, :] = v_ref[rows, :].astype(BF16)
        km_scr[n:n + 1, :] = jnp.mean(kn, axis=0, keepdims=True)
    kmean = km_scr[...]
    lane = lax.broadcasted_iota(jnp.int32, (blk, LANES), 1)
    rr = lax.broadcasted_iota(jnp.int32, (blk, blk), 0)
    cc = lax.broadcasted_iota(jnp.int32, (blk, blk), 1)

    for qi in range(nb):
        rows = slice(qi * blk, (qi + 1) * blk)
        q = q_ref[0, rows, :]
        qb = q.astype(BF16)
        bias = None
        if qi > MB_TOPK:
            gate = _dot_nt(q, kmean, precision=HIGHEST)
            cnt = jnp.zeros((blk, LANES), F32)
            for m in range(qi):
                gm = gate[:, m:m + 1]
                beats = (gm > gate) | ((gm == gate) & (lane > m))
                cnt = cnt + jnp.where(beats, 1.0, 0.0)
            bias = jnp.where(cnt < MB_TOPK, 0.0, NEG)
        ss = []
        for j in range(qi + 1):
            s = _dot_nt(qb, kb_scr[j * blk:(j + 1) * blk, :]) * scale
            if j == qi:
                s = jnp.where(cc <= rr, s, NEG)
            elif bias is not None:
                s = s + bias[:, j:j + 1]
            ss.append(s)
        lane_tiles = lambda xs: [x[:, i:i + LANES] for x in xs for i in range(0, blk, LANES)]
        m_i = jnp.max(functools.reduce(jnp.maximum, lane_tiles(ss)), axis=-1, keepdims=True)
        ps = [jnp.exp(s - m_i) for s in ss]
        l_i = jnp.sum(functools.reduce(jnp.add, lane_tiles(ps)), axis=-1, keepdims=True)
        acc = _dot(ps[0].astype(BF16), vb_scr[0:blk, :])
        for j in range(1, qi + 1):
            acc = acc + _dot(ps[j].astype(BF16), vb_scr[j * blk:(j + 1) * blk, :])
        o_ref[0, rows, :] = ((acc / l_i) * _silu(z_ref[0, rows, :])).astype(o_ref.dtype)


def _moba_prompt(q_rope, k4, v4, layer, proj, out_dtype):
    bsz, t, _ = proj.shape
    assert t % MB_BLOCK == 0 and t // MB_BLOCK <= LANES
    zcol = _MAIN_OFF["a_z"] // MB_DH
    hd = pl.BlockSpec((1, t, MB_DH), lambda b, h: (b, 0, h))
    kv = pl.BlockSpec((None, None, t * MB_HEADS, MB_DH), lambda b, h: (layer, b, 0, 0))
    return pl.pallas_call(
        functools.partial(_moba_prompt_kernel, t=t),
        grid=(bsz, MB_HEADS),
        in_specs=[hd, kv, kv, pl.BlockSpec((1, t, MB_DH), lambda b, h: (b, 0, zcol + h))],
        out_specs=hd,
        out_shape=jax.ShapeDtypeStruct((bsz, t, GROUP_W), out_dtype),
        scratch_shapes=[pltpu.VMEM((t, MB_DH), BF16), pltpu.VMEM((t, MB_DH), BF16),
                        pltpu.VMEM((LANES, MB_DH), F32)],
        compiler_params=_cparams(("parallel", "arbitrary")),
        name="moba_prompt",
    )(q_rope, k4, v4, proj)


QROWS = MB_HEADS * SAMPLE_T
ST_M, ST_L, ST_G = 0, 1, 2


def _moba_past_kernel(pt_ref, q_ref, bias_ref, *refs, nbs):
    del pt_ref
    npg = 2 * nbs
    k_refs, v_refs = refs[:npg], refs[npg:2 * npg]
    o_ref, st_ref = refs[2 * npg:]
    scale = MB_DH ** -0.5
    page = k_refs[0].shape[0] // MB_HEADS
    q = q_ref[0]
    qb = q.astype(BF16)
    bias = bias_ref[...]
    lane = lax.broadcasted_iota(jnp.int32, (QROWS, LANES), 1)
    ss = [_dot_nt(qb, r[...].astype(BF16)) * scale + bias for r in k_refs]
    ps, ms, ls = [], [], []
    for n in range(nbs):
        s0, s1 = ss[2 * n], ss[2 * n + 1]
        m = jnp.maximum(jnp.max(s0, axis=-1, keepdims=True), jnp.max(s1, axis=-1, keepdims=True))
        p0, p1 = jnp.exp(s0 - m), jnp.exp(s1 - m)
        ps += [p0.astype(BF16), p1.astype(BF16)]
        ms.append(m)
        ls.append(jnp.sum(p0, axis=-1, keepdims=True) + jnp.sum(p1, axis=-1, keepdims=True))
    for n in range(nbs):
        o_ref[n] = (_dot(ps[2 * n], v_refs[2 * n][...].astype(BF16))
                    + _dot(ps[2 * n + 1], v_refs[2 * n + 1][...].astype(BF16)))
        ksum = (jnp.sum(k_refs[2 * n][...].reshape(page, MB_HEADS, MB_DH), axis=0)
                + jnp.sum(k_refs[2 * n + 1][...].reshape(page, MB_HEADS, MB_DH), axis=0))
        ksum_rows = jnp.concatenate(
            [jnp.broadcast_to(ksum[h:h + 1, :], (SAMPLE_T, MB_DH)) for h in range(MB_HEADS)], axis=0)
        gate = jnp.sum(q * ksum_rows, axis=-1, keepdims=True) * (1.0 / MB_BLOCK)
        st_ref[n] = jnp.where(lane == ST_M, ms[n], jnp.where(lane == ST_L, ls[n], gate))


def _moba_past(layer, q_rows, cache_k, cache_v, page_table):
    db = q_rows.shape[0]
    rows = cache_k.shape[2]
    assert 2 * rows == MB_BLOCK * MB_HEADS
    nb = page_table.shape[1] // 2
    nbs = _pick(nb, (4, 2, 1))
    key_head = np.arange(rows) % MB_HEADS
    row_head = np.arange(QROWS) // SAMPLE_T
    bias = jnp.asarray(np.where(key_head[None, :] == row_head[:, None], 0.0, NEG).astype(np.float32))
    pg = lambda i: pl.BlockSpec((None, None, rows, MB_DH),
                                lambda b, n, pt, i=i: (layer, pt[b, 2 * nbs * n + i], 0, 0))
    pages = [pg(i) for i in range(2 * nbs)]
    part = pl.BlockSpec((None, nbs, QROWS, MB_DH), lambda b, n, pt: (b, n, 0, 0))
    shape = jax.ShapeDtypeStruct((db, nb, QROWS, MB_DH), F32)
    grid_spec = pltpu.PrefetchScalarGridSpec(
        num_scalar_prefetch=1,
        grid=(db, nb // nbs),
        in_specs=[pl.BlockSpec((1, QROWS, MB_DH), lambda b, n, pt: (b, 0, 0)),
                  pl.BlockSpec((QROWS, rows), lambda b, n, pt: (0, 0))] + pages + pages,
        out_specs=[part, part],
    )
    return pl.pallas_call(
        functools.partial(_moba_past_kernel, nbs=nbs),
        grid_spec=grid_spec,
        out_shape=[shape, shape],
        compiler_params=_cparams(("parallel", "parallel")),
        name="moba_past",
    )(page_table, q_rows, bias, *([cache_k] * (2 * nbs)), *([cache_v] * (2 * nbs)))


def _moba_merge_kernel(op_ref, st_ref, q_ref, k_ref, v_ref, z_ref, o_ref, *, nb, t_valid):
    scale = MB_DH ** -0.5
    g = st_ref[0, :, :, ST_G:ST_G + 1]
    m = st_ref[0, :, :, ST_M:ST_M + 1]
    l = st_ref[0, :, :, ST_L:ST_L + 1]
    nidx = lax.broadcasted_iota(jnp.int32, g.shape, 0)
    sel = jnp.zeros(g.shape, jnp.bool_)
    gm = g
    for _ in range(min(MB_TOPK, nb)):
        mx = jnp.max(gm, axis=0, keepdims=True)
        first = jnp.min(jnp.where(gm == mx, nidx, nb), axis=0, keepdims=True)
        pick = nidx == first
        sel = sel | pick
        gm = jnp.where(pick, -jnp.inf, gm)

    rr = lax.broadcasted_iota(jnp.int32, (SAMPLE_T, SAMPLE_T), 0)
    cc = lax.broadcasted_iota(jnp.int32, (SAMPLE_T, SAMPLE_T), 1)
    mo, lo, oo = [], [], []
    for h in range(MB_HEADS):
        sl = slice(h * MB_DH, (h + 1) * MB_DH)
        qh = q_ref[0, h * SAMPLE_T:(h + 1) * SAMPLE_T, :]
        s = _dot_nt(qh.astype(BF16), k_ref[0, :, sl].astype(BF16)) * scale
        s = jnp.where((cc <= rr) & (cc < t_valid), s, NEG)
        mh = jnp.max(s, axis=-1, keepdims=True)
        p = jnp.exp(s - mh)
        mo.append(mh)
        lo.append(jnp.sum(p, axis=-1, keepdims=True))
        oo.append(_dot(p.astype(BF16), v_ref[0, :, sl].astype(BF16)))
    m_own, l_own, o_own = (jnp.concatenate(x, axis=0) for x in (mo, lo, oo))

    m_tot = jnp.maximum(jnp.max(jnp.where(sel, m, NEG), axis=0), m_own)
    w = jnp.where(sel, jnp.exp(m - m_tot[None]), 0.0)
    w_own = jnp.exp(m_own - m_tot)
    den = jnp.sum(w * l, axis=0) + w_own * l_own
    acc = w_own * o_own
    for n in range(nb):
        acc = acc + w[n] * op_ref[0, n]
    o = acc / den
    for h in range(MB_HEADS):
        sl = slice(h * MB_DH, (h + 1) * MB_DH)
        o_ref[0, :, sl] = o[h * SAMPLE_T:(h + 1) * SAMPLE_T, :] * _silu(z_ref[0, :, sl])


def _moba_merge(o_part, stats, q_rows, k_rope, proj, t_valid):
    db, nb = o_part.shape[:2]
    vcol, zcol = _MAIN_OFF["a_v"] // GROUP_W, _MAIN_OFF["a_z"] // GROUP_W
    part = pl.BlockSpec((1, nb, QROWS, MB_DH), lambda b: (b, 0, 0, 0))
    row = pl.BlockSpec((1, SAMPLE_T, GROUP_W), lambda b: (b, 0, 0))
    return pl.pallas_call(
        functools.partial(_moba_merge_kernel, nb=nb, t_valid=t_valid),
        grid=(db,),
        in_specs=[part, part, pl.BlockSpec((1, QROWS, MB_DH), lambda b: (b, 0, 0)), row,
                  pl.BlockSpec((1, SAMPLE_T, GROUP_W), lambda b: (b, 0, vcol)),
                  pl.BlockSpec((1, SAMPLE_T, GROUP_W), lambda b: (b, 0, zcol))],
        out_specs=row,
        out_shape=jax.ShapeDtypeStruct((db, SAMPLE_T, GROUP_W), F32),
        compiler_params=_cparams(("parallel",)),
        name="moba_merge",
    )(o_part, stats, q_rows, k_rope, proj, proj)


def _layer_weights(gla_w_gate_l, ml_b_i_l, ml_b_f_l):
    wg_pad = jnp.concatenate([gla_w_gate_l, jnp.zeros((LANES - GLA_GATE_RANK, gla_w_gate_l.shape[1]), F32)],
                             axis=0)
    bias_row = jnp.concatenate([jnp.zeros((SM_I,), F32), ml_b_i_l, ml_b_f_l,
                                jnp.zeros((LANES - SM_F - ML_HEADS,), F32)]).reshape(1, LANES)
    return wg_pad, bias_row


def _mixers(x, bsz, t, t_valid, layer, lw, rope_tables, gla_s0, ml_state, out_dtype):
    (norm_w, w_main, w_small, wg_pad, bias_row, gla_b_gate, gla_norm_w, ml_norm_w,
     gm_ln_w, gm_ln_b, gm_ws, gm_bs_t) = lw
    h, small = _rms_small(x, norm_w, w_small, layer)
    proj = _matmul(h, w_main, layer).reshape(bsz, t, N_MAIN)
    small = small.reshape(bsz, t, LANES)

    out_a, gla_s = _gla(proj, small, wg_pad, gla_b_gate, gla_norm_w, gla_s0, t_valid, out_dtype)

    gates = _gates(small, bias_row, t_valid)
    out_b, ml_c, ml_n, ml_m = _mlstm(proj, gates, ml_norm_w, *ml_state, out_dtype)

    out_c, vn = _gmlp(proj, gm_ln_w, gm_ln_b, gm_ws, gm_bs_t, out_dtype)

    q_rope = _rope(proj, _MAIN_OFF["a_q"], rope_tables)
    return proj, (out_a, out_b, out_c), q_rope, gla_s, (ml_c, ml_n, ml_m[..., :1]), vn


def kernel(x_prompt, x_sample, cache_k, cache_v, page_table, state_gla, state_mlstm_C, state_mlstm_n,
           state_mlstm_m, norm_w, w_in, gla_w_gate, gla_b_gate, gla_norm_w, ml_b_i, ml_b_f, ml_norm_w,
           gm_ln_w, gm_ln_b, gm_w_s, gm_b_s, w_out, final_norm_w):
    bp, tp, d = x_prompt.shape
    db, ts, _ = x_sample.shape
    depth = w_in.shape[0]
    page = cache_k.shape[2]
    past_len = page_table.shape[1] * page
    assert w_out.shape[1] == 4 * GROUP_W and ts <= SAMPLE_T
    assert past_len % MB_BLOCK == 0 and tp % MB_BLOCK == 0

    tables_p = _rope_tables(jnp.arange(tp, dtype=jnp.int32))
    tables_s = _rope_tables(past_len + jnp.arange(SAMPLE_T, dtype=jnp.int32))
    cache_k = cache_k.reshape(depth, cache_k.shape[1], page * MB_HEADS, MB_DH)
    cache_v = cache_v.reshape(depth, cache_v.shape[1], page * MB_HEADS, MB_DH)

    yp = x_prompt.reshape(bp * tp, d)
    ys = jnp.pad(x_sample, ((0, 0), (0, SAMPLE_T - ts), (0, 0))).reshape(db * SAMPLE_T, d)
    dt_p = BF16
    dt_s = F32

    zero_gla = jnp.zeros((bp, GLA_HEADS // 2, GLA_DV, LANES), F32)
    zero_ml = (jnp.zeros((bp, ML_HEADS, ML_DH, ML_DH), F32), jnp.zeros((bp, ML_HEADS, 1, ML_DH), F32),
               jnp.zeros((bp, ML_HEADS, 1, LANES), F32))

    outs = {n: [] for n in ("ks", "vs", "gp", "gs", "cp", "cs", "np", "ns", "mp", "ms", "vv")}
    lp = min(tp, GM_CHUNK)
    w_t = jnp.swapaxes(w_in, 1, 2)
    w_main = _wprep(w_t)
    w_small = _wsmall(w_t)
    w_out4 = w_out.reshape(depth, 4, GROUP_W, d)
    kv_stacks = None
    for l in range(depth):
        wg_pad, bias_row = _layer_weights(gla_w_gate[l], ml_b_i[l], ml_b_f[l])
        common = (norm_w[l], w_main, w_small, wg_pad, bias_row, gla_b_gate[l], gla_norm_w[l], ml_norm_w[l],
                  gm_ln_w[l], gm_ln_b[l])

        lw = common + (gm_w_s[l][:, :lp, :lp], gm_b_s[l][:, :lp].T)
        proj, mix, q_rope, gla_s, ml_s, _ = _mixers(
            yp, bp, tp, tp, l, lw, tables_p, zero_gla, zero_ml, dt_p)
        kv_stacks = _kv_heads(proj, tables_p, l, depth, kv_stacks)
        out_d = _moba_prompt(q_rope, kv_stacks[0], kv_stacks[1], l, proj, dt_p)
        yp = _out_proj([a.reshape(bp * tp, GROUP_W) for a in mix + (out_d,)], w_out4, l, yp)
        outs["gp"].append(_gla_state_from_pairs(gla_s))
        outs["cp"].append(ml_s[0])
        outs["np"].append(ml_s[1][:, :, 0, :])
        outs["mp"].append(ml_s[2][:, :, 0, 0])

        lw = common + (gm_w_s[l][:, :SAMPLE_T, :SAMPLE_T], gm_b_s[l][:, :SAMPLE_T].T)
        ml_state = (state_mlstm_C[l], state_mlstm_n[l][:, :, None, :],
                    jnp.broadcast_to(state_mlstm_m[l][:, :, None, None], (db, ML_HEADS, 1, LANES)))
        proj, mix, q_rope, gla_s, ml_s, vn = _mixers(
            ys, db, SAMPLE_T, ts, l, lw, tables_s, _gla_state_to_pairs(state_gla[l]), ml_state, dt_s)
        k_rope = _rope(proj, _MAIN_OFF["a_k"], tables_s)
        v_new = proj[:, :, _MAIN_OFF["a_v"]:_MAIN_OFF["a_v"] + GROUP_W]
        q_rows = (q_rope.reshape(db, SAMPLE_T, MB_HEADS, MB_DH).transpose(0, 2, 1, 3)
                  .reshape(db, QROWS, MB_DH))
        o_part, stats = _moba_past(l, q_rows, cache_k, cache_v, page_table)
        out_d = _moba_merge(o_part, stats, q_rows, k_rope, proj, ts)
        ys = _out_proj([a.reshape(db * SAMPLE_T, GROUP_W) for a in mix + (out_d,)], w_out4, l, ys)
        outs["ks"].append(k_rope[:, :ts].reshape(db, ts, MB_HEADS, MB_DH))
        outs["vs"].append(v_new[:, :ts].reshape(db, ts, MB_HEADS, MB_DH))
        outs["gs"].append(_gla_state_from_pairs(gla_s))
        outs["cs"].append(ml_s[0])
        outs["ns"].append(ml_s[1][:, :, 0, :])
        outs["ms"].append(ml_s[2][:, :, 0, 0])
        outs["vv"].append(vn[:, :ts])

    y_prompt = _rms(yp, final_norm_w).reshape(bp, tp, d)
    y_sample = _rms(ys, final_norm_w).reshape(db, SAMPLE_T, d)[:, :ts]
    st = jnp.stack
    k_prompt = kv_stacks[0].reshape(depth, bp, tp, MB_HEADS, MB_DH)
    v_prompt = kv_stacks[1].reshape(depth, bp, tp, MB_HEADS, MB_DH)
    return (y_prompt, y_sample, k_prompt, v_prompt, st(outs["ks"]), st(outs["vs"]),
            st(outs["gp"]), st(outs["gs"]), st(outs["cp"]), st(outs["cs"]), st(outs["np"]), st(outs["ns"]),
            st(outs["mp"]), st(outs["ms"]), st(outs["vv"]))
```

```python
import functools

import numpy as np
import jax
import jax.numpy as jnp
from jax import lax
from jax.experimental import pallas as pl
from jax.experimental.pallas import tpu as pltpu

F32 = jnp.float32
BF16 = jnp.bfloat16
HIGHEST = lax.Precision.HIGHEST

GROUP_W = 1024
GLA_HEADS, GLA_DK, GLA_DV = 8, 64, 128
GLA_GATE_RANK, GLA_GATE_NORM = 16, 16.0
ML_HEADS, ML_DH = 8, 128
GM_GROUPS, GM_CH, GM_CHUNK = 8, 128, 128
MB_HEADS, MB_DH, MB_BLOCK, MB_TOPK = 8, 128, 256, 3
ROT_DIM, ROPE_THETA = 32, 500000.0
EPS = 1e-6

LANES = 128
SUBLANES = 8
VMEM_LIMIT = 56 * 1024 * 1024

NEG = -1e30
SAMPLE_T = SUBLANES

_MAIN_ORDER = ("g_q", "g_k", "g_v", "g_z", "m_q", "m_k", "m_v", "m_o", "m_z",
               "c_u", "c_v", "c_z", "a_q", "a_k", "a_v", "a_z")
_SPLIT_NAMES = ("g_q", "g_k", "g_v", "g_lr", "g_z", "m_q", "m_k", "m_v", "m_i", "m_f", "m_o", "m_z",
                "c_u", "c_v", "c_z", "a_q", "a_k", "a_v", "a_z")
_SPLIT_W = (GLA_HEADS * GLA_DK, GLA_HEADS * GLA_DK, GLA_HEADS * GLA_DV, GLA_GATE_RANK, GROUP_W,
            GROUP_W, GROUP_W, GROUP_W, ML_HEADS, ML_HEADS, GROUP_W, GROUP_W,
            GROUP_W, GROUP_W, GROUP_W, GROUP_W, GROUP_W, GROUP_W, GROUP_W)
_SRC_OFF = dict(zip(_SPLIT_NAMES, np.concatenate([[0], np.cumsum(_SPLIT_W)[:-1]]).tolist()))
_SRC_W = dict(zip(_SPLIT_NAMES, _SPLIT_W))
_MAIN_OFF = {}
_off = 0
for _n in _MAIN_ORDER:
    _MAIN_OFF[_n] = _off
    _off += _SRC_W[_n]
N_MAIN = _off
SM_LR, SM_I, SM_F = 0, GLA_GATE_RANK, GLA_GATE_RANK + ML_HEADS


def _cparams(sem):
    return pltpu.CompilerParams(dimension_semantics=sem, vmem_limit_bytes=VMEM_LIMIT)


def _pick(n, cands):
    for c in cands:
        if n % c == 0:
            return c
    return n


def _silu(x):
    return x / (1.0 + jnp.exp(-x))


def _sigmoid(x):
    return 1.0 / (1.0 + jnp.exp(-x))


def _log_sigmoid(x):
    return jnp.minimum(x, 0.0) - jnp.log1p(jnp.exp(-jnp.abs(x)))


def _gelu(x):
    c = np.sqrt(2.0 / np.pi).astype(np.float32)
    return 0.5 * x * (1.0 + jnp.tanh(c * (x + 0.044715 * (x * x * x))))


def _dot_nt(a, b, precision=None):
    return lax.dot_general(a, b, (((1,), (1,)), ((), ())), preferred_element_type=F32, precision=precision)


def _dot_tn(a, b):
    return lax.dot_general(a, b, (((0,), (0,)), ((), ())), preferred_element_type=F32)


def _dot(a, b, precision=None):
    return jnp.dot(a, b, preferred_element_type=F32, precision=precision)


def _rms_small_kernel(x_ref, nw_ref, ws_ref, h_ref, sm_ref):
    x = x_ref[...]
    ms = jnp.mean(x * x, axis=-1, keepdims=True)
    hb = (x * lax.rsqrt(ms + EPS) * nw_ref[...]).astype(BF16)
    h_ref[...] = hb
    sm_ref[...] = _dot(hb, ws_ref[...])


def _rms_small(x, nw, w_small, layer):
    m, d = x.shape
    bm = _pick(m, (256, 128, 64, 32, 16, 8))
    return pl.pallas_call(
        _rms_small_kernel,
        grid=(m // bm,),
        in_specs=[pl.BlockSpec((bm, d), lambda i: (i, 0)),
                  pl.BlockSpec((1, d), lambda i: (0, 0)),
                  pl.BlockSpec((None, d, LANES), lambda i: (layer, 0, 0))],
        out_specs=[pl.BlockSpec((bm, d), lambda i: (i, 0)),
                   pl.BlockSpec((bm, LANES), lambda i: (i, 0))],
        out_shape=[jax.ShapeDtypeStruct((m, d), BF16), jax.ShapeDtypeStruct((m, LANES), F32)],
        compiler_params=_cparams(("parallel",)),
        name="rms_small",
    )(x, nw.reshape(1, d), w_small)


def _rms_kernel(x_ref, nw_ref, o_ref):
    x = x_ref[...]
    ms = jnp.mean(x * x, axis=-1, keepdims=True)
    o_ref[...] = x * lax.rsqrt(ms + EPS) * nw_ref[...]


def _rms(x, nw):
    m, d = x.shape
    bm = _pick(m, (256, 128, 64, 32, 16, 8))
    return pl.pallas_call(
        _rms_kernel,
        grid=(m // bm,),
        in_specs=[pl.BlockSpec((bm, d), lambda i: (i, 0)), pl.BlockSpec((1, d), lambda i: (0, 0))],
        out_specs=pl.BlockSpec((bm, d), lambda i: (i, 0)),
        out_shape=jax.ShapeDtypeStruct((m, d), F32),
        compiler_params=_cparams(("parallel",)),
        name="rms_final",
    )(x, nw.reshape(1, d))


WP_BN = 1024
WP_B1 = _MAIN_OFF["g_z"] // WP_BN
WP_B2 = _MAIN_OFF["m_o"] // WP_BN
WP_S1 = GLA_GATE_RANK
WP_S2 = GLA_GATE_RANK + 2 * ML_HEADS
assert _MAIN_OFF["g_z"] % WP_BN == 0 and _MAIN_OFF["m_o"] % WP_BN == 0 and N_MAIN % WP_BN == 0
assert WP_S1 % SUBLANES == 0 and WP_S2 % SUBLANES == 0 and sum(_SPLIT_W) % WP_S2 == 0


def _wprep_kernel(a_ref, t_ref, o_ref):
    j = pl.program_id(2)

    def emit(s):
        x = a_ref[...] if s == 0 else jnp.concatenate([a_ref[...], t_ref[...]], axis=0)[s:s + WP_BN]
        o_ref[...] = x.T.astype(BF16)

    @pl.when(j < WP_B1)
    def _():
        emit(0)

    @pl.when((j >= WP_B1) & (j < WP_B2))
    def _():
        emit(WP_S1)

    @pl.when(j >= WP_B2)
    def _():
        emit(WP_S2)


def _wprep(w_t):
    depth, _, d = w_t.shape
    kb = _pick(d, (512, 256, 128))
    return pl.pallas_call(
        _wprep_kernel,
        grid=(depth, d // kb, N_MAIN // WP_BN),
        in_specs=[pl.BlockSpec((None, WP_BN, kb), lambda l, i, j: (l, j, i)),
                  pl.BlockSpec((None, WP_S2, kb), lambda l, i, j: (l, (j + 1) * (WP_BN // WP_S2), i))],
        out_specs=pl.BlockSpec((None, kb, WP_BN), lambda l, i, j: (l, i, j)),
        out_shape=jax.ShapeDtypeStruct((depth, d, N_MAIN), BF16),
        compiler_params=_cparams(("parallel", "parallel", "parallel")),
        name="w_prep",
    )(w_t, w_t)


def _wsmall_kernel(lr_ref, if_ref, o_ref):
    kb = lr_ref.shape[1]
    x = jnp.concatenate([lr_ref[...], if_ref[...],
                         jnp.zeros((LANES - GLA_GATE_RANK - 2 * ML_HEADS, kb), F32)], axis=0)
    o_ref[...] = x.T.astype(BF16)


def _wsmall(w_t):
    depth, _, d = w_t.shape
    kb = _pick(d, (512, 256, 128))
    lr, gi = _SRC_OFF["g_lr"], _SRC_OFF["m_i"]
    assert _SRC_OFF["m_f"] == gi + ML_HEADS and lr % GLA_GATE_RANK == 0 and gi % (2 * ML_HEADS) == 0
    return pl.pallas_call(
        _wsmall_kernel,
        grid=(depth, d // kb),
        in_specs=[pl.BlockSpec((None, GLA_GATE_RANK, kb), lambda l, i: (l, lr // GLA_GATE_RANK, i)),
                  pl.BlockSpec((None, 2 * ML_HEADS, kb), lambda l, i: (l, gi // (2 * ML_HEADS), i))],
        out_specs=pl.BlockSpec((None, kb, LANES), lambda l, i: (l, i, 0)),
        out_shape=jax.ShapeDtypeStruct((depth, d, LANES), BF16),
        compiler_params=_cparams(("parallel", "parallel")),
        name="w_small",
    )(w_t, w_t)


def _mm_kernel(a_ref, b_ref, o_ref):
    o_ref[...] = _dot(a_ref[...], b_ref[...])


def _matmul(a, w_all, layer):
    m, k = a.shape
    n = w_all.shape[2]
    bm = _pick(m, (1024, 512, 256, 128, 64))
    bn = _pick(n, (1024, 512, 256, 128))
    return pl.pallas_call(
        _mm_kernel,
        grid=(n // bn, m // bm),
        in_specs=[pl.BlockSpec((bm, k), lambda j, i: (i, 0)),
                  pl.BlockSpec((None, k, bn), lambda j, i: (layer, 0, j))],
        out_specs=pl.BlockSpec((bm, bn), lambda j, i: (i, j)),
        out_shape=jax.ShapeDtypeStruct((m, n), F32),
        compiler_params=_cparams(("parallel", "parallel")),
        name="in_proj",
    )(a, w_all)


def _out_kernel(a0, a1, a2, a3, w_ref, x_ref, o_ref):
    acc = x_ref[...]
    for g, a in enumerate((a0, a1, a2, a3)):
        acc = acc + _dot(a[...].astype(BF16), w_ref[g].astype(BF16))
    o_ref[...] = acc


def _out_proj(mix, w_all, layer, x):
    m, d = x.shape
    bm = _pick(m, (1024, 512, 256, 128, 64))
    bn = _pick(d, (512, 256, 128))
    a_spec = pl.BlockSpec((bm, GROUP_W), lambda i, j: (i, 0))
    return pl.pallas_call(
        _out_kernel,
        grid=(m // bm, d // bn),
        in_specs=[a_spec, a_spec, a_spec, a_spec,
                  pl.BlockSpec((None, 4, GROUP_W, bn), lambda i, j: (layer, 0, 0, j)),
                  pl.BlockSpec((bm, bn), lambda i, j: (i, j))],
        out_specs=pl.BlockSpec((bm, bn), lambda i, j: (i, j)),
        out_shape=jax.ShapeDtypeStruct((m, d), F32),
        compiler_params=_cparams(("parallel", "parallel")),
        name="out_proj",
    )(*mix, w_all, x)


def _rope_kernel(x_ref, a_ref, b_ref, c_ref, o_ref):
    a, b, c = a_ref[...], b_ref[...], c_ref[...]
    for h in range(MB_HEADS):
        sl = slice(h * MB_DH, (h + 1) * MB_DH)
        x = x_ref[0, :, sl]
        o_ref[0, :, sl] = (x * a + pltpu.roll(x, MB_DH - ROT_DIM // 2, 1) * b
                           + pltpu.roll(x, ROT_DIM // 2, 1) * c)


def _rope_tables(pos):
    half = ROT_DIM // 2
    inv_freq = jnp.power(ROPE_THETA, -jnp.arange(0, ROT_DIM, 2, dtype=F32) / ROT_DIM)
    ang = pos.astype(F32)[:, None] * inv_freq[None, :]
    cos, sin = jnp.cos(ang), jnp.sin(ang)
    t = pos.shape[0]
    a = jnp.concatenate([cos, cos, jnp.ones((t, MB_DH - ROT_DIM), F32)], axis=1)
    b = jnp.concatenate([-sin, jnp.zeros((t, MB_DH - half), F32)], axis=1)
    c = jnp.concatenate([jnp.zeros((t, half), F32), sin, jnp.zeros((t, MB_DH - ROT_DIM), F32)], axis=1)
    return a, b, c


def _rope(proj, col, tables):
    bsz, t, _ = proj.shape
    bt = _pick(t, (512, 256, 128, 64, 32, 16, 8))
    tab = pl.BlockSpec((bt, MB_DH), lambda b, i: (i, 0))
    return pl.pallas_call(
        _rope_kernel,
        grid=(bsz, t // bt),
        in_specs=[pl.BlockSpec((1, bt, GROUP_W), lambda b, i: (b, i, col // GROUP_W)), tab, tab, tab],
        out_specs=pl.BlockSpec((1, bt, GROUP_W), lambda b, i: (b, i, 0)),
        out_shape=jax.ShapeDtypeStruct((bsz, t, GROUP_W), F32),
        compiler_params=_cparams(("parallel", "parallel")),
        name="rope",
    )(proj, *tables)


def _kv_heads_kernel(k_ref, v_ref, a_ref, b_ref, c_ref, *refs):
    k4_ref, v4_ref = refs[-2:]
    a, b, c = a_ref[...], b_ref[...], c_ref[...]
    bt = a.shape[0]
    for h in range(MB_HEADS):
        sl = slice(h * MB_DH, (h + 1) * MB_DH)
        x = k_ref[0, :, sl]
        rows = pl.ds(h, bt, stride=MB_HEADS)
        k4_ref[rows, :] = (x * a + pltpu.roll(x, MB_DH - ROT_DIM // 2, 1) * b
                           + pltpu.roll(x, ROT_DIM // 2, 1) * c)
        v4_ref[rows, :] = v_ref[0, :, sl]


def _kv_heads(proj, tables, layer, depth, stacks):
    bsz, t, _ = proj.shape
    bt = _pick(t, (256, 128, 64, 32, 16, 8))
    tab = pl.BlockSpec((bt, MB_DH), lambda b, i: (i, 0))
    kcol, vcol = _MAIN_OFF["a_k"] // GROUP_W, _MAIN_OFF["a_v"] // GROUP_W
    out = pl.BlockSpec((None, None, bt * MB_HEADS, MB_DH), lambda b, i: (layer, b, i, 0))
    shape = jax.ShapeDtypeStruct((depth, bsz, t * MB_HEADS, MB_DH), F32)
    in_specs = [pl.BlockSpec((1, bt, GROUP_W), lambda b, i: (b, i, kcol)),
                pl.BlockSpec((1, bt, GROUP_W), lambda b, i: (b, i, vcol)), tab, tab, tab]
    aliases = {}
    if stacks is not None:
        in_specs += [pl.BlockSpec(memory_space=pl.ANY), pl.BlockSpec(memory_space=pl.ANY)]
        aliases = {5: 0, 6: 1}
    return pl.pallas_call(
        _kv_heads_kernel,
        grid=(bsz, t // bt),
        in_specs=in_specs,
        out_specs=[out, out],
        out_shape=[shape, shape],
        input_output_aliases=aliases,
        compiler_params=_cparams(("parallel", "parallel")),
        name="kv_heads",
    )(proj, proj, *tables, *(stacks or ()))


def _gla_levels(c):
    return [c >> i for i in range(1, c.bit_length())]


def _gla_sum_matrices(c, nch):
    sup = c * nch
    t = np.arange(sup)[:, None]
    r = np.arange(sup)[None, :]
    cb = (t // c) * c
    mats = [(r >= cb) & (r <= t), (r > t) & (r <= cb + c - 1)]
    for hs in _gla_levels(c):
        base = (t // (2 * hs)) * (2 * hs)
        ref = base + hs - 1
        mats.append(np.where(t - base >= hs, (r > ref) & (r <= t), (r > t) & (r <= ref)))
    return np.concatenate(mats, axis=0).astype(np.float32)


def _gla_kernel(q_ref, k_ref, v_ref, z_ref, sm_ref, wg_ref, bg_ref, nw_ref, s0_ref, sum_ref,
                o_ref, s_ref, g_scr, *, t, c, nch, t_valid):
    dk = GLA_DK
    sup = c * nch
    levels = _gla_levels(c)
    split = sum_ref.dtype == BF16
    x = _dot(sm_ref[0], wg_ref[...], precision=HIGHEST) + bg_ref[...]
    g_all = _log_sigmoid(x) * (1.0 / GLA_GATE_NORM)
    if t_valid < t:
        rows_t = lax.broadcasted_iota(jnp.int32, (t, LANES), 0)
        g_all = jnp.where(rows_t < t_valid, g_all, 0.0)
    g_scr[...] = g_all

    lane = lax.broadcasted_iota(jnp.int32, (sup, LANES), 1)
    row = lax.broadcasted_iota(jnp.int32, (sup, LANES), 0)
    head0 = lane < dk
    rr = lax.broadcasted_iota(jnp.int32, (sup, sup), 0)
    cc = lax.broadcasted_iota(jnp.int32, (sup, sup), 1)
    mm = BF16 if split else F32
    nw = nw_ref[...]

    def body(si, s2t):
        r0 = pl.multiple_of(si * sup, sup)
        q = q_ref[0, pl.ds(r0, sup), :] * (dk ** -0.5)
        k = k_ref[0, pl.ds(r0, sup), :]
        if t_valid < t:
            k = jnp.where(row < t_valid, k, 0.0)
        g = g_scr[pl.ds(r0, sup), :]
        v = v_ref[0, pl.ds(r0, sup), :]
        z = z_ref[0, pl.ds(r0, sup), :]
        if split:
            g1 = g.astype(BF16)
            r1 = g - g1.astype(F32)
            g2 = r1.astype(BF16)
            g3 = (r1 - g2.astype(F32)).astype(BF16)
            xs = _dot(sum_ref[...], jnp.concatenate([g1, g2, g3], axis=1))
            xs = xs[:, :LANES] + xs[:, LANES:2 * LANES] + xs[:, 2 * LANES:]
        else:
            xs = _dot(sum_ref[...], g, precision=HIGHEST)
        b = xs[0:sup]
        to_end = xs[sup:2 * sup]

        qk = q * k
        att0 = jnp.where(rr == cc, jnp.sum(jnp.where(head0, qk, 0.0), axis=-1, keepdims=True), 0.0)
        att1 = jnp.where(rr == cc, jnp.sum(jnp.where(head0, 0.0, qk), axis=-1, keepdims=True), 0.0)
        for i, hs in enumerate(levels):
            e = jnp.exp(xs[(2 + i) * sup:(3 + i) * sup])
            upper = (row & hs) != 0
            qt = jnp.where(upper, q * e, 0.0)
            kl = jnp.where(upper, 0.0, k * e).astype(mm)
            sh = (2 * hs).bit_length() - 1
            same = (rr >> sh) == (cc >> sh)
            att0 = att0 + jnp.where(same, _dot_nt(jnp.where(head0, qt, 0.0).astype(mm), kl), 0.0)
            att1 = att1 + jnp.where(same, _dot_nt(jnp.where(head0, 0.0, qt).astype(mm), kl), 0.0)

        kt = k * jnp.exp(to_end)
        qe = q * jnp.exp(b)
        qe0, qe1 = jnp.where(head0, qe, 0.0), jnp.where(head0, 0.0, qe)
        kt0, kt1 = jnp.where(head0, kt, 0.0), jnp.where(head0, 0.0, kt)
        v0, v1 = v[:, :GLA_DV], v[:, GLA_DV:]
        rows = [slice(ch * c, (ch + 1) * c) for ch in range(nch)]
        upds = [_dot_tn(jnp.concatenate([v0[r], v1[r]], axis=0).astype(mm),
                        jnp.concatenate([kt0[r], kt1[r]], axis=0).astype(mm)) for r in rows]
        o_int = []
        for ch, r in enumerate(rows):
            lhs = jnp.concatenate([qe0[r], qe1[r]], axis=0).astype(mm)
            o_int.append(_dot_nt(lhs, s2t.astype(mm)))
            s2t = jnp.exp(b[ch * c + c - 1:ch * c + c, :]) * s2t + upds[ch]

        for h, (att, vh) in enumerate(((att0, v0), (att1, v1))):
            o = (jnp.concatenate([oi[h * c:(h + 1) * c, :] for oi in o_int], axis=0)
                 + _dot(att.astype(BF16), vh.astype(BF16)))
            on = o * lax.rsqrt(jnp.mean(o * o, axis=-1, keepdims=True) + EPS) * nw
            zh = z[:, h * GLA_DV:(h + 1) * GLA_DV]
            o_ref[0, pl.ds(r0, sup), h * GLA_DV:(h + 1) * GLA_DV] = (on * _silu(zh)).astype(o_ref.dtype)
        return s2t

    s_ref[0, 0] = lax.fori_loop(0, t // sup, body, s0_ref[0, 0])


def _gla(proj, small, wg_pad, bg, nw, s0t, t_valid, out_dtype):
    bsz, t, _ = proj.shape
    c = min(LANES, t)
    nch = 1
    pairs = GLA_HEADS // 2
    qb, kb = _MAIN_OFF["g_q"] // LANES, _MAIN_OFF["g_k"] // LANES
    vb, zb = _MAIN_OFF["g_v"] // (2 * GLA_DV), _MAIN_OFF["g_z"] // (2 * GLA_DV)
    kern = functools.partial(_gla_kernel, t=t, c=c, nch=nch, t_valid=t_valid)
    sums = _gla_sum_matrices(c, nch)
    sums = jnp.asarray(sums, BF16 if (c * nch) % 16 == 0 else F32)
    return pl.pallas_call(
        kern,
        grid=(bsz, pairs),
        in_specs=[pl.BlockSpec((1, t, LANES), lambda b, p: (b, 0, qb + p)),
                  pl.BlockSpec((1, t, LANES), lambda b, p: (b, 0, kb + p)),
                  pl.BlockSpec((1, t, 2 * GLA_DV), lambda b, p: (b, 0, vb + p)),
                  pl.BlockSpec((1, t, 2 * GLA_DV), lambda b, p: (b, 0, zb + p)),
                  pl.BlockSpec((1, t, LANES), lambda b, p: (b, 0, 0)),
                  pl.BlockSpec((LANES, LANES), lambda b, p: (0, p)),
                  pl.BlockSpec((1, LANES), lambda b, p: (0, p)),
                  pl.BlockSpec((1, GLA_DV), lambda b, p: (0, 0)),
                  pl.BlockSpec((1, 1, GLA_DV, LANES), lambda b, p: (b, p, 0, 0)),
                  pl.BlockSpec(sums.shape, lambda b, p: (0, 0))],
        out_specs=[pl.BlockSpec((1, t, 2 * GLA_DV), lambda b, p: (b, 0, p)),
                   pl.BlockSpec((1, 1, GLA_DV, LANES), lambda b, p: (b, p, 0, 0))],
        out_shape=[jax.ShapeDtypeStruct((bsz, t, GROUP_W), out_dtype),
                   jax.ShapeDtypeStruct((bsz, pairs, GLA_DV, LANES), F32)],
        scratch_shapes=[pltpu.VMEM((t, LANES), F32)],
        compiler_params=_cparams(("parallel", "parallel")),
        name="gla",
    )(proj, proj, proj, proj, small, wg_pad, bg.reshape(1, -1), nw.reshape(1, -1), s0t, sums)


def _gla_state_to_pairs(s):
    bsz = s.shape[0]
    s = s.reshape(bsz, GLA_HEADS // 2, 2, GLA_DK, GLA_DV)
    return s.transpose(0, 1, 4, 2, 3).reshape(bsz, GLA_HEADS // 2, GLA_DV, 2 * GLA_DK)


def _gla_state_from_pairs(s):
    bsz = s.shape[0]
    s = s.reshape(bsz, GLA_HEADS // 2, GLA_DV, 2, GLA_DK)
    return s.transpose(0, 1, 3, 4, 2).reshape(bsz, GLA_HEADS, GLA_DK, GLA_DV)


def _gates_kernel(sm_ref, bias_ref, o_ref, *, t_valid):
    x = sm_ref[0] + bias_ref[...]
    lane = lax.broadcasted_iota(jnp.int32, x.shape, 1)
    is_f = (lane >= SM_F) & (lane < SM_F + ML_HEADS)
    out = jnp.where(is_f, _log_sigmoid(x), x)
    if t_valid is not None:
        row = lax.broadcasted_iota(jnp.int32, x.shape, 0)
        out = jnp.where(row < t_valid, out, jnp.where(is_f, 0.0, NEG))
    o_ref[0] = out


def _gates(small, bias_row, t_valid):
    bsz, t, _ = small.shape
    bt = t if t_valid < t else _pick(t, (1024, 512, 256, 128, 64, 32, 16, 8))
    return pl.pallas_call(
        functools.partial(_gates_kernel, t_valid=t_valid if t_valid < t else None),
        grid=(bsz, t // bt),
        in_specs=[pl.BlockSpec((1, bt, LANES), lambda b, i: (b, i, 0)),
                  pl.BlockSpec((1, LANES), lambda b, i: (0, 0))],
        out_specs=pl.BlockSpec((1, bt, LANES), lambda b, i: (b, i, 0)),
        out_shape=jax.ShapeDtypeStruct((bsz, t, LANES), F32),
        compiler_params=_cparams(("parallel", "parallel")),
        name="ml_gates",
    )(small, bias_row)


ML_PAIR = 2


def _mlstm_kernel(q_ref, k_ref, v_ref, og_ref, z_ref, gt_ref, nw_ref,
                  c0_ref, n0_ref, m0_ref, o_ref, c_ref, n_ref, m_ref, t_scr, *, t, c):
    h0 = pl.program_id(1) * ML_PAIR
    hs = range(ML_PAIR)
    lane = lax.broadcasted_iota(jnp.int32, (c, LANES), 1)
    rr = lax.broadcasted_iota(jnp.int32, (c, c), 0)
    cc = lax.broadcasted_iota(jnp.int32, (c, c), 1)
    causal = rr >= cc
    nw = nw_ref[...]
    hsl = lambda j: slice(j * ML_DH, (j + 1) * ML_DH)

    def body(ci, carry):
        cms, ns, m_prevs = carry
        r0 = pl.multiple_of(ci * c, c)
        gt = gt_ref[0, pl.ds(r0, c), :]
        if c == LANES:
            t_scr[...] = gt.T
        qs = [q_ref[0, pl.ds(r0, c), hsl(j)] * (ML_DH ** -0.5) for j in hs]
        ks = [k_ref[0, pl.ds(r0, c), hsl(j)] for j in hs]
        vs = [v_ref[0, pl.ds(r0, c), hsl(j)] for j in hs]
        qbs = [q.astype(BF16) for q in qs]
        kbs = [k.astype(BF16) for k in ks]
        s_qk = [_dot_nt(qbs[j], kbs[j]) for j in hs]
        s_qc = [_dot_nt(qbs[j], cms[j].astype(BF16)) for j in hs]

        d, inter, m_t, i_cols, fc_cols = [], [], [], [], []
        for j in hs:
            i_col = jnp.sum(jnp.where(lane == SM_I + h0 + j, gt, 0.0), axis=-1, keepdims=True)
            f_col = jnp.sum(jnp.where(lane == SM_F + h0 + j, gt, 0.0), axis=-1, keepdims=True)
            if c == LANES:
                i_row = t_scr[pl.ds(SM_I + h0 + j, 1), :]
                f_row = t_scr[pl.ds(SM_F + h0 + j, 1), :]
            else:
                i_row = jnp.sum(jnp.where(rr == cc, i_col, 0.0), axis=0, keepdims=True)
                f_row = jnp.sum(jnp.where(rr == cc, f_col, 0.0), axis=0, keepdims=True)
            fc_col = jnp.sum(jnp.where(causal, f_row, 0.0), axis=-1, keepdims=True)
            fc_row = jnp.sum(jnp.where(rr <= cc, f_col, 0.0), axis=0, keepdims=True)
            dj = jnp.where(causal, fc_col - fc_row + i_row, NEG)
            d.append(dj)
            inter.append(fc_col + m_prevs[j])
            m_t.append(jnp.maximum(inter[j], jnp.max(dj, axis=-1, keepdims=True)))
            i_cols.append(i_col)
            fc_cols.append(fc_col)

        w_state = [jnp.exp(inter[j] - m_t[j]) for j in hs]
        qk = [s_qk[j] * jnp.exp(d[j] - m_t[j]) for j in hs]
        pv = [_dot(qk[j].astype(BF16), vs[j].astype(BF16)) for j in hs]
        w_end, dec, m_new = [], [], []
        for j in hs:
            m_new.append(m_t[j][c - 1:c, :])
            f_end = fc_cols[j][c - 1:c, :]
            w_end.append(jnp.exp(f_end - fc_cols[j] + i_cols[j] - m_new[j]))
            dec.append(jnp.exp(f_end + m_prevs[j] - m_new[j]))
        upd = [_dot_tn((vs[j] * w_end[j]).astype(BF16), kbs[j]) for j in hs]

        c_new, n_new = [], []
        for j in hs:
            num = w_state[j] * s_qc[j] + pv[j]
            den = (w_state[j] * jnp.sum(qs[j] * ns[j], axis=-1, keepdims=True)
                   + jnp.sum(qk[j], axis=-1, keepdims=True))
            hout = num / jnp.maximum(jnp.abs(den), jnp.exp(-m_t[j]))
            c_new.append(dec[j] * cms[j] + upd[j])
            n_new.append(dec[j] * ns[j] + jnp.sum(ks[j] * w_end[j], axis=0, keepdims=True))
            o = _sigmoid(og_ref[0, pl.ds(r0, c), hsl(j)]) * hout
            on = o * lax.rsqrt(jnp.mean(o * o, axis=-1, keepdims=True) + EPS) * nw
            o_ref[0, pl.ds(r0, c), hsl(j)] = (on * _silu(z_ref[0, pl.ds(r0, c), hsl(j)])).astype(o_ref.dtype)
        return tuple(c_new), tuple(n_new), tuple(m_new)

    init = (tuple(c0_ref[0, j] for j in hs), tuple(n0_ref[0, j] for j in hs),
            tuple(m0_ref[0, j][:, :1] for j in hs))
    cms, ns, ms = lax.fori_loop(0, t // c, body, init)
    for j in hs:
        c_ref[0, j] = cms[j]
        n_ref[0, j] = ns[j]
        m_ref[0, j] = jnp.broadcast_to(ms[j], (1, LANES))


def _split3(x):
    a = x.astype(BF16)
    r = x - a.astype(F32)
    b = r.astype(BF16)
    return a, b, (r - b.astype(F32)).astype(BF16)


def _mlstm_wide_kernel(q_ref, k_ref, v_ref, og_ref, z_ref, gt_ref, nw_ref, c0_ref, n0_ref, m0_ref,
                       o_ref, c_ref, n_ref, m_ref, t_scr, g_scr, i_scr, fc_scr, *, t):
    c = LANES
    nc = t // c
    h0 = pl.program_id(1) * ML_PAIR
    hs = range(ML_PAIR)
    rr = lax.broadcasted_iota(jnp.int32, (c, c), 0)
    cc = lax.broadcasted_iota(jnp.int32, (c, c), 1)
    causal = rr >= cc
    nw = nw_ref[...]
    hsl = lambda j: slice(j * ML_DH, (j + 1) * ML_DH)
    ones_b = jnp.ones((c, ML_DH), BF16)
    tril_b = jnp.where(causal, 1.0, 0.0).astype(BF16)

    lane_t = lax.broadcasted_iota(jnp.int32, (c, LANES), 1)
    for ci in range(nc):
        rows = slice(ci * c, (ci + 1) * c)
        g = gt_ref[0, rows, :]
        g_scr[rows, :] = jnp.where(lane_t < SM_F, g, sum(_dot(tril_b, x) for x in _split3(g)))
    sel_r = lax.broadcasted_iota(jnp.int32, (LANES, 2 * LANES), 0)
    sel_c = lax.broadcasted_iota(jnp.int32, (LANES, 2 * LANES), 1)
    g_terms = _split3(g_scr[...])
    for j in hs:
        src = jnp.where(sel_c < LANES, SM_I + h0 + j, SM_F + h0 + j)
        sel = jnp.where(sel_r == src, 1.0, 0.0).astype(BF16)
        both = sum(_dot(g, sel) for g in g_terms)
        i_scr[j] = both[:, :LANES]
        fc_scr[j] = both[:, LANES:]

    def body(ci, carry):
        cms, ns, m_prevs = carry
        r0 = pl.multiple_of(ci * c, c)
        t_scr[...] = g_scr[pl.ds(r0, c), :].T
        qbs = [(q_ref[0, pl.ds(r0, c), hsl(j)] * (ML_DH ** -0.5)).astype(BF16) for j in hs]
        kbs = [k_ref[0, pl.ds(r0, c), hsl(j)].astype(BF16) for j in hs]
        vs = [v_ref[0, pl.ds(r0, c), hsl(j)] for j in hs]
        s_qk = [_dot_nt(qbs[j], kbs[j]) for j in hs]
        s_qc = [_dot_nt(qbs[j], jnp.concatenate([cms[j], jnp.broadcast_to(ns[j], (c, ML_DH))],
                                                axis=0).astype(BF16)) for j in hs]

        d, inter, m_t, fcs = [], [], [], []
        for j in hs:
            fc = fc_scr[j, pl.ds(r0, c), :]
            i_row = t_scr[pl.ds(SM_I + h0 + j, 1), :]
            fc_row = t_scr[pl.ds(SM_F + h0 + j, 1), :]
            dj = jnp.where(causal, fc - fc_row + i_row, NEG)
            d.append(dj)
            inter.append(fc + m_prevs[j])
            m_t.append(jnp.maximum(inter[j], jnp.max(dj, axis=-1, keepdims=True)))
            fcs.append(fc)

        w_state = [jnp.exp(inter[j] - m_t[j]) for j in hs]
        qk = [(s_qk[j] * jnp.exp(d[j] - m_t[j])).astype(BF16) for j in hs]
        pv = [_dot(qk[j], jnp.concatenate([vs[j].astype(BF16), ones_b], axis=1)) for j in hs]
        w_end, dec, m_new = [], [], []
        for j in hs:
            m_new.append(m_t[j][c - 1:c, :1])
            f_end = fcs[j][c - 1:c, :]
            w_end.append(jnp.exp(f_end - fcs[j] + i_scr[j, pl.ds(r0, c), :] - m_new[j]))
            dec.append(jnp.exp(f_end[:, :1] + m_prevs[j] - m_new[j]))
        upd = [_dot_tn(jnp.concatenate([vs[j] * w_end[j], w_end[j]], axis=1).astype(BF16), kbs[j])
               for j in hs]

        c_new, n_new = [], []
        hout = []
        for j in hs:
            num = w_state[j] * s_qc[j][:, :ML_DH] + pv[j][:, :ML_DH]
            den = w_state[j] * s_qc[j][:, ML_DH:] + pv[j][:, ML_DH:]
            hout.append(_sigmoid(og_ref[0, pl.ds(r0, c), hsl(j)]) * num
                        / jnp.maximum(jnp.abs(den), jnp.exp(-m_t[j])))
            c_new.append(dec[j] * cms[j] + upd[j][:ML_DH, :])
            n_new.append(dec[j] * ns[j] + upd[j][ML_DH:ML_DH + 1, :])
        msq = [_dot((o * o).astype(BF16), ones_b) * (1.0 / ML_DH) for o in hout]
        for j in hs:
            on = hout[j] * lax.rsqrt(msq[j] + EPS) * nw
            o_ref[0, pl.ds(r0, c), hsl(j)] = (on * _silu(z_ref[0, pl.ds(r0, c), hsl(j)])).astype(o_ref.dtype)
        return tuple(c_new), tuple(n_new), tuple(m_new)

    init = (tuple(c0_ref[0, j] for j in hs), tuple(n0_ref[0, j] for j in hs),
            tuple(m0_ref[0, j][:, :1] for j in hs))
    cms, ns, ms = lax.fori_loop(0, nc, body, init)
    for j in hs:
        c_ref[0, j] = cms[j]
        n_ref[0, j] = ns[j]
        m_ref[0, j] = jnp.broadcast_to(ms[j], (1, LANES))


def _mlstm(proj, gates, nw, c0, n0, m0, out_dtype):
    bsz, t, _ = proj.shape
    c = min(128, t)
    w = ML_PAIR * ML_DH
    col = lambda name: _MAIN_OFF[name] // w
    spec = lambda name: pl.BlockSpec((1, t, w), lambda b, p, o=col(name): (b, 0, o + p))
    st = lambda rows: pl.BlockSpec((1, ML_PAIR, rows, ML_DH), lambda b, p: (b, p, 0, 0))
    if c == LANES:
        kern = functools.partial(_mlstm_wide_kernel, t=t)
        scratch = ([pltpu.VMEM((LANES, LANES), F32), pltpu.VMEM((t, LANES), F32)]
                   + [pltpu.VMEM((ML_PAIR, t, LANES), F32)] * 2)
    else:
        kern = functools.partial(_mlstm_kernel, t=t, c=c)
        scratch = [pltpu.VMEM((LANES, LANES), F32)]
    return pl.pallas_call(
        kern,
        grid=(bsz, ML_HEADS // ML_PAIR),
        in_specs=[spec("m_q"), spec("m_k"), spec("m_v"), spec("m_o"), spec("m_z"),
                  pl.BlockSpec((1, t, LANES), lambda b, p: (b, 0, 0)),
                  pl.BlockSpec((1, ML_DH), lambda b, p: (0, 0)),
                  st(ML_DH), st(1), st(1)],
        out_specs=[pl.BlockSpec((1, t, w), lambda b, p: (b, 0, p)), st(ML_DH), st(1), st(1)],
        out_shape=[jax.ShapeDtypeStruct((bsz, t, GROUP_W), out_dtype),
                   jax.ShapeDtypeStruct((bsz, ML_HEADS, ML_DH, ML_DH), F32),
                   jax.ShapeDtypeStruct((bsz, ML_HEADS, 1, ML_DH), F32),
                   jax.ShapeDtypeStruct((bsz, ML_HEADS, 1, LANES), F32)],
        scratch_shapes=scratch,
        compiler_params=_cparams(("parallel", "parallel")),
        name="mlstm",
    )(proj, proj, proj, proj, proj, gates, nw.reshape(1, -1), c0, n0, m0)


def _gmlp_kernel(u_ref, v_ref, z_ref, lw_ref, lb_ref, ws_ref, bs_ref, o_ref, vn_ref, *, l):
    gv = _gelu(v_ref[0])
    mu = jnp.mean(gv, axis=-1, keepdims=True)
    xc = gv - mu
    vn = xc * lax.rsqrt(jnp.mean(xc * xc, axis=-1, keepdims=True) + EPS) * lw_ref[...] + lb_ref[...]
    vn_ref[0] = vn
    rr = lax.broadcasted_iota(jnp.int32, (l, l), 0)
    cc = lax.broadcasted_iota(jnp.int32, (l, l), 1)
    for g in range(GM_GROUPS):
        sl = slice(g * GM_CH, (g + 1) * GM_CH)
        w = jnp.where(rr >= cc, ws_ref[g], 0.0)
        vg = vn[:, sl]
        if l >= GM_CHUNK:
            s = _dot(w.astype(BF16), vg.astype(BF16))
        else:
            s = jnp.zeros((l, GM_CH), F32)
            for r in range(l):
                s = s + w[:, r:r + 1] * vg[r:r + 1, :]
        s = s + bs_ref[:, g:g + 1]
        o_ref[0, :, sl] = (_gelu(u_ref[0, :, sl]) * s * _silu(z_ref[0, :, sl])).astype(o_ref.dtype)


def _gmlp(proj, lw, lb, ws, bs_t, out_dtype):
    bsz, t, _ = proj.shape
    l = min(t, GM_CHUNK)
    col = lambda name: _MAIN_OFF[name] // GROUP_W
    spec = lambda name: pl.BlockSpec((1, l, GROUP_W), lambda b, i, o=col(name): (b, i, o))
    return pl.pallas_call(
        functools.partial(_gmlp_kernel, l=l),
        grid=(bsz, t // l),
        in_specs=[spec("c_u"), spec("c_v"), spec("c_z"),
                  pl.BlockSpec((1, GROUP_W), lambda b, i: (0, 0)),
                  pl.BlockSpec((1, GROUP_W), lambda b, i: (0, 0)),
                  pl.BlockSpec((GM_GROUPS, l, l), lambda b, i: (0, 0, 0)),
                  pl.BlockSpec((l, GM_GROUPS), lambda b, i: (0, 0))],
        out_specs=[pl.BlockSpec((1, l, GROUP_W), lambda b, i: (b, i, 0)),
                   pl.BlockSpec((1, l, GROUP_W), lambda b, i: (b, i, 0))],
        out_shape=[jax.ShapeDtypeStruct((bsz, t, GROUP_W), out_dtype),
                   jax.ShapeDtypeStruct((bsz, t, GROUP_W), F32)],
        compiler_params=_cparams(("parallel", "parallel")),
        name="gmlp",
    )(proj, proj, proj, lw.reshape(1, -1), lb.reshape(1, -1), ws, bs_t)


def _moba_prompt_kernel(q_ref, k_ref, v_ref, z_ref, o_ref, kb_scr, vb_scr, km_scr, *, t):
    blk = MB_BLOCK
    nb = t // blk
    scale = MB_DH ** -0.5
    h = pl.program_id(1)
    km_scr[...] = jnp.zeros((LANES, MB_DH), F32)
    for n in range(nb):
        rows = pl.ds(n * blk * MB_HEADS + h, blk, stride=MB_HEADS)
        kn = k_ref[rows, :]
        kb_scr[n * blk:(n + 1) * blk, :] = kn.astype(BF16)
        vb_scr[n * blk:(n + 1) * blk, :] = v_ref[rows, :].astype(BF16)
        km_scr[n:n + 1, :] = jnp.mean(kn, axis=0, keepdims=True)
    kmean = km_scr[...]
    lane = lax.broadcasted_iota(jnp.int32, (blk, LANES), 1)
    rr = lax.broadcasted_iota(jnp.int32, (blk, blk), 0)
    cc = lax.broadcasted_iota(jnp.int32, (blk, blk), 1)

    for qi in range(nb):
        rows = slice(qi * blk, (qi + 1) * blk)
        q = q_ref[0, rows, :]
        qb = q.astype(BF16)
        bias = None
        if qi > MB_TOPK:
            gate = _dot_nt(q, kmean, precision=HIGHEST)
            cnt = jnp.zeros((blk, LANES), F32)
            for m in range(qi):
                gm = gate[:, m:m + 1]
                beats = (gm > gate) | ((gm == gate) & (lane > m))
                cnt = cnt + jnp.where(beats, 1.0, 0.0)
            bias = jnp.where(cnt < MB_TOPK, 0.0, NEG)
        ss = []
        for j in range(qi + 1):
            s = _dot_nt(qb, kb_scr[j * blk:(j + 1) * blk, :]) * scale
            if j == qi:
                s = jnp.where(cc <= rr, s, NEG)
            elif bias is not None:
                s = s + bias[:, j:j + 1]
            ss.append(s)
        lane_tiles = lambda xs: [x[:, i:i + LANES] for x in xs for i in range(0, blk, LANES)]
        m_i = jnp.max(functools.reduce(jnp.maximum, lane_tiles(ss)), axis=-1, keepdims=True)
        ps = [jnp.exp(s - m_i) for s in ss]
        l_i = jnp.sum(functools.reduce(jnp.add, lane_tiles(ps)), axis=-1, keepdims=True)
        acc = _dot(ps[0].astype(BF16), vb_scr[0:blk, :])
        for j in range(1, qi + 1):
            acc = acc + _dot(ps[j].astype(BF16), vb_scr[j * blk:(j + 1) * blk, :])
        o_ref[0, rows, :] = ((acc / l_i) * _silu(z_ref[0, rows, :])).astype(o_ref.dtype)


def _moba_prompt(q_rope, k4, v4, layer, proj, out_dtype):
    bsz, t, _ = proj.shape
    assert t % MB_BLOCK == 0 and t // MB_BLOCK <= LANES
    zcol = _MAIN_OFF["a_z"] // MB_DH
    hd = pl.BlockSpec((1, t, MB_DH), lambda b, h: (b, 0, h))
    kv = pl.BlockSpec((None, None, t * MB_HEADS, MB_DH), lambda b, h: (layer, b, 0, 0))
    return pl.pallas_call(
        functools.partial(_moba_prompt_kernel, t=t),
        grid=(bsz, MB_HEADS),
        in_specs=[hd, kv, kv, pl.BlockSpec((1, t, MB_DH), lambda b, h: (b, 0, zcol + h))],
        out_specs=hd,
        out_shape=jax.ShapeDtypeStruct((bsz, t, GROUP_W), out_dtype),
        scratch_shapes=[pltpu.VMEM((t, MB_DH), BF16), pltpu.VMEM((t, MB_DH), BF16),
                        pltpu.VMEM((LANES, MB_DH), F32)],
        compiler_params=_cparams(("parallel", "arbitrary")),
        name="moba_prompt",
    )(q_rope, k4, v4, proj)


QROWS = MB_HEADS * SAMPLE_T
ST_M, ST_L, ST_G = 0, 1, 2


def _moba_past_kernel(pt_ref, q_ref, bias_ref, *refs, nbs):
    del pt_ref
    npg = 2 * nbs
    k_refs, v_refs = refs[:npg], refs[npg:2 * npg]
    o_ref, st_ref = refs[2 * npg:]
    scale = MB_DH ** -0.5
    page = k_refs[0].shape[0] // MB_HEADS
    q = q_ref[0]
    qb = q.astype(BF16)
    bias = bias_ref[...]
    lane = lax.broadcasted_iota(jnp.int32, (QROWS, LANES), 1)
    ss = [_dot_nt(qb, r[...].astype(BF16)) * scale + bias for r in k_refs]
    ps, ms, ls = [], [], []
    for n in range(nbs):
        s0, s1 = ss[2 * n], ss[2 * n + 1]
        m = jnp.maximum(jnp.max(s0, axis=-1, keepdims=True), jnp.max(s1, axis=-1, keepdims=True))
        p0, p1 = jnp.exp(s0 - m), jnp.exp(s1 - m)
        ps += [p0.astype(BF16), p1.astype(BF16)]
        ms.append(m)
        ls.append(jnp.sum(p0, axis=-1, keepdims=True) + jnp.sum(p1, axis=-1, keepdims=True))
    for n in range(nbs):
        o_ref[n] = (_dot(ps[2 * n], v_refs[2 * n][...].astype(BF16))
                    + _dot(ps[2 * n + 1], v_refs[2 * n + 1][...].astype(BF16)))
        ksum = (jnp.sum(k_refs[2 * n][...].reshape(page, MB_HEADS, MB_DH), axis=0)
                + jnp.sum(k_refs[2 * n + 1][...].reshape(page, MB_HEADS, MB_DH), axis=0))
        ksum_rows = jnp.concatenate(
            [jnp.broadcast_to(ksum[h:h + 1, :], (SAMPLE_T, MB_DH)) for h in range(MB_HEADS)], axis=0)
        gate = jnp.sum(q * ksum_rows, axis=-1, keepdims=True) * (1.0 / MB_BLOCK)
        st_ref[n] = jnp.where(lane == ST_M, ms[n], jnp.where(lane == ST_L, ls[n], gate))


def _moba_past(layer, q_rows, cache_k, cache_v, page_table):
    db = q_rows.shape[0]
    rows = cache_k.shape[2]
    assert 2 * rows == MB_BLOCK * MB_HEADS
    nb = page_table.shape[1] // 2
    nbs = _pick(nb, (4, 2, 1))
    key_head = np.arange(rows) % MB_HEADS
    row_head = np.arange(QROWS) // SAMPLE_T
    bias = jnp.asarray(np.where(key_head[None, :] == row_head[:, None], 0.0, NEG).astype(np.float32))
    pg = lambda i: pl.BlockSpec((None, None, rows, MB_DH),
                                lambda b, n, pt, i=i: (layer, pt[b, 2 * nbs * n + i], 0, 0))
    pages = [pg(i) for i in range(2 * nbs)]
    part = pl.BlockSpec((None, nbs, QROWS, MB_DH), lambda b, n, pt: (b, n, 0, 0))
    shape = jax.ShapeDtypeStruct((db, nb, QROWS, MB_DH), F32)
    grid_spec = pltpu.PrefetchScalarGridSpec(
        num_scalar_prefetch=1,
        grid=(db, nb // nbs),
        in_specs=[pl.BlockSpec((1, QROWS, MB_DH), lambda b, n, pt: (b, 0, 0)),
                  pl.BlockSpec((QROWS, rows), lambda b, n, pt: (0, 0))] + pages + pages,
        out_specs=[part, part],
    )
    return pl.pallas_call(
        functools.partial(_moba_past_kernel, nbs=nbs),
        grid_spec=grid_spec,
        out_shape=[shape, shape],
        compiler_params=_cparams(("parallel", "parallel")),
        name="moba_past",
    )(page_table, q_rows, bias, *([cache_k] * (2 * nbs)), *([cache_v] * (2 * nbs)))


def _moba_merge_kernel(op_ref, st_ref, q_ref, k_ref, v_ref, z_ref, o_ref, *, nb, t_valid):
    scale = MB_DH ** -0.5
    g = st_ref[0, :, :, ST_G:ST_G + 1]
    m = st_ref[0, :, :, ST_M:ST_M + 1]
    l = st_ref[0, :, :, ST_L:ST_L + 1]
    nidx = lax.broadcasted_iota(jnp.int32, g.shape, 0)
    sel = jnp.zeros(g.shape, jnp.bool_)
    gm = g
    for _ in range(min(MB_TOPK, nb)):
        mx = jnp.max(gm, axis=0, keepdims=True)
        first = jnp.min(jnp.where(gm == mx, nidx, nb), axis=0, keepdims=True)
        pick = nidx == first
        sel = sel | pick
        gm = jnp.where(pick, -jnp.inf, gm)

    rr = lax.broadcasted_iota(jnp.int32, (SAMPLE_T, SAMPLE_T), 0)
    cc = lax.broadcasted_iota(jnp.int32, (SAMPLE_T, SAMPLE_T), 1)
    mo, lo, oo = [], [], []
    for h in range(MB_HEADS):
        sl = slice(h * MB_DH, (h + 1) * MB_DH)
        qh = q_ref[0, h * SAMPLE_T:(h + 1) * SAMPLE_T, :]
        s = _dot_nt(qh.astype(BF16), k_ref[0, :, sl].astype(BF16)) * scale
        s = jnp.where((cc <= rr) & (cc < t_valid), s, NEG)
        mh = jnp.max(s, axis=-1, keepdims=True)
        p = jnp.exp(s - mh)
        mo.append(mh)
        lo.append(jnp.sum(p, axis=-1, keepdims=True))
        oo.append(_dot(p.astype(BF16), v_ref[0, :, sl].astype(BF16)))
    m_own, l_own, o_own = (jnp.concatenate(x, axis=0) for x in (mo, lo, oo))

    m_tot = jnp.maximum(jnp.max(jnp.where(sel, m, NEG), axis=0), m_own)
    w = jnp.where(sel, jnp.exp(m - m_tot[None]), 0.0)
    w_own = jnp.exp(m_own - m_tot)
    den = jnp.sum(w * l, axis=0) + w_own * l_own
    acc = w_own * o_own
    for n in range(nb):
        acc = acc + w[n] * op_ref[0, n]
    o = acc / den
    for h in range(MB_HEADS):
        sl = slice(h * MB_DH, (h + 1) * MB_DH)
        o_ref[0, :, sl] = o[h * SAMPLE_T:(h + 1) * SAMPLE_T, :] * _silu(z_ref[0, :, sl])


def _moba_merge(o_part, stats, q_rows, k_rope, proj, t_valid):
    db, nb = o_part.shape[:2]
    vcol, zcol = _MAIN_OFF["a_v"] // GROUP_W, _MAIN_OFF["a_z"] // GROUP_W
    part = pl.BlockSpec((1, nb, QROWS, MB_DH), lambda b: (b, 0, 0, 0))
    row = pl.BlockSpec((1, SAMPLE_T, GROUP_W), lambda b: (b, 0, 0))
    return pl.pallas_call(
        functools.partial(_moba_merge_kernel, nb=nb, t_valid=t_valid),
        grid=(db,),
        in_specs=[part, part, pl.BlockSpec((1, QROWS, MB_DH), lambda b: (b, 0, 0)), row,
                  pl.BlockSpec((1, SAMPLE_T, GROUP_W), lambda b: (b, 0, vcol)),
                  pl.BlockSpec((1, SAMPLE_T, GROUP_W), lambda b: (b, 0, zcol))],
        out_specs=row,
        out_shape=jax.ShapeDtypeStruct((db, SAMPLE_T, GROUP_W), F32),
        compiler_params=_cparams(("parallel",)),
        name="moba_merge",
    )(o_part, stats, q_rows, k_rope, proj, proj)


def _layer_weights(gla_w_gate_l, ml_b_i_l, ml_b_f_l):
    wg_pad = jnp.concatenate([gla_w_gate_l, jnp.zeros((LANES - GLA_GATE_RANK, gla_w_gate_l.shape[1]), F32)],
                             axis=0)
    bias_row = jnp.concatenate([jnp.zeros((SM_I,), F32), ml_b_i_l, ml_b_f_l,
                                jnp.zeros((LANES - SM_F - ML_HEADS,), F32)]).reshape(1, LANES)
    return wg_pad, bias_row


def _mixers(x, bsz, t, t_valid, layer, lw, rope_tables, gla_s0, ml_state, out_dtype):
    (norm_w, w_main, w_small, wg_pad, bias_row, gla_b_gate, gla_norm_w, ml_norm_w,
     gm_ln_w, gm_ln_b, gm_ws, gm_bs_t) = lw
    h, small = _rms_small(x, norm_w, w_small, layer)
    proj = _matmul(h, w_main, layer).reshape(bsz, t, N_MAIN)
    small = small.reshape(bsz, t, LANES)

    out_a, gla_s = _gla(proj, small, wg_pad, gla_b_gate, gla_norm_w, gla_s0, t_valid, out_dtype)

    gates = _gates(small, bias_row, t_valid)
    out_b, ml_c, ml_n, ml_m = _mlstm(proj, gates, ml_norm_w, *ml_state, out_dtype)

    out_c, vn = _gmlp(proj, gm_ln_w, gm_ln_b, gm_ws, gm_bs_t, out_dtype)

    q_rope = _rope(proj, _MAIN_OFF["a_q"], rope_tables)
    return proj, (out_a, out_b, out_c), q_rope, gla_s, (ml_c, ml_n, ml_m[..., :1]), vn


def kernel(x_prompt, x_sample, cache_k, cache_v, page_table, state_gla, state_mlstm_C, state_mlstm_n,
           state_mlstm_m, norm_w, w_in, gla_w_gate, gla_b_gate, gla_norm_w, ml_b_i, ml_b_f, ml_norm_w,
           gm_ln_w, gm_ln_b, gm_w_s, gm_b_s, w_out, final_norm_w):
    bp, tp, d = x_prompt.shape
    db, ts, _ = x_sample.shape
    depth = w_in.shape[0]
    page = cache_k.shape[2]
    past_len = page_table.shape[1] * page
    assert w_out.shape[1] == 4 * GROUP_W and ts <= SAMPLE_T
    assert past_len % MB_BLOCK == 0 and tp % MB_BLOCK == 0

    tables_p = _rope_tables(jnp.arange(tp, dtype=jnp.int32))
    tables_s = _rope_tables(past_len + jnp.arange(SAMPLE_T, dtype=jnp.int32))
    cache_k = cache_k.reshape(depth, cache_k.shape[1], page * MB_HEADS, MB_DH)
    cache_v = cache_v.reshape(depth, cache_v.shape[1], page * MB_HEADS, MB_DH)

    yp = x_prompt.reshape(bp * tp, d)
    ys = jnp.pad(x_sample, ((0, 0), (0, SAMPLE_T - ts), (0, 0))).reshape(db * SAMPLE_T, d)
    dt_p = BF16
    dt_s = F32

    zero_gla = jnp.zeros((bp, GLA_HEADS // 2, GLA_DV, LANES), F32)
    zero_ml = (jnp.zeros((bp, ML_HEADS, ML_DH, ML_DH), F32), jnp.zeros((bp, ML_HEADS, 1, ML_DH), F32),
               jnp.zeros((bp, ML_HEADS, 1, LANES), F32))

    outs = {n: [] for n in ("ks", "vs", "gp", "gs", "cp", "cs", "np", "ns", "mp", "ms", "vv")}
    lp = min(tp, GM_CHUNK)
    w_t = jnp.swapaxes(w_in, 1, 2)
    w_main = _wprep(w_t)
    w_small = _wsmall(w_t)
    w_out4 = w_out.reshape(depth, 4, GROUP_W, d)
    kv_stacks = None
    for l in range(depth):
        wg_pad, bias_row = _layer_weights(gla_w_gate[l], ml_b_i[l], ml_b_f[l])
        common = (norm_w[l], w_main, w_small, wg_pad, bias_row, gla_b_gate[l], gla_norm_w[l], ml_norm_w[l],
                  gm_ln_w[l], gm_ln_b[l])

        lw = common + (gm_w_s[l][:, :lp, :lp], gm_b_s[l][:, :lp].T)
        proj, mix, q_rope, gla_s, ml_s, _ = _mixers(
            yp, bp, tp, tp, l, lw, tables_p, zero_gla, zero_ml, dt_p)
        kv_stacks = _kv_heads(proj, tables_p, l, depth, kv_stacks)
        out_d = _moba_prompt(q_rope, kv_stacks[0], kv_stacks[1], l, proj, dt_p)
        yp = _out_proj([a.reshape(bp * tp, GROUP_W) for a in mix + (out_d,)], w_out4, l, yp)
        outs["gp"].append(_gla_state_from_pairs(gla_s))
        outs["cp"].append(ml_s[0])
        outs["np"].append(ml_s[1][:, :, 0, :])
        outs["mp"].append(ml_s[2][:, :, 0, 0])

        lw = common + (gm_w_s[l][:, :SAMPLE_T, :SAMPLE_T], gm_b_s[l][:, :SAMPLE_T].T)
        ml_state = (state_mlstm_C[l], state_mlstm_n[l][:, :, None, :],
                    jnp.broadcast_to(state_mlstm_m[l][:, :, None, None], (db, ML_HEADS, 1, LANES)))
        proj, mix, q_rope, gla_s, ml_s, vn = _mixers(
            ys, db, SAMPLE_T, ts, l, lw, tables_s, _gla_state_to_pairs(state_gla[l]), ml_state, dt_s)
        k_rope = _rope(proj, _MAIN_OFF["a_k"], tables_s)
        v_new = proj[:, :, _MAIN_OFF["a_v"]:_MAIN_OFF["a_v"] + GROUP_W]
        q_rows = (q_rope.reshape(db, SAMPLE_T, MB_HEADS, MB_DH).transpose(0, 2, 1, 3)
                  .reshape(db, QROWS, MB_DH))
        o_part, stats = _moba_past(l, q_rows, cache_k, cache_v, page_table)
        out_d = _moba_merge(o_part, stats, q_rows, k_rope, proj, ts)
        ys = _out_proj([a.reshape(db * SAMPLE_T, GROUP_W) for a in mix + (out_d,)], w_out4, l, ys)
        outs["ks"].append(k_rope[:, :ts].reshape(db, ts, MB_HEADS, MB_DH))
        outs["vs"].append(v_new[:, :ts].reshape(db, ts, MB_HEADS, MB_DH))
        outs["gs"].append(_gla_state_from_pairs(gla_s))
        outs["cs"].append(ml_s[0])
        outs["ns"].append(ml_s[1][:, :, 0, :])
        outs["ms"].append(ml_s[2][:, :, 0, 0])
        outs["vv"].append(vn[:, :ts])

    y_prompt = _rms(yp, final_norm_w).reshape(bp, tp, d)
    y_sample = _rms(ys, final_norm_w).reshape(db, SAMPLE_T, d)[:, :ts]
    st = jnp.stack
    k_prompt = kv_stacks[0].reshape(depth, bp, tp, MB_HEADS, MB_DH)
    v_prompt = kv_stacks[1].reshape(depth, bp, tp, MB_HEADS, MB_DH)
    return (y_prompt, y_sample, k_prompt, v_prompt, st(outs["ks"]), st(outs["vs"]),
            st(outs["gp"]), st(outs["gs"]), st(outs["cp"]), st(outs["cs"]), st(outs["np"]), st(outs["ns"]),
            st(outs["mp"]), st(outs["ms"]), st(outs["vv"]))
```

```python
import functools

import numpy as np
import jax
import jax.numpy as jnp
from jax import lax
from jax.experimental import pallas as pl
from jax.experimental.pallas import tpu as pltpu

F32 = jnp.float32
BF16 = jnp.bfloat16
HIGHEST = lax.Precision.HIGHEST

GROUP_W = 1024
GLA_HEADS, GLA_DK, GLA_DV = 8, 64, 128
GLA_GATE_RANK, GLA_GATE_NORM = 16, 16.0
ML_HEADS, ML_DH = 8, 128
GM_GROUPS, GM_CH, GM_CHUNK = 8, 128, 128
MB_HEADS, MB_DH, MB_BLOCK, MB_TOPK = 8, 128, 256, 3
ROT_DIM, ROPE_THETA = 32, 500000.0
EPS = 1e-6

LANES = 128
SUBLANES = 8
VMEM_LIMIT = 56 * 1024 * 1024

NEG = -1e30
SAMPLE_T = SUBLANES

_MAIN_ORDER = ("g_q", "g_k", "g_v", "g_z", "m_q", "m_k", "m_v", "m_o", "m_z",
               "c_u", "c_v", "c_z", "a_q", "a_k", "a_v", "a_z")
_SPLIT_NAMES = ("g_q", "g_k", "g_v", "g_lr", "g_z", "m_q", "m_k", "m_v", "m_i", "m_f", "m_o", "m_z",
                "c_u", "c_v", "c_z", "a_q", "a_k", "a_v", "a_z")
_SPLIT_W = (GLA_HEADS * GLA_DK, GLA_HEADS * GLA_DK, GLA_HEADS * GLA_DV, GLA_GATE_RANK, GROUP_W,
            GROUP_W, GROUP_W, GROUP_W, ML_HEADS, ML_HEADS, GROUP_W, GROUP_W,
            GROUP_W, GROUP_W, GROUP_W, GROUP_W, GROUP_W, GROUP_W, GROUP_W)
_SRC_OFF = dict(zip(_SPLIT_NAMES, np.concatenate([[0], np.cumsum(_SPLIT_W)[:-1]]).tolist()))
_SRC_W = dict(zip(_SPLIT_NAMES, _SPLIT_W))
_MAIN_OFF = {}
_off = 0
for _n in _MAIN_ORDER:
    _MAIN_OFF[_n] = _off
    _off += _SRC_W[_n]
N_MAIN = _off
SM_LR, SM_I, SM_F = 0, GLA_GATE_RANK, GLA_GATE_RANK + ML_HEADS


def _cparams(sem):
    return pltpu.CompilerParams(dimension_semantics=sem, vmem_limit_bytes=VMEM_LIMIT)


def _pick(n, cands):
    for c in cands:
        if n % c == 0:
            return c
    return n


def _silu(x):
    return x / (1.0 + jnp.exp(-x))


def _sigmoid(x):
    return 1.0 / (1.0 + jnp.exp(-x))


def _log_sigmoid(x):
    return jnp.minimum(x, 0.0) - jnp.log1p(jnp.exp(-jnp.abs(x)))


def _gelu(x):
    c = np.sqrt(2.0 / np.pi).astype(np.float32)
    return 0.5 * x * (1.0 + jnp.tanh(c * (x + 0.044715 * (x * x * x))))


def _dot_nt(a, b, precision=None):
    return lax.dot_general(a, b, (((1,), (1,)), ((), ())), preferred_element_type=F32, precision=precision)


def _dot_tn(a, b, precision=None):
    return lax.dot_general(a, b, (((0,), (0,)), ((), ())), preferred_element_type=F32, precision=precision)


def _dot(a, b, precision=None):
    return jnp.dot(a, b, preferred_element_type=F32, precision=precision)


def _rms_small_kernel(x_ref, nw_ref, ws_ref, h_ref, sm_ref):
    x = x_ref[...]
    ms = jnp.mean(x * x, axis=-1, keepdims=True)
    hb = (x * lax.rsqrt(ms + EPS) * nw_ref[...]).astype(BF16)
    h_ref[...] = hb
    sm_ref[...] = _dot(hb, ws_ref[...])


def _rms_small(x, nw, w_small, layer):
    m, d = x.shape
    bm = _pick(m, (256, 128, 64, 32, 16, 8))
    return pl.pallas_call(
        _rms_small_kernel,
        grid=(m // bm,),
        in_specs=[pl.BlockSpec((bm, d), lambda i: (i, 0)),
                  pl.BlockSpec((1, d), lambda i: (0, 0)),
                  pl.BlockSpec((None, d, LANES), lambda i: (layer, 0, 0))],
        out_specs=[pl.BlockSpec((bm, d), lambda i: (i, 0)),
                   pl.BlockSpec((bm, LANES), lambda i: (i, 0))],
        out_shape=[jax.ShapeDtypeStruct((m, d), BF16), jax.ShapeDtypeStruct((m, LANES), F32)],
        compiler_params=_cparams(("parallel",)),
        name="rms_small",
    )(x, nw.reshape(1, d), w_small)


def _rms_kernel(x_ref, nw_ref, o_ref):
    x = x_ref[...]
    ms = jnp.mean(x * x, axis=-1, keepdims=True)
    o_ref[...] = x * lax.rsqrt(ms + EPS) * nw_ref[...]


def _rms(x, nw):
    m, d = x.shape
    bm = _pick(m, (256, 128, 64, 32, 16, 8))
    return pl.pallas_call(
        _rms_kernel,
        grid=(m // bm,),
        in_specs=[pl.BlockSpec((bm, d), lambda i: (i, 0)), pl.BlockSpec((1, d), lambda i: (0, 0))],
        out_specs=pl.BlockSpec((bm, d), lambda i: (i, 0)),
        out_shape=jax.ShapeDtypeStruct((m, d), F32),
        compiler_params=_cparams(("parallel",)),
        name="rms_final",
    )(x, nw.reshape(1, d))


WP_BN = 1024
WP_B1 = _MAIN_OFF["g_z"] // WP_BN
WP_B2 = _MAIN_OFF["m_o"] // WP_BN
WP_S1 = GLA_GATE_RANK
WP_S2 = GLA_GATE_RANK + 2 * ML_HEADS
assert _MAIN_OFF["g_z"] % WP_BN == 0 and _MAIN_OFF["m_o"] % WP_BN == 0 and N_MAIN % WP_BN == 0
assert WP_S1 % SUBLANES == 0 and WP_S2 % SUBLANES == 0 and sum(_SPLIT_W) % WP_S2 == 0


def _wprep_kernel(a_ref, t_ref, o_ref):
    j = pl.program_id(2)

    def emit(s):
        x = a_ref[...] if s == 0 else jnp.concatenate([a_ref[...], t_ref[...]], axis=0)[s:s + WP_BN]
        o_ref[...] = x.T.astype(BF16)

    @pl.when(j < WP_B1)
    def _():
        emit(0)

    @pl.when((j >= WP_B1) & (j < WP_B2))
    def _():
        emit(WP_S1)

    @pl.when(j >= WP_B2)
    def _():
        emit(WP_S2)


def _wprep(w_t):
    depth, _, d = w_t.shape
    kb = _pick(d, (2048, 1024, 512, 256, 128))
    return pl.pallas_call(
        _wprep_kernel,
        grid=(depth, d // kb, N_MAIN // WP_BN),
        in_specs=[pl.BlockSpec((None, WP_BN, kb), lambda l, i, j: (l, j, i)),
                  pl.BlockSpec((None, WP_S2, kb), lambda l, i, j: (l, (j + 1) * (WP_BN // WP_S2), i))],
        out_specs=pl.BlockSpec((None, kb, WP_BN), lambda l, i, j: (l, i, j)),
        out_shape=jax.ShapeDtypeStruct((depth, d, N_MAIN), BF16),
        compiler_params=_cparams(("parallel", "parallel", "parallel")),
        name="w_prep",
    )(w_t, w_t)


def _wsmall_kernel(lr_ref, if_ref, o_ref):
    kb = lr_ref.shape[1]
    x = jnp.concatenate([lr_ref[...], if_ref[...],
                         jnp.zeros((LANES - GLA_GATE_RANK - 2 * ML_HEADS, kb), F32)], axis=0)
    o_ref[...] = x.T.astype(BF16)


def _wsmall(w_t):
    depth, _, d = w_t.shape
    kb = _pick(d, (512, 256, 128))
    lr, gi = _SRC_OFF["g_lr"], _SRC_OFF["m_i"]
    assert _SRC_OFF["m_f"] == gi + ML_HEADS and lr % GLA_GATE_RANK == 0 and gi % (2 * ML_HEADS) == 0
    return pl.pallas_call(
        _wsmall_kernel,
        grid=(depth, d // kb),
        in_specs=[pl.BlockSpec((None, GLA_GATE_RANK, kb), lambda l, i: (l, lr // GLA_GATE_RANK, i)),
                  pl.BlockSpec((None, 2 * ML_HEADS, kb), lambda l, i: (l, gi // (2 * ML_HEADS), i))],
        out_specs=pl.BlockSpec((None, kb, LANES), lambda l, i: (l, i, 0)),
        out_shape=jax.ShapeDtypeStruct((depth, d, LANES), BF16),
        compiler_params=_cparams(("parallel", "parallel")),
        name="w_small",
    )(w_t, w_t)


def _mm_kernel(a_ref, b_ref, o_ref):
    o_ref[...] = _dot(a_ref[...], b_ref[...])


def _matmul(a, w_all, layer):
    m, k = a.shape
    n = w_all.shape[2]
    bm = _pick(m, (1024, 512, 256, 128, 64))
    bn = _pick(n, (1024, 512, 256, 128))
    return pl.pallas_call(
        _mm_kernel,
        grid=(n // bn, m // bm),
        in_specs=[pl.BlockSpec((bm, k), lambda j, i: (i, 0)),
                  pl.BlockSpec((None, k, bn), lambda j, i: (layer, 0, j))],
        out_specs=pl.BlockSpec((bm, bn), lambda j, i: (i, j)),
        out_shape=jax.ShapeDtypeStruct((m, n), F32),
        compiler_params=_cparams(("parallel", "parallel")),
        name="in_proj",
    )(a, w_all)


def _out_kernel(a0, a1, a2, a3, w_ref, x_ref, o_ref):
    acc = x_ref[...]
    for g, a in enumerate((a0, a1, a2, a3)):
        acc = acc + _dot(a[...].astype(BF16), w_ref[g].astype(BF16))
    o_ref[...] = acc


def _out_proj(mix, w_all, layer, x):
    m, d = x.shape
    bm = _pick(m, (1024, 512, 256, 128, 64))
    bn = _pick(d, (512, 256, 128))
    a_spec = pl.BlockSpec((bm, GROUP_W), lambda i, j: (i, 0))
    return pl.pallas_call(
        _out_kernel,
        grid=(m // bm, d // bn),
        in_specs=[a_spec, a_spec, a_spec, a_spec,
                  pl.BlockSpec((None, 4, GROUP_W, bn), lambda i, j: (layer, 0, 0, j)),
                  pl.BlockSpec((bm, bn), lambda i, j: (i, j))],
        out_specs=pl.BlockSpec((bm, bn), lambda i, j: (i, j)),
        out_shape=jax.ShapeDtypeStruct((m, d), F32),
        compiler_params=_cparams(("parallel", "parallel")),
        name="out_proj",
    )(*mix, w_all, x)


def _rope_kernel(x_ref, a_ref, b_ref, c_ref, o_ref):
    a, b, c = a_ref[...], b_ref[...], c_ref[...]
    for h in range(MB_HEADS):
        sl = slice(h * MB_DH, (h + 1) * MB_DH)
        x = x_ref[0, :, sl]
        o_ref[0, :, sl] = (x * a + pltpu.roll(x, MB_DH - ROT_DIM // 2, 1) * b
                           + pltpu.roll(x, ROT_DIM // 2, 1) * c)


def _rope_tables(pos):
    half = ROT_DIM // 2
    inv_freq = jnp.power(ROPE_THETA, -jnp.arange(0, ROT_DIM, 2, dtype=F32) / ROT_DIM)
    ang = pos.astype(F32)[:, None] * inv_freq[None, :]
    cos, sin = jnp.cos(ang), jnp.sin(ang)
    t = pos.shape[0]
    a = jnp.concatenate([cos, cos, jnp.ones((t, MB_DH - ROT_DIM), F32)], axis=1)
    b = jnp.concatenate([-sin, jnp.zeros((t, MB_DH - half), F32)], axis=1)
    c = jnp.concatenate([jnp.zeros((t, half), F32), sin, jnp.zeros((t, MB_DH - ROT_DIM), F32)], axis=1)
    return a, b, c


def _rope(proj, col, tables):
    bsz, t, _ = proj.shape
    bt = _pick(t, (512, 256, 128, 64, 32, 16, 8))
    tab = pl.BlockSpec((bt, MB_DH), lambda b, i: (i, 0))
    return pl.pallas_call(
        _rope_kernel,
        grid=(bsz, t // bt),
        in_specs=[pl.BlockSpec((1, bt, GROUP_W), lambda b, i: (b, i, col // GROUP_W)), tab, tab, tab],
        out_specs=pl.BlockSpec((1, bt, GROUP_W), lambda b, i: (b, i, 0)),
        out_shape=jax.ShapeDtypeStruct((bsz, t, GROUP_W), F32),
        compiler_params=_cparams(("parallel", "parallel")),
        name="rope",
    )(proj, *tables)


def _kv_heads_kernel(k_ref, v_ref, a_ref, b_ref, c_ref, *refs):
    k4_ref, v4_ref = refs[-2:]
    a, b, c = a_ref[...], b_ref[...], c_ref[...]
    bt = a.shape[0]
    for h in range(MB_HEADS):
        sl = slice(h * MB_DH, (h + 1) * MB_DH)
        x = k_ref[0, :, sl]
        rows = pl.ds(h, bt, stride=MB_HEADS)
        k4_ref[rows, :] = (x * a + pltpu.roll(x, MB_DH - ROT_DIM // 2, 1) * b
                           + pltpu.roll(x, ROT_DIM // 2, 1) * c)
        v4_ref[rows, :] = v_ref[0, :, sl]


def _kv_heads(proj, tables, layer, depth, stacks):
    bsz, t, _ = proj.shape
    bt = _pick(t, (256, 128, 64, 32, 16, 8))
    tab = pl.BlockSpec((bt, MB_DH), lambda b, i: (i, 0))
    kcol, vcol = _MAIN_OFF["a_k"] // GROUP_W, _MAIN_OFF["a_v"] // GROUP_W
    out = pl.BlockSpec((None, None, bt * MB_HEADS, MB_DH), lambda b, i: (layer, b, i, 0))
    shape = jax.ShapeDtypeStruct((depth, bsz, t * MB_HEADS, MB_DH), F32)
    in_specs = [pl.BlockSpec((1, bt, GROUP_W), lambda b, i: (b, i, kcol)),
                pl.BlockSpec((1, bt, GROUP_W), lambda b, i: (b, i, vcol)), tab, tab, tab]
    aliases = {}
    if stacks is not None:
        in_specs += [pl.BlockSpec(memory_space=pl.ANY), pl.BlockSpec(memory_space=pl.ANY)]
        aliases = {5: 0, 6: 1}
    return pl.pallas_call(
        _kv_heads_kernel,
        grid=(bsz, t // bt),
        in_specs=in_specs,
        out_specs=[out, out],
        out_shape=[shape, shape],
        input_output_aliases=aliases,
        compiler_params=_cparams(("parallel", "parallel")),
        name="kv_heads",
    )(proj, proj, *tables, *(stacks or ()))


def _gla_levels(c):
    return [c >> i for i in range(1, c.bit_length())]


def _gla_sum_matrices(c, nch):
    sup = c * nch
    t = np.arange(sup)[:, None]
    r = np.arange(sup)[None, :]
    cb = (t // c) * c
    mats = [(r >= cb) & (r <= t), (r > t) & (r <= cb + c - 1)]
    for hs in _gla_levels(c):
        base = (t // (2 * hs)) * (2 * hs)
        ref = base + hs - 1
        mats.append(np.where(t - base >= hs, (r > ref) & (r <= t), (r > t) & (r <= ref)))
    return np.concatenate(mats, axis=0).astype(np.float32)


def _gla_kernel(q_ref, k_ref, v_ref, z_ref, sm_ref, wg_ref, bg_ref, nw_ref, s0_ref, sum_ref,
                o_ref, s_ref, g_scr, *, t, c, nch, t_valid):
    dk = GLA_DK
    sup = c * nch
    levels = _gla_levels(c)
    split = sum_ref.dtype == BF16
    x = _dot(sm_ref[0], wg_ref[...], precision=HIGHEST) + bg_ref[...]
    g_all = _log_sigmoid(x) * (1.0 / GLA_GATE_NORM)
    if t_valid < t:
        rows_t = lax.broadcasted_iota(jnp.int32, (t, LANES), 0)
        g_all = jnp.where(rows_t < t_valid, g_all, 0.0)
    g_scr[...] = g_all

    lane = lax.broadcasted_iota(jnp.int32, (sup, LANES), 1)
    row = lax.broadcasted_iota(jnp.int32, (sup, LANES), 0)
    head0 = lane < dk
    rr = lax.broadcasted_iota(jnp.int32, (sup, sup), 0)
    cc = lax.broadcasted_iota(jnp.int32, (sup, sup), 1)
    mm = BF16 if split else F32
    prec = None if split else HIGHEST
    nw = nw_ref[...]

    def body(si, s2t):
        r0 = pl.multiple_of(si * sup, sup)
        q = q_ref[0, pl.ds(r0, sup), :] * (dk ** -0.5)
        k = k_ref[0, pl.ds(r0, sup), :]
        if t_valid < t:
            k = jnp.where(row < t_valid, k, 0.0)
        g = g_scr[pl.ds(r0, sup), :]
        v = v_ref[0, pl.ds(r0, sup), :]
        z = z_ref[0, pl.ds(r0, sup), :]
        if split:
            g1 = g.astype(BF16)
            r1 = g - g1.astype(F32)
            g2 = r1.astype(BF16)
            g3 = (r1 - g2.astype(F32)).astype(BF16)
            xs = _dot(sum_ref[...], jnp.concatenate([g1, g2, g3], axis=1))
            xs = xs[:, :LANES] + xs[:, LANES:2 * LANES] + xs[:, 2 * LANES:]
        else:
            xs = _dot(sum_ref[...], g, precision=HIGHEST)
        b = xs[0:sup]
        to_end = xs[sup:2 * sup]

        qk = q * k
        att0 = jnp.where(rr == cc, jnp.sum(jnp.where(head0, qk, 0.0), axis=-1, keepdims=True), 0.0)
        att1 = jnp.where(rr == cc, jnp.sum(jnp.where(head0, 0.0, qk), axis=-1, keepdims=True), 0.0)
        for i, hs in enumerate(levels):
            e = jnp.exp(xs[(2 + i) * sup:(3 + i) * sup])
            upper = (row & hs) != 0
            qt = jnp.where(upper, q * e, 0.0)
            kl = jnp.where(upper, 0.0, k * e).astype(mm)
            sh = (2 * hs).bit_length() - 1
            same = (rr >> sh) == (cc >> sh)
            att0 = att0 + jnp.where(same, _dot_nt(jnp.where(head0, qt, 0.0).astype(mm), kl, prec), 0.0)
            att1 = att1 + jnp.where(same, _dot_nt(jnp.where(head0, 0.0, qt).astype(mm), kl, prec), 0.0)

        kt = k * jnp.exp(to_end)
        qe = q * jnp.exp(b)
        qe0, qe1 = jnp.where(head0, qe, 0.0), jnp.where(head0, 0.0, qe)
        kt0, kt1 = jnp.where(head0, kt, 0.0), jnp.where(head0, 0.0, kt)
        v0, v1 = v[:, :GLA_DV], v[:, GLA_DV:]
        rows = [slice(ch * c, (ch + 1) * c) for ch in range(nch)]
        upds = [_dot_tn(jnp.concatenate([v0[r], v1[r]], axis=0).astype(mm),
                        jnp.concatenate([kt0[r], kt1[r]], axis=0).astype(mm), prec) for r in rows]
        o_int = []
        for ch, r in enumerate(rows):
            lhs = jnp.concatenate([qe0[r], qe1[r]], axis=0).astype(mm)
            o_int.append(_dot_nt(lhs, s2t.astype(mm), prec))
            s2t = jnp.exp(b[ch * c + c - 1:ch * c + c, :]) * s2t + upds[ch]

        for h, (att, vh) in enumerate(((att0, v0), (att1, v1))):
            o = (jnp.concatenate([oi[h * c:(h + 1) * c, :] for oi in o_int], axis=0)
                 + _dot(att.astype(mm), vh.astype(mm), prec))
            on = o * lax.rsqrt(jnp.mean(o * o, axis=-1, keepdims=True) + EPS) * nw
            zh = z[:, h * GLA_DV:(h + 1) * GLA_DV]
            o_ref[0, pl.ds(r0, sup), h * GLA_DV:(h + 1) * GLA_DV] = (on * _silu(zh)).astype(o_ref.dtype)
        return s2t

    s_ref[0, 0] = lax.fori_loop(0, t // sup, body, s0_ref[0, 0])


def _gla(proj, small, wg_pad, bg, nw, s0t, t_valid, out_dtype):
    bsz, t, _ = proj.shape
    c = min(LANES, t)
    nch = 1
    pairs = GLA_HEADS // 2
    qb, kb = _MAIN_OFF["g_q"] // LANES, _MAIN_OFF["g_k"] // LANES
    vb, zb = _MAIN_OFF["g_v"] // (2 * GLA_DV), _MAIN_OFF["g_z"] // (2 * GLA_DV)
    kern = functools.partial(_gla_kernel, t=t, c=c, nch=nch, t_valid=t_valid)
    sums = _gla_sum_matrices(c, nch)
    sums = jnp.asarray(sums, BF16 if (c * nch) % 16 == 0 else F32)
    return pl.pallas_call(
        kern,
        grid=(bsz, pairs),
        in_specs=[pl.BlockSpec((1, t, LANES), lambda b, p: (b, 0, qb + p)),
                  pl.BlockSpec((1, t, LANES), lambda b, p: (b, 0, kb + p)),
                  pl.BlockSpec((1, t, 2 * GLA_DV), lambda b, p: (b, 0, vb + p)),
                  pl.BlockSpec((1, t, 2 * GLA_DV), lambda b, p: (b, 0, zb + p)),
                  pl.BlockSpec((1, t, LANES), lambda b, p: (b, 0, 0)),
                  pl.BlockSpec((LANES, LANES), lambda b, p: (0, p)),
                  pl.BlockSpec((1, LANES), lambda b, p: (0, p)),
                  pl.BlockSpec((1, GLA_DV), lambda b, p: (0, 0)),
                  pl.BlockSpec((1, 1, GLA_DV, LANES), lambda b, p: (b, p, 0, 0)),
                  pl.BlockSpec(sums.shape, lambda b, p: (0, 0))],
        out_specs=[pl.BlockSpec((1, t, 2 * GLA_DV), lambda b, p: (b, 0, p)),
                   pl.BlockSpec((1, 1, GLA_DV, LANES), lambda b, p: (b, p, 0, 0))],
        out_shape=[jax.ShapeDtypeStruct((bsz, t, GROUP_W), out_dtype),
                   jax.ShapeDtypeStruct((bsz, pairs, GLA_DV, LANES), F32)],
        scratch_shapes=[pltpu.VMEM((t, LANES), F32)],
        compiler_params=_cparams(("parallel", "parallel")),
        name="gla",
    )(proj, proj, proj, proj, small, wg_pad, bg.reshape(1, -1), nw.reshape(1, -1), s0t, sums)


def _gla_state_to_pairs(s):
    bsz = s.shape[0]
    s = s.reshape(bsz, GLA_HEADS // 2, 2, GLA_DK, GLA_DV)
    return s.transpose(0, 1, 4, 2, 3).reshape(bsz, GLA_HEADS // 2, GLA_DV, 2 * GLA_DK)


def _gla_state_from_pairs(s):
    bsz = s.shape[0]
    s = s.reshape(bsz, GLA_HEADS // 2, GLA_DV, 2, GLA_DK)
    return s.transpose(0, 1, 3, 4, 2).reshape(bsz, GLA_HEADS, GLA_DK, GLA_DV)


def _gates_kernel(sm_ref, bias_ref, o_ref, *, t_valid):
    x = sm_ref[0] + bias_ref[...]
    lane = lax.broadcasted_iota(jnp.int32, x.shape, 1)
    is_f = (lane >= SM_F) & (lane < SM_F + ML_HEADS)
    out = jnp.where(is_f, _log_sigmoid(x), x)
    if t_valid is not None:
        row = lax.broadcasted_iota(jnp.int32, x.shape, 0)
        out = jnp.where(row < t_valid, out, jnp.where(is_f, 0.0, NEG))
    o_ref[0] = out


def _gates(small, bias_row, t_valid):
    bsz, t, _ = small.shape
    bt = t if t_valid < t else _pick(t, (1024, 512, 256, 128, 64, 32, 16, 8))
    return pl.pallas_call(
        functools.partial(_gates_kernel, t_valid=t_valid if t_valid < t else None),
        grid=(bsz, t // bt),
        in_specs=[pl.BlockSpec((1, bt, LANES), lambda b, i: (b, i, 0)),
                  pl.BlockSpec((1, LANES), lambda b, i: (0, 0))],
        out_specs=pl.BlockSpec((1, bt, LANES), lambda b, i: (b, i, 0)),
        out_shape=jax.ShapeDtypeStruct((bsz, t, LANES), F32),
        compiler_params=_cparams(("parallel", "parallel")),
        name="ml_gates",
    )(small, bias_row)


ML_PAIR = 2


def _mlstm_kernel(q_ref, k_ref, v_ref, og_ref, z_ref, gt_ref, nw_ref,
                  c0_ref, n0_ref, m0_ref, o_ref, c_ref, n_ref, m_ref, t_scr, *, t, c):
    h0 = pl.program_id(1) * ML_PAIR
    hs = range(ML_PAIR)
    lane = lax.broadcasted_iota(jnp.int32, (c, LANES), 1)
    rr = lax.broadcasted_iota(jnp.int32, (c, c), 0)
    cc = lax.broadcasted_iota(jnp.int32, (c, c), 1)
    causal = rr >= cc
    nw = nw_ref[...]
    hsl = lambda j: slice(j * ML_DH, (j + 1) * ML_DH)

    def body(ci, carry):
        cms, ns, m_prevs = carry
        r0 = pl.multiple_of(ci * c, c)
        gt = gt_ref[0, pl.ds(r0, c), :]
        if c == LANES:
            t_scr[...] = gt.T
        qs = [q_ref[0, pl.ds(r0, c), hsl(j)] * (ML_DH ** -0.5) for j in hs]
        ks = [k_ref[0, pl.ds(r0, c), hsl(j)] for j in hs]
        vs = [v_ref[0, pl.ds(r0, c), hsl(j)] for j in hs]
        s_qk = [_dot_nt(qs[j], ks[j], precision=HIGHEST) for j in hs]
        s_qc = [_dot_nt(qs[j], cms[j], precision=HIGHEST) for j in hs]

        d, inter, m_t, i_cols, fc_cols = [], [], [], [], []
        for j in hs:
            i_col = jnp.sum(jnp.where(lane == SM_I + h0 + j, gt, 0.0), axis=-1, keepdims=True)
            f_col = jnp.sum(jnp.where(lane == SM_F + h0 + j, gt, 0.0), axis=-1, keepdims=True)
            if c == LANES:
                i_row = t_scr[pl.ds(SM_I + h0 + j, 1), :]
                f_row = t_scr[pl.ds(SM_F + h0 + j, 1), :]
            else:
                i_row = jnp.sum(jnp.where(rr == cc, i_col, 0.0), axis=0, keepdims=True)
                f_row = jnp.sum(jnp.where(rr == cc, f_col, 0.0), axis=0, keepdims=True)
            fc_col = jnp.sum(jnp.where(causal, f_row, 0.0), axis=-1, keepdims=True)
            fc_row = jnp.sum(jnp.where(rr <= cc, f_col, 0.0), axis=0, keepdims=True)
            dj = jnp.where(causal, fc_col - fc_row + i_row, NEG)
            d.append(dj)
            inter.append(fc_col + m_prevs[j])
            m_t.append(jnp.maximum(inter[j], jnp.max(dj, axis=-1, keepdims=True)))
            i_cols.append(i_col)
            fc_cols.append(fc_col)

        w_state = [jnp.exp(inter[j] - m_t[j]) for j in hs]
        qk = [s_qk[j] * jnp.exp(d[j] - m_t[j]) for j in hs]
        pv = [_dot(qk[j], vs[j], precision=HIGHEST) for j in hs]
        w_end, dec, m_new = [], [], []
        for j in hs:
            m_new.append(m_t[j][c - 1:c, :])
            f_end = fc_cols[j][c - 1:c, :]
            w_end.append(jnp.exp(f_end - fc_cols[j] + i_cols[j] - m_new[j]))
            dec.append(jnp.exp(f_end + m_prevs[j] - m_new[j]))
        upd = [_dot_tn(vs[j] * w_end[j], ks[j], precision=HIGHEST) for j in hs]

        c_new, n_new = [], []
        for j in hs:
            num = w_state[j] * s_qc[j] + pv[j]
            den = (w_state[j] * jnp.sum(qs[j] * ns[j], axis=-1, keepdims=True)
                   + jnp.sum(qk[j], axis=-1, keepdims=True))
            hout = num / jnp.maximum(jnp.abs(den), jnp.exp(-m_t[j]))
            c_new.append(dec[j] * cms[j] + upd[j])
            n_new.append(dec[j] * ns[j] + jnp.sum(ks[j] * w_end[j], axis=0, keepdims=True))
            o = _sigmoid(og_ref[0, pl.ds(r0, c), hsl(j)]) * hout
            on = o * lax.rsqrt(jnp.mean(o * o, axis=-1, keepdims=True) + EPS) * nw
            o_ref[0, pl.ds(r0, c), hsl(j)] = (on * _silu(z_ref[0, pl.ds(r0, c), hsl(j)])).astype(o_ref.dtype)
        return tuple(c_new), tuple(n_new), tuple(m_new)

    init = (tuple(c0_ref[0, j] for j in hs), tuple(n0_ref[0, j] for j in hs),
            tuple(m0_ref[0, j][:, :1] for j in hs))
    cms, ns, ms = lax.fori_loop(0, t // c, body, init)
    for j in hs:
        c_ref[0, j] = cms[j]
        n_ref[0, j] = ns[j]
        m_ref[0, j] = jnp.broadcast_to(ms[j], (1, LANES))


def _split3(x):
    a = x.astype(BF16)
    r = x - a.astype(F32)
    b = r.astype(BF16)
    return a, b, (r - b.astype(F32)).astype(BF16)


def _mlstm_wide_kernel(q_ref, k_ref, v_ref, og_ref, z_ref, gt_ref, nw_ref, c0_ref, n0_ref, m0_ref,
                       o_ref, c_ref, n_ref, m_ref, t_scr, g_scr, i_scr, fc_scr, *, t):
    c = LANES
    nc = t // c
    h0 = pl.program_id(1) * ML_PAIR
    hs = range(ML_PAIR)
    rr = lax.broadcasted_iota(jnp.int32, (c, c), 0)
    cc = lax.broadcasted_iota(jnp.int32, (c, c), 1)
    causal = rr >= cc
    nw = nw_ref[...]
    hsl = lambda j: slice(j * ML_DH, (j + 1) * ML_DH)
    ones_b = jnp.ones((c, ML_DH), BF16)
    tril_b = jnp.where(causal, 1.0, 0.0).astype(BF16)

    lane_t = lax.broadcasted_iota(jnp.int32, (c, LANES), 1)
    for ci in range(nc):
        rows = slice(ci * c, (ci + 1) * c)
        g = gt_ref[0, rows, :]
        g_scr[rows, :] = jnp.where(lane_t < SM_F, g, sum(_dot(tril_b, x) for x in _split3(g)))
    sel_r = lax.broadcasted_iota(jnp.int32, (LANES, 2 * LANES), 0)
    sel_c = lax.broadcasted_iota(jnp.int32, (LANES, 2 * LANES), 1)
    g_terms = _split3(g_scr[...])
    for j in hs:
        src = jnp.where(sel_c < LANES, SM_I + h0 + j, SM_F + h0 + j)
        sel = jnp.where(sel_r == src, 1.0, 0.0).astype(BF16)
        both = sum(_dot(g, sel) for g in g_terms)
        i_scr[j] = both[:, :LANES]
        fc_scr[j] = both[:, LANES:]

    def body(ci, carry):
        cms, ns, m_prevs = carry
        r0 = pl.multiple_of(ci * c, c)
        t_scr[...] = g_scr[pl.ds(r0, c), :].T
        qbs = [(q_ref[0, pl.ds(r0, c), hsl(j)] * (ML_DH ** -0.5)).astype(BF16) for j in hs]
        kbs = [k_ref[0, pl.ds(r0, c), hsl(j)].astype(BF16) for j in hs]
        vs = [v_ref[0, pl.ds(r0, c), hsl(j)] for j in hs]
        s_qk = [_dot_nt(qbs[j], kbs[j]) for j in hs]
        s_qc = [_dot_nt(qbs[j], jnp.concatenate([cms[j], jnp.broadcast_to(ns[j], (c, ML_DH))],
                                                axis=0).astype(BF16)) for j in hs]

        d, inter, m_t, fcs = [], [], [], []
        for j in hs:
            fc = fc_scr[j, pl.ds(r0, c), :]
            i_row = t_scr[pl.ds(SM_I + h0 + j, 1), :]
            fc_row = t_scr[pl.ds(SM_F + h0 + j, 1), :]
            dj = jnp.where(causal, fc - fc_row + i_row, NEG)
            d.append(dj)
            inter.append(fc + m_prevs[j])
            m_t.append(jnp.maximum(inter[j], jnp.max(dj, axis=-1, keepdims=True)))
            fcs.append(fc)

        w_state = [jnp.exp(inter[j] - m_t[j]) for j in hs]
        qk = [(s_qk[j] * jnp.exp(d[j] - m_t[j])).astype(BF16) for j in hs]
        pv = [_dot(qk[j], jnp.concatenate([vs[j].astype(BF16), ones_b], axis=1)) for j in hs]
        w_end, dec, m_new = [], [], []
        for j in hs:
            m_new.append(m_t[j][c - 1:c, :1])
            f_end = fcs[j][c - 1:c, :]
            w_end.append(jnp.exp(f_end - fcs[j] + i_scr[j, pl.ds(r0, c), :] - m_new[j]))
            dec.append(jnp.exp(f_end[:, :1] + m_prevs[j] - m_new[j]))
        upd = [_dot_tn(jnp.concatenate([vs[j] * w_end[j], w_end[j]], axis=1).astype(BF16), kbs[j])
               for j in hs]

        c_new, n_new = [], []
        hout = []
        for j in hs:
            num = w_state[j] * s_qc[j][:, :ML_DH] + pv[j][:, :ML_DH]
            den = w_state[j] * s_qc[j][:, ML_DH:] + pv[j][:, ML_DH:]
            hout.append(_sigmoid(og_ref[0, pl.ds(r0, c), hsl(j)]) * num
                        / jnp.maximum(jnp.abs(den), jnp.exp(-m_t[j])))
            c_new.append(dec[j] * cms[j] + upd[j][:ML_DH, :])
            n_new.append(dec[j] * ns[j] + upd[j][ML_DH:ML_DH + 1, :])
        msq = [_dot((o * o).astype(BF16), ones_b) * (1.0 / ML_DH) for o in hout]
        for j in hs:
            on = hout[j] * lax.rsqrt(msq[j] + EPS) * nw
            o_ref[0, pl.ds(r0, c), hsl(j)] = (on * _silu(z_ref[0, pl.ds(r0, c), hsl(j)])).astype(o_ref.dtype)
        return tuple(c_new), tuple(n_new), tuple(m_new)

    init = (tuple(c0_ref[0, j] for j in hs), tuple(n0_ref[0, j] for j in hs),
            tuple(m0_ref[0, j][:, :1] for j in hs))
    cms, ns, ms = lax.fori_loop(0, nc, body, init)
    for j in hs:
        c_ref[0, j] = cms[j]
        n_ref[0, j] = ns[j]
        m_ref[0, j] = jnp.broadcast_to(ms[j], (1, LANES))


def _mlstm(proj, gates, nw, c0, n0, m0, out_dtype):
    bsz, t, _ = proj.shape
    c = min(128, t)
    w = ML_PAIR * ML_DH
    col = lambda name: _MAIN_OFF[name] // w
    spec = lambda name: pl.BlockSpec((1, t, w), lambda b, p, o=col(name): (b, 0, o + p))
    st = lambda rows: pl.BlockSpec((1, ML_PAIR, rows, ML_DH), lambda b, p: (b, p, 0, 0))
    if c == LANES:
        kern = functools.partial(_mlstm_wide_kernel, t=t)
        scratch = ([pltpu.VMEM((LANES, LANES), F32), pltpu.VMEM((t, LANES), F32)]
                   + [pltpu.VMEM((ML_PAIR, t, LANES), F32)] * 2)
    else:
        kern = functools.partial(_mlstm_kernel, t=t, c=c)
        scratch = [pltpu.VMEM((LANES, LANES), F32)]
    return pl.pallas_call(
        kern,
        grid=(bsz, ML_HEADS // ML_PAIR),
        in_specs=[spec("m_q"), spec("m_k"), spec("m_v"), spec("m_o"), spec("m_z"),
                  pl.BlockSpec((1, t, LANES), lambda b, p: (b, 0, 0)),
                  pl.BlockSpec((1, ML_DH), lambda b, p: (0, 0)),
                  st(ML_DH), st(1), st(1)],
        out_specs=[pl.BlockSpec((1, t, w), lambda b, p: (b, 0, p)), st(ML_DH), st(1), st(1)],
        out_shape=[jax.ShapeDtypeStruct((bsz, t, GROUP_W), out_dtype),
                   jax.ShapeDtypeStruct((bsz, ML_HEADS, ML_DH, ML_DH), F32),
                   jax.ShapeDtypeStruct((bsz, ML_HEADS, 1, ML_DH), F32),
                   jax.ShapeDtypeStruct((bsz, ML_HEADS, 1, LANES), F32)],
        scratch_shapes=scratch,
        compiler_params=_cparams(("parallel", "parallel")),
        name="mlstm",
    )(proj, proj, proj, proj, proj, gates, nw.reshape(1, -1), c0, n0, m0)


def _gmlp_kernel(u_ref, v_ref, z_ref, lw_ref, lb_ref, ws_ref, bs_ref, o_ref, *vn_refs, l, nl):
    rr = lax.broadcasted_iota(jnp.int32, (l, l), 0)
    cc = lax.broadcasted_iota(jnp.int32, (l, l), 1)
    wts = [jnp.where(rr >= cc, ws_ref[g], 0.0) for g in range(GM_GROUPS)]
    if l >= GM_CHUNK:
        wts = [w.astype(BF16) for w in wts]
    for ch in range(nl):
        rows = slice(ch * l, (ch + 1) * l)
        gv = _gelu(v_ref[0, rows, :])
        mu = jnp.mean(gv, axis=-1, keepdims=True)
        xc = gv - mu
        vn = xc * lax.rsqrt(jnp.mean(xc * xc, axis=-1, keepdims=True) + EPS) * lw_ref[...] + lb_ref[...]
        if vn_refs:
            vn_refs[0][0, rows, :] = vn
        for g in range(GM_GROUPS):
            sl = slice(g * GM_CH, (g + 1) * GM_CH)
            vg = vn[:, sl]
            if l >= GM_CHUNK:
                s = _dot(wts[g], vg.astype(BF16))
            else:
                s = jnp.zeros((l, GM_CH), F32)
                for r in range(l):
                    s = s + wts[g][:, r:r + 1] * vg[r:r + 1, :]
            s = s + bs_ref[:, g:g + 1]
            o_ref[0, rows, sl] = (_gelu(u_ref[0, rows, sl]) * s * _silu(z_ref[0, rows, sl])).astype(o_ref.dtype)


def _gmlp(proj, lw, lb, ws, bs_t, out_dtype, want_vn):
    bsz, t, _ = proj.shape
    l = min(t, GM_CHUNK)
    nl = _pick(t // l, (4, 2, 1))
    col = lambda name: _MAIN_OFF[name] // GROUP_W
    spec = lambda name: pl.BlockSpec((1, l * nl, GROUP_W), lambda b, i, o=col(name): (b, i, o))
    out = pl.BlockSpec((1, l * nl, GROUP_W), lambda b, i: (b, i, 0))
    out_specs = [out] + ([out] if want_vn else [])
    out_shape = ([jax.ShapeDtypeStruct((bsz, t, GROUP_W), out_dtype)]
                 + ([jax.ShapeDtypeStruct((bsz, t, GROUP_W), F32)] if want_vn else []))
    res = pl.pallas_call(
        functools.partial(_gmlp_kernel, l=l, nl=nl),
        grid=(bsz, t // (l * nl)),
        in_specs=[spec("c_u"), spec("c_v"), spec("c_z"),
                  pl.BlockSpec((1, GROUP_W), lambda b, i: (0, 0)),
                  pl.BlockSpec((1, GROUP_W), lambda b, i: (0, 0)),
                  pl.BlockSpec((GM_GROUPS, l, l), lambda b, i: (0, 0, 0)),
                  pl.BlockSpec((l, GM_GROUPS), lambda b, i: (0, 0))],
        out_specs=out_specs,
        out_shape=out_shape,
        compiler_params=_cparams(("parallel", "parallel")),
        name="gmlp",
    )(proj, proj, proj, lw.reshape(1, -1), lb.reshape(1, -1), ws, bs_t)
    return (res[0], res[1]) if want_vn else (res[0], None)


def _moba_prompt_kernel(q_ref, k_ref, v_ref, z_ref, o_ref, kb_scr, vb_scr, km_scr, *, t):
    blk = MB_BLOCK
    nb = t // blk
    scale = MB_DH ** -0.5
    h = pl.program_id(1)
    km_scr[...] = jnp.zeros((LANES, MB_DH), F32)
    for n in range(nb):
        rows = pl.ds(n * blk * MB_HEADS + h, blk, stride=MB_HEADS)
        kn = k_ref[rows, :]
        kb_scr[n * blk:(n + 1) * blk, :] = kn.astype(BF16)
        vb_scr[n * blk:(n + 1) * blk, :] = v_ref[rows, :].astype(BF16)
        km_scr[n:n + 1, :] = jnp.mean(kn, axis=0, keepdims=True)
    kmean = km_scr[...]
    nbp = -(-nb // SUBLANES) * SUBLANES
    blk_row = lax.broadcasted_iota(jnp.int32, (nbp, blk), 0)
    rr = lax.broadcasted_iota(jnp.int32, (blk, blk), 0)
    cc = lax.broadcasted_iota(jnp.int32, (blk, blk), 1)
    log2_scale = float(scale * np.log2(np.e))

    for qi in range(nb):
        rows = slice(qi * blk, (qi + 1) * blk)
        q = q_ref[0, rows, :]
        qb = q.astype(BF16)
        bias = None
        if qi > MB_TOPK:
            gate = _dot_nt(kmean, q, precision=HIGHEST)[0:nbp, :]
            cnt = jnp.zeros((nbp, blk), F32)
            for m in range(qi):
                gm = gate[m:m + 1, :]
                beats = (gm > gate) | ((gm == gate) & (blk_row > m))
                cnt = cnt + jnp.where(beats, 1.0, 0.0)
            bias_t = jnp.where(cnt < MB_TOPK, 0.0, NEG)
            bias = jnp.concatenate([bias_t, jnp.zeros((LANES - nbp, blk), F32)], axis=0).T
        ss = []
        for j in range(qi + 1):
            s = _dot_nt(qb, kb_scr[j * blk:(j + 1) * blk, :]) * log2_scale
            if j == qi:
                s = jnp.where(cc <= rr, s, NEG)
            elif bias is not None:
                s = s + bias[:, j:j + 1]
            ss.append(s)
        lane_tiles = lambda xs: [x[:, i:i + LANES] for x in xs for i in range(0, blk, LANES)]
        m_i = jnp.max(functools.reduce(jnp.maximum, lane_tiles(ss)), axis=-1, keepdims=True)
        ps = [jnp.exp2(s - m_i) for s in ss]
        l_i = jnp.sum(functools.reduce(jnp.add, lane_tiles(ps)), axis=-1, keepdims=True)
        acc = _dot(ps[0].astype(BF16), vb_scr[0:blk, :])
        for j in range(1, qi + 1):
            acc = acc + _dot(ps[j].astype(BF16), vb_scr[j * blk:(j + 1) * blk, :])
        o_ref[0, rows, :] = ((acc / l_i) * _silu(z_ref[0, rows, :])).astype(o_ref.dtype)


def _moba_prompt(q_rope, k4, v4, layer, proj, out_dtype):
    bsz, t, _ = proj.shape
    assert t % MB_BLOCK == 0 and t // MB_BLOCK <= LANES
    zcol = _MAIN_OFF["a_z"] // MB_DH
    hd = pl.BlockSpec((1, t, MB_DH), lambda b, h: (b, 0, h))
    kv = pl.BlockSpec((None, None, t * MB_HEADS, MB_DH), lambda b, h: (layer, b, 0, 0))
    return pl.pallas_call(
        functools.partial(_moba_prompt_kernel, t=t),
        grid=(bsz, MB_HEADS),
        in_specs=[hd, kv, kv, pl.BlockSpec((1, t, MB_DH), lambda b, h: (b, 0, zcol + h))],
        out_specs=hd,
        out_shape=jax.ShapeDtypeStruct((bsz, t, GROUP_W), out_dtype),
        scratch_shapes=[pltpu.VMEM((t, MB_DH), BF16), pltpu.VMEM((t, MB_DH), BF16),
                        pltpu.VMEM((LANES, MB_DH), F32)],
        compiler_params=_cparams(("parallel", "arbitrary")),
        name="moba_prompt",
    )(q_rope, k4, v4, proj)


QROWS = MB_HEADS * SAMPLE_T
ST_M, ST_L, ST_G = 0, 1, 2


def _moba_past_kernel(pt_ref, q_ref, bias_ref, *refs, nbs):
    del pt_ref
    npg = 2 * nbs
    k_refs, v_refs = refs[:npg], refs[npg:2 * npg]
    o_ref, st_ref = refs[2 * npg:]
    scale = MB_DH ** -0.5
    page = k_refs[0].shape[0] // MB_HEADS
    q = q_ref[0]
    qb = q.astype(BF16)
    bias = bias_ref[...]
    lane = lax.broadcasted_iota(jnp.int32, (QROWS, LANES), 1)
    ss = [_dot_nt(qb, r[...].astype(BF16)) * scale + bias for r in k_refs]
    ps, ms, ls = [], [], []
    for n in range(nbs):
        s0, s1 = ss[2 * n], ss[2 * n + 1]
        m = jnp.maximum(jnp.max(s0, axis=-1, keepdims=True), jnp.max(s1, axis=-1, keepdims=True))
        p0, p1 = jnp.exp(s0 - m), jnp.exp(s1 - m)
        ps += [p0.astype(BF16), p1.astype(BF16)]
        ms.append(m)
        ls.append(jnp.sum(p0, axis=-1, keepdims=True) + jnp.sum(p1, axis=-1, keepdims=True))
    for n in range(nbs):
        o_ref[n] = (_dot(ps[2 * n], v_refs[2 * n][...].astype(BF16))
                    + _dot(ps[2 * n + 1], v_refs[2 * n + 1][...].astype(BF16)))
        ksum = (jnp.sum(k_refs[2 * n][...].reshape(page, MB_HEADS, MB_DH), axis=0)
                + jnp.sum(k_refs[2 * n + 1][...].reshape(page, MB_HEADS, MB_DH), axis=0))
        ksum_rows = jnp.concatenate(
            [jnp.broadcast_to(ksum[h:h + 1, :], (SAMPLE_T, MB_DH)) for h in range(MB_HEADS)], axis=0)
        gate = jnp.sum(q * ksum_rows, axis=-1, keepdims=True) * (1.0 / MB_BLOCK)
        st_ref[n] = jnp.where(lane == ST_M, ms[n], jnp.where(lane == ST_L, ls[n], gate))


def _moba_past(layer, q_rows, cache_k, cache_v, page_table):
    db = q_rows.shape[0]
    rows = cache_k.shape[2]
    assert 2 * rows == MB_BLOCK * MB_HEADS
    nb = page_table.shape[1] // 2
    nbs = _pick(nb, (4, 2, 1))
    key_head = np.arange(rows) % MB_HEADS
    row_head = np.arange(QROWS) // SAMPLE_T
    bias = jnp.asarray(np.where(key_head[None, :] == row_head[:, None], 0.0, NEG).astype(np.float32))
    pg = lambda i: pl.BlockSpec((None, None, rows, MB_DH),
                                lambda b, n, pt, i=i: (layer, pt[b, 2 * nbs * n + i], 0, 0))
    pages = [pg(i) for i in range(2 * nbs)]
    part = pl.BlockSpec((None, nbs, QROWS, MB_DH), lambda b, n, pt: (b, n, 0, 0))
    shape = jax.ShapeDtypeStruct((db, nb, QROWS, MB_DH), F32)
    grid_spec = pltpu.PrefetchScalarGridSpec(
        num_scalar_prefetch=1,
        grid=(db, nb // nbs),
        in_specs=[pl.BlockSpec((1, QROWS, MB_DH), lambda b, n, pt: (b, 0, 0)),
                  pl.BlockSpec((QROWS, rows), lambda b, n, pt: (0, 0))] + pages + pages,
        out_specs=[part, part],
    )
    return pl.pallas_call(
        functools.partial(_moba_past_kernel, nbs=nbs),
        grid_spec=grid_spec,
        out_shape=[shape, shape],
        compiler_params=_cparams(("parallel", "parallel")),
        name="moba_past",
    )(page_table, q_rows, bias, *([cache_k] * (2 * nbs)), *([cache_v] * (2 * nbs)))


def _moba_merge_kernel(op_ref, st_ref, q_ref, k_ref, v_ref, z_ref, o_ref, *, nb, t_valid):
    scale = MB_DH ** -0.5
    g = st_ref[0, :, :, ST_G:ST_G + 1]
    m = st_ref[0, :, :, ST_M:ST_M + 1]
    l = st_ref[0, :, :, ST_L:ST_L + 1]
    nidx = lax.broadcasted_iota(jnp.int32, g.shape, 0)
    sel = jnp.zeros(g.shape, jnp.bool_)
    gm = g
    for _ in range(min(MB_TOPK, nb)):
        mx = jnp.max(gm, axis=0, keepdims=True)
        first = jnp.min(jnp.where(gm == mx, nidx, nb), axis=0, keepdims=True)
        pick = nidx == first
        sel = sel | pick
        gm = jnp.where(pick, -jnp.inf, gm)

    rr = lax.broadcasted_iota(jnp.int32, (SAMPLE_T, SAMPLE_T), 0)
    cc = lax.broadcasted_iota(jnp.int32, (SAMPLE_T, SAMPLE_T), 1)
    mo, lo, oo = [], [], []
    for h in range(MB_HEADS):
        sl = slice(h * MB_DH, (h + 1) * MB_DH)
        qh = q_ref[0, h * SAMPLE_T:(h + 1) * SAMPLE_T, :]
        s = _dot_nt(qh.astype(BF16), k_ref[0, :, sl].astype(BF16)) * scale
        s = jnp.where((cc <= rr) & (cc < t_valid), s, NEG)
        mh = jnp.max(s, axis=-1, keepdims=True)
        p = jnp.exp(s - mh)
        mo.append(mh)
        lo.append(jnp.sum(p, axis=-1, keepdims=True))
        oo.append(_dot(p.astype(BF16), v_ref[0, :, sl].astype(BF16)))
    m_own, l_own, o_own = (jnp.concatenate(x, axis=0) for x in (mo, lo, oo))

    m_tot = jnp.maximum(jnp.max(jnp.where(sel, m, NEG), axis=0), m_own)
    w = jnp.where(sel, jnp.exp(m - m_tot[None]), 0.0)
    w_own = jnp.exp(m_own - m_tot)
    den = jnp.sum(w * l, axis=0) + w_own * l_own
    acc = w_own * o_own
    for n in range(nb):
        acc = acc + w[n] * op_ref[0, n]
    o = acc / den
    for h in range(MB_HEADS):
        sl = slice(h * MB_DH, (h + 1) * MB_DH)
        o_ref[0, :, sl] = o[h * SAMPLE_T:(h + 1) * SAMPLE_T, :] * _silu(z_ref[0, :, sl])


def _moba_merge(o_part, stats, q_rows, k_rope, proj, t_valid):
    db, nb = o_part.shape[:2]
    vcol, zcol = _MAIN_OFF["a_v"] // GROUP_W, _MAIN_OFF["a_z"] // GROUP_W
    part = pl.BlockSpec((1, nb, QROWS, MB_DH), lambda b: (b, 0, 0, 0))
    row = pl.BlockSpec((1, SAMPLE_T, GROUP_W), lambda b: (b, 0, 0))
    return pl.pallas_call(
        functools.partial(_moba_merge_kernel, nb=nb, t_valid=t_valid),
        grid=(db,),
        in_specs=[part, part, pl.BlockSpec((1, QROWS, MB_DH), lambda b: (b, 0, 0)), row,
                  pl.BlockSpec((1, SAMPLE_T, GROUP_W), lambda b: (b, 0, vcol)),
                  pl.BlockSpec((1, SAMPLE_T, GROUP_W), lambda b: (b, 0, zcol))],
        out_specs=row,
        out_shape=jax.ShapeDtypeStruct((db, SAMPLE_T, GROUP_W), F32),
        compiler_params=_cparams(("parallel",)),
        name="moba_merge",
    )(o_part, stats, q_rows, k_rope, proj, proj)


def _layer_weights(gla_w_gate_l, ml_b_i_l, ml_b_f_l):
    wg_pad = jnp.concatenate([gla_w_gate_l, jnp.zeros((LANES - GLA_GATE_RANK, gla_w_gate_l.shape[1]), F32)],
                             axis=0)
    bias_row = jnp.concatenate([jnp.zeros((SM_I,), F32), ml_b_i_l, ml_b_f_l,
                                jnp.zeros((LANES - SM_F - ML_HEADS,), F32)]).reshape(1, LANES)
    return wg_pad, bias_row


def _mixers(x, bsz, t, t_valid, layer, lw, rope_tables, gla_s0, ml_state, out_dtype, want_vn):
    (norm_w, w_main, w_small, wg_pad, bias_row, gla_b_gate, gla_norm_w, ml_norm_w,
     gm_ln_w, gm_ln_b, gm_ws, gm_bs_t) = lw
    h, small = _rms_small(x, norm_w, w_small, layer)
    proj = _matmul(h, w_main, layer).reshape(bsz, t, N_MAIN)
    small = small.reshape(bsz, t, LANES)

    out_a, gla_s = _gla(proj, small, wg_pad, gla_b_gate, gla_norm_w, gla_s0, t_valid, out_dtype)

    gates = _gates(small, bias_row, t_valid)
    out_b, ml_c, ml_n, ml_m = _mlstm(proj, gates, ml_norm_w, *ml_state, out_dtype)

    out_c, vn = _gmlp(proj, gm_ln_w, gm_ln_b, gm_ws, gm_bs_t, out_dtype, want_vn)

    q_rope = _rope(proj, _MAIN_OFF["a_q"], rope_tables)
    return proj, (out_a, out_b, out_c), q_rope, gla_s, (ml_c, ml_n, ml_m[..., :1]), vn


def kernel(x_prompt, x_sample, cache_k, cache_v, page_table, state_gla, state_mlstm_C, state_mlstm_n,
           state_mlstm_m, norm_w, w_in, gla_w_gate, gla_b_gate, gla_norm_w, ml_b_i, ml_b_f, ml_norm_w,
           gm_ln_w, gm_ln_b, gm_w_s, gm_b_s, w_out, final_norm_w):
    bp, tp, d = x_prompt.shape
    db, ts, _ = x_sample.shape
    depth = w_in.shape[0]
    page = cache_k.shape[2]
    past_len = page_table.shape[1] * page
    assert w_out.shape[1] == 4 * GROUP_W and ts <= SAMPLE_T
    assert past_len % MB_BLOCK == 0 and tp % MB_BLOCK == 0

    tables_p = _rope_tables(jnp.arange(tp, dtype=jnp.int32))
    tables_s = _rope_tables(past_len + jnp.arange(SAMPLE_T, dtype=jnp.int32))
    cache_k = cache_k.reshape(depth, cache_k.shape[1], page * MB_HEADS, MB_DH)
    cache_v = cache_v.reshape(depth, cache_v.shape[1], page * MB_HEADS, MB_DH)

    yp = x_prompt.reshape(bp * tp, d)
    ys = jnp.pad(x_sample, ((0, 0), (0, SAMPLE_T - ts), (0, 0))).reshape(db * SAMPLE_T, d)
    dt_p = BF16
    dt_s = F32

    zero_gla = jnp.zeros((bp, GLA_HEADS // 2, GLA_DV, LANES), F32)
    zero_ml = (jnp.zeros((bp, ML_HEADS, ML_DH, ML_DH), F32), jnp.zeros((bp, ML_HEADS, 1, ML_DH), F32),
               jnp.zeros((bp, ML_HEADS, 1, LANES), F32))

    outs = {n: [] for n in ("ks", "vs", "gp", "gs", "cp", "cs", "np", "ns", "mp", "ms", "vv")}
    lp = min(tp, GM_CHUNK)
    w_t = jnp.swapaxes(w_in, 1, 2)
    w_main = _wprep(w_t)
    w_small = _wsmall(w_t)
    w_out4 = w_out.reshape(depth, 4, GROUP_W, d)
    kv_stacks = None
    for l in range(depth):
        wg_pad, bias_row = _layer_weights(gla_w_gate[l], ml_b_i[l], ml_b_f[l])
        common = (norm_w[l], w_main, w_small, wg_pad, bias_row, gla_b_gate[l], gla_norm_w[l], ml_norm_w[l],
                  gm_ln_w[l], gm_ln_b[l])

        lw = common + (gm_w_s[l][:, :lp, :lp], gm_b_s[l][:, :lp].T)
        proj, mix, q_rope, gla_s, ml_s, _ = _mixers(
            yp, bp, tp, tp, l, lw, tables_p, zero_gla, zero_ml, dt_p, want_vn=False)
        kv_stacks = _kv_heads(proj, tables_p, l, depth, kv_stacks)
        out_d = _moba_prompt(q_rope, kv_stacks[0], kv_stacks[1], l, proj, dt_p)
        yp = _out_proj([a.reshape(bp * tp, GROUP_W) for a in mix + (out_d,)], w_out4, l, yp)
        outs["gp"].append(_gla_state_from_pairs(gla_s))
        outs["cp"].append(ml_s[0])
        outs["np"].append(ml_s[1][:, :, 0, :])
        outs["mp"].append(ml_s[2][:, :, 0, 0])

        lw = common + (gm_w_s[l][:, :SAMPLE_T, :SAMPLE_T], gm_b_s[l][:, :SAMPLE_T].T)
        ml_state = (state_mlstm_C[l], state_mlstm_n[l][:, :, None, :],
                    jnp.broadcast_to(state_mlstm_m[l][:, :, None, None], (db, ML_HEADS, 1, LANES)))
        proj, mix, q_rope, gla_s, ml_s, vn = _mixers(
            ys, db, SAMPLE_T, ts, l, lw, tables_s, _gla_state_to_pairs(state_gla[l]), ml_state, dt_s,
            want_vn=True)
        k_rope = _rope(proj, _MAIN_OFF["a_k"], tables_s)
        v_new = proj[:, :, _MAIN_OFF["a_v"]:_MAIN_OFF["a_v"] + GROUP_W]
        q_rows = (q_rope.reshape(db, SAMPLE_T, MB_HEADS, MB_DH).transpose(0, 2, 1, 3)
                  .reshape(db, QROWS, MB_DH))
        o_part, stats = _moba_past(l, q_rows, cache_k, cache_v, page_table)
        out_d = _moba_merge(o_part, stats, q_rows, k_rope, proj, ts)
        ys = _out_proj([a.reshape(db * SAMPLE_T, GROUP_W) for a in mix + (out_d,)], w_out4, l, ys)
        outs["ks"].append(k_rope[:, :ts].reshape(db, ts, MB_HEADS, MB_DH))
        outs["vs"].append(v_new[:, :ts].reshape(db, ts, MB_HEADS, MB_DH))
        outs["gs"].append(_gla_state_from_pairs(gla_s))
        outs["cs"].append(ml_s[0])
        outs["ns"].append(ml_s[1][:, :, 0, :])
        outs["ms"].append(ml_s[2][:, :, 0, 0])
        outs["vv"].append(vn[:, :ts])

    y_prompt = _rms(yp, final_norm_w).reshape(bp, tp, d)
    y_sample = _rms(ys, final_norm_w).reshape(db, SAMPLE_T, d)[:, :ts]
    st = jnp.stack
    k_prompt = kv_stacks[0].reshape(depth, bp, tp, MB_HEADS, MB_DH)
    v_prompt = kv_stacks[1].reshape(depth, bp, tp, MB_HEADS, MB_DH)
    return (y_prompt, y_sample, k_prompt, v_prompt, st(outs["ks"]), st(outs["vs"]),
            st(outs["gp"]), st(outs["gs"]), st(outs["cp"]), st(outs["cs"]), st(outs["np"]), st(outs["ns"]),
            st(outs["mp"]), st(outs["ms"]), st(outs["vv"]))
```

```python
import functools

import numpy as np
import jax
import jax.numpy as jnp
from jax import lax
from jax.experimental import pallas as pl
from jax.experimental.pallas import tpu as pltpu

F32 = jnp.float32
BF16 = jnp.bfloat16
HIGHEST = lax.Precision.HIGHEST

GROUP_W = 1024
GLA_HEADS, GLA_DK, GLA_DV = 8, 64, 128
GLA_GATE_RANK, GLA_GATE_NORM = 16, 16.0
ML_HEADS, ML_DH = 8, 128
GM_GROUPS, GM_CH, GM_CHUNK = 8, 128, 128
MB_HEADS, MB_DH, MB_BLOCK, MB_TOPK = 8, 128, 256, 3
ROT_DIM, ROPE_THETA = 32, 500000.0
EPS = 1e-6

LANES = 128
SUBLANES = 8
VMEM_LIMIT = 56 * 1024 * 1024
VMEM_LIMIT_TALL = 60 * 1024 * 1024

NEG = -1e30
SAMPLE_T = SUBLANES

_MAIN_ORDER = ("g_q", "g_k", "g_v", "g_z", "m_q", "m_k", "m_v", "m_o", "m_z",
               "c_u", "c_v", "c_z", "a_q", "a_k", "a_v", "a_z")
_SPLIT_NAMES = ("g_q", "g_k", "g_v", "g_lr", "g_z", "m_q", "m_k", "m_v", "m_i", "m_f", "m_o", "m_z",
                "c_u", "c_v", "c_z", "a_q", "a_k", "a_v", "a_z")
_SPLIT_W = (GLA_HEADS * GLA_DK, GLA_HEADS * GLA_DK, GLA_HEADS * GLA_DV, GLA_GATE_RANK, GROUP_W,
            GROUP_W, GROUP_W, GROUP_W, ML_HEADS, ML_HEADS, GROUP_W, GROUP_W,
            GROUP_W, GROUP_W, GROUP_W, GROUP_W, GROUP_W, GROUP_W, GROUP_W)
_SRC_OFF = dict(zip(_SPLIT_NAMES, np.concatenate([[0], np.cumsum(_SPLIT_W)[:-1]]).tolist()))
_SRC_W = dict(zip(_SPLIT_NAMES, _SPLIT_W))
_MAIN_OFF = {}
_off = 0
for _n in _MAIN_ORDER:
    _MAIN_OFF[_n] = _off
    _off += _SRC_W[_n]
N_MAIN = _off
SM_LR, SM_I, SM_F = 0, GLA_GATE_RANK, GLA_GATE_RANK + ML_HEADS


def _cparams(sem, vmem_limit=VMEM_LIMIT):
    return pltpu.CompilerParams(dimension_semantics=sem, vmem_limit_bytes=vmem_limit)


def _pick(n, cands):
    for c in cands:
        if n % c == 0:
            return c
    return n


def _silu(x):
    return x / (1.0 + jnp.exp(-x))


def _sigmoid(x):
    return 1.0 / (1.0 + jnp.exp(-x))


def _log_sigmoid(x):
    return jnp.minimum(x, 0.0) - jnp.log(1.0 + jnp.exp(-jnp.abs(x)))


def _gelu(x):
    c = np.sqrt(2.0 / np.pi).astype(np.float32)
    return 0.5 * x * (1.0 + jnp.tanh(c * (x + 0.044715 * (x * x * x))))


def _dot_nt(a, b, precision=None):
    return lax.dot_general(a, b, (((1,), (1,)), ((), ())), preferred_element_type=F32, precision=precision)


def _dot_tn(a, b, precision=None):
    return lax.dot_general(a, b, (((0,), (0,)), ((), ())), preferred_element_type=F32, precision=precision)


def _dot(a, b, precision=None):
    return jnp.dot(a, b, preferred_element_type=F32, precision=precision)


def _rms_small_kernel(x_ref, nw_ref, ws_ref, h_ref, sm_ref):
    x = x_ref[...]
    ms = jnp.mean(x * x, axis=-1, keepdims=True)
    hb = (x * lax.rsqrt(ms + EPS) * nw_ref[...]).astype(BF16)
    h_ref[...] = hb
    sm_ref[...] = _dot(hb, ws_ref[...])


def _rms_small(x, nw, w_small, layer):
    m, d = x.shape
    bm = _pick(m, (256, 128, 64, 32, 16, 8))
    return pl.pallas_call(
        _rms_small_kernel,
        grid=(m // bm,),
        in_specs=[pl.BlockSpec((bm, d), lambda i: (i, 0)),
                  pl.BlockSpec((1, d), lambda i: (0, 0)),
                  pl.BlockSpec((None, d, LANES), lambda i: (layer, 0, 0))],
        out_specs=[pl.BlockSpec((bm, d), lambda i: (i, 0)),
                   pl.BlockSpec((bm, LANES), lambda i: (i, 0))],
        out_shape=[jax.ShapeDtypeStruct((m, d), BF16), jax.ShapeDtypeStruct((m, LANES), F32)],
        compiler_params=_cparams(("parallel",)),
        name="rms_small",
    )(x, nw.reshape(1, d), w_small)


def _rms_kernel(x_ref, nw_ref, o_ref):
    x = x_ref[...]
    ms = jnp.mean(x * x, axis=-1, keepdims=True)
    o_ref[...] = x * lax.rsqrt(ms + EPS) * nw_ref[...]


def _rms(x, nw):
    m, d = x.shape
    bm = _pick(m, (256, 128, 64, 32, 16, 8))
    return pl.pallas_call(
        _rms_kernel,
        grid=(m // bm,),
        in_specs=[pl.BlockSpec((bm, d), lambda i: (i, 0)), pl.BlockSpec((1, d), lambda i: (0, 0))],
        out_specs=pl.BlockSpec((bm, d), lambda i: (i, 0)),
        out_shape=jax.ShapeDtypeStruct((m, d), F32),
        compiler_params=_cparams(("parallel",)),
        name="rms_final",
    )(x, nw.reshape(1, d))


WP_BN = 1024
WP_B1 = _MAIN_OFF["g_z"] // WP_BN
WP_B2 = _MAIN_OFF["m_o"] // WP_BN
WP_S1 = GLA_GATE_RANK
WP_S2 = GLA_GATE_RANK + 2 * ML_HEADS
assert _MAIN_OFF["g_z"] % WP_BN == 0 and _MAIN_OFF["m_o"] % WP_BN == 0 and N_MAIN % WP_BN == 0
assert WP_S1 % SUBLANES == 0 and WP_S2 % SUBLANES == 0 and sum(_SPLIT_W) % WP_S2 == 0


def _wprep_kernel(a_ref, t_ref, o_ref):
    j = pl.program_id(2)

    def emit(s):
        x = a_ref[...] if s == 0 else jnp.concatenate([a_ref[...], t_ref[...]], axis=0)[s:s + WP_BN]
        o_ref[...] = x.T.astype(BF16)

    @pl.when(j < WP_B1)
    def _():
        emit(0)

    @pl.when((j >= WP_B1) & (j < WP_B2))
    def _():
        emit(WP_S1)

    @pl.when(j >= WP_B2)
    def _():
        emit(WP_S2)


def _wprep(w_t):
    depth, _, d = w_t.shape
    kb = _pick(d, (2048, 1024, 512, 256, 128))
    return pl.pallas_call(
        _wprep_kernel,
        grid=(depth, d // kb, N_MAIN // WP_BN),
        in_specs=[pl.BlockSpec((None, WP_BN, kb), lambda l, i, j: (l, j, i)),
                  pl.BlockSpec((None, WP_S2, kb), lambda l, i, j: (l, (j + 1) * (WP_BN // WP_S2), i))],
        out_specs=pl.BlockSpec((None, kb, WP_BN), lambda l, i, j: (l, i, j)),
        out_shape=jax.ShapeDtypeStruct((depth, d, N_MAIN), BF16),
        compiler_params=_cparams(("parallel", "parallel", "parallel")),
        name="w_prep",
    )(w_t, w_t)


def _wsmall_kernel(lr_ref, if_ref, o_ref):
    kb = lr_ref.shape[1]
    x = jnp.concatenate([lr_ref[...], if_ref[...],
                         jnp.zeros((LANES - GLA_GATE_RANK - 2 * ML_HEADS, kb), F32)], axis=0)
    o_ref[...] = x.T.astype(BF16)


def _wsmall(w_t):
    depth, _, d = w_t.shape
    kb = _pick(d, (512, 256, 128))
    lr, gi = _SRC_OFF["g_lr"], _SRC_OFF["m_i"]
    assert _SRC_OFF["m_f"] == gi + ML_HEADS and lr % GLA_GATE_RANK == 0 and gi % (2 * ML_HEADS) == 0
    return pl.pallas_call(
        _wsmall_kernel,
        grid=(depth, d // kb),
        in_specs=[pl.BlockSpec((None, GLA_GATE_RANK, kb), lambda l, i: (l, lr // GLA_GATE_RANK, i)),
                  pl.BlockSpec((None, 2 * ML_HEADS, kb), lambda l, i: (l, gi // (2 * ML_HEADS), i))],
        out_specs=pl.BlockSpec((None, kb, LANES), lambda l, i: (l, i, 0)),
        out_shape=jax.ShapeDtypeStruct((depth, d, LANES), BF16),
        compiler_params=_cparams(("parallel", "parallel")),
        name="w_small",
    )(w_t, w_t)


def _mm_kernel(a_ref, as_ref, b_ref, o_ref, os_ref):
    w = b_ref[...]
    o_ref[...] = _dot(a_ref[...], w)

    @pl.when(pl.program_id(1) == 0)
    def _():
        os_ref[...] = _dot(as_ref[...], w)


def _in_proj(a, a_s, w_all, layer):
    m, k = a.shape
    ms = a_s.shape[0]
    n = w_all.shape[2]
    bm = _pick(m, (1024, 512, 256, 128, 64))
    bn = _pick(n, (1024, 512, 256, 128))
    return pl.pallas_call(
        _mm_kernel,
        grid=(n // bn, m // bm),
        in_specs=[pl.BlockSpec((bm, k), lambda j, i: (i, 0)),
                  pl.BlockSpec((ms, k), lambda j, i: (0, 0)),
                  pl.BlockSpec((None, k, bn), lambda j, i: (layer, 0, j))],
        out_specs=[pl.BlockSpec((bm, bn), lambda j, i: (i, j)),
                   pl.BlockSpec((ms, bn), lambda j, i: (0, j))],
        out_shape=[jax.ShapeDtypeStruct((m, n), F32), jax.ShapeDtypeStruct((ms, n), F32)],
        compiler_params=_cparams(("parallel", "arbitrary")),
        name="in_proj",
    )(a, a_s, w_all)


def _out_kernel(a0, a1, a2, a3, w_ref, x_ref, o_ref):
    acc = x_ref[...]
    for g, a in enumerate((a0, a1, a2, a3)):
        acc = acc + _dot(a[...].astype(BF16), w_ref[g].astype(BF16))
    o_ref[...] = acc


def _out_proj(mix, w_all, layer, x):
    m, d = x.shape
    bm = _pick(m, (2048, 1024, 512, 256, 128, 64))
    bn = _pick(d, (256, 128))
    a_spec = pl.BlockSpec((bm, GROUP_W), lambda i, j: (i, 0))
    return pl.pallas_call(
        _out_kernel,
        grid=(m // bm, d // bn),
        in_specs=[a_spec, a_spec, a_spec, a_spec,
                  pl.BlockSpec((None, 4, GROUP_W, bn), lambda i, j: (layer, 0, 0, j)),
                  pl.BlockSpec((bm, bn), lambda i, j: (i, j))],
        out_specs=pl.BlockSpec((bm, bn), lambda i, j: (i, j)),
        out_shape=jax.ShapeDtypeStruct((m, d), F32),
        compiler_params=_cparams(("parallel", "parallel"), VMEM_LIMIT_TALL),
        name="out_proj",
    )(*mix, w_all, x)


def _rope_kernel(x_ref, a_ref, b_ref, c_ref, o_ref):
    a, b, c = a_ref[...], b_ref[...], c_ref[...]
    for h in range(MB_HEADS):
        sl = slice(h * MB_DH, (h + 1) * MB_DH)
        x = x_ref[0, :, sl]
        o_ref[0, :, sl] = (x * a + pltpu.roll(x, MB_DH - ROT_DIM // 2, 1) * b
                           + pltpu.roll(x, ROT_DIM // 2, 1) * c)


def _rope_tables(pos):
    half = ROT_DIM // 2
    inv_freq = jnp.power(ROPE_THETA, -jnp.arange(0, ROT_DIM, 2, dtype=F32) / ROT_DIM)
    ang = pos.astype(F32)[:, None] * inv_freq[None, :]
    cos, sin = jnp.cos(ang), jnp.sin(ang)
    t = pos.shape[0]
    a = jnp.concatenate([cos, cos, jnp.ones((t, MB_DH - ROT_DIM), F32)], axis=1)
    b = jnp.concatenate([-sin, jnp.zeros((t, MB_DH - half), F32)], axis=1)
    c = jnp.concatenate([jnp.zeros((t, half), F32), sin, jnp.zeros((t, MB_DH - ROT_DIM), F32)], axis=1)
    return a, b, c


def _rope(proj, col, tables):
    bsz, t, _ = proj.shape
    bt = _pick(t, (512, 256, 128, 64, 32, 16, 8))
    tab = pl.BlockSpec((bt, MB_DH), lambda b, i: (i, 0))
    return pl.pallas_call(
        _rope_kernel,
        grid=(bsz, t // bt),
        in_specs=[pl.BlockSpec((1, bt, GROUP_W), lambda b, i: (b, i, col // GROUP_W)), tab, tab, tab],
        out_specs=pl.BlockSpec((1, bt, GROUP_W), lambda b, i: (b, i, 0)),
        out_shape=jax.ShapeDtypeStruct((bsz, t, GROUP_W), F32),
        compiler_params=_cparams(("parallel", "parallel")),
        name="rope",
    )(proj, *tables)


def _kv_heads_kernel(k_ref, v_ref, a_ref, b_ref, c_ref, *refs):
    k4_ref, v4_ref = refs[-2:]
    a, b, c = a_ref[...], b_ref[...], c_ref[...]
    bt = a.shape[0]
    for h in range(MB_HEADS):
        sl = slice(h * MB_DH, (h + 1) * MB_DH)
        x = k_ref[0, :, sl]
        rows = pl.ds(h, bt, stride=MB_HEADS)
        k4_ref[rows, :] = (x * a + pltpu.roll(x, MB_DH - ROT_DIM // 2, 1) * b
                           + pltpu.roll(x, ROT_DIM // 2, 1) * c)
        v4_ref[rows, :] = v_ref[0, :, sl]


def _kv_heads(proj, tables, layer, depth, stacks):
    bsz, t, _ = proj.shape
    bt = _pick(t, (256, 128, 64, 32, 16, 8))
    tab = pl.BlockSpec((bt, MB_DH), lambda b, i: (i, 0))
    kcol, vcol = _MAIN_OFF["a_k"] // GROUP_W, _MAIN_OFF["a_v"] // GROUP_W
    out = pl.BlockSpec((None, None, bt * MB_HEADS, MB_DH), lambda b, i: (layer, b, i, 0))
    shape = jax.ShapeDtypeStruct((depth, bsz, t * MB_HEADS, MB_DH), F32)
    in_specs = [pl.BlockSpec((1, bt, GROUP_W), lambda b, i: (b, i, kcol)),
                pl.BlockSpec((1, bt, GROUP_W), lambda b, i: (b, i, vcol)), tab, tab, tab]
    aliases = {}
    if stacks is not None:
        in_specs += [pl.BlockSpec(memory_space=pl.ANY), pl.BlockSpec(memory_space=pl.ANY)]
        aliases = {5: 0, 6: 1}
    return pl.pallas_call(
        _kv_heads_kernel,
        grid=(bsz, t // bt),
        in_specs=in_specs,
        out_specs=[out, out],
        out_shape=[shape, shape],
        input_output_aliases=aliases,
        compiler_params=_cparams(("parallel", "parallel")),
        name="kv_heads",
    )(proj, proj, *tables, *(stacks or ()))


def _gla_levels(c):
    return [c >> i for i in range(1, c.bit_length())]


def _gla_sum_matrices(c, nch):
    sup = c * nch
    t = np.arange(sup)[:, None]
    r = np.arange(sup)[None, :]
    cb = (t // c) * c
    mats = [(r >= cb) & (r <= t), (r > t) & (r <= cb + c - 1)]
    for hs in _gla_levels(c):
        base = (t // (2 * hs)) * (2 * hs)
        ref = base + hs - 1
        mats.append(np.where(t - base >= hs, (r > ref) & (r <= t), (r > t) & (r <= ref)))
    return np.concatenate(mats, axis=0).astype(np.float32)


def _gla_kernel(q_ref, k_ref, v_ref, z_ref, sm_ref, wg_ref, bg_ref, nw_ref, s0_ref, sum_ref,
                o_ref, s_ref, g_scr, *, t, c, nch, t_valid):
    dk = GLA_DK
    sup = c * nch
    levels = _gla_levels(c)
    split = sum_ref.dtype == BF16
    x = _dot(sm_ref[0], wg_ref[...], precision=HIGHEST) + bg_ref[...]
    g_all = _log_sigmoid(x) * (1.0 / GLA_GATE_NORM)
    if t_valid < t:
        rows_t = lax.broadcasted_iota(jnp.int32, (t, LANES), 0)
        g_all = jnp.where(rows_t < t_valid, g_all, 0.0)
    g_scr[...] = g_all

    lane = lax.broadcasted_iota(jnp.int32, (sup, LANES), 1)
    row = lax.broadcasted_iota(jnp.int32, (sup, LANES), 0)
    head0 = lane < dk
    rr = lax.broadcasted_iota(jnp.int32, (sup, sup), 0)
    cc = lax.broadcasted_iota(jnp.int32, (sup, sup), 1)
    mm = BF16 if split else F32
    prec = None if split else HIGHEST
    nw = nw_ref[...]

    def body(si, s2t):
        r0 = pl.multiple_of(si * sup, sup)
        q = q_ref[0, pl.ds(r0, sup), :] * (dk ** -0.5)
        k = k_ref[0, pl.ds(r0, sup), :]
        if t_valid < t:
            k = jnp.where(row < t_valid, k, 0.0)
        g = g_scr[pl.ds(r0, sup), :]
        v = v_ref[0, pl.ds(r0, sup), :]
        z = z_ref[0, pl.ds(r0, sup), :]
        if split:
            g1 = g.astype(BF16)
            r1 = g - g1.astype(F32)
            g2 = r1.astype(BF16)
            g3 = (r1 - g2.astype(F32)).astype(BF16)
            xs = _dot(sum_ref[...], jnp.concatenate([g1, g2, g3], axis=1))
            xs = xs[:, :LANES] + xs[:, LANES:2 * LANES] + xs[:, 2 * LANES:]
        else:
            xs = _dot(sum_ref[...], g, precision=HIGHEST)
        b = xs[0:sup]
        to_end = xs[sup:2 * sup]

        qk = q * k
        att0 = jnp.where(rr == cc, jnp.sum(jnp.where(head0, qk, 0.0), axis=-1, keepdims=True), 0.0)
        att1 = jnp.where(rr == cc, jnp.sum(jnp.where(head0, 0.0, qk), axis=-1, keepdims=True), 0.0)
        for i, hs in enumerate(levels):
            e = jnp.exp(xs[(2 + i) * sup:(3 + i) * sup])
            upper = (row & hs) != 0
            qt = jnp.where(upper, q * e, 0.0)
            kl = jnp.where(upper, 0.0, k * e).astype(mm)
            sh = (2 * hs).bit_length() - 1
            same = (rr >> sh) == (cc >> sh)
            att0 = att0 + jnp.where(same, _dot_nt(jnp.where(head0, qt, 0.0).astype(mm), kl, prec), 0.0)
            att1 = att1 + jnp.where(same, _dot_nt(jnp.where(head0, 0.0, qt).astype(mm), kl, prec), 0.0)

        kt = k * jnp.exp(to_end)
        qe = q * jnp.exp(b)
        qe0, qe1 = jnp.where(head0, qe, 0.0), jnp.where(head0, 0.0, qe)
        kt0, kt1 = jnp.where(head0, kt, 0.0), jnp.where(head0, 0.0, kt)
        v0, v1 = v[:, :GLA_DV], v[:, GLA_DV:]
        rows = [slice(ch * c, (ch + 1) * c) for ch in range(nch)]
        upds = [_dot_tn(jnp.concatenate([v0[r], v1[r]], axis=0).astype(mm),
                        jnp.concatenate([kt0[r], kt1[r]], axis=0).astype(mm), prec) for r in rows]
        o_int = []
        for ch, r in enumerate(rows):
            lhs = jnp.concatenate([qe0[r], qe1[r]], axis=0).astype(mm)
            o_int.append(_dot_nt(lhs, s2t.astype(mm), prec))
            s2t = jnp.exp(b[ch * c + c - 1:ch * c + c, :]) * s2t + upds[ch]

        for h, (att, vh) in enumerate(((att0, v0), (att1, v1))):
            o = (jnp.concatenate([oi[h * c:(h + 1) * c, :] for oi in o_int], axis=0)
                 + _dot(att.astype(mm), vh.astype(mm), prec))
            on = o * lax.rsqrt(jnp.mean(o * o, axis=-1, keepdims=True) + EPS) * nw
            zh = z[:, h * GLA_DV:(h + 1) * GLA_DV]
            o_ref[0, pl.ds(r0, sup), h * GLA_DV:(h + 1) * GLA_DV] = (on * _silu(zh)).astype(o_ref.dtype)
        return s2t

    s_ref[0, 0] = lax.fori_loop(0, t // sup, body, s0_ref[0, 0])


def _gla(proj, small, wg_pad, bg, nw, s0t, t_valid, out_dtype):
    bsz, t, _ = proj.shape
    c = min(LANES, t)
    nch = 1
    pairs = GLA_HEADS // 2
    qb, kb = _MAIN_OFF["g_q"] // LANES, _MAIN_OFF["g_k"] // LANES
    vb, zb = _MAIN_OFF["g_v"] // (2 * GLA_DV), _MAIN_OFF["g_z"] // (2 * GLA_DV)
    kern = functools.partial(_gla_kernel, t=t, c=c, nch=nch, t_valid=t_valid)
    sums = _gla_sum_matrices(c, nch)
    sums = jnp.asarray(sums, BF16 if (c * nch) % 16 == 0 else F32)
    return pl.pallas_call(
        kern,
        grid=(bsz, pairs),
        in_specs=[pl.BlockSpec((1, t, LANES), lambda b, p: (b, 0, qb + p)),
                  pl.BlockSpec((1, t, LANES), lambda b, p: (b, 0, kb + p)),
                  pl.BlockSpec((1, t, 2 * GLA_DV), lambda b, p: (b, 0, vb + p)),
                  pl.BlockSpec((1, t, 2 * GLA_DV), lambda b, p: (b, 0, zb + p)),
                  pl.BlockSpec((1, t, LANES), lambda b, p: (b, 0, 0)),
                  pl.BlockSpec((LANES, LANES), lambda b, p: (0, p)),
                  pl.BlockSpec((1, LANES), lambda b, p: (0, p)),
                  pl.BlockSpec((1, GLA_DV), lambda b, p: (0, 0)),
                  pl.BlockSpec((1, 1, GLA_DV, LANES), lambda b, p: (b, p, 0, 0)),
                  pl.BlockSpec(sums.shape, lambda b, p: (0, 0))],
        out_specs=[pl.BlockSpec((1, t, 2 * GLA_DV), lambda b, p: (b, 0, p)),
                   pl.BlockSpec((1, 1, GLA_DV, LANES), lambda b, p: (b, p, 0, 0))],
        out_shape=[jax.ShapeDtypeStruct((bsz, t, GROUP_W), out_dtype),
                   jax.ShapeDtypeStruct((bsz, pairs, GLA_DV, LANES), F32)],
        scratch_shapes=[pltpu.VMEM((t, LANES), F32)],
        compiler_params=_cparams(("parallel", "parallel")),
        name="gla",
    )(proj, proj, proj, proj, small, wg_pad, bg.reshape(1, -1), nw.reshape(1, -1), s0t, sums)


def _gla_state_to_pairs(s):
    bsz = s.shape[0]
    s = s.reshape(bsz, GLA_HEADS // 2, 2, GLA_DK, GLA_DV)
    return s.transpose(0, 1, 4, 2, 3).reshape(bsz, GLA_HEADS // 2, GLA_DV, 2 * GLA_DK)


def _gla_state_from_pairs(s):
    bsz = s.shape[0]
    s = s.reshape(bsz, GLA_HEADS // 2, GLA_DV, 2, GLA_DK)
    return s.transpose(0, 1, 3, 4, 2).reshape(bsz, GLA_HEADS, GLA_DK, GLA_DV)


def _gates_kernel(sm_ref, bias_ref, o_ref, *, t_valid):
    x = sm_ref[0] + bias_ref[...]
    lane = lax.broadcasted_iota(jnp.int32, x.shape, 1)
    is_f = (lane >= SM_F) & (lane < SM_F + ML_HEADS)
    out = jnp.where(is_f, _log_sigmoid(x), x)
    if t_valid is not None:
        row = lax.broadcasted_iota(jnp.int32, x.shape, 0)
        out = jnp.where(row < t_valid, out, jnp.where(is_f, 0.0, NEG))
    o_ref[0] = out


def _gates(small, bias_row, t_valid):
    bsz, t, _ = small.shape
    bt = t if t_valid < t else _pick(t, (1024, 512, 256, 128, 64, 32, 16, 8))
    return pl.pallas_call(
        functools.partial(_gates_kernel, t_valid=t_valid if t_valid < t else None),
        grid=(bsz, t // bt),
        in_specs=[pl.BlockSpec((1, bt, LANES), lambda b, i: (b, i, 0)),
                  pl.BlockSpec((1, LANES), lambda b, i: (0, 0))],
        out_specs=pl.BlockSpec((1, bt, LANES), lambda b, i: (b, i, 0)),
        out_shape=jax.ShapeDtypeStruct((bsz, t, LANES), F32),
        compiler_params=_cparams(("parallel", "parallel")),
        name="ml_gates",
    )(small, bias_row)


ML_PAIR = 2


def _mlstm_kernel(q_ref, k_ref, v_ref, og_ref, z_ref, gt_ref, nw_ref,
                  c0_ref, n0_ref, m0_ref, o_ref, c_ref, n_ref, m_ref, t_scr, *, t, c):
    h0 = pl.program_id(1) * ML_PAIR
    hs = range(ML_PAIR)
    lane = lax.broadcasted_iota(jnp.int32, (c, LANES), 1)
    rr = lax.broadcasted_iota(jnp.int32, (c, c), 0)
    cc = lax.broadcasted_iota(jnp.int32, (c, c), 1)
    causal = rr >= cc
    nw = nw_ref[...]
    hsl = lambda j: slice(j * ML_DH, (j + 1) * ML_DH)

    def body(ci, carry):
        cms, ns, m_prevs = carry
        r0 = pl.multiple_of(ci * c, c)
        gt = gt_ref[0, pl.ds(r0, c), :]
        if c == LANES:
            t_scr[...] = gt.T
        qs = [q_ref[0, pl.ds(r0, c), hsl(j)] * (ML_DH ** -0.5) for j in hs]
        ks = [k_ref[0, pl.ds(r0, c), hsl(j)] for j in hs]
        vs = [v_ref[0, pl.ds(r0, c), hsl(j)] for j in hs]
        s_qk = [_dot_nt(qs[j], ks[j], precision=HIGHEST) for j in hs]
        s_qc = [_dot_nt(qs[j], cms[j], precision=HIGHEST) for j in hs]

        d, inter, m_t, i_cols, fc_cols = [], [], [], [], []
        for j in hs:
            i_col = jnp.sum(jnp.where(lane == SM_I + h0 + j, gt, 0.0), axis=-1, keepdims=True)
            f_col = jnp.sum(jnp.where(lane == SM_F + h0 + j, gt, 0.0), axis=-1, keepdims=True)
            if c == LANES:
                i_row = t_scr[pl.ds(SM_I + h0 + j, 1), :]
                f_row = t_scr[pl.ds(SM_F + h0 + j, 1), :]
            else:
                i_row = jnp.sum(jnp.where(rr == cc, i_col, 0.0), axis=0, keepdims=True)
                f_row = jnp.sum(jnp.where(rr == cc, f_col, 0.0), axis=0, keepdims=True)
            fc_col = jnp.sum(jnp.where(causal, f_row, 0.0), axis=-1, keepdims=True)
            fc_row = jnp.sum(jnp.where(rr <= cc, f_col, 0.0), axis=0, keepdims=True)
            dj = jnp.where(causal, fc_col - fc_row + i_row, NEG)
            d.append(dj)
            inter.append(fc_col + m_prevs[j])
            m_t.append(jnp.maximum(inter[j], jnp.max(dj, axis=-1, keepdims=True)))
            i_cols.append(i_col)
            fc_cols.append(fc_col)

        w_state = [jnp.exp(inter[j] - m_t[j]) for j in hs]
        qk = [s_qk[j] * jnp.exp(d[j] - m_t[j]) for j in hs]
        pv = [_dot(qk[j], vs[j], precision=HIGHEST) for j in hs]
        w_end, dec, m_new = [], [], []
        for j in hs:
            m_new.append(m_t[j][c - 1:c, :])
            f_end = fc_cols[j][c - 1:c, :]
            w_end.append(jnp.exp(f_end - fc_cols[j] + i_cols[j] - m_new[j]))
            dec.append(jnp.exp(f_end + m_prevs[j] - m_new[j]))
        upd = [_dot_tn(vs[j] * w_end[j], ks[j], precision=HIGHEST) for j in hs]

        c_new, n_new = [], []
        for j in hs:
            num = w_state[j] * s_qc[j] + pv[j]
            den = (w_state[j] * jnp.sum(qs[j] * ns[j], axis=-1, keepdims=True)
                   + jnp.sum(qk[j], axis=-1, keepdims=True))
            hout = num / jnp.maximum(jnp.abs(den), jnp.exp(-m_t[j]))
            c_new.append(dec[j] * cms[j] + upd[j])
            n_new.append(dec[j] * ns[j] + jnp.sum(ks[j] * w_end[j], axis=0, keepdims=True))
            o = _sigmoid(og_ref[0, pl.ds(r0, c), hsl(j)]) * hout
            on = o * lax.rsqrt(jnp.mean(o * o, axis=-1, keepdims=True) + EPS) * nw
            o_ref[0, pl.ds(r0, c), hsl(j)] = (on * _silu(z_ref[0, pl.ds(r0, c), hsl(j)])).astype(o_ref.dtype)
        return tuple(c_new), tuple(n_new), tuple(m_new)

    init = (tuple(c0_ref[0, j] for j in hs), tuple(n0_ref[0, j] for j in hs),
            tuple(m0_ref[0, j][:, :1] for j in hs))
    cms, ns, ms = lax.fori_loop(0, t // c, body, init)
    for j in hs:
        c_ref[0, j] = cms[j]
        n_ref[0, j] = ns[j]
        m_ref[0, j] = jnp.broadcast_to(ms[j], (1, LANES))


def _split3(x):
    a = x.astype(BF16)
    r = x - a.astype(F32)
    b = r.astype(BF16)
    return a, b, (r - b.astype(F32)).astype(BF16)


def _mlstm_wide_kernel(q_ref, k_ref, v_ref, og_ref, z_ref, gt_ref, nw_ref, c0_ref, n0_ref, m0_ref,
                       o_ref, c_ref, n_ref, m_ref, t_scr, g_scr, i_scr, fc_scr, *, t):
    c = LANES
    nc = t // c
    h0 = pl.program_id(1) * ML_PAIR
    hs = range(ML_PAIR)
    rr = lax.broadcasted_iota(jnp.int32, (c, c), 0)
    cc = lax.broadcasted_iota(jnp.int32, (c, c), 1)
    causal = rr >= cc
    nw = nw_ref[...]
    hsl = lambda j: slice(j * ML_DH, (j + 1) * ML_DH)
    ones_b = jnp.ones((c, ML_DH), BF16)
    tril_b = jnp.where(causal, 1.0, 0.0).astype(BF16)

    lane_t = lax.broadcasted_iota(jnp.int32, (c, LANES), 1)
    for ci in range(nc):
        rows = slice(ci * c, (ci + 1) * c)
        g = gt_ref[0, rows, :]
        g_scr[rows, :] = jnp.where(lane_t < SM_F, g, sum(_dot(tril_b, x) for x in _split3(g)))
    sel_r = lax.broadcasted_iota(jnp.int32, (LANES, 2 * LANES), 0)
    sel_c = lax.broadcasted_iota(jnp.int32, (LANES, 2 * LANES), 1)
    g_terms = _split3(g_scr[...])
    for j in hs:
        src = jnp.where(sel_c < LANES, SM_I + h0 + j, SM_F + h0 + j)
        sel = jnp.where(sel_r == src, 1.0, 0.0).astype(BF16)
        both = sum(_dot(g, sel) for g in g_terms)
        i_scr[j] = both[:, :LANES]
        fc_scr[j] = both[:, LANES:]

    def body(ci, carry):
        cms, ns, m_prevs = carry
        r0 = pl.multiple_of(ci * c, c)
        t_scr[...] = g_scr[pl.ds(r0, c), :].T
        qbs = [(q_ref[0, pl.ds(r0, c), hsl(j)] * (ML_DH ** -0.5)).astype(BF16) for j in hs]
        kbs = [k_ref[0, pl.ds(r0, c), hsl(j)].astype(BF16) for j in hs]
        vs = [v_ref[0, pl.ds(r0, c), hsl(j)] for j in hs]
        s_qk = [_dot_nt(qbs[j], kbs[j]) for j in hs]
        s_qc = [_dot_nt(qbs[j], jnp.concatenate([cms[j], jnp.broadcast_to(ns[j], (c, ML_DH))],
                                                axis=0).astype(BF16)) for j in hs]

        d, inter, m_t, fcs = [], [], [], []
        for j in hs:
            fc = fc_scr[j, pl.ds(r0, c), :]
            i_row = t_scr[pl.ds(SM_I + h0 + j, 1), :]
            fc_row = t_scr[pl.ds(SM_F + h0 + j, 1), :]
            dj = jnp.where(causal, fc - fc_row + i_row, NEG)
            d.append(dj)
            inter.append(fc + m_prevs[j])
            m_t.append(jnp.maximum(inter[j], jnp.max(dj, axis=-1, keepdims=True)))
            fcs.append(fc)

        w_state = [jnp.exp(inter[j] - m_t[j]) for j in hs]
        qk = [(s_qk[j] * jnp.exp(d[j] - m_t[j])).astype(BF16) for j in hs]
        pv = [_dot(qk[j], jnp.concatenate([vs[j].astype(BF16), ones_b], axis=1)) for j in hs]
        w_end, dec, m_new = [], [], []
        for j in hs:
            m_new.append(m_t[j][c - 1:c, :1])
            f_end = fcs[j][c - 1:c, :]
            w_end.append(jnp.exp(f_end - fcs[j] + i_scr[j, pl.ds(r0, c), :] - m_new[j]))
            dec.append(jnp.exp(f_end[:, :1] + m_prevs[j] - m_new[j]))
        upd = [_dot_tn(jnp.concatenate([vs[j] * w_end[j], w_end[j]], axis=1).astype(BF16), kbs[j])
               for j in hs]

        c_new, n_new = [], []
        hout = []
        for j in hs:
            num = w_state[j] * s_qc[j][:, :ML_DH] + pv[j][:, :ML_DH]
            den = w_state[j] * s_qc[j][:, ML_DH:] + pv[j][:, ML_DH:]
            hout.append(_sigmoid(og_ref[0, pl.ds(r0, c), hsl(j)]) * num
                        / jnp.maximum(jnp.abs(den), jnp.exp(-m_t[j])))
            c_new.append(dec[j] * cms[j] + upd[j][:ML_DH, :])
            n_new.append(dec[j] * ns[j] + upd[j][ML_DH:ML_DH + 1, :])
        msq = [_dot((o * o).astype(BF16), ones_b) * (1.0 / ML_DH) for o in hout]
        for j in hs:
            on = hout[j] * lax.rsqrt(msq[j] + EPS) * nw
            o_ref[0, pl.ds(r0, c), hsl(j)] = (on * _silu(z_ref[0, pl.ds(r0, c), hsl(j)])).astype(o_ref.dtype)
        return tuple(c_new), tuple(n_new), tuple(m_new)

    init = (tuple(c0_ref[0, j] for j in hs), tuple(n0_ref[0, j] for j in hs),
            tuple(m0_ref[0, j][:, :1] for j in hs))
    cms, ns, ms = lax.fori_loop(0, nc, body, init)
    for j in hs:
        c_ref[0, j] = cms[j]
        n_ref[0, j] = ns[j]
        m_ref[0, j] = jnp.broadcast_to(ms[j], (1, LANES))


def _mlstm(proj, gates, nw, c0, n0, m0, out_dtype):
    bsz, t, _ = proj.shape
    c = min(128, t)
    w = ML_PAIR * ML_DH
    col = lambda name: _MAIN_OFF[name] // w
    spec = lambda name: pl.BlockSpec((1, t, w), lambda b, p, o=col(name): (b, 0, o + p))
    st = lambda rows: pl.BlockSpec((1, ML_PAIR, rows, ML_DH), lambda b, p: (b, p, 0, 0))
    if c == LANES:
        kern = functools.partial(_mlstm_wide_kernel, t=t)
        scratch = ([pltpu.VMEM((LANES, LANES), F32), pltpu.VMEM((t, LANES), F32)]
                   + [pltpu.VMEM((ML_PAIR, t, LANES), F32)] * 2)
    else:
        kern = functools.partial(_mlstm_kernel, t=t, c=c)
        scratch = [pltpu.VMEM((LANES, LANES), F32)]
    return pl.pallas_call(
        kern,
        grid=(bsz, ML_HEADS // ML_PAIR),
        in_specs=[spec("m_q"), spec("m_k"), spec("m_v"), spec("m_o"), spec("m_z"),
                  pl.BlockSpec((1, t, LANES), lambda b, p: (b, 0, 0)),
                  pl.BlockSpec((1, ML_DH), lambda b, p: (0, 0)),
                  st(ML_DH), st(1), st(1)],
        out_specs=[pl.BlockSpec((1, t, w), lambda b, p: (b, 0, p)), st(ML_DH), st(1), st(1)],
        out_shape=[jax.ShapeDtypeStruct((bsz, t, GROUP_W), out_dtype),
                   jax.ShapeDtypeStruct((bsz, ML_HEADS, ML_DH, ML_DH), F32),
                   jax.ShapeDtypeStruct((bsz, ML_HEADS, 1, ML_DH), F32),
                   jax.ShapeDtypeStruct((bsz, ML_HEADS, 1, LANES), F32)],
        scratch_shapes=scratch,
        compiler_params=_cparams(("parallel", "parallel")),
        name="mlstm",
    )(proj, proj, proj, proj, proj, gates, nw.reshape(1, -1), c0, n0, m0)


def _gmlp_kernel(u_ref, v_ref, z_ref, lw_ref, lb_ref, ws_ref, bs_ref, o_ref, *vn_refs, l, nl):
    rr = lax.broadcasted_iota(jnp.int32, (l, l), 0)
    cc = lax.broadcasted_iota(jnp.int32, (l, l), 1)
    wts = [jnp.where(rr >= cc, ws_ref[g], 0.0) for g in range(GM_GROUPS)]
    if l >= GM_CHUNK:
        wts = [w.astype(BF16) for w in wts]
    for ch in range(nl):
        rows = slice(ch * l, (ch + 1) * l)
        gv = _gelu(v_ref[0, rows, :])
        mu = jnp.mean(gv, axis=-1, keepdims=True)
        xc = gv - mu
        vn = xc * lax.rsqrt(jnp.mean(xc * xc, axis=-1, keepdims=True) + EPS) * lw_ref[...] + lb_ref[...]
        if vn_refs:
            vn_refs[0][0, rows, :] = vn
        for g in range(GM_GROUPS):
            sl = slice(g * GM_CH, (g + 1) * GM_CH)
            vg = vn[:, sl]
            if l >= GM_CHUNK:
                s = _dot(wts[g], vg.astype(BF16))
            else:
                s = jnp.zeros((l, GM_CH), F32)
                for r in range(l):
                    s = s + wts[g][:, r:r + 1] * vg[r:r + 1, :]
            s = s + bs_ref[:, g:g + 1]
            o_ref[0, rows, sl] = (_gelu(u_ref[0, rows, sl]) * s * _silu(z_ref[0, rows, sl])).astype(o_ref.dtype)


def _gmlp(proj, lw, lb, ws, bs_t, out_dtype, want_vn):
    bsz, t, _ = proj.shape
    l = min(t, GM_CHUNK)
    nl = _pick(t // l, (4, 2, 1))
    col = lambda name: _MAIN_OFF[name] // GROUP_W
    spec = lambda name: pl.BlockSpec((1, l * nl, GROUP_W), lambda b, i, o=col(name): (b, i, o))
    out = pl.BlockSpec((1, l * nl, GROUP_W), lambda b, i: (b, i, 0))
    out_specs = [out] + ([out] if want_vn else [])
    out_shape = ([jax.ShapeDtypeStruct((bsz, t, GROUP_W), out_dtype)]
                 + ([jax.ShapeDtypeStruct((bsz, t, GROUP_W), F32)] if want_vn else []))
    res = pl.pallas_call(
        functools.partial(_gmlp_kernel, l=l, nl=nl),
        grid=(bsz, t // (l * nl)),
        in_specs=[spec("c_u"), spec("c_v"), spec("c_z"),
                  pl.BlockSpec((1, GROUP_W), lambda b, i: (0, 0)),
                  pl.BlockSpec((1, GROUP_W), lambda b, i: (0, 0)),
                  pl.BlockSpec((GM_GROUPS, l, l), lambda b, i: (0, 0, 0)),
                  pl.BlockSpec((l, GM_GROUPS), lambda b, i: (0, 0))],
        out_specs=out_specs,
        out_shape=out_shape,
        compiler_params=_cparams(("parallel", "parallel")),
        name="gmlp",
    )(proj, proj, proj, lw.reshape(1, -1), lb.reshape(1, -1), ws, bs_t)
    return (res[0], res[1]) if want_vn else (res[0], None)


def _moba_prompt_kernel(q_ref, k_ref, v_ref, z_ref, o_ref, kb_scr, vb_scr, km_scr, *, t):
    blk = MB_BLOCK
    nb = t // blk
    scale = MB_DH ** -0.5
    h = pl.program_id(1)
    km_scr[...] = jnp.zeros((LANES, MB_DH), F32)
    for n in range(nb):
        rows = pl.ds(n * blk * MB_HEADS + h, blk, stride=MB_HEADS)
        kn = k_ref[rows, :]
        kb_scr[n * blk:(n + 1) * blk, :] = kn.astype(BF16)
        vb_scr[n * blk:(n + 1) * blk, :] = v_ref[rows, :].astype(BF16)
        km_scr[n:n + 1, :] = jnp.mean(kn, axis=0, keepdims=True)
    kmean = km_scr[...]
    nbp = -(-nb // SUBLANES) * SUBLANES
    blk_row = lax.broadcasted_iota(jnp.int32, (nbp, blk), 0)
    rr = lax.broadcasted_iota(jnp.int32, (blk, blk), 0)
    cc = lax.broadcasted_iota(jnp.int32, (blk, blk), 1)
    log2_scale = float(scale * np.log2(np.e))

    for qi in range(nb):
        rows = slice(qi * blk, (qi + 1) * blk)
        q = q_ref[0, rows, :]
        qb = q.astype(BF16)
        bias = None
        if qi > MB_TOPK:
            gate = _dot_nt(kmean, q, precision=HIGHEST)[0:nbp, :]
            cnt = jnp.zeros((nbp, blk), F32)
            for m in range(qi):
                gm = gate[m:m + 1, :]
                beats = (gm > gate) | ((gm == gate) & (blk_row > m))
                cnt = cnt + jnp.where(beats, 1.0, 0.0)
            bias_t = jnp.where(cnt < MB_TOPK, 0.0, NEG)
            bias = jnp.concatenate([bias_t, jnp.zeros((LANES - nbp, blk), F32)], axis=0).T
        ss = []
        for j in range(qi + 1):
            s = _dot_nt(qb, kb_scr[j * blk:(j + 1) * blk, :]) * log2_scale
            if j == qi:
                s = jnp.where(cc <= rr, s, NEG)
            elif bias is not None:
                s = s + bias[:, j:j + 1]
            ss.append(s)
        lane_tiles = lambda xs: [x[:, i:i + LANES] for x in xs for i in range(0, blk, LANES)]
        m_i = jnp.max(functools.reduce(jnp.maximum, lane_tiles(ss)), axis=-1, keepdims=True)
        ps = [jnp.exp2(s - m_i) for s in ss]
        l_i = jnp.sum(functools.reduce(jnp.add, lane_tiles(ps)), axis=-1, keepdims=True)
        acc = _dot(ps[0].astype(BF16), vb_scr[0:blk, :])
        for j in range(1, qi + 1):
            acc = acc + _dot(ps[j].astype(BF16), vb_scr[j * blk:(j + 1) * blk, :])
        o_ref[0, rows, :] = ((acc / l_i) * _silu(z_ref[0, rows, :])).astype(o_ref.dtype)


def _moba_prompt(q_rope, k4, v4, layer, proj, out_dtype):
    bsz, t, _ = proj.shape
    assert t % MB_BLOCK == 0 and t // MB_BLOCK <= LANES
    zcol = _MAIN_OFF["a_z"] // MB_DH
    hd = pl.BlockSpec((1, t, MB_DH), lambda b, h: (b, 0, h))
    kv = pl.BlockSpec((None, None, t * MB_HEADS, MB_DH), lambda b, h: (layer, b, 0, 0))
    return pl.pallas_call(
        functools.partial(_moba_prompt_kernel, t=t),
        grid=(bsz, MB_HEADS),
        in_specs=[hd, kv, kv, pl.BlockSpec((1, t, MB_DH), lambda b, h: (b, 0, zcol + h))],
        out_specs=hd,
        out_shape=jax.ShapeDtypeStruct((bsz, t, GROUP_W), out_dtype),
        scratch_shapes=[pltpu.VMEM((t, MB_DH), BF16), pltpu.VMEM((t, MB_DH), BF16),
                        pltpu.VMEM((LANES, MB_DH), F32)],
        compiler_params=_cparams(("parallel", "arbitrary")),
        name="moba_prompt",
    )(q_rope, k4, v4, proj)


QROWS = MB_HEADS * SAMPLE_T
ST_M, ST_L, ST_G = 0, 1, 2


def _moba_past_kernel(pt_ref, q_ref, bias_ref, *refs, nbs):
    del pt_ref
    npg = 2 * nbs
    k_refs, v_refs = refs[:npg], refs[npg:2 * npg]
    o_ref, st_ref = refs[2 * npg:]
    scale = MB_DH ** -0.5
    page = k_refs[0].shape[0] // MB_HEADS
    q = q_ref[0]
    qb = q.astype(BF16)
    bias = bias_ref[...]
    lane = lax.broadcasted_iota(jnp.int32, (QROWS, LANES), 1)
    ss = [_dot_nt(qb, r[...].astype(BF16)) * scale + bias for r in k_refs]
    ps, ms, ls = [], [], []
    for n in range(nbs):
        s0, s1 = ss[2 * n], ss[2 * n + 1]
        m = jnp.maximum(jnp.max(s0, axis=-1, keepdims=True), jnp.max(s1, axis=-1, keepdims=True))
        p0, p1 = jnp.exp(s0 - m), jnp.exp(s1 - m)
        ps += [p0.astype(BF16), p1.astype(BF16)]
        ms.append(m)
        ls.append(jnp.sum(p0, axis=-1, keepdims=True) + jnp.sum(p1, axis=-1, keepdims=True))
    for n in range(nbs):
        o_ref[n] = (_dot(ps[2 * n], v_refs[2 * n][...].astype(BF16))
                    + _dot(ps[2 * n + 1], v_refs[2 * n + 1][...].astype(BF16)))
        ksum = (jnp.sum(k_refs[2 * n][...].reshape(page, MB_HEADS, MB_DH), axis=0)
                + jnp.sum(k_refs[2 * n + 1][...].reshape(page, MB_HEADS, MB_DH), axis=0))
        ksum_rows = jnp.concatenate(
            [jnp.broadcast_to(ksum[h:h + 1, :], (SAMPLE_T, MB_DH)) for h in range(MB_HEADS)], axis=0)
        gate = jnp.sum(q * ksum_rows, axis=-1, keepdims=True) * (1.0 / MB_BLOCK)
        st_ref[n] = jnp.where(lane == ST_M, ms[n], jnp.where(lane == ST_L, ls[n], gate))


def _moba_past(layer, q_rows, cache_k, cache_v, page_table):
    db = q_rows.shape[0]
    rows = cache_k.shape[2]
    assert 2 * rows == MB_BLOCK * MB_HEADS
    nb = page_table.shape[1] // 2
    nbs = _pick(nb, (4, 2, 1))
    key_head = np.arange(rows) % MB_HEADS
    row_head = np.arange(QROWS) // SAMPLE_T
    bias = jnp.asarray(np.where(key_head[None, :] == row_head[:, None], 0.0, NEG).astype(np.float32))
    pg = lambda i: pl.BlockSpec((None, None, rows, MB_DH),
                                lambda b, n, pt, i=i: (layer, pt[b, 2 * nbs * n + i], 0, 0))
    pages = [pg(i) for i in range(2 * nbs)]
    part = pl.BlockSpec((None, nbs, QROWS, MB_DH), lambda b, n, pt: (b, n, 0, 0))
    shape = jax.ShapeDtypeStruct((db, nb, QROWS, MB_DH), F32)
    grid_spec = pltpu.PrefetchScalarGridSpec(
        num_scalar_prefetch=1,
        grid=(db, nb // nbs),
        in_specs=[pl.BlockSpec((1, QROWS, MB_DH), lambda b, n, pt: (b, 0, 0)),
                  pl.BlockSpec((QROWS, rows), lambda b, n, pt: (0, 0))] + pages + pages,
        out_specs=[part, part],
    )
    return pl.pallas_call(
        functools.partial(_moba_past_kernel, nbs=nbs),
        grid_spec=grid_spec,
        out_shape=[shape, shape],
        compiler_params=_cparams(("parallel", "parallel")),
        name="moba_past",
    )(page_table, q_rows, bias, *([cache_k] * (2 * nbs)), *([cache_v] * (2 * nbs)))


def _moba_merge_kernel(op_ref, st_ref, q_ref, k_ref, v_ref, z_ref, o_ref, *, nb, t_valid):
    scale = MB_DH ** -0.5
    g = st_ref[0, :, :, ST_G:ST_G + 1]
    m = st_ref[0, :, :, ST_M:ST_M + 1]
    l = st_ref[0, :, :, ST_L:ST_L + 1]
    nidx = lax.broadcasted_iota(jnp.int32, g.shape, 0)
    sel = jnp.zeros(g.shape, jnp.bool_)
    gm = g
    for _ in range(min(MB_TOPK, nb)):
        mx = jnp.max(gm, axis=0, keepdims=True)
        first = jnp.min(jnp.where(gm == mx, nidx, nb), axis=0, keepdims=True)
        pick = nidx == first
        sel = sel | pick
        gm = jnp.where(pick, -jnp.inf, gm)

    rr = lax.broadcasted_iota(jnp.int32, (SAMPLE_T, SAMPLE_T), 0)
    cc = lax.broadcasted_iota(jnp.int32, (SAMPLE_T, SAMPLE_T), 1)
    mo, lo, oo = [], [], []
    for h in range(MB_HEADS):
        sl = slice(h * MB_DH, (h + 1) * MB_DH)
        qh = q_ref[0, h * SAMPLE_T:(h + 1) * SAMPLE_T, :]
        s = _dot_nt(qh.astype(BF16), k_ref[0, :, sl].astype(BF16)) * scale
        s = jnp.where((cc <= rr) & (cc < t_valid), s, NEG)
        mh = jnp.max(s, axis=-1, keepdims=True)
        p = jnp.exp(s - mh)
        mo.append(mh)
        lo.append(jnp.sum(p, axis=-1, keepdims=True))
        oo.append(_dot(p.astype(BF16), v_ref[0, :, sl].astype(BF16)))
    m_own, l_own, o_own = (jnp.concatenate(x, axis=0) for x in (mo, lo, oo))

    m_tot = jnp.maximum(jnp.max(jnp.where(sel, m, NEG), axis=0), m_own)
    w = jnp.where(sel, jnp.exp(m - m_tot[None]), 0.0)
    w_own = jnp.exp(m_own - m_tot)
    den = jnp.sum(w * l, axis=0) + w_own * l_own
    acc = w_own * o_own
    for n in range(nb):
        acc = acc + w[n] * op_ref[0, n]
    o = acc / den
    for h in range(MB_HEADS):
        sl = slice(h * MB_DH, (h + 1) * MB_DH)
        o_ref[0, :, sl] = o[h * SAMPLE_T:(h + 1) * SAMPLE_T, :] * _silu(z_ref[0, :, sl])


def _moba_merge(o_part, stats, q_rows, k_rope, proj, t_valid):
    db, nb = o_part.shape[:2]
    vcol, zcol = _MAIN_OFF["a_v"] // GROUP_W, _MAIN_OFF["a_z"] // GROUP_W
    part = pl.BlockSpec((1, nb, QROWS, MB_DH), lambda b: (b, 0, 0, 0))
    row = pl.BlockSpec((1, SAMPLE_T, GROUP_W), lambda b: (b, 0, 0))
    return pl.pallas_call(
        functools.partial(_moba_merge_kernel, nb=nb, t_valid=t_valid),
        grid=(db,),
        in_specs=[part, part, pl.BlockSpec((1, QROWS, MB_DH), lambda b: (b, 0, 0)), row,
                  pl.BlockSpec((1, SAMPLE_T, GROUP_W), lambda b: (b, 0, vcol)),
                  pl.BlockSpec((1, SAMPLE_T, GROUP_W), lambda b: (b, 0, zcol))],
        out_specs=row,
        out_shape=jax.ShapeDtypeStruct((db, SAMPLE_T, GROUP_W), F32),
        compiler_params=_cparams(("parallel",)),
        name="moba_merge",
    )(o_part, stats, q_rows, k_rope, proj, proj)


def _layer_weights(gla_w_gate_l, ml_b_i_l, ml_b_f_l):
    wg_pad = jnp.concatenate([gla_w_gate_l, jnp.zeros((LANES - GLA_GATE_RANK, gla_w_gate_l.shape[1]), F32)],
                             axis=0)
    bias_row = jnp.concatenate([jnp.zeros((SM_I,), F32), ml_b_i_l, ml_b_f_l,
                                jnp.zeros((LANES - SM_F - ML_HEADS,), F32)]).reshape(1, LANES)
    return wg_pad, bias_row


def _mixers(proj, small, bsz, t, t_valid, lw, rope_tables, gla_s0, ml_state, out_dtype, want_vn):
    (wg_pad, bias_row, gla_b_gate, gla_norm_w, ml_norm_w, gm_ln_w, gm_ln_b, gm_ws, gm_bs_t) = lw
    proj = proj.reshape(bsz, t, N_MAIN)
    small = small.reshape(bsz, t, LANES)

    out_a, gla_s = _gla(proj, small, wg_pad, gla_b_gate, gla_norm_w, gla_s0, t_valid, out_dtype)

    gates = _gates(small, bias_row, t_valid)
    out_b, ml_c, ml_n, ml_m = _mlstm(proj, gates, ml_norm_w, *ml_state, out_dtype)

    out_c, vn = _gmlp(proj, gm_ln_w, gm_ln_b, gm_ws, gm_bs_t, out_dtype, want_vn)

    q_rope = _rope(proj, _MAIN_OFF["a_q"], rope_tables)
    return proj, (out_a, out_b, out_c), q_rope, gla_s, (ml_c, ml_n, ml_m[..., :1]), vn


def kernel(x_prompt, x_sample, cache_k, cache_v, page_table, state_gla, state_mlstm_C, state_mlstm_n,
           state_mlstm_m, norm_w, w_in, gla_w_gate, gla_b_gate, gla_norm_w, ml_b_i, ml_b_f, ml_norm_w,
           gm_ln_w, gm_ln_b, gm_w_s, gm_b_s, w_out, final_norm_w):
    bp, tp, d = x_prompt.shape
    db, ts, _ = x_sample.shape
    depth = w_in.shape[0]
    page = cache_k.shape[2]
    past_len = page_table.shape[1] * page
    assert w_out.shape[1] == 4 * GROUP_W and ts <= SAMPLE_T
    assert past_len % MB_BLOCK == 0 and tp % MB_BLOCK == 0

    tables_p = _rope_tables(jnp.arange(tp, dtype=jnp.int32))
    tables_s = _rope_tables(past_len + jnp.arange(SAMPLE_T, dtype=jnp.int32))
    cache_k = cache_k.reshape(depth, cache_k.shape[1], page * MB_HEADS, MB_DH)
    cache_v = cache_v.reshape(depth, cache_v.shape[1], page * MB_HEADS, MB_DH)

    yp = x_prompt.reshape(bp * tp, d)
    ys = jnp.pad(x_sample, ((0, 0), (0, SAMPLE_T - ts), (0, 0))).reshape(db * SAMPLE_T, d)
    dt_p = BF16
    dt_s = F32

    zero_gla = jnp.zeros((bp, GLA_HEADS // 2, GLA_DV, LANES), F32)
    zero_ml = (jnp.zeros((bp, ML_HEADS, ML_DH, ML_DH), F32), jnp.zeros((bp, ML_HEADS, 1, ML_DH), F32),
               jnp.zeros((bp, ML_HEADS, 1, LANES), F32))

    outs = {n: [] for n in ("ks", "vs", "gp", "gs", "cp", "cs", "np", "ns", "mp", "ms", "vv")}
    lp = min(tp, GM_CHUNK)
    w_t = jnp.swapaxes(w_in, 1, 2)
    w_main = _wprep(w_t)
    w_small = _wsmall(w_t)
    w_out4 = w_out.reshape(depth, 4, GROUP_W, d)
    kv_stacks = None
    for l in range(depth):
        wg_pad, bias_row = _layer_weights(gla_w_gate[l], ml_b_i[l], ml_b_f[l])
        common = (wg_pad, bias_row, gla_b_gate[l], gla_norm_w[l], ml_norm_w[l], gm_ln_w[l], gm_ln_b[l])
        h_p, small_p = _rms_small(yp, norm_w[l], w_small, l)
        h_s, small_s = _rms_small(ys, norm_w[l], w_small, l)
        proj_p, proj_s = _in_proj(h_p, h_s, w_main, l)

        lw = common + (gm_w_s[l][:, :lp, :lp], gm_b_s[l][:, :lp].T)
        proj, mix, q_rope, gla_s, ml_s, _ = _mixers(
            proj_p, small_p, bp, tp, tp, lw, tables_p, zero_gla, zero_ml, dt_p, want_vn=False)
        kv_stacks = _kv_heads(proj, tables_p, l, depth, kv_stacks)
        out_d = _moba_prompt(q_rope, kv_stacks[0], kv_stacks[1], l, proj, dt_p)
        yp = _out_proj([a.reshape(bp * tp, GROUP_W) for a in mix + (out_d,)], w_out4, l, yp)
        outs["gp"].append(_gla_state_from_pairs(gla_s))
        outs["cp"].append(ml_s[0])
        outs["np"].append(ml_s[1][:, :, 0, :])
        outs["mp"].append(ml_s[2][:, :, 0, 0])

        lw = common + (gm_w_s[l][:, :SAMPLE_T, :SAMPLE_T], gm_b_s[l][:, :SAMPLE_T].T)
        ml_state = (state_mlstm_C[l], state_mlstm_n[l][:, :, None, :],
                    jnp.broadcast_to(state_mlstm_m[l][:, :, None, None], (db, ML_HEADS, 1, LANES)))
        proj, mix, q_rope, gla_s, ml_s, vn = _mixers(
            proj_s, small_s, db, SAMPLE_T, ts, lw, tables_s, _gla_state_to_pairs(state_gla[l]), ml_state,
            dt_s, want_vn=True)
        k_rope = _rope(proj, _MAIN_OFF["a_k"], tables_s)
        v_new = proj[:, :, _MAIN_OFF["a_v"]:_MAIN_OFF["a_v"] + GROUP_W]
        q_rows = (q_rope.reshape(db, SAMPLE_T, MB_HEADS, MB_DH).transpose(0, 2, 1, 3)
                  .reshape(db, QROWS, MB_DH))
        o_part, stats = _moba_past(l, q_rows, cache_k, cache_v, page_table)
        out_d = _moba_merge(o_part, stats, q_rows, k_rope, proj, ts)
        ys = _out_proj([a.reshape(db * SAMPLE_T, GROUP_W) for a in mix + (out_d,)], w_out4, l, ys)
        outs["ks"].append(k_rope[:, :ts].reshape(db, ts, MB_HEADS, MB_DH))
        outs["vs"].append(v_new[:, :ts].reshape(db, ts, MB_HEADS, MB_DH))
        outs["gs"].append(_gla_state_from_pairs(gla_s))
        outs["cs"].append(ml_s[0])
        outs["ns"].append(ml_s[1][:, :, 0, :])
        outs["ms"].append(ml_s[2][:, :, 0, 0])
        outs["vv"].append(vn[:, :ts])

    y_prompt = _rms(yp, final_norm_w).reshape(bp, tp, d)
    y_sample = _rms(ys, final_norm_w).reshape(db, SAMPLE_T, d)[:, :ts]
    st = jnp.stack
    k_prompt = kv_stacks[0].reshape(depth, bp, tp, MB_HEADS, MB_DH)
    v_prompt = kv_stacks[1].reshape(depth, bp, tp, MB_HEADS, MB_DH)
    return (y_prompt, y_sample, k_prompt, v_prompt, st(outs["ks"]), st(outs["vs"]),
            st(outs["gp"]), st(outs["gs"]), st(outs["cp"]), st(outs["cs"]), st(outs["np"]), st(outs["ns"]),
            st(outs["mp"]), st(outs["ms"]), st(outs["vv"]))
```

```python
import functools

import numpy as np
import jax
import jax.numpy as jnp
from jax import lax
from jax.experimental import pallas as pl
from jax.experimental.pallas import tpu as pltpu

F32 = jnp.float32
BF16 = jnp.bfloat16
HIGHEST = lax.Precision.HIGHEST

GROUP_W = 1024
GLA_HEADS, GLA_DK, GLA_DV = 8, 64, 128
GLA_GATE_RANK, GLA_GATE_NORM = 16, 16.0
ML_HEADS, ML_DH = 8, 128
GM_GROUPS, GM_CH, GM_CHUNK = 8, 128, 128
MB_HEADS, MB_DH, MB_BLOCK, MB_TOPK = 8, 128, 256, 3
ROT_DIM, ROPE_THETA = 32, 500000.0
EPS = 1e-6

LANES = 128
SUBLANES = 8
VMEM_LIMIT = 56 * 1024 * 1024
VMEM_LIMIT_TALL = 60 * 1024 * 1024

NEG = -1e30
SAMPLE_T = SUBLANES

_MAIN_ORDER = ("g_q", "g_k", "g_v", "g_z", "m_q", "m_k", "m_v", "m_o", "m_z",
               "c_u", "c_v", "c_z", "a_q", "a_k", "a_v", "a_z")
_SPLIT_NAMES = ("g_q", "g_k", "g_v", "g_lr", "g_z", "m_q", "m_k", "m_v", "m_i", "m_f", "m_o", "m_z",
                "c_u", "c_v", "c_z", "a_q", "a_k", "a_v", "a_z")
_SPLIT_W = (GLA_HEADS * GLA_DK, GLA_HEADS * GLA_DK, GLA_HEADS * GLA_DV, GLA_GATE_RANK, GROUP_W,
            GROUP_W, GROUP_W, GROUP_W, ML_HEADS, ML_HEADS, GROUP_W, GROUP_W,
            GROUP_W, GROUP_W, GROUP_W, GROUP_W, GROUP_W, GROUP_W, GROUP_W)
_SRC_OFF = dict(zip(_SPLIT_NAMES, np.concatenate([[0], np.cumsum(_SPLIT_W)[:-1]]).tolist()))
_SRC_W = dict(zip(_SPLIT_NAMES, _SPLIT_W))
_MAIN_OFF = {}
_off = 0
for _n in _MAIN_ORDER:
    _MAIN_OFF[_n] = _off
    _off += _SRC_W[_n]
N_MAIN = _off
SM_LR, SM_I, SM_F = 0, GLA_GATE_RANK, GLA_GATE_RANK + ML_HEADS


def _cparams(sem, vmem_limit=VMEM_LIMIT):
    return pltpu.CompilerParams(dimension_semantics=sem, vmem_limit_bytes=vmem_limit)


def _pick(n, cands):
    for c in cands:
        if n % c == 0:
            return c
    return n


def _silu(x):
    return x / (1.0 + jnp.exp(-x))


def _sigmoid(x):
    return 1.0 / (1.0 + jnp.exp(-x))


def _log_sigmoid(x):
    return jnp.minimum(x, 0.0) - jnp.log(1.0 + jnp.exp(-jnp.abs(x)))


def _gelu(x):
    c = np.sqrt(2.0 / np.pi).astype(np.float32)
    return 0.5 * x * (1.0 + jnp.tanh(c * (x + 0.044715 * (x * x * x))))


def _dot_nt(a, b, precision=None):
    return lax.dot_general(a, b, (((1,), (1,)), ((), ())), preferred_element_type=F32, precision=precision)


def _dot_tn(a, b, precision=None):
    return lax.dot_general(a, b, (((0,), (0,)), ((), ())), preferred_element_type=F32, precision=precision)


def _dot(a, b, precision=None):
    return jnp.dot(a, b, preferred_element_type=F32, precision=precision)


def _rms_small_kernel(x_ref, nw_ref, ws_ref, h_ref, sm_ref):
    x = x_ref[...]
    ms = jnp.mean(x * x, axis=-1, keepdims=True)
    hb = (x * lax.rsqrt(ms + EPS) * nw_ref[...]).astype(BF16)
    h_ref[...] = hb
    sm_ref[...] = _dot(hb, ws_ref[...])


def _rms_small(x, nw, w_small, layer):
    m, d = x.shape
    bm = _pick(m, (256, 128, 64, 32, 16, 8))
    return pl.pallas_call(
        _rms_small_kernel,
        grid=(m // bm,),
        in_specs=[pl.BlockSpec((bm, d), lambda i: (i, 0)),
                  pl.BlockSpec((1, d), lambda i: (0, 0)),
                  pl.BlockSpec((None, d, LANES), lambda i: (layer, 0, 0))],
        out_specs=[pl.BlockSpec((bm, d), lambda i: (i, 0)),
                   pl.BlockSpec((bm, LANES), lambda i: (i, 0))],
        out_shape=[jax.ShapeDtypeStruct((m, d), BF16), jax.ShapeDtypeStruct((m, LANES), F32)],
        compiler_params=_cparams(("parallel",)),
        name="rms_small",
    )(x, nw.reshape(1, d), w_small)


def _rms_kernel(x_ref, nw_ref, o_ref):
    x = x_ref[...]
    ms = jnp.mean(x * x, axis=-1, keepdims=True)
    o_ref[...] = x * lax.rsqrt(ms + EPS) * nw_ref[...]


def _rms(x, nw):
    m, d = x.shape
    bm = _pick(m, (256, 128, 64, 32, 16, 8))
    return pl.pallas_call(
        _rms_kernel,
        grid=(m // bm,),
        in_specs=[pl.BlockSpec((bm, d), lambda i: (i, 0)), pl.BlockSpec((1, d), lambda i: (0, 0))],
        out_specs=pl.BlockSpec((bm, d), lambda i: (i, 0)),
        out_shape=jax.ShapeDtypeStruct((m, d), F32),
        compiler_params=_cparams(("parallel",)),
        name="rms_final",
    )(x, nw.reshape(1, d))


WP_BN = 1024
WP_B1 = _MAIN_OFF["g_z"] // WP_BN
WP_B2 = _MAIN_OFF["m_o"] // WP_BN
WP_S1 = GLA_GATE_RANK
WP_S2 = GLA_GATE_RANK + 2 * ML_HEADS
assert _MAIN_OFF["g_z"] % WP_BN == 0 and _MAIN_OFF["m_o"] % WP_BN == 0 and N_MAIN % WP_BN == 0
assert WP_S1 % SUBLANES == 0 and WP_S2 % SUBLANES == 0 and sum(_SPLIT_W) % WP_S2 == 0


def _wprep_kernel(a_ref, t_ref, o_ref):
    j = pl.program_id(2)

    def emit(s):
        x = a_ref[...] if s == 0 else jnp.concatenate([a_ref[...], t_ref[...]], axis=0)[s:s + WP_BN]
        o_ref[...] = x.T.astype(BF16)

    @pl.when(j < WP_B1)
    def _():
        emit(0)

    @pl.when((j >= WP_B1) & (j < WP_B2))
    def _():
        emit(WP_S1)

    @pl.when(j >= WP_B2)
    def _():
        emit(WP_S2)


def _wprep(w_t):
    depth, _, d = w_t.shape
    kb = _pick(d, (2048, 1024, 512, 256, 128))
    return pl.pallas_call(
        _wprep_kernel,
        grid=(depth, d // kb, N_MAIN // WP_BN),
        in_specs=[pl.BlockSpec((None, WP_BN, kb), lambda l, i, j: (l, j, i)),
                  pl.BlockSpec((None, WP_S2, kb), lambda l, i, j: (l, (j + 1) * (WP_BN // WP_S2), i))],
        out_specs=pl.BlockSpec((None, kb, WP_BN), lambda l, i, j: (l, i, j)),
        out_shape=jax.ShapeDtypeStruct((depth, d, N_MAIN), BF16),
        compiler_params=_cparams(("parallel", "parallel", "parallel")),
        name="w_prep",
    )(w_t, w_t)


def _wsmall_kernel(lr_ref, if_ref, o_ref):
    kb = lr_ref.shape[1]
    x = jnp.concatenate([lr_ref[...], if_ref[...],
                         jnp.zeros((LANES - GLA_GATE_RANK - 2 * ML_HEADS, kb), F32)], axis=0)
    o_ref[...] = x.T.astype(BF16)


def _wsmall(w_t):
    depth, _, d = w_t.shape
    kb = _pick(d, (512, 256, 128))
    lr, gi = _SRC_OFF["g_lr"], _SRC_OFF["m_i"]
    assert _SRC_OFF["m_f"] == gi + ML_HEADS and lr % GLA_GATE_RANK == 0 and gi % (2 * ML_HEADS) == 0
    return pl.pallas_call(
        _wsmall_kernel,
        grid=(depth, d // kb),
        in_specs=[pl.BlockSpec((None, GLA_GATE_RANK, kb), lambda l, i: (l, lr // GLA_GATE_RANK, i)),
                  pl.BlockSpec((None, 2 * ML_HEADS, kb), lambda l, i: (l, gi // (2 * ML_HEADS), i))],
        out_specs=pl.BlockSpec((None, kb, LANES), lambda l, i: (l, i, 0)),
        out_shape=jax.ShapeDtypeStruct((depth, d, LANES), BF16),
        compiler_params=_cparams(("parallel", "parallel")),
        name="w_small",
    )(w_t, w_t)


def _mm_kernel(a_ref, as_ref, b_ref, o_ref, os_ref):
    w = b_ref[...]
    o_ref[...] = _dot(a_ref[...], w)

    @pl.when(pl.program_id(1) == 0)
    def _():
        os_ref[...] = _dot(as_ref[...], w)


def _in_proj(a, a_s, w_all, layer):
    m, k = a.shape
    ms = a_s.shape[0]
    n = w_all.shape[2]
    bm = _pick(m, (1024, 512, 256, 128, 64))
    bn = _pick(n, (1024, 512, 256, 128))
    return pl.pallas_call(
        _mm_kernel,
        grid=(n // bn, m // bm),
        in_specs=[pl.BlockSpec((bm, k), lambda j, i: (i, 0)),
                  pl.BlockSpec((ms, k), lambda j, i: (0, 0)),
                  pl.BlockSpec((None, k, bn), lambda j, i: (layer, 0, j))],
        out_specs=[pl.BlockSpec((bm, bn), lambda j, i: (i, j)),
                   pl.BlockSpec((ms, bn), lambda j, i: (0, j))],
        out_shape=[jax.ShapeDtypeStruct((m, n), F32), jax.ShapeDtypeStruct((ms, n), F32)],
        compiler_params=_cparams(("parallel", "arbitrary")),
        name="in_proj",
    )(a, a_s, w_all)


def _out_kernel(a0, a1, a2, a3, w_ref, x_ref, o_ref):
    acc = x_ref[...]
    for g, a in enumerate((a0, a1, a2, a3)):
        acc = acc + _dot(a[...].astype(BF16), w_ref[g].astype(BF16))
    o_ref[...] = acc


def _out_proj(mix, w_all, layer, x):
    m, d = x.shape
    bm = _pick(m, (2048, 1024, 512, 256, 128, 64))
    bn = _pick(d, (256, 128))
    a_spec = pl.BlockSpec((bm, GROUP_W), lambda i, j: (i, 0))
    return pl.pallas_call(
        _out_kernel,
        grid=(m // bm, d // bn),
        in_specs=[a_spec, a_spec, a_spec, a_spec,
                  pl.BlockSpec((None, 4, GROUP_W, bn), lambda i, j: (layer, 0, 0, j)),
                  pl.BlockSpec((bm, bn), lambda i, j: (i, j))],
        out_specs=pl.BlockSpec((bm, bn), lambda i, j: (i, j)),
        out_shape=jax.ShapeDtypeStruct((m, d), F32),
        compiler_params=_cparams(("parallel", "parallel"), VMEM_LIMIT_TALL),
        name="out_proj",
    )(*mix, w_all, x)


def _rope_kernel(x_ref, a_ref, b_ref, c_ref, o_ref):
    a, b, c = a_ref[...], b_ref[...], c_ref[...]
    for h in range(MB_HEADS):
        sl = slice(h * MB_DH, (h + 1) * MB_DH)
        x = x_ref[0, :, sl]
        o_ref[0, :, sl] = (x * a + pltpu.roll(x, MB_DH - ROT_DIM // 2, 1) * b
                           + pltpu.roll(x, ROT_DIM // 2, 1) * c)


def _rope_tables(pos):
    half = ROT_DIM // 2
    inv_freq = jnp.power(ROPE_THETA, -jnp.arange(0, ROT_DIM, 2, dtype=F32) / ROT_DIM)
    ang = pos.astype(F32)[:, None] * inv_freq[None, :]
    cos, sin = jnp.cos(ang), jnp.sin(ang)
    t = pos.shape[0]
    a = jnp.concatenate([cos, cos, jnp.ones((t, MB_DH - ROT_DIM), F32)], axis=1)
    b = jnp.concatenate([-sin, jnp.zeros((t, MB_DH - half), F32)], axis=1)
    c = jnp.concatenate([jnp.zeros((t, half), F32), sin, jnp.zeros((t, MB_DH - ROT_DIM), F32)], axis=1)
    return a, b, c


def _rope(proj, col, tables):
    bsz, t, _ = proj.shape
    bt = _pick(t, (512, 256, 128, 64, 32, 16, 8))
    tab = pl.BlockSpec((bt, MB_DH), lambda b, i: (i, 0))
    return pl.pallas_call(
        _rope_kernel,
        grid=(bsz, t // bt),
        in_specs=[pl.BlockSpec((1, bt, GROUP_W), lambda b, i: (b, i, col // GROUP_W)), tab, tab, tab],
        out_specs=pl.BlockSpec((1, bt, GROUP_W), lambda b, i: (b, i, 0)),
        out_shape=jax.ShapeDtypeStruct((bsz, t, GROUP_W), F32),
        compiler_params=_cparams(("parallel", "parallel")),
        name="rope",
    )(proj, *tables)


def _kv_heads_kernel(k_ref, v_ref, a_ref, b_ref, c_ref, *refs):
    k4_ref, v4_ref = refs[-2:]
    a, b, c = a_ref[...], b_ref[...], c_ref[...]
    bt = a.shape[0]
    for h in range(MB_HEADS):
        sl = slice(h * MB_DH, (h + 1) * MB_DH)
        x = k_ref[0, :, sl]
        rows = pl.ds(h, bt, stride=MB_HEADS)
        k4_ref[rows, :] = (x * a + pltpu.roll(x, MB_DH - ROT_DIM // 2, 1) * b
                           + pltpu.roll(x, ROT_DIM // 2, 1) * c)
        v4_ref[rows, :] = v_ref[0, :, sl]


def _kv_heads(proj, tables, layer, depth, stacks):
    bsz, t, _ = proj.shape
    bt = _pick(t, (256, 128, 64, 32, 16, 8))
    tab = pl.BlockSpec((bt, MB_DH), lambda b, i: (i, 0))
    kcol, vcol = _MAIN_OFF["a_k"] // GROUP_W, _MAIN_OFF["a_v"] // GROUP_W
    out = pl.BlockSpec((None, None, bt * MB_HEADS, MB_DH), lambda b, i: (layer, b, i, 0))
    shape = jax.ShapeDtypeStruct((depth, bsz, t * MB_HEADS, MB_DH), F32)
    in_specs = [pl.BlockSpec((1, bt, GROUP_W), lambda b, i: (b, i, kcol)),
                pl.BlockSpec((1, bt, GROUP_W), lambda b, i: (b, i, vcol)), tab, tab, tab]
    aliases = {}
    if stacks is not None:
        in_specs += [pl.BlockSpec(memory_space=pl.ANY), pl.BlockSpec(memory_space=pl.ANY)]
        aliases = {5: 0, 6: 1}
    return pl.pallas_call(
        _kv_heads_kernel,
        grid=(bsz, t // bt),
        in_specs=in_specs,
        out_specs=[out, out],
        out_shape=[shape, shape],
        input_output_aliases=aliases,
        compiler_params=_cparams(("parallel", "parallel")),
        name="kv_heads",
    )(proj, proj, *tables, *(stacks or ()))


GLA_GROUP = 2
assert (GLA_HEADS // 2) % GLA_GROUP == 0


def _gla_levels(c):
    return [c >> i for i in range(1, c.bit_length())]


def _gla_sum_matrices(c, nch):
    sup = c * nch
    t = np.arange(sup)[:, None]
    r = np.arange(sup)[None, :]
    cb = (t // c) * c
    mats = [(r >= cb) & (r <= t), (r > t) & (r <= cb + c - 1)]
    for hs in _gla_levels(c):
        base = (t // (2 * hs)) * (2 * hs)
        ref = base + hs - 1
        mats.append(np.where(t - base >= hs, (r > ref) & (r <= t), (r > t) & (r <= ref)))
    return np.concatenate(mats, axis=0).astype(np.float32)


def _gla_kernel(q_ref, k_ref, v_ref, z_ref, sm_ref, wg_ref, bg_ref, nw_ref, s0_ref, sum_ref,
                o_ref, s_ref, g_scr, *, t, c, nch, t_valid):
    dk = GLA_DK
    sup = c * nch
    levels = _gla_levels(c)
    split = sum_ref.dtype == BF16
    x = _dot(sm_ref[0], wg_ref[...], precision=HIGHEST) + bg_ref[...]
    g_all = _log_sigmoid(x) * (1.0 / GLA_GATE_NORM)
    if t_valid < t:
        rows_t = lax.broadcasted_iota(jnp.int32, g_all.shape, 0)
        g_all = jnp.where(rows_t < t_valid, g_all, 0.0)
    g_scr[...] = g_all

    lane = lax.broadcasted_iota(jnp.int32, (sup, LANES), 1)
    row = lax.broadcasted_iota(jnp.int32, (sup, LANES), 0)
    head0 = lane < dk
    rr = lax.broadcasted_iota(jnp.int32, (sup, sup), 0)
    cc = lax.broadcasted_iota(jnp.int32, (sup, sup), 1)
    mm = BF16 if split else F32
    prec = None if split else HIGHEST
    nw = nw_ref[...]

    def pair_trip(p, r0, s2t, result):
        ql = slice(p * LANES, (p + 1) * LANES)
        vl = slice(p * 2 * GLA_DV, (p + 1) * 2 * GLA_DV)
        q = q_ref[0, pl.ds(r0, sup), ql] * (dk ** -0.5)
        k = k_ref[0, pl.ds(r0, sup), ql]
        if t_valid < t:
            k = jnp.where(row < t_valid, k, 0.0)
        g = g_scr[pl.ds(r0, sup), ql]
        v = v_ref[0, pl.ds(r0, sup), vl]
        z = z_ref[0, pl.ds(r0, sup), vl]
        if split:
            g1 = g.astype(BF16)
            r1 = g - g1.astype(F32)
            g2 = r1.astype(BF16)
            g3 = (r1 - g2.astype(F32)).astype(BF16)
            xs = _dot(sum_ref[...], jnp.concatenate([g1, g2, g3], axis=1))
            xs = xs[:, :LANES] + xs[:, LANES:2 * LANES] + xs[:, 2 * LANES:]
        else:
            xs = _dot(sum_ref[...], g, precision=HIGHEST)
        yield
        b = xs[0:sup]
        to_end = xs[sup:2 * sup]

        qk = q * k
        att0 = jnp.where(rr == cc, jnp.sum(jnp.where(head0, qk, 0.0), axis=-1, keepdims=True), 0.0)
        att1 = jnp.where(rr == cc, jnp.sum(jnp.where(head0, 0.0, qk), axis=-1, keepdims=True), 0.0)
        for i, hs in enumerate(levels):
            e = jnp.exp(xs[(2 + i) * sup:(3 + i) * sup])
            upper = (row & hs) != 0
            qt = jnp.where(upper, q * e, 0.0)
            kl = jnp.where(upper, 0.0, k * e).astype(mm)
            sh = (2 * hs).bit_length() - 1
            same = (rr >> sh) == (cc >> sh)
            att0 = att0 + jnp.where(same, _dot_nt(jnp.where(head0, qt, 0.0).astype(mm), kl, prec), 0.0)
            att1 = att1 + jnp.where(same, _dot_nt(jnp.where(head0, 0.0, qt).astype(mm), kl, prec), 0.0)
            yield

        kt = k * jnp.exp(to_end)
        qe = q * jnp.exp(b)
        qe0, qe1 = jnp.where(head0, qe, 0.0), jnp.where(head0, 0.0, qe)
        kt0, kt1 = jnp.where(head0, kt, 0.0), jnp.where(head0, 0.0, kt)
        v0, v1 = v[:, :GLA_DV], v[:, GLA_DV:]
        rows = [slice(ch * c, (ch + 1) * c) for ch in range(nch)]
        upds = [_dot_tn(jnp.concatenate([v0[r], v1[r]], axis=0).astype(mm),
                        jnp.concatenate([kt0[r], kt1[r]], axis=0).astype(mm), prec) for r in rows]
        yield
        o_int = []
        for ch, r in enumerate(rows):
            lhs = jnp.concatenate([qe0[r], qe1[r]], axis=0).astype(mm)
            o_int.append(_dot_nt(lhs, s2t.astype(mm), prec))
            s2t = jnp.exp(b[ch * c + c - 1:ch * c + c, :]) * s2t + upds[ch]
        result[p] = s2t
        yield

        for h, (att, vh) in enumerate(((att0, v0), (att1, v1))):
            o = (jnp.concatenate([oi[h * c:(h + 1) * c, :] for oi in o_int], axis=0)
                 + _dot(att.astype(mm), vh.astype(mm), prec))
            yield
            on = o * lax.rsqrt(jnp.mean(o * o, axis=-1, keepdims=True) + EPS) * nw
            zh = z[:, h * GLA_DV:(h + 1) * GLA_DV]
            cols = slice(p * 2 * GLA_DV + h * GLA_DV, p * 2 * GLA_DV + (h + 1) * GLA_DV)
            o_ref[0, pl.ds(r0, sup), cols] = (on * _silu(zh)).astype(o_ref.dtype)

    def body(si, states):
        r0 = pl.multiple_of(si * sup, sup)
        result = [None] * GLA_GROUP
        trips = [pair_trip(p, r0, states[p], result) for p in range(GLA_GROUP)]
        while trips:
            trips = [g for g in trips if next(g, True) is None]
        return tuple(result)

    states = lax.fori_loop(0, t // sup, body, tuple(s0_ref[0, p] for p in range(GLA_GROUP)))
    for p in range(GLA_GROUP):
        s_ref[0, p] = states[p]


def _gla(proj, small, wg_pad, bg, nw, s0t, t_valid, out_dtype):
    bsz, t, _ = proj.shape
    c = min(LANES, t)
    nch = 1
    pairs = GLA_HEADS // 2
    gw = GLA_GROUP * LANES
    vw = GLA_GROUP * 2 * GLA_DV
    qb, kb = _MAIN_OFF["g_q"] // gw, _MAIN_OFF["g_k"] // gw
    vb, zb = _MAIN_OFF["g_v"] // vw, _MAIN_OFF["g_z"] // vw
    kern = functools.partial(_gla_kernel, t=t, c=c, nch=nch, t_valid=t_valid)
    sums = _gla_sum_matrices(c, nch)
    sums = jnp.asarray(sums, BF16 if (c * nch) % 16 == 0 else F32)
    return pl.pallas_call(
        kern,
        grid=(bsz, pairs // GLA_GROUP),
        in_specs=[pl.BlockSpec((1, t, gw), lambda b, p: (b, 0, qb + p)),
                  pl.BlockSpec((1, t, gw), lambda b, p: (b, 0, kb + p)),
                  pl.BlockSpec((1, t, vw), lambda b, p: (b, 0, vb + p)),
                  pl.BlockSpec((1, t, vw), lambda b, p: (b, 0, zb + p)),
                  pl.BlockSpec((1, t, LANES), lambda b, p: (b, 0, 0)),
                  pl.BlockSpec((LANES, gw), lambda b, p: (0, p)),
                  pl.BlockSpec((1, gw), lambda b, p: (0, p)),
                  pl.BlockSpec((1, GLA_DV), lambda b, p: (0, 0)),
                  pl.BlockSpec((1, GLA_GROUP, GLA_DV, LANES), lambda b, p: (b, p, 0, 0)),
                  pl.BlockSpec(sums.shape, lambda b, p: (0, 0))],
        out_specs=[pl.BlockSpec((1, t, vw), lambda b, p: (b, 0, p)),
                   pl.BlockSpec((1, GLA_GROUP, GLA_DV, LANES), lambda b, p: (b, p, 0, 0))],
        out_shape=[jax.ShapeDtypeStruct((bsz, t, GROUP_W), out_dtype),
                   jax.ShapeDtypeStruct((bsz, pairs, GLA_DV, LANES), F32)],
        scratch_shapes=[pltpu.VMEM((t, gw), F32)],
        compiler_params=_cparams(("parallel", "parallel")),
        name="gla",
    )(proj, proj, proj, proj, small, wg_pad, bg.reshape(1, -1), nw.reshape(1, -1), s0t, sums)


def _gla_state_to_pairs(s):
    bsz = s.shape[0]
    s = s.reshape(bsz, GLA_HEADS // 2, 2, GLA_DK, GLA_DV)
    return s.transpose(0, 1, 4, 2, 3).reshape(bsz, GLA_HEADS // 2, GLA_DV, 2 * GLA_DK)


def _gla_state_from_pairs(s):
    bsz = s.shape[0]
    s = s.reshape(bsz, GLA_HEADS // 2, GLA_DV, 2, GLA_DK)
    return s.transpose(0, 1, 3, 4, 2).reshape(bsz, GLA_HEADS, GLA_DK, GLA_DV)


def _gates_kernel(sm_ref, bias_ref, o_ref, *, t_valid):
    x = sm_ref[0] + bias_ref[...]
    lane = lax.broadcasted_iota(jnp.int32, x.shape, 1)
    is_f = (lane >= SM_F) & (lane < SM_F + ML_HEADS)
    out = jnp.where(is_f, _log_sigmoid(x), x)
    if t_valid is not None:
        row = lax.broadcasted_iota(jnp.int32, x.shape, 0)
        out = jnp.where(row < t_valid, out, jnp.where(is_f, 0.0, NEG))
    o_ref[0] = out


def _gates(small, bias_row, t_valid):
    bsz, t, _ = small.shape
    bt = t if t_valid < t else _pick(t, (1024, 512, 256, 128, 64, 32, 16, 8))
    return pl.pallas_call(
        functools.partial(_gates_kernel, t_valid=t_valid if t_valid < t else None),
        grid=(bsz, t // bt),
        in_specs=[pl.BlockSpec((1, bt, LANES), lambda b, i: (b, i, 0)),
                  pl.BlockSpec((1, LANES), lambda b, i: (0, 0))],
        out_specs=pl.BlockSpec((1, bt, LANES), lambda b, i: (b, i, 0)),
        out_shape=jax.ShapeDtypeStruct((bsz, t, LANES), F32),
        compiler_params=_cparams(("parallel", "parallel")),
        name="ml_gates",
    )(small, bias_row)


ML_PAIR = 2


def _mlstm_kernel(q_ref, k_ref, v_ref, og_ref, z_ref, gt_ref, nw_ref,
                  c0_ref, n0_ref, m0_ref, o_ref, c_ref, n_ref, m_ref, t_scr, *, t, c):
    h0 = pl.program_id(1) * ML_PAIR
    hs = range(ML_PAIR)
    lane = lax.broadcasted_iota(jnp.int32, (c, LANES), 1)
    rr = lax.broadcasted_iota(jnp.int32, (c, c), 0)
    cc = lax.broadcasted_iota(jnp.int32, (c, c), 1)
    causal = rr >= cc
    nw = nw_ref[...]
    hsl = lambda j: slice(j * ML_DH, (j + 1) * ML_DH)

    def body(ci, carry):
        cms, ns, m_prevs = carry
        r0 = pl.multiple_of(ci * c, c)
        gt = gt_ref[0, pl.ds(r0, c), :]
        if c == LANES:
            t_scr[...] = gt.T
        qs = [q_ref[0, pl.ds(r0, c), hsl(j)] * (ML_DH ** -0.5) for j in hs]
        ks = [k_ref[0, pl.ds(r0, c), hsl(j)] for j in hs]
        vs = [v_ref[0, pl.ds(r0, c), hsl(j)] for j in hs]
        s_qk = [_dot_nt(qs[j], ks[j], precision=HIGHEST) for j in hs]
        s_qc = [_dot_nt(qs[j], cms[j], precision=HIGHEST) for j in hs]

        d, inter, m_t, i_cols, fc_cols = [], [], [], [], []
        for j in hs:
            i_col = jnp.sum(jnp.where(lane == SM_I + h0 + j, gt, 0.0), axis=-1, keepdims=True)
            f_col = jnp.sum(jnp.where(lane == SM_F + h0 + j, gt, 0.0), axis=-1, keepdims=True)
            if c == LANES:
                i_row = t_scr[pl.ds(SM_I + h0 + j, 1), :]
                f_row = t_scr[pl.ds(SM_F + h0 + j, 1), :]
            else:
                i_row = jnp.sum(jnp.where(rr == cc, i_col, 0.0), axis=0, keepdims=True)
                f_row = jnp.sum(jnp.where(rr == cc, f_col, 0.0), axis=0, keepdims=True)
            fc_col = jnp.sum(jnp.where(causal, f_row, 0.0), axis=-1, keepdims=True)
            fc_row = jnp.sum(jnp.where(rr <= cc, f_col, 0.0), axis=0, keepdims=True)
            dj = jnp.where(causal, fc_col - fc_row + i_row, NEG)
            d.append(dj)
            inter.append(fc_col + m_prevs[j])
            m_t.append(jnp.maximum(inter[j], jnp.max(dj, axis=-1, keepdims=True)))
            i_cols.append(i_col)
            fc_cols.append(fc_col)

        w_state = [jnp.exp(inter[j] - m_t[j]) for j in hs]
        qk = [s_qk[j] * jnp.exp(d[j] - m_t[j]) for j in hs]
        pv = [_dot(qk[j], vs[j], precision=HIGHEST) for j in hs]
        w_end, dec, m_new = [], [], []
        for j in hs:
            m_new.append(m_t[j][c - 1:c, :])
            f_end = fc_cols[j][c - 1:c, :]
            w_end.append(jnp.exp(f_end - fc_cols[j] + i_cols[j] - m_new[j]))
            dec.append(jnp.exp(f_end + m_prevs[j] - m_new[j]))
        upd = [_dot_tn(vs[j] * w_end[j], ks[j], precision=HIGHEST) for j in hs]

        c_new, n_new = [], []
        for j in hs:
            num = w_state[j] * s_qc[j] + pv[j]
            den = (w_state[j] * jnp.sum(qs[j] * ns[j], axis=-1, keepdims=True)
                   + jnp.sum(qk[j], axis=-1, keepdims=True))
            hout = num / jnp.maximum(jnp.abs(den), jnp.exp(-m_t[j]))
            c_new.append(dec[j] * cms[j] + upd[j])
            n_new.append(dec[j] * ns[j] + jnp.sum(ks[j] * w_end[j], axis=0, keepdims=True))
            o = _sigmoid(og_ref[0, pl.ds(r0, c), hsl(j)]) * hout
            on = o * lax.rsqrt(jnp.mean(o * o, axis=-1, keepdims=True) + EPS) * nw
            o_ref[0, pl.ds(r0, c), hsl(j)] = (on * _silu(z_ref[0, pl.ds(r0, c), hsl(j)])).astype(o_ref.dtype)
        return tuple(c_new), tuple(n_new), tuple(m_new)

    init = (tuple(c0_ref[0, j] for j in hs), tuple(n0_ref[0, j] for j in hs),
            tuple(m0_ref[0, j][:, :1] for j in hs))
    cms, ns, ms = lax.fori_loop(0, t // c, body, init)
    for j in hs:
        c_ref[0, j] = cms[j]
        n_ref[0, j] = ns[j]
        m_ref[0, j] = jnp.broadcast_to(ms[j], (1, LANES))


def _split3(x):
    a = x.astype(BF16)
    r = x - a.astype(F32)
    b = r.astype(BF16)
    return a, b, (r - b.astype(F32)).astype(BF16)


def _mlstm_wide_kernel(q_ref, k_ref, v_ref, og_ref, z_ref, gt_ref, nw_ref, c0_ref, n0_ref, m0_ref,
                       o_ref, c_ref, n_ref, m_ref, t_scr, g_scr, i_scr, fc_scr, *, t):
    c = LANES
    nc = t // c
    h0 = pl.program_id(1) * ML_PAIR
    hs = range(ML_PAIR)
    rr = lax.broadcasted_iota(jnp.int32, (c, c), 0)
    cc = lax.broadcasted_iota(jnp.int32, (c, c), 1)
    causal = rr >= cc
    nw = nw_ref[...]
    hsl = lambda j: slice(j * ML_DH, (j + 1) * ML_DH)
    ones_b = jnp.ones((c, ML_DH), BF16)
    tril_b = jnp.where(causal, 1.0, 0.0).astype(BF16)

    lane_t = lax.broadcasted_iota(jnp.int32, (c, LANES), 1)
    for ci in range(nc):
        rows = slice(ci * c, (ci + 1) * c)
        g = gt_ref[0, rows, :]
        g_scr[rows, :] = jnp.where(lane_t < SM_F, g, sum(_dot(tril_b, x) for x in _split3(g)))
    sel_r = lax.broadcasted_iota(jnp.int32, (LANES, 2 * LANES), 0)
    sel_c = lax.broadcasted_iota(jnp.int32, (LANES, 2 * LANES), 1)
    g_terms = _split3(g_scr[...])
    for j in hs:
        src = jnp.where(sel_c < LANES, SM_I + h0 + j, SM_F + h0 + j)
        sel = jnp.where(sel_r == src, 1.0, 0.0).astype(BF16)
        both = sum(_dot(g, sel) for g in g_terms)
        i_scr[j] = both[:, :LANES]
        fc_scr[j] = both[:, LANES:]

    def body(ci, carry):
        cms, ns, m_prevs = carry
        r0 = pl.multiple_of(ci * c, c)
        t_scr[...] = g_scr[pl.ds(r0, c), :].T
        qbs = [(q_ref[0, pl.ds(r0, c), hsl(j)] * (ML_DH ** -0.5)).astype(BF16) for j in hs]
        kbs = [k_ref[0, pl.ds(r0, c), hsl(j)].astype(BF16) for j in hs]
        vs = [v_ref[0, pl.ds(r0, c), hsl(j)] for j in hs]
        s_qk = [_dot_nt(qbs[j], kbs[j]) for j in hs]
        s_qc = [_dot_nt(qbs[j], jnp.concatenate([cms[j], jnp.broadcast_to(ns[j], (c, ML_DH))],
                                                axis=0).astype(BF16)) for j in hs]

        d, inter, m_t, fcs = [], [], [], []
        for j in hs:
            fc = fc_scr[j, pl.ds(r0, c), :]
            i_row = t_scr[pl.ds(SM_I + h0 + j, 1), :]
            fc_row = t_scr[pl.ds(SM_F + h0 + j, 1), :]
            dj = jnp.where(causal, fc - fc_row + i_row, NEG)
            d.append(dj)
            inter.append(fc + m_prevs[j])
            m_t.append(jnp.maximum(inter[j], jnp.max(dj, axis=-1, keepdims=True)))
            fcs.append(fc)

        w_state = [jnp.exp(inter[j] - m_t[j]) for j in hs]
        qk = [(s_qk[j] * jnp.exp(d[j] - m_t[j])).astype(BF16) for j in hs]
        pv = [_dot(qk[j], jnp.concatenate([vs[j].astype(BF16), ones_b], axis=1)) for j in hs]
        w_end, dec, m_new = [], [], []
        for j in hs:
            m_new.append(m_t[j][c - 1:c, :1])
            f_end = fcs[j][c - 1:c, :]
            w_end.append(jnp.exp(f_end - fcs[j] + i_scr[j, pl.ds(r0, c), :] - m_new[j]))
            dec.append(jnp.exp(f_end[:, :1] + m_prevs[j] - m_new[j]))
        upd = [_dot_tn(jnp.concatenate([vs[j] * w_end[j], w_end[j]], axis=1).astype(BF16), kbs[j])
               for j in hs]

        c_new, n_new = [], []
        hout = []
        for j in hs:
            num = w_state[j] * s_qc[j][:, :ML_DH] + pv[j][:, :ML_DH]
            den = w_state[j] * s_qc[j][:, ML_DH:] + pv[j][:, ML_DH:]
            hout.append(_sigmoid(og_ref[0, pl.ds(r0, c), hsl(j)]) * num
                        / jnp.maximum(jnp.abs(den), jnp.exp(-m_t[j])))
            c_new.append(dec[j] * cms[j] + upd[j][:ML_DH, :])
            n_new.append(dec[j] * ns[j] + upd[j][ML_DH:ML_DH + 1, :])
        msq = [_dot((o * o).astype(BF16), ones_b) * (1.0 / ML_DH) for o in hout]
        for j in hs:
            on = hout[j] * lax.rsqrt(msq[j] + EPS) * nw
            o_ref[0, pl.ds(r0, c), hsl(j)] = (on * _silu(z_ref[0, pl.ds(r0, c), hsl(j)])).astype(o_ref.dtype)
        return tuple(c_new), tuple(n_new), tuple(m_new)

    init = (tuple(c0_ref[0, j] for j in hs), tuple(n0_ref[0, j] for j in hs),
            tuple(m0_ref[0, j][:, :1] for j in hs))
    cms, ns, ms = lax.fori_loop(0, nc, body, init)
    for j in hs:
        c_ref[0, j] = cms[j]
        n_ref[0, j] = ns[j]
        m_ref[0, j] = jnp.broadcast_to(ms[j], (1, LANES))


def _mlstm(proj, gates, nw, c0, n0, m0, out_dtype):
    bsz, t, _ = proj.shape
    c = min(128, t)
    w = ML_PAIR * ML_DH
    col = lambda name: _MAIN_OFF[name] // w
    spec = lambda name: pl.BlockSpec((1, t, w), lambda b, p, o=col(name): (b, 0, o + p))
    st = lambda rows: pl.BlockSpec((1, ML_PAIR, rows, ML_DH), lambda b, p: (b, p, 0, 0))
    if c == LANES:
        kern = functools.partial(_mlstm_wide_kernel, t=t)
        scratch = ([pltpu.VMEM((LANES, LANES), F32), pltpu.VMEM((t, LANES), F32)]
                   + [pltpu.VMEM((ML_PAIR, t, LANES), F32)] * 2)
    else:
        kern = functools.partial(_mlstm_kernel, t=t, c=c)
        scratch = [pltpu.VMEM((LANES, LANES), F32)]
    return pl.pallas_call(
        kern,
        grid=(bsz, ML_HEADS // ML_PAIR),
        in_specs=[spec("m_q"), spec("m_k"), spec("m_v"), spec("m_o"), spec("m_z"),
                  pl.BlockSpec((1, t, LANES), lambda b, p: (b, 0, 0)),
                  pl.BlockSpec((1, ML_DH), lambda b, p: (0, 0)),
                  st(ML_DH), st(1), st(1)],
        out_specs=[pl.BlockSpec((1, t, w), lambda b, p: (b, 0, p)), st(ML_DH), st(1), st(1)],
        out_shape=[jax.ShapeDtypeStruct((bsz, t, GROUP_W), out_dtype),
                   jax.ShapeDtypeStruct((bsz, ML_HEADS, ML_DH, ML_DH), F32),
                   jax.ShapeDtypeStruct((bsz, ML_HEADS, 1, ML_DH), F32),
                   jax.ShapeDtypeStruct((bsz, ML_HEADS, 1, LANES), F32)],
        scratch_shapes=scratch,
        compiler_params=_cparams(("parallel", "parallel")),
        name="mlstm",
    )(proj, proj, proj, proj, proj, gates, nw.reshape(1, -1), c0, n0, m0)


def _gmlp_kernel(u_ref, v_ref, z_ref, lw_ref, lb_ref, ws_ref, bs_ref, o_ref, *vn_refs, l, nl):
    rr = lax.broadcasted_iota(jnp.int32, (l, l), 0)
    cc = lax.broadcasted_iota(jnp.int32, (l, l), 1)
    wts = [jnp.where(rr >= cc, ws_ref[g], 0.0) for g in range(GM_GROUPS)]
    if l >= GM_CHUNK:
        wts = [w.astype(BF16) for w in wts]
    for ch in range(nl):
        rows = slice(ch * l, (ch + 1) * l)
        gv = _gelu(v_ref[0, rows, :])
        mu = jnp.mean(gv, axis=-1, keepdims=True)
        xc = gv - mu
        vn = xc * lax.rsqrt(jnp.mean(xc * xc, axis=-1, keepdims=True) + EPS) * lw_ref[...] + lb_ref[...]
        if vn_refs:
            vn_refs[0][0, rows, :] = vn
        for g in range(GM_GROUPS):
            sl = slice(g * GM_CH, (g + 1) * GM_CH)
            vg = vn[:, sl]
            if l >= GM_CHUNK:
                s = _dot(wts[g], vg.astype(BF16))
            else:
                s = jnp.zeros((l, GM_CH), F32)
                for r in range(l):
                    s = s + wts[g][:, r:r + 1] * vg[r:r + 1, :]
            s = s + bs_ref[:, g:g + 1]
            o_ref[0, rows, sl] = (_gelu(u_ref[0, rows, sl]) * s * _silu(z_ref[0, rows, sl])).astype(o_ref.dtype)


def _gmlp(proj, lw, lb, ws, bs_t, out_dtype, want_vn):
    bsz, t, _ = proj.shape
    l = min(t, GM_CHUNK)
    nl = _pick(t // l, (4, 2, 1))
    col = lambda name: _MAIN_OFF[name] // GROUP_W
    spec = lambda name: pl.BlockSpec((1, l * nl, GROUP_W), lambda b, i, o=col(name): (b, i, o))
    out = pl.BlockSpec((1, l * nl, GROUP_W), lambda b, i: (b, i, 0))
    out_specs = [out] + ([out] if want_vn else [])
    out_shape = ([jax.ShapeDtypeStruct((bsz, t, GROUP_W), out_dtype)]
                 + ([jax.ShapeDtypeStruct((bsz, t, GROUP_W), F32)] if want_vn else []))
    res = pl.pallas_call(
        functools.partial(_gmlp_kernel, l=l, nl=nl),
        grid=(bsz, t // (l * nl)),
        in_specs=[spec("c_u"), spec("c_v"), spec("c_z"),
                  pl.BlockSpec((1, GROUP_W), lambda b, i: (0, 0)),
                  pl.BlockSpec((1, GROUP_W), lambda b, i: (0, 0)),
                  pl.BlockSpec((GM_GROUPS, l, l), lambda b, i: (0, 0, 0)),
                  pl.BlockSpec((l, GM_GROUPS), lambda b, i: (0, 0))],
        out_specs=out_specs,
        out_shape=out_shape,
        compiler_params=_cparams(("parallel", "parallel")),
        name="gmlp",
    )(proj, proj, proj, lw.reshape(1, -1), lb.reshape(1, -1), ws, bs_t)
    return (res[0], res[1]) if want_vn else (res[0], None)


MB_GROUP = 2
assert MB_HEADS % MB_GROUP == 0


def _moba_prompt_kernel(q_ref, k_ref, v_ref, z_ref, o_ref, kb_scr, vb_scr, km_scr, *, t):
    blk = MB_BLOCK
    nb = t // blk
    scale = MB_DH ** -0.5
    nbp = -(-nb // SUBLANES) * SUBLANES
    blk_row = lax.broadcasted_iota(jnp.int32, (nbp, blk), 0)
    rr = lax.broadcasted_iota(jnp.int32, (blk, blk), 0)
    cc = lax.broadcasted_iota(jnp.int32, (blk, blk), 1)
    log2_scale = float(scale * np.log2(np.e))
    lane_tiles = lambda xs: [x[:, i:i + LANES] for x in xs for i in range(0, blk, LANES)]

    def head_steps(hh):
        h = pl.program_id(1) * MB_GROUP + hh
        cols = slice(hh * MB_DH, (hh + 1) * MB_DH)
        km_scr[hh] = jnp.zeros((LANES, MB_DH), F32)
        for n in range(nb):
            rows = pl.ds(n * blk * MB_HEADS + h, blk, stride=MB_HEADS)
            kn = k_ref[rows, :]
            kb_scr[hh, n * blk:(n + 1) * blk, :] = kn.astype(BF16)
            vb_scr[hh, n * blk:(n + 1) * blk, :] = v_ref[rows, :].astype(BF16)
            km_scr[hh, n:n + 1, :] = jnp.mean(kn, axis=0, keepdims=True)
        kmean = km_scr[hh]
        yield
        for qi in range(nb):
            rows = slice(qi * blk, (qi + 1) * blk)
            q = q_ref[0, rows, cols]
            qb = q.astype(BF16)
            bias = None
            if qi > MB_TOPK:
                gate = _dot_nt(kmean, q, precision=HIGHEST)[0:nbp, :]
                yield
                cnt = jnp.zeros((nbp, blk), F32)
                for m in range(qi):
                    gm = gate[m:m + 1, :]
                    beats = (gm > gate) | ((gm == gate) & (blk_row > m))
                    cnt = cnt + jnp.where(beats, 1.0, 0.0)
                bias_t = jnp.where(cnt < MB_TOPK, 0.0, NEG)
                bias = jnp.concatenate([bias_t, jnp.zeros((LANES - nbp, blk), F32)], axis=0).T
            ss = []
            for j in range(qi + 1):
                s = _dot_nt(qb, kb_scr[hh, j * blk:(j + 1) * blk, :]) * log2_scale
                if j == qi:
                    s = jnp.where(cc <= rr, s, NEG)
                elif bias is not None:
                    s = s + bias[:, j:j + 1]
                ss.append(s)
            yield
            m_i = jnp.max(functools.reduce(jnp.maximum, lane_tiles(ss)), axis=-1, keepdims=True)
            ps = [jnp.exp2(s - m_i) for s in ss]
            l_i = jnp.sum(functools.reduce(jnp.add, lane_tiles(ps)), axis=-1, keepdims=True)
            acc = _dot(ps[0].astype(BF16), vb_scr[hh, 0:blk, :])
            for j in range(1, qi + 1):
                acc = acc + _dot(ps[j].astype(BF16), vb_scr[hh, j * blk:(j + 1) * blk, :])
            yield
            o_ref[0, rows, cols] = ((acc / l_i) * _silu(z_ref[0, rows, cols])).astype(o_ref.dtype)

    heads = [head_steps(hh) for hh in range(MB_GROUP)]
    while heads:
        heads = [g for g in heads if next(g, True) is None]


def _moba_prompt(q_rope, k4, v4, layer, proj, out_dtype):
    bsz, t, _ = proj.shape
    assert t % MB_BLOCK == 0 and t // MB_BLOCK <= LANES
    gw = MB_GROUP * MB_DH
    zcol = _MAIN_OFF["a_z"] // gw
    hd = pl.BlockSpec((1, t, gw), lambda b, h: (b, 0, h))
    kv = pl.BlockSpec((None, None, t * MB_HEADS, MB_DH), lambda b, h: (layer, b, 0, 0))
    return pl.pallas_call(
        functools.partial(_moba_prompt_kernel, t=t),
        grid=(bsz, MB_HEADS // MB_GROUP),
        in_specs=[hd, kv, kv, pl.BlockSpec((1, t, gw), lambda b, h: (b, 0, zcol + h))],
        out_specs=hd,
        out_shape=jax.ShapeDtypeStruct((bsz, t, GROUP_W), out_dtype),
        scratch_shapes=[pltpu.VMEM((MB_GROUP, t, MB_DH), BF16), pltpu.VMEM((MB_GROUP, t, MB_DH), BF16),
                        pltpu.VMEM((MB_GROUP, LANES, MB_DH), F32)],
        compiler_params=_cparams(("parallel", "arbitrary"), VMEM_LIMIT_TALL),
        name="moba_prompt",
    )(q_rope, k4, v4, proj)


QROWS = MB_HEADS * SAMPLE_T
ST_M, ST_L, ST_G = 0, 1, 2


def _moba_past_kernel(pt_ref, q_ref, bias_ref, *refs, nbs):
    del pt_ref
    npg = 2 * nbs
    k_refs, v_refs = refs[:npg], refs[npg:2 * npg]
    o_ref, st_ref = refs[2 * npg:]
    scale = MB_DH ** -0.5
    page = k_refs[0].shape[0] // MB_HEADS
    q = q_ref[0]
    qb = q.astype(BF16)
    bias = bias_ref[...]
    lane = lax.broadcasted_iota(jnp.int32, (QROWS, LANES), 1)
    ss = [_dot_nt(qb, r[...].astype(BF16)) * scale + bias for r in k_refs]
    ps, ms, ls = [], [], []
    for n in range(nbs):
        s0, s1 = ss[2 * n], ss[2 * n + 1]
        m = jnp.maximum(jnp.max(s0, axis=-1, keepdims=True), jnp.max(s1, axis=-1, keepdims=True))
        p0, p1 = jnp.exp(s0 - m), jnp.exp(s1 - m)
        ps += [p0.astype(BF16), p1.astype(BF16)]
        ms.append(m)
        ls.append(jnp.sum(p0, axis=-1, keepdims=True) + jnp.sum(p1, axis=-1, keepdims=True))
    for n in range(nbs):
        o_ref[n] = (_dot(ps[2 * n], v_refs[2 * n][...].astype(BF16))
                    + _dot(ps[2 * n + 1], v_refs[2 * n + 1][...].astype(BF16)))
        ksum = (jnp.sum(k_refs[2 * n][...].reshape(page, MB_HEADS, MB_DH), axis=0)
                + jnp.sum(k_refs[2 * n + 1][...].reshape(page, MB_HEADS, MB_DH), axis=0))
        ksum_rows = jnp.concatenate(
            [jnp.broadcast_to(ksum[h:h + 1, :], (SAMPLE_T, MB_DH)) for h in range(MB_HEADS)], axis=0)
        gate = jnp.sum(q * ksum_rows, axis=-1, keepdims=True) * (1.0 / MB_BLOCK)
        st_ref[n] = jnp.where(lane == ST_M, ms[n], jnp.where(lane == ST_L, ls[n], gate))


def _moba_past(layer, q_rows, cache_k, cache_v, page_table):
    db = q_rows.shape[0]
    rows = cache_k.shape[2]
    assert 2 * rows == MB_BLOCK * MB_HEADS
    nb = page_table.shape[1] // 2
    nbs = _pick(nb, (4, 2, 1))
    key_head = np.arange(rows) % MB_HEADS
    row_head = np.arange(QROWS) // SAMPLE_T
    bias = jnp.asarray(np.where(key_head[None, :] == row_head[:, None], 0.0, NEG).astype(np.float32))
    pg = lambda i: pl.BlockSpec((None, None, rows, MB_DH),
                                lambda b, n, pt, i=i: (layer, pt[b, 2 * nbs * n + i], 0, 0))
    pages = [pg(i) for i in range(2 * nbs)]
    part = pl.BlockSpec((None, nbs, QROWS, MB_DH), lambda b, n, pt: (b, n, 0, 0))
    shape = jax.ShapeDtypeStruct((db, nb, QROWS, MB_DH), F32)
    grid_spec = pltpu.PrefetchScalarGridSpec(
        num_scalar_prefetch=1,
        grid=(db, nb // nbs),
        in_specs=[pl.BlockSpec((1, QROWS, MB_DH), lambda b, n, pt: (b, 0, 0)),
                  pl.BlockSpec((QROWS, rows), lambda b, n, pt: (0, 0))] + pages + pages,
        out_specs=[part, part],
    )
    return pl.pallas_call(
        functools.partial(_moba_past_kernel, nbs=nbs),
        grid_spec=grid_spec,
        out_shape=[shape, shape],
        compiler_params=_cparams(("parallel", "parallel")),
        name="moba_past",
    )(page_table, q_rows, bias, *([cache_k] * (2 * nbs)), *([cache_v] * (2 * nbs)))


def _moba_merge_kernel(op_ref, st_ref, q_ref, k_ref, v_ref, z_ref, o_ref, *, nb, t_valid):
    scale = MB_DH ** -0.5
    g = st_ref[0, :, :, ST_G:ST_G + 1]
    m = st_ref[0, :, :, ST_M:ST_M + 1]
    l = st_ref[0, :, :, ST_L:ST_L + 1]
    nidx = lax.broadcasted_iota(jnp.int32, g.shape, 0)
    sel = jnp.zeros(g.shape, jnp.bool_)
    gm = g
    for _ in range(min(MB_TOPK, nb)):
        mx = jnp.max(gm, axis=0, keepdims=True)
        first = jnp.min(jnp.where(gm == mx, nidx, nb), axis=0, keepdims=True)
        pick = nidx == first
        sel = sel | pick
        gm = jnp.where(pick, -jnp.inf, gm)

    rr = lax.broadcasted_iota(jnp.int32, (SAMPLE_T, SAMPLE_T), 0)
    cc = lax.broadcasted_iota(jnp.int32, (SAMPLE_T, SAMPLE_T), 1)
    mo, lo, oo = [], [], []
    for h in range(MB_HEADS):
        sl = slice(h * MB_DH, (h + 1) * MB_DH)
        qh = q_ref[0, h * SAMPLE_T:(h + 1) * SAMPLE_T, :]
        s = _dot_nt(qh.astype(BF16), k_ref[0, :, sl].astype(BF16)) * scale
        s = jnp.where((cc <= rr) & (cc < t_valid), s, NEG)
        mh = jnp.max(s, axis=-1, keepdims=True)
        p = jnp.exp(s - mh)
        mo.append(mh)
        lo.append(jnp.sum(p, axis=-1, keepdims=True))
        oo.append(_dot(p.astype(BF16), v_ref[0, :, sl].astype(BF16)))
    m_own, l_own, o_own = (jnp.concatenate(x, axis=0) for x in (mo, lo, oo))

    m_tot = jnp.maximum(jnp.max(jnp.where(sel, m, NEG), axis=0), m_own)
    w = jnp.where(sel, jnp.exp(m - m_tot[None]), 0.0)
    w_own = jnp.exp(m_own - m_tot)
    den = jnp.sum(w * l, axis=0) + w_own * l_own
    acc = w_own * o_own
    for n in range(nb):
        acc = acc + w[n] * op_ref[0, n]
    o = acc / den
    for h in range(MB_HEADS):
        sl = slice(h * MB_DH, (h + 1) * MB_DH)
        o_ref[0, :, sl] = o[h * SAMPLE_T:(h + 1) * SAMPLE_T, :] * _silu(z_ref[0, :, sl])


def _moba_merge(o_part, stats, q_rows, k_rope, proj, t_valid):
    db, nb = o_part.shape[:2]
    vcol, zcol = _MAIN_OFF["a_v"] // GROUP_W, _MAIN_OFF["a_z"] // GROUP_W
    part = pl.BlockSpec((1, nb, QROWS, MB_DH), lambda b: (b, 0, 0, 0))
    row = pl.BlockSpec((1, SAMPLE_T, GROUP_W), lambda b: (b, 0, 0))
    return pl.pallas_call(
        functools.partial(_moba_merge_kernel, nb=nb, t_valid=t_valid),
        grid=(db,),
        in_specs=[part, part, pl.BlockSpec((1, QROWS, MB_DH), lambda b: (b, 0, 0)), row,
                  pl.BlockSpec((1, SAMPLE_T, GROUP_W), lambda b: (b, 0, vcol)),
                  pl.BlockSpec((1, SAMPLE_T, GROUP_W), lambda b: (b, 0, zcol))],
        out_specs=row,
        out_shape=jax.ShapeDtypeStruct((db, SAMPLE_T, GROUP_W), F32),
        compiler_params=_cparams(("parallel",)),
        name="moba_merge",
    )(o_part, stats, q_rows, k_rope, proj, proj)


def _layer_weights(gla_w_gate_l, ml_b_i_l, ml_b_f_l):
    wg_pad = jnp.concatenate([gla_w_gate_l, jnp.zeros((LANES - GLA_GATE_RANK, gla_w_gate_l.shape[1]), F32)],
                             axis=0)
    bias_row = jnp.concatenate([jnp.zeros((SM_I,), F32), ml_b_i_l, ml_b_f_l,
                                jnp.zeros((LANES - SM_F - ML_HEADS,), F32)]).reshape(1, LANES)
    return wg_pad, bias_row


def _mixers(proj, small, bsz, t, t_valid, lw, rope_tables, gla_s0, ml_state, out_dtype, want_vn):
    (wg_pad, bias_row, gla_b_gate, gla_norm_w, ml_norm_w, gm_ln_w, gm_ln_b, gm_ws, gm_bs_t) = lw
    proj = proj.reshape(bsz, t, N_MAIN)
    small = small.reshape(bsz, t, LANES)

    out_a, gla_s = _gla(proj, small, wg_pad, gla_b_gate, gla_norm_w, gla_s0, t_valid, out_dtype)

    gates = _gates(small, bias_row, t_valid)
    out_b, ml_c, ml_n, ml_m = _mlstm(proj, gates, ml_norm_w, *ml_state, out_dtype)

    out_c, vn = _gmlp(proj, gm_ln_w, gm_ln_b, gm_ws, gm_bs_t, out_dtype, want_vn)

    q_rope = _rope(proj, _MAIN_OFF["a_q"], rope_tables)
    return proj, (out_a, out_b, out_c), q_rope, gla_s, (ml_c, ml_n, ml_m[..., :1]), vn


def kernel(x_prompt, x_sample, cache_k, cache_v, page_table, state_gla, state_mlstm_C, state_mlstm_n,
           state_mlstm_m, norm_w, w_in, gla_w_gate, gla_b_gate, gla_norm_w, ml_b_i, ml_b_f, ml_norm_w,
           gm_ln_w, gm_ln_b, gm_w_s, gm_b_s, w_out, final_norm_w):
    bp, tp, d = x_prompt.shape
    db, ts, _ = x_sample.shape
    depth = w_in.shape[0]
    page = cache_k.shape[2]
    past_len = page_table.shape[1] * page
    assert w_out.shape[1] == 4 * GROUP_W and ts <= SAMPLE_T
    assert past_len % MB_BLOCK == 0 and tp % MB_BLOCK == 0

    tables_p = _rope_tables(jnp.arange(tp, dtype=jnp.int32))
    tables_s = _rope_tables(past_len + jnp.arange(SAMPLE_T, dtype=jnp.int32))
    cache_k = cache_k.reshape(depth, cache_k.shape[1], page * MB_HEADS, MB_DH)
    cache_v = cache_v.reshape(depth, cache_v.shape[1], page * MB_HEADS, MB_DH)

    yp = x_prompt.reshape(bp * tp, d)
    ys = jnp.pad(x_sample, ((0, 0), (0, SAMPLE_T - ts), (0, 0))).reshape(db * SAMPLE_T, d)
    dt_p = BF16
    dt_s = F32

    zero_gla = jnp.zeros((bp, GLA_HEADS // 2, GLA_DV, LANES), F32)
    zero_ml = (jnp.zeros((bp, ML_HEADS, ML_DH, ML_DH), F32), jnp.zeros((bp, ML_HEADS, 1, ML_DH), F32),
               jnp.zeros((bp, ML_HEADS, 1, LANES), F32))

    outs = {n: [] for n in ("ks", "vs", "gp", "gs", "cp", "cs", "np", "ns", "mp", "ms", "vv")}
    lp = min(tp, GM_CHUNK)
    w_t = jnp.swapaxes(w_in, 1, 2)
    w_main = _wprep(w_t)
    w_small = _wsmall(w_t)
    w_out4 = w_out.reshape(depth, 4, GROUP_W, d)
    kv_stacks = None
    for l in range(depth):
        wg_pad, bias_row = _layer_weights(gla_w_gate[l], ml_b_i[l], ml_b_f[l])
        common = (wg_pad, bias_row, gla_b_gate[l], gla_norm_w[l], ml_norm_w[l], gm_ln_w[l], gm_ln_b[l])
        h_p, small_p = _rms_small(yp, norm_w[l], w_small, l)
        h_s, small_s = _rms_small(ys, norm_w[l], w_small, l)
        proj_p, proj_s = _in_proj(h_p, h_s, w_main, l)

        lw = common + (gm_w_s[l][:, :lp, :lp], gm_b_s[l][:, :lp].T)
        proj, mix, q_rope, gla_s, ml_s, _ = _mixers(
            proj_p, small_p, bp, tp, tp, lw, tables_p, zero_gla, zero_ml, dt_p, want_vn=False)
        kv_stacks = _kv_heads(proj, tables_p, l, depth, kv_stacks)
        out_d = _moba_prompt(q_rope, kv_stacks[0], kv_stacks[1], l, proj, dt_p)
        yp = _out_proj([a.reshape(bp * tp, GROUP_W) for a in mix + (out_d,)], w_out4, l, yp)
        outs["gp"].append(_gla_state_from_pairs(gla_s))
        outs["cp"].append(ml_s[0])
        outs["np"].append(ml_s[1][:, :, 0, :])
        outs["mp"].append(ml_s[2][:, :, 0, 0])

        lw = common + (gm_w_s[l][:, :SAMPLE_T, :SAMPLE_T], gm_b_s[l][:, :SAMPLE_T].T)
        ml_state = (state_mlstm_C[l], state_mlstm_n[l][:, :, None, :],
                    jnp.broadcast_to(state_mlstm_m[l][:, :, None, None], (db, ML_HEADS, 1, LANES)))
        proj, mix, q_rope, gla_s, ml_s, vn = _mixers(
            proj_s, small_s, db, SAMPLE_T, ts, lw, tables_s, _gla_state_to_pairs(state_gla[l]), ml_state,
            dt_s, want_vn=True)
        k_rope = _rope(proj, _MAIN_OFF["a_k"], tables_s)
        v_new = proj[:, :, _MAIN_OFF["a_v"]:_MAIN_OFF["a_v"] + GROUP_W]
        q_rows = (q_rope.reshape(db, SAMPLE_T, MB_HEADS, MB_DH).transpose(0, 2, 1, 3)
                  .reshape(db, QROWS, MB_DH))
        o_part, stats = _moba_past(l, q_rows, cache_k, cache_v, page_table)
        out_d = _moba_merge(o_part, stats, q_rows, k_rope, proj, ts)
        ys = _out_proj([a.reshape(db * SAMPLE_T, GROUP_W) for a in mix + (out_d,)], w_out4, l, ys)
        outs["ks"].append(k_rope[:, :ts].reshape(db, ts, MB_HEADS, MB_DH))
        outs["vs"].append(v_new[:, :ts].reshape(db, ts, MB_HEADS, MB_DH))
        outs["gs"].append(_gla_state_from_pairs(gla_s))
        outs["cs"].append(ml_s[0])
        outs["ns"].append(ml_s[1][:, :, 0, :])
        outs["ms"].append(ml_s[2][:, :, 0, 0])
        outs["vv"].append(vn[:, :ts])

    y_prompt = _rms(yp, final_norm_w).reshape(bp, tp, d)
    y_sample = _rms(ys, final_norm_w).reshape(db, SAMPLE_T, d)[:, :ts]
    st = jnp.stack
    k_prompt = kv_stacks[0].reshape(depth, bp, tp, MB_HEADS, MB_DH)
    v_prompt = kv_stacks[1].reshape(depth, bp, tp, MB_HEADS, MB_DH)
    return (y_prompt, y_sample, k_prompt, v_prompt, st(outs["ks"]), st(outs["vs"]),
            st(outs["gp"]), st(outs["gs"]), st(outs["cp"]), st(outs["cs"]), st(outs["np"]), st(outs["ns"]),
            st(outs["mp"]), st(outs["ms"]), st(outs["vv"]))
```

```python
import functools

import numpy as np
import jax
import jax.numpy as jnp
from jax import lax
from jax.experimental import pallas as pl
from jax.experimental.pallas import tpu as pltpu

F32 = jnp.float32
BF16 = jnp.bfloat16
HIGHEST = lax.Precision.HIGHEST

GROUP_W = 1024
GLA_HEADS, GLA_DK, GLA_DV = 8, 64, 128
GLA_GATE_RANK, GLA_GATE_NORM = 16, 16.0
ML_HEADS, ML_DH = 8, 128
GM_GROUPS, GM_CH, GM_CHUNK = 8, 128, 128
MB_HEADS, MB_DH, MB_BLOCK, MB_TOPK = 8, 128, 256, 3
ROT_DIM, ROPE_THETA = 32, 500000.0
EPS = 1e-6

LANES = 128
SUBLANES = 8
VMEM_LIMIT = 56 * 1024 * 1024
VMEM_LIMIT_TALL = 60 * 1024 * 1024

NEG = -1e30
SAMPLE_T = SUBLANES

_MAIN_ORDER = ("g_q", "g_k", "g_v", "g_z", "m_q", "m_k", "m_v", "m_o", "m_z",
               "c_u", "c_v", "c_z", "a_q", "a_k", "a_v", "a_z")
_SPLIT_NAMES = ("g_q", "g_k", "g_v", "g_lr", "g_z", "m_q", "m_k", "m_v", "m_i", "m_f", "m_o", "m_z",
                "c_u", "c_v", "c_z", "a_q", "a_k", "a_v", "a_z")
_SPLIT_W = (GLA_HEADS * GLA_DK, GLA_HEADS * GLA_DK, GLA_HEADS * GLA_DV, GLA_GATE_RANK, GROUP_W,
            GROUP_W, GROUP_W, GROUP_W, ML_HEADS, ML_HEADS, GROUP_W, GROUP_W,
            GROUP_W, GROUP_W, GROUP_W, GROUP_W, GROUP_W, GROUP_W, GROUP_W)
_SRC_OFF = dict(zip(_SPLIT_NAMES, np.concatenate([[0], np.cumsum(_SPLIT_W)[:-1]]).tolist()))
_SRC_W = dict(zip(_SPLIT_NAMES, _SPLIT_W))
_MAIN_OFF = {}
_off = 0
for _n in _MAIN_ORDER:
    _MAIN_OFF[_n] = _off
    _off += _SRC_W[_n]
N_MAIN = _off
SM_LR, SM_I, SM_F = 0, GLA_GATE_RANK, GLA_GATE_RANK + ML_HEADS


def _cparams(sem, vmem_limit=VMEM_LIMIT):
    return pltpu.CompilerParams(dimension_semantics=sem, vmem_limit_bytes=vmem_limit)


def _pick(n, cands):
    for c in cands:
        if n % c == 0:
            return c
    return n


def _silu(x):
    return x / (1.0 + jnp.exp(-x))


def _sigmoid(x):
    return 1.0 / (1.0 + jnp.exp(-x))


def _log_sigmoid(x):
    return jnp.minimum(x, 0.0) - jnp.log(1.0 + jnp.exp(-jnp.abs(x)))


def _gelu(x):
    c = np.sqrt(2.0 / np.pi).astype(np.float32)
    return 0.5 * x * (1.0 + jnp.tanh(c * (x + 0.044715 * (x * x * x))))


def _dot_nt(a, b, precision=None):
    return lax.dot_general(a, b, (((1,), (1,)), ((), ())), preferred_element_type=F32, precision=precision)


def _dot_tn(a, b, precision=None):
    return lax.dot_general(a, b, (((0,), (0,)), ((), ())), preferred_element_type=F32, precision=precision)


def _dot(a, b, precision=None):
    return jnp.dot(a, b, preferred_element_type=F32, precision=precision)


def _rms_small_kernel(x_ref, nw_ref, ws_ref, h_ref, sm_ref):
    x = x_ref[...]
    ms = jnp.mean(x * x, axis=-1, keepdims=True)
    hb = (x * lax.rsqrt(ms + EPS) * nw_ref[...]).astype(BF16)
    h_ref[...] = hb
    sm_ref[...] = _dot(hb, ws_ref[...])


def _rms_small(x, nw, w_small, layer):
    m, d = x.shape
    bm = _pick(m, (512, 256, 128, 64, 32, 16, 8))
    return pl.pallas_call(
        _rms_small_kernel,
        grid=(m // bm,),
        in_specs=[pl.BlockSpec((bm, d), lambda i: (i, 0)),
                  pl.BlockSpec((1, d), lambda i: (0, 0)),
                  pl.BlockSpec((None, d, LANES), lambda i: (layer, 0, 0))],
        out_specs=[pl.BlockSpec((bm, d), lambda i: (i, 0)),
                   pl.BlockSpec((bm, LANES), lambda i: (i, 0))],
        out_shape=[jax.ShapeDtypeStruct((m, d), BF16), jax.ShapeDtypeStruct((m, LANES), F32)],
        compiler_params=_cparams(("parallel",)),
        name="rms_small",
    )(x, nw.reshape(1, d), w_small)


def _rms_kernel(x_ref, nw_ref, o_ref):
    x = x_ref[...]
    ms = jnp.mean(x * x, axis=-1, keepdims=True)
    o_ref[...] = x * lax.rsqrt(ms + EPS) * nw_ref[...]


def _rms(x, nw):
    m, d = x.shape
    bm = _pick(m, (512, 256, 128, 64, 32, 16, 8))
    return pl.pallas_call(
        _rms_kernel,
        grid=(m // bm,),
        in_specs=[pl.BlockSpec((bm, d), lambda i: (i, 0)), pl.BlockSpec((1, d), lambda i: (0, 0))],
        out_specs=pl.BlockSpec((bm, d), lambda i: (i, 0)),
        out_shape=jax.ShapeDtypeStruct((m, d), F32),
        compiler_params=_cparams(("parallel",)),
        name="rms_final",
    )(x, nw.reshape(1, d))


WP_BN = 1024
WP_B1 = _MAIN_OFF["g_z"] // WP_BN
WP_B2 = _MAIN_OFF["m_o"] // WP_BN
WP_S1 = GLA_GATE_RANK
WP_S2 = GLA_GATE_RANK + 2 * ML_HEADS
assert _MAIN_OFF["g_z"] % WP_BN == 0 and _MAIN_OFF["m_o"] % WP_BN == 0 and N_MAIN % WP_BN == 0
assert WP_S1 % SUBLANES == 0 and WP_S2 % SUBLANES == 0 and sum(_SPLIT_W) % WP_S2 == 0


def _wprep_kernel(a_ref, t_ref, o_ref):
    j = pl.program_id(2)

    def emit(s):
        x = a_ref[...] if s == 0 else jnp.concatenate([a_ref[...], t_ref[...]], axis=0)[s:s + WP_BN]
        o_ref[...] = x.T.astype(BF16)

    @pl.when(j < WP_B1)
    def _():
        emit(0)

    @pl.when((j >= WP_B1) & (j < WP_B2))
    def _():
        emit(WP_S1)

    @pl.when(j >= WP_B2)
    def _():
        emit(WP_S2)


def _wprep(w_t):
    depth, _, d = w_t.shape
    kb = _pick(d, (2048, 1024, 512, 256, 128))
    return pl.pallas_call(
        _wprep_kernel,
        grid=(depth, d // kb, N_MAIN // WP_BN),
        in_specs=[pl.BlockSpec((None, WP_BN, kb), lambda l, i, j: (l, j, i)),
                  pl.BlockSpec((None, WP_S2, kb), lambda l, i, j: (l, (j + 1) * (WP_BN // WP_S2), i))],
        out_specs=pl.BlockSpec((None, kb, WP_BN), lambda l, i, j: (l, i, j)),
        out_shape=jax.ShapeDtypeStruct((depth, d, N_MAIN), BF16),
        compiler_params=_cparams(("parallel", "parallel", "parallel")),
        name="w_prep",
    )(w_t, w_t)


def _wsmall_kernel(lr_ref, if_ref, o_ref):
    kb = lr_ref.shape[1]
    x = jnp.concatenate([lr_ref[...], if_ref[...],
                         jnp.zeros((LANES - GLA_GATE_RANK - 2 * ML_HEADS, kb), F32)], axis=0)
    o_ref[...] = x.T.astype(BF16)


def _wsmall(w_t):
    depth, _, d = w_t.shape
    kb = _pick(d, (512, 256, 128))
    lr, gi = _SRC_OFF["g_lr"], _SRC_OFF["m_i"]
    assert _SRC_OFF["m_f"] == gi + ML_HEADS and lr % GLA_GATE_RANK == 0 and gi % (2 * ML_HEADS) == 0
    return pl.pallas_call(
        _wsmall_kernel,
        grid=(depth, d // kb),
        in_specs=[pl.BlockSpec((None, GLA_GATE_RANK, kb), lambda l, i: (l, lr // GLA_GATE_RANK, i)),
                  pl.BlockSpec((None, 2 * ML_HEADS, kb), lambda l, i: (l, gi // (2 * ML_HEADS), i))],
        out_specs=pl.BlockSpec((None, kb, LANES), lambda l, i: (l, i, 0)),
        out_shape=jax.ShapeDtypeStruct((depth, d, LANES), BF16),
        compiler_params=_cparams(("parallel", "parallel")),
        name="w_small",
    )(w_t, w_t)


def _mm_kernel(a_ref, as_ref, b_ref, o_ref, os_ref):
    w = b_ref[...]
    o_ref[...] = _dot(a_ref[...], w)

    @pl.when(pl.program_id(1) == 0)
    def _():
        os_ref[...] = _dot(as_ref[...], w)


def _in_proj(a, a_s, w_all, layer):
    m, k = a.shape
    ms = a_s.shape[0]
    n = w_all.shape[2]
    bm = _pick(m, (1024, 512, 256, 128, 64))
    bn = _pick(n, (1024, 512, 256, 128))
    return pl.pallas_call(
        _mm_kernel,
        grid=(n // bn, m // bm),
        in_specs=[pl.BlockSpec((bm, k), lambda j, i: (i, 0)),
                  pl.BlockSpec((ms, k), lambda j, i: (0, 0)),
                  pl.BlockSpec((None, k, bn), lambda j, i: (layer, 0, j))],
        out_specs=[pl.BlockSpec((bm, bn), lambda j, i: (i, j)),
                   pl.BlockSpec((ms, bn), lambda j, i: (0, j))],
        out_shape=[jax.ShapeDtypeStruct((m, n), F32), jax.ShapeDtypeStruct((ms, n), F32)],
        compiler_params=_cparams(("parallel", "arbitrary")),
        name="in_proj",
    )(a, a_s, w_all)


def _out_kernel(a0, a1, a2, a3, w_ref, x_ref, o_ref):
    acc = x_ref[...]
    for g, a in enumerate((a0, a1, a2, a3)):
        acc = acc + _dot(a[...].astype(BF16), w_ref[g].astype(BF16))
    o_ref[...] = acc


def _out_proj(mix, w_all, layer, x):
    m, d = x.shape
    bm = _pick(m, (2048, 1024, 512, 256, 128, 64))
    bn = _pick(d, (256, 128))
    a_spec = pl.BlockSpec((bm, GROUP_W), lambda i, j: (i, 0))
    return pl.pallas_call(
        _out_kernel,
        grid=(m // bm, d // bn),
        in_specs=[a_spec, a_spec, a_spec, a_spec,
                  pl.BlockSpec((None, 4, GROUP_W, bn), lambda i, j: (layer, 0, 0, j)),
                  pl.BlockSpec((bm, bn), lambda i, j: (i, j))],
        out_specs=pl.BlockSpec((bm, bn), lambda i, j: (i, j)),
        out_shape=jax.ShapeDtypeStruct((m, d), F32),
        compiler_params=_cparams(("parallel", "parallel"), VMEM_LIMIT_TALL),
        name="out_proj",
    )(*mix, w_all, x)


def _rope_kernel(x_ref, a_ref, b_ref, c_ref, o_ref):
    a, b, c = a_ref[...], b_ref[...], c_ref[...]
    for h in range(MB_HEADS):
        sl = slice(h * MB_DH, (h + 1) * MB_DH)
        x = x_ref[0, :, sl]
        o_ref[0, :, sl] = (x * a + pltpu.roll(x, MB_DH - ROT_DIM // 2, 1) * b
                           + pltpu.roll(x, ROT_DIM // 2, 1) * c)


def _rope_tables(pos):
    half = ROT_DIM // 2
    inv_freq = jnp.power(ROPE_THETA, -jnp.arange(0, ROT_DIM, 2, dtype=F32) / ROT_DIM)
    ang = pos.astype(F32)[:, None] * inv_freq[None, :]
    cos, sin = jnp.cos(ang), jnp.sin(ang)
    t = pos.shape[0]
    a = jnp.concatenate([cos, cos, jnp.ones((t, MB_DH - ROT_DIM), F32)], axis=1)
    b = jnp.concatenate([-sin, jnp.zeros((t, MB_DH - half), F32)], axis=1)
    c = jnp.concatenate([jnp.zeros((t, half), F32), sin, jnp.zeros((t, MB_DH - ROT_DIM), F32)], axis=1)
    return a, b, c


def _rope(proj, col, tables):
    bsz, t, _ = proj.shape
    bt = _pick(t, (512, 256, 128, 64, 32, 16, 8))
    tab = pl.BlockSpec((bt, MB_DH), lambda b, i: (i, 0))
    return pl.pallas_call(
        _rope_kernel,
        grid=(bsz, t // bt),
        in_specs=[pl.BlockSpec((1, bt, GROUP_W), lambda b, i: (b, i, col // GROUP_W)), tab, tab, tab],
        out_specs=pl.BlockSpec((1, bt, GROUP_W), lambda b, i: (b, i, 0)),
        out_shape=jax.ShapeDtypeStruct((bsz, t, GROUP_W), F32),
        compiler_params=_cparams(("parallel", "parallel")),
        name="rope",
    )(proj, *tables)


def _kv_heads_kernel(k_ref, v_ref, a_ref, b_ref, c_ref, *refs):
    k4_ref, v4_ref = refs[-2:]
    a, b, c = a_ref[...], b_ref[...], c_ref[...]
    bt = a.shape[0]
    for h in range(MB_HEADS):
        sl = slice(h * MB_DH, (h + 1) * MB_DH)
        x = k_ref[0, :, sl]
        rows = pl.ds(h, bt, stride=MB_HEADS)
        k4_ref[rows, :] = (x * a + pltpu.roll(x, MB_DH - ROT_DIM // 2, 1) * b
                           + pltpu.roll(x, ROT_DIM // 2, 1) * c)
        v4_ref[rows, :] = v_ref[0, :, sl]


def _kv_heads(proj, tables, layer, depth, stacks):
    bsz, t, _ = proj.shape
    bt = _pick(t, (512, 256, 128, 64, 32, 16, 8))
    tab = pl.BlockSpec((bt, MB_DH), lambda b, i: (i, 0))
    kcol, vcol = _MAIN_OFF["a_k"] // GROUP_W, _MAIN_OFF["a_v"] // GROUP_W
    out = pl.BlockSpec((None, None, bt * MB_HEADS, MB_DH), lambda b, i: (layer, b, i, 0))
    shape = jax.ShapeDtypeStruct((depth, bsz, t * MB_HEADS, MB_DH), F32)
    in_specs = [pl.BlockSpec((1, bt, GROUP_W), lambda b, i: (b, i, kcol)),
                pl.BlockSpec((1, bt, GROUP_W), lambda b, i: (b, i, vcol)), tab, tab, tab]
    aliases = {}
    if stacks is not None:
        in_specs += [pl.BlockSpec(memory_space=pl.ANY), pl.BlockSpec(memory_space=pl.ANY)]
        aliases = {5: 0, 6: 1}
    return pl.pallas_call(
        _kv_heads_kernel,
        grid=(bsz, t // bt),
        in_specs=in_specs,
        out_specs=[out, out],
        out_shape=[shape, shape],
        input_output_aliases=aliases,
        compiler_params=_cparams(("parallel", "parallel")),
        name="kv_heads",
    )(proj, proj, *tables, *(stacks or ()))


GLA_GROUP = 2
assert (GLA_HEADS // 2) % GLA_GROUP == 0


def _gla_levels(c):
    return [c >> i for i in range(1, c.bit_length())]


def _gla_sum_matrices(c, nch):
    sup = c * nch
    t = np.arange(sup)[:, None]
    r = np.arange(sup)[None, :]
    cb = (t // c) * c
    mats = [(r >= cb) & (r <= t), (r > t) & (r <= cb + c - 1)]
    for hs in _gla_levels(c):
        base = (t // (2 * hs)) * (2 * hs)
        ref = base + hs - 1
        mats.append(np.where(t - base >= hs, (r > ref) & (r <= t), (r > t) & (r <= ref)))
    return np.concatenate(mats, axis=0).astype(np.float32)


def _gla_kernel(q_ref, k_ref, v_ref, z_ref, sm_ref, wg_ref, bg_ref, nw_ref, s0_ref, sum_ref,
                o_ref, s_ref, g_scr, *, t, c, nch, t_valid):
    dk = GLA_DK
    sup = c * nch
    levels = _gla_levels(c)
    split = sum_ref.dtype == BF16
    x = _dot(sm_ref[0], wg_ref[...], precision=HIGHEST) + bg_ref[...]
    g_all = _log_sigmoid(x) * (1.0 / GLA_GATE_NORM)
    if t_valid < t:
        rows_t = lax.broadcasted_iota(jnp.int32, g_all.shape, 0)
        g_all = jnp.where(rows_t < t_valid, g_all, 0.0)
    g_scr[...] = g_all

    lane = lax.broadcasted_iota(jnp.int32, (sup, LANES), 1)
    row = lax.broadcasted_iota(jnp.int32, (sup, LANES), 0)
    head0 = lane < dk
    rr = lax.broadcasted_iota(jnp.int32, (sup, sup), 0)
    cc = lax.broadcasted_iota(jnp.int32, (sup, sup), 1)
    mm = BF16 if split else F32
    prec = None if split else HIGHEST
    nw = nw_ref[...]

    def pair_trip(p, r0, s2t, result):
        ql = slice(p * LANES, (p + 1) * LANES)
        vl = slice(p * 2 * GLA_DV, (p + 1) * 2 * GLA_DV)
        q = q_ref[0, pl.ds(r0, sup), ql] * (dk ** -0.5)
        k = k_ref[0, pl.ds(r0, sup), ql]
        if t_valid < t:
            k = jnp.where(row < t_valid, k, 0.0)
        g = g_scr[pl.ds(r0, sup), ql]
        v = v_ref[0, pl.ds(r0, sup), vl]
        z = z_ref[0, pl.ds(r0, sup), vl]
        if split:
            g1 = g.astype(BF16)
            r1 = g - g1.astype(F32)
            g2 = r1.astype(BF16)
            g3 = (r1 - g2.astype(F32)).astype(BF16)
            xs = _dot(sum_ref[...], jnp.concatenate([g1, g2, g3], axis=1))
            xs = xs[:, :LANES] + xs[:, LANES:2 * LANES] + xs[:, 2 * LANES:]
        else:
            xs = _dot(sum_ref[...], g, precision=HIGHEST)
        yield
        b = xs[0:sup]
        to_end = xs[sup:2 * sup]

        qk = q * k
        att0 = jnp.where(rr == cc, jnp.sum(jnp.where(head0, qk, 0.0), axis=-1, keepdims=True), 0.0)
        att1 = jnp.where(rr == cc, jnp.sum(jnp.where(head0, 0.0, qk), axis=-1, keepdims=True), 0.0)
        for i, hs in enumerate(levels):
            e = jnp.exp(xs[(2 + i) * sup:(3 + i) * sup])
            upper = (row & hs) != 0
            qt = jnp.where(upper, q * e, 0.0)
            kl = jnp.where(upper, 0.0, k * e).astype(mm)
            sh = (2 * hs).bit_length() - 1
            same = (rr >> sh) == (cc >> sh)
            att0 = att0 + jnp.where(same, _dot_nt(jnp.where(head0, qt, 0.0).astype(mm), kl, prec), 0.0)
            att1 = att1 + jnp.where(same, _dot_nt(jnp.where(head0, 0.0, qt).astype(mm), kl, prec), 0.0)
            yield

        kt = k * jnp.exp(to_end)
        qe = q * jnp.exp(b)
        qe0, qe1 = jnp.where(head0, qe, 0.0), jnp.where(head0, 0.0, qe)
        kt0, kt1 = jnp.where(head0, kt, 0.0), jnp.where(head0, 0.0, kt)
        v0, v1 = v[:, :GLA_DV], v[:, GLA_DV:]
        rows = [slice(ch * c, (ch + 1) * c) for ch in range(nch)]
        upds = [_dot_tn(jnp.concatenate([v0[r], v1[r]], axis=0).astype(mm),
                        jnp.concatenate([kt0[r], kt1[r]], axis=0).astype(mm), prec) for r in rows]
        yield
        o_int = []
        for ch, r in enumerate(rows):
            lhs = jnp.concatenate([qe0[r], qe1[r]], axis=0).astype(mm)
            o_int.append(_dot_nt(lhs, s2t.astype(mm), prec))
            s2t = jnp.exp(b[ch * c + c - 1:ch * c + c, :]) * s2t + upds[ch]
        result[p] = s2t
        yield

        for h, (att, vh) in enumerate(((att0, v0), (att1, v1))):
            o = (jnp.concatenate([oi[h * c:(h + 1) * c, :] for oi in o_int], axis=0)
                 + _dot(att.astype(mm), vh.astype(mm), prec))
            yield
            on = o * lax.rsqrt(jnp.mean(o * o, axis=-1, keepdims=True) + EPS) * nw
            zh = z[:, h * GLA_DV:(h + 1) * GLA_DV]
            cols = slice(p * 2 * GLA_DV + h * GLA_DV, p * 2 * GLA_DV + (h + 1) * GLA_DV)
            o_ref[0, pl.ds(r0, sup), cols] = (on * _silu(zh)).astype(o_ref.dtype)

    def body(si, states):
        r0 = pl.multiple_of(si * sup, sup)
        result = [None] * GLA_GROUP
        trips = [pair_trip(p, r0, states[p], result) for p in range(GLA_GROUP)]
        while trips:
            trips = [g for g in trips if next(g, True) is None]
        return tuple(result)

    states = lax.fori_loop(0, t // sup, body, tuple(s0_ref[0, p] for p in range(GLA_GROUP)))
    for p in range(GLA_GROUP):
        s_ref[0, p] = states[p]


def _gla(proj, small, wg_pad, bg, nw, s0t, t_valid, out_dtype):
    bsz, t, _ = proj.shape
    c = min(LANES, t)
    nch = 1
    pairs = GLA_HEADS // 2
    gw = GLA_GROUP * LANES
    vw = GLA_GROUP * 2 * GLA_DV
    qb, kb = _MAIN_OFF["g_q"] // gw, _MAIN_OFF["g_k"] // gw
    vb, zb = _MAIN_OFF["g_v"] // vw, _MAIN_OFF["g_z"] // vw
    kern = functools.partial(_gla_kernel, t=t, c=c, nch=nch, t_valid=t_valid)
    sums = _gla_sum_matrices(c, nch)
    sums = jnp.asarray(sums, BF16 if (c * nch) % 16 == 0 else F32)
    return pl.pallas_call(
        kern,
        grid=(bsz, pairs // GLA_GROUP),
        in_specs=[pl.BlockSpec((1, t, gw), lambda b, p: (b, 0, qb + p)),
                  pl.BlockSpec((1, t, gw), lambda b, p: (b, 0, kb + p)),
                  pl.BlockSpec((1, t, vw), lambda b, p: (b, 0, vb + p)),
                  pl.BlockSpec((1, t, vw), lambda b, p: (b, 0, zb + p)),
                  pl.BlockSpec((1, t, LANES), lambda b, p: (b, 0, 0)),
                  pl.BlockSpec((LANES, gw), lambda b, p: (0, p)),
                  pl.BlockSpec((1, gw), lambda b, p: (0, p)),
                  pl.BlockSpec((1, GLA_DV), lambda b, p: (0, 0)),
                  pl.BlockSpec((1, GLA_GROUP, GLA_DV, LANES), lambda b, p: (b, p, 0, 0)),
                  pl.BlockSpec(sums.shape, lambda b, p: (0, 0))],
        out_specs=[pl.BlockSpec((1, t, vw), lambda b, p: (b, 0, p)),
                   pl.BlockSpec((1, GLA_GROUP, GLA_DV, LANES), lambda b, p: (b, p, 0, 0))],
        out_shape=[jax.ShapeDtypeStruct((bsz, t, GROUP_W), out_dtype),
                   jax.ShapeDtypeStruct((bsz, pairs, GLA_DV, LANES), F32)],
        scratch_shapes=[pltpu.VMEM((t, gw), F32)],
        compiler_params=_cparams(("parallel", "parallel")),
        name="gla",
    )(proj, proj, proj, proj, small, wg_pad, bg.reshape(1, -1), nw.reshape(1, -1), s0t, sums)


def _gla_state_to_pairs(s):
    bsz = s.shape[0]
    s = s.reshape(bsz, GLA_HEADS // 2, 2, GLA_DK, GLA_DV)
    return s.transpose(0, 1, 4, 2, 3).reshape(bsz, GLA_HEADS // 2, GLA_DV, 2 * GLA_DK)


def _gla_state_from_pairs(s):
    bsz = s.shape[0]
    s = s.reshape(bsz, GLA_HEADS // 2, GLA_DV, 2, GLA_DK)
    return s.transpose(0, 1, 3, 4, 2).reshape(bsz, GLA_HEADS, GLA_DK, GLA_DV)


def _gates_kernel(sm_ref, bias_ref, o_ref, *, t_valid):
    x = sm_ref[0] + bias_ref[...]
    lane = lax.broadcasted_iota(jnp.int32, x.shape, 1)
    is_f = (lane >= SM_F) & (lane < SM_F + ML_HEADS)
    out = jnp.where(is_f, _log_sigmoid(x), x)
    if t_valid is not None:
        row = lax.broadcasted_iota(jnp.int32, x.shape, 0)
        out = jnp.where(row < t_valid, out, jnp.where(is_f, 0.0, NEG))
    o_ref[0] = out


def _gates(small, bias_row, t_valid):
    bsz, t, _ = small.shape
    bt = t if t_valid < t else _pick(t, (1024, 512, 256, 128, 64, 32, 16, 8))
    return pl.pallas_call(
        functools.partial(_gates_kernel, t_valid=t_valid if t_valid < t else None),
        grid=(bsz, t // bt),
        in_specs=[pl.BlockSpec((1, bt, LANES), lambda b, i: (b, i, 0)),
                  pl.BlockSpec((1, LANES), lambda b, i: (0, 0))],
        out_specs=pl.BlockSpec((1, bt, LANES), lambda b, i: (b, i, 0)),
        out_shape=jax.ShapeDtypeStruct((bsz, t, LANES), F32),
        compiler_params=_cparams(("parallel", "parallel")),
        name="ml_gates",
    )(small, bias_row)


ML_PAIR = 2


def _mlstm_kernel(q_ref, k_ref, v_ref, og_ref, z_ref, gt_ref, nw_ref,
                  c0_ref, n0_ref, m0_ref, o_ref, c_ref, n_ref, m_ref, *, t, c):
    h0 = pl.program_id(1) * ML_PAIR
    hs = range(ML_PAIR)
    lane = lax.broadcasted_iota(jnp.int32, (c, LANES), 1)
    rr = lax.broadcasted_iota(jnp.int32, (c, c), 0)
    cc = lax.broadcasted_iota(jnp.int32, (c, c), 1)
    causal = rr >= cc
    nw = nw_ref[...]
    hsl = lambda j: slice(j * ML_DH, (j + 1) * ML_DH)

    def body(ci, carry):
        cms, ns, m_prevs = carry
        r0 = pl.multiple_of(ci * c, c)
        gt = gt_ref[0, pl.ds(r0, c), :]
        qs = [q_ref[0, pl.ds(r0, c), hsl(j)] * (ML_DH ** -0.5) for j in hs]
        ks = [k_ref[0, pl.ds(r0, c), hsl(j)] for j in hs]
        vs = [v_ref[0, pl.ds(r0, c), hsl(j)] for j in hs]
        s_qk = [_dot_nt(qs[j], ks[j], precision=HIGHEST) for j in hs]
        s_qc = [_dot_nt(qs[j], cms[j], precision=HIGHEST) for j in hs]

        d, inter, m_t, i_cols, fc_cols = [], [], [], [], []
        for j in hs:
            i_col = jnp.sum(jnp.where(lane == SM_I + h0 + j, gt, 0.0), axis=-1, keepdims=True)
            f_col = jnp.sum(jnp.where(lane == SM_F + h0 + j, gt, 0.0), axis=-1, keepdims=True)
            i_row = jnp.sum(jnp.where(rr == cc, i_col, 0.0), axis=0, keepdims=True)
            f_row = jnp.sum(jnp.where(rr == cc, f_col, 0.0), axis=0, keepdims=True)
            fc_col = jnp.sum(jnp.where(causal, f_row, 0.0), axis=-1, keepdims=True)
            fc_row = jnp.sum(jnp.where(rr <= cc, f_col, 0.0), axis=0, keepdims=True)
            dj = jnp.where(causal, fc_col - fc_row + i_row, NEG)
            d.append(dj)
            inter.append(fc_col + m_prevs[j])
            m_t.append(jnp.maximum(inter[j], jnp.max(dj, axis=-1, keepdims=True)))
            i_cols.append(i_col)
            fc_cols.append(fc_col)

        w_state = [jnp.exp(inter[j] - m_t[j]) for j in hs]
        qk = [s_qk[j] * jnp.exp(d[j] - m_t[j]) for j in hs]
        pv = [_dot(qk[j], vs[j], precision=HIGHEST) for j in hs]
        w_end, dec, m_new = [], [], []
        for j in hs:
            m_new.append(m_t[j][c - 1:c, :])
            f_end = fc_cols[j][c - 1:c, :]
            w_end.append(jnp.exp(f_end - fc_cols[j] + i_cols[j] - m_new[j]))
            dec.append(jnp.exp(f_end + m_prevs[j] - m_new[j]))
        upd = [_dot_tn(vs[j] * w_end[j], ks[j], precision=HIGHEST) for j in hs]

        c_new, n_new = [], []
        for j in hs:
            num = w_state[j] * s_qc[j] + pv[j]
            den = (w_state[j] * jnp.sum(qs[j] * ns[j], axis=-1, keepdims=True)
                   + jnp.sum(qk[j], axis=-1, keepdims=True))
            hout = num / jnp.maximum(jnp.abs(den), jnp.exp(-m_t[j]))
            c_new.append(dec[j] * cms[j] + upd[j])
            n_new.append(dec[j] * ns[j] + jnp.sum(ks[j] * w_end[j], axis=0, keepdims=True))
            o = _sigmoid(og_ref[0, pl.ds(r0, c), hsl(j)]) * hout
            on = o * lax.rsqrt(jnp.mean(o * o, axis=-1, keepdims=True) + EPS) * nw
            o_ref[0, pl.ds(r0, c), hsl(j)] = (on * _silu(z_ref[0, pl.ds(r0, c), hsl(j)])).astype(o_ref.dtype)
        return tuple(c_new), tuple(n_new), tuple(m_new)

    init = (tuple(c0_ref[0, j] for j in hs), tuple(n0_ref[0, j] for j in hs),
            tuple(m0_ref[0, j][:, :1] for j in hs))
    cms, ns, ms = lax.fori_loop(0, t // c, body, init)
    for j in hs:
        c_ref[0, j] = cms[j]
        n_ref[0, j] = ns[j]
        m_ref[0, j] = jnp.broadcast_to(ms[j], (1, LANES))


def _split3(x):
    a = x.astype(BF16)
    r = x - a.astype(F32)
    b = r.astype(BF16)
    return a, b, (r - b.astype(F32)).astype(BF16)


def _mlstm_wide_kernel(q_ref, k_ref, v_ref, og_ref, z_ref, gt_ref, nw_ref, c0_ref, n0_ref, m0_ref,
                       o_ref, c_ref, n_ref, m_ref, t_scr, g_scr, i_scr, fc_scr, *, t):
    c = LANES
    nc = t // c
    h0 = pl.program_id(1) * ML_PAIR
    hs = range(ML_PAIR)
    rr = lax.broadcasted_iota(jnp.int32, (c, c), 0)
    cc = lax.broadcasted_iota(jnp.int32, (c, c), 1)
    causal = rr >= cc
    nw = nw_ref[...]
    hsl = lambda j: slice(j * ML_DH, (j + 1) * ML_DH)
    ones_b = jnp.ones((c, ML_DH), BF16)
    tril_b = jnp.where(causal, 1.0, 0.0).astype(BF16)

    lane_t = lax.broadcasted_iota(jnp.int32, (c, LANES), 1)
    for ci in range(nc):
        rows = slice(ci * c, (ci + 1) * c)
        g = gt_ref[0, rows, :]
        g_scr[rows, :] = jnp.where(lane_t < SM_F, g, sum(_dot(tril_b, x) for x in _split3(g)))
    sel_r = lax.broadcasted_iota(jnp.int32, (LANES, 2 * LANES), 0)
    sel_c = lax.broadcasted_iota(jnp.int32, (LANES, 2 * LANES), 1)
    g_terms = _split3(g_scr[...])
    for j in hs:
        src = jnp.where(sel_c < LANES, SM_I + h0 + j, SM_F + h0 + j)
        sel = jnp.where(sel_r == src, 1.0, 0.0).astype(BF16)
        both = sum(_dot(g, sel) for g in g_terms)
        i_scr[j] = both[:, :LANES]
        fc_scr[j] = both[:, LANES:]

    def body(ci, carry):
        cms, ns, m_prevs = carry
        r0 = pl.multiple_of(ci * c, c)
        t_scr[...] = g_scr[pl.ds(r0, c), :].T
        qbs = [(q_ref[0, pl.ds(r0, c), hsl(j)] * (ML_DH ** -0.5)).astype(BF16) for j in hs]
        kbs = [k_ref[0, pl.ds(r0, c), hsl(j)].astype(BF16) for j in hs]
        vs = [v_ref[0, pl.ds(r0, c), hsl(j)] for j in hs]
        s_qk = [_dot_nt(qbs[j], kbs[j]) for j in hs]
        s_qc = [_dot_nt(qbs[j], jnp.concatenate([cms[j], jnp.broadcast_to(ns[j], (c, ML_DH))],
                                                axis=0).astype(BF16)) for j in hs]

        d, inter, m_t, fcs = [], [], [], []
        for j in hs:
            fc = fc_scr[j, pl.ds(r0, c), :]
            i_row = t_scr[pl.ds(SM_I + h0 + j, 1), :]
            fc_row = t_scr[pl.ds(SM_F + h0 + j, 1), :]
            dj = jnp.where(causal, fc - fc_row + i_row, NEG)
            d.append(dj)
            inter.append(fc + m_prevs[j])
            m_t.append(jnp.maximum(inter[j], jnp.max(dj, axis=-1, keepdims=True)))
            fcs.append(fc)

        w_state = [jnp.exp(inter[j] - m_t[j]) for j in hs]
        qk = [(s_qk[j] * jnp.exp(d[j] - m_t[j])).astype(BF16) for j in hs]
        pv = [_dot(qk[j], jnp.concatenate([vs[j].astype(BF16), ones_b], axis=1)) for j in hs]
        w_end, dec, m_new = [], [], []
        for j in hs:
            m_new.append(m_t[j][c - 1:c, :1])
            f_end = fcs[j][c - 1:c, :]
            w_end.append(jnp.exp(f_end - fcs[j] + i_scr[j, pl.ds(r0, c), :] - m_new[j]))
            dec.append(jnp.exp(f_end[:, :1] + m_prevs[j] - m_new[j]))
        upd = [_dot_tn(jnp.concatenate([vs[j] * w_end[j], w_end[j]], axis=1).astype(BF16), kbs[j])
               for j in hs]

        c_new, n_new = [], []
        hout = []
        for j in hs:
            num = w_state[j] * s_qc[j][:, :ML_DH] + pv[j][:, :ML_DH]
            den = w_state[j] * s_qc[j][:, ML_DH:] + pv[j][:, ML_DH:]
            hout.append(_sigmoid(og_ref[0, pl.ds(r0, c), hsl(j)]) * num
                        / jnp.maximum(jnp.abs(den), jnp.exp(-m_t[j])))
            c_new.append(dec[j] * cms[j] + upd[j][:ML_DH, :])
            n_new.append(dec[j] * ns[j] + upd[j][ML_DH:ML_DH + 1, :])
        msq = [_dot((o * o).astype(BF16), ones_b) * (1.0 / ML_DH) for o in hout]
        for j in hs:
            on = hout[j] * lax.rsqrt(msq[j] + EPS) * nw
            o_ref[0, pl.ds(r0, c), hsl(j)] = (on * _silu(z_ref[0, pl.ds(r0, c), hsl(j)])).astype(o_ref.dtype)
        return tuple(c_new), tuple(n_new), tuple(m_new)

    init = (tuple(c0_ref[0, j] for j in hs), tuple(n0_ref[0, j] for j in hs),
            tuple(m0_ref[0, j][:, :1] for j in hs))
    cms, ns, ms = lax.fori_loop(0, nc, body, init)
    for j in hs:
        c_ref[0, j] = cms[j]
        n_ref[0, j] = ns[j]
        m_ref[0, j] = jnp.broadcast_to(ms[j], (1, LANES))


def _mlstm(proj, gates, nw, c0, n0, m0, out_dtype):
    bsz, t, _ = proj.shape
    c = min(128, t)
    w = ML_PAIR * ML_DH
    col = lambda name: _MAIN_OFF[name] // w
    spec = lambda name: pl.BlockSpec((1, t, w), lambda b, p, o=col(name): (b, 0, o + p))
    st = lambda rows: pl.BlockSpec((1, ML_PAIR, rows, ML_DH), lambda b, p: (b, p, 0, 0))
    if c == LANES:
        kern = functools.partial(_mlstm_wide_kernel, t=t)
        scratch = ([pltpu.VMEM((LANES, LANES), F32), pltpu.VMEM((t, LANES), F32)]
                   + [pltpu.VMEM((ML_PAIR, t, LANES), F32)] * 2)
    else:
        kern = functools.partial(_mlstm_kernel, t=t, c=c)
        scratch = []
    return pl.pallas_call(
        kern,
        grid=(bsz, ML_HEADS // ML_PAIR),
        in_specs=[spec("m_q"), spec("m_k"), spec("m_v"), spec("m_o"), spec("m_z"),
                  pl.BlockSpec((1, t, LANES), lambda b, p: (b, 0, 0)),
                  pl.BlockSpec((1, ML_DH), lambda b, p: (0, 0)),
                  st(ML_DH), st(1), st(1)],
        out_specs=[pl.BlockSpec((1, t, w), lambda b, p: (b, 0, p)), st(ML_DH), st(1), st(1)],
        out_shape=[jax.ShapeDtypeStruct((bsz, t, GROUP_W), out_dtype),
                   jax.ShapeDtypeStruct((bsz, ML_HEADS, ML_DH, ML_DH), F32),
                   jax.ShapeDtypeStruct((bsz, ML_HEADS, 1, ML_DH), F32),
                   jax.ShapeDtypeStruct((bsz, ML_HEADS, 1, LANES), F32)],
        scratch_shapes=scratch,
        compiler_params=_cparams(("parallel", "parallel")),
        name="mlstm",
    )(proj, proj, proj, proj, proj, gates, nw.reshape(1, -1), c0, n0, m0)


def _gmlp_kernel(u_ref, v_ref, z_ref, lw_ref, lb_ref, ws_ref, bs_ref, o_ref, *vn_refs, l, nl):
    rr = lax.broadcasted_iota(jnp.int32, (l, l), 0)
    cc = lax.broadcasted_iota(jnp.int32, (l, l), 1)
    wts = [jnp.where(rr >= cc, ws_ref[g], 0.0) for g in range(GM_GROUPS)]
    if l >= GM_CHUNK:
        wts = [w.astype(BF16) for w in wts]
    for ch in range(nl):
        rows = slice(ch * l, (ch + 1) * l)
        gv = _gelu(v_ref[0, rows, :])
        mu = jnp.mean(gv, axis=-1, keepdims=True)
        xc = gv - mu
        vn = xc * lax.rsqrt(jnp.mean(xc * xc, axis=-1, keepdims=True) + EPS) * lw_ref[...] + lb_ref[...]
        if vn_refs:
            vn_refs[0][0, rows, :] = vn
        for g in range(GM_GROUPS):
            sl = slice(g * GM_CH, (g + 1) * GM_CH)
            vg = vn[:, sl]
            if l >= GM_CHUNK:
                s = _dot(wts[g], vg.astype(BF16))
            else:
                s = jnp.zeros((l, GM_CH), F32)
                for r in range(l):
                    s = s + wts[g][:, r:r + 1] * vg[r:r + 1, :]
            s = s + bs_ref[:, g:g + 1]
            o_ref[0, rows, sl] = (_gelu(u_ref[0, rows, sl]) * s * _silu(z_ref[0, rows, sl])).astype(o_ref.dtype)


def _gmlp(proj, lw, lb, ws, bs_t, out_dtype, want_vn):
    bsz, t, _ = proj.shape
    l = min(t, GM_CHUNK)
    nl = _pick(t // l, (4, 2, 1))
    col = lambda name: _MAIN_OFF[name] // GROUP_W
    spec = lambda name: pl.BlockSpec((1, l * nl, GROUP_W), lambda b, i, o=col(name): (b, i, o))
    out = pl.BlockSpec((1, l * nl, GROUP_W), lambda b, i: (b, i, 0))
    out_specs = [out] + ([out] if want_vn else [])
    out_shape = ([jax.ShapeDtypeStruct((bsz, t, GROUP_W), out_dtype)]
                 + ([jax.ShapeDtypeStruct((bsz, t, GROUP_W), F32)] if want_vn else []))
    res = pl.pallas_call(
        functools.partial(_gmlp_kernel, l=l, nl=nl),
        grid=(bsz, t // (l * nl)),
        in_specs=[spec("c_u"), spec("c_v"), spec("c_z"),
                  pl.BlockSpec((1, GROUP_W), lambda b, i: (0, 0)),
                  pl.BlockSpec((1, GROUP_W), lambda b, i: (0, 0)),
                  pl.BlockSpec((GM_GROUPS, l, l), lambda b, i: (0, 0, 0)),
                  pl.BlockSpec((l, GM_GROUPS), lambda b, i: (0, 0))],
        out_specs=out_specs,
        out_shape=out_shape,
        compiler_params=_cparams(("parallel", "parallel")),
        name="gmlp",
    )(proj, proj, proj, lw.reshape(1, -1), lb.reshape(1, -1), ws, bs_t)
    return (res[0], res[1]) if want_vn else (res[0], None)


MB_GROUP = 2
assert MB_HEADS % MB_GROUP == 0


def _moba_prompt_kernel(q_ref, k_ref, v_ref, z_ref, o_ref, kb_scr, vb_scr, km_scr, *, t):
    blk = MB_BLOCK
    nb = t // blk
    scale = MB_DH ** -0.5
    nbp = -(-nb // SUBLANES) * SUBLANES
    blk_row = lax.broadcasted_iota(jnp.int32, (nbp, blk), 0)
    rr = lax.broadcasted_iota(jnp.int32, (blk, blk), 0)
    cc = lax.broadcasted_iota(jnp.int32, (blk, blk), 1)
    log2_scale = float(scale * np.log2(np.e))
    lane_tiles = lambda xs: [x[:, i:i + LANES] for x in xs for i in range(0, blk, LANES)]

    def head_steps(hh):
        h = pl.program_id(1) * MB_GROUP + hh
        cols = slice(hh * MB_DH, (hh + 1) * MB_DH)
        km_scr[hh] = jnp.zeros((LANES, MB_DH), F32)
        for n in range(nb):
            rows = pl.ds(n * blk * MB_HEADS + h, blk, stride=MB_HEADS)
            kn = k_ref[rows, :]
            kb_scr[hh, n * blk:(n + 1) * blk, :] = kn.astype(BF16)
            vb_scr[hh, n * blk:(n + 1) * blk, :] = v_ref[rows, :].astype(BF16)
            km_scr[hh, n:n + 1, :] = jnp.mean(kn, axis=0, keepdims=True)
        kmean = km_scr[hh]
        yield
        for qi in range(nb):
            rows = slice(qi * blk, (qi + 1) * blk)
            q = q_ref[0, rows, cols]
            qb = q.astype(BF16)
            bias = None
            if qi > MB_TOPK:
                gate = _dot_nt(kmean, q, precision=HIGHEST)[0:nbp, :]
                yield
                cnt = jnp.zeros((nbp, blk), F32)
                for m in range(qi):
                    gm = gate[m:m + 1, :]
                    beats = (gm > gate) | ((gm == gate) & (blk_row > m))
                    cnt = cnt + jnp.where(beats, 1.0, 0.0)
                bias_t = jnp.where(cnt < MB_TOPK, 0.0, NEG)
                bias = jnp.concatenate([bias_t, jnp.zeros((LANES - nbp, blk), F32)], axis=0).T
            ss = []
            for j in range(qi + 1):
                s = _dot_nt(qb, kb_scr[hh, j * blk:(j + 1) * blk, :]) * log2_scale
                if j == qi:
                    s = jnp.where(cc <= rr, s, NEG)
                elif bias is not None:
                    s = s + bias[:, j:j + 1]
                ss.append(s)
            yield
            m_i = jnp.max(functools.reduce(jnp.maximum, lane_tiles(ss)), axis=-1, keepdims=True)
            ps = [jnp.exp2(s - m_i) for s in ss]
            l_i = jnp.sum(functools.reduce(jnp.add, lane_tiles(ps)), axis=-1, keepdims=True)
            acc = _dot(ps[0].astype(BF16), vb_scr[hh, 0:blk, :])
            for j in range(1, qi + 1):
                acc = acc + _dot(ps[j].astype(BF16), vb_scr[hh, j * blk:(j + 1) * blk, :])
            yield
            o_ref[0, rows, cols] = ((acc / l_i) * _silu(z_ref[0, rows, cols])).astype(o_ref.dtype)

    heads = [head_steps(hh) for hh in range(MB_GROUP)]
    while heads:
        heads = [g for g in heads if next(g, True) is None]


def _moba_prompt(q_rope, k4, v4, layer, proj, out_dtype):
    bsz, t, _ = proj.shape
    assert t % MB_BLOCK == 0 and t // MB_BLOCK <= LANES
    gw = MB_GROUP * MB_DH
    zcol = _MAIN_OFF["a_z"] // gw
    hd = pl.BlockSpec((1, t, gw), lambda b, h: (b, 0, h))
    kv = pl.BlockSpec((None, None, t * MB_HEADS, MB_DH), lambda b, h: (layer, b, 0, 0))
    return pl.pallas_call(
        functools.partial(_moba_prompt_kernel, t=t),
        grid=(bsz, MB_HEADS // MB_GROUP),
        in_specs=[hd, kv, kv, pl.BlockSpec((1, t, gw), lambda b, h: (b, 0, zcol + h))],
        out_specs=hd,
        out_shape=jax.ShapeDtypeStruct((bsz, t, GROUP_W), out_dtype),
        scratch_shapes=[pltpu.VMEM((MB_GROUP, t, MB_DH), BF16), pltpu.VMEM((MB_GROUP, t, MB_DH), BF16),
                        pltpu.VMEM((MB_GROUP, LANES, MB_DH), F32)],
        compiler_params=_cparams(("parallel", "arbitrary"), VMEM_LIMIT_TALL),
        name="moba_prompt",
    )(q_rope, k4, v4, proj)


QROWS = MB_HEADS * SAMPLE_T
ST_M, ST_L, ST_G = 0, 1, 2


def _moba_past_kernel(pt_ref, q_ref, bias_ref, *refs, nbs):
    del pt_ref
    npg = 2 * nbs
    k_refs, v_refs = refs[:npg], refs[npg:2 * npg]
    o_ref, st_ref = refs[2 * npg:]
    scale = MB_DH ** -0.5
    page = k_refs[0].shape[0] // MB_HEADS
    q = q_ref[0]
    qb = q.astype(BF16)
    bias = bias_ref[...]
    lane = lax.broadcasted_iota(jnp.int32, (QROWS, LANES), 1)
    ss = [_dot_nt(qb, r[...].astype(BF16)) * scale + bias for r in k_refs]
    ps, ms, ls = [], [], []
    for n in range(nbs):
        s0, s1 = ss[2 * n], ss[2 * n + 1]
        m = jnp.maximum(jnp.max(s0, axis=-1, keepdims=True), jnp.max(s1, axis=-1, keepdims=True))
        p0, p1 = jnp.exp(s0 - m), jnp.exp(s1 - m)
        ps += [p0.astype(BF16), p1.astype(BF16)]
        ms.append(m)
        ls.append(jnp.sum(p0, axis=-1, keepdims=True) + jnp.sum(p1, axis=-1, keepdims=True))
    for n in range(nbs):
        o_ref[n] = (_dot(ps[2 * n], v_refs[2 * n][...].astype(BF16))
                    + _dot(ps[2 * n + 1], v_refs[2 * n + 1][...].astype(BF16)))
        ksum = (jnp.sum(k_refs[2 * n][...].reshape(page, MB_HEADS, MB_DH), axis=0)
                + jnp.sum(k_refs[2 * n + 1][...].reshape(page, MB_HEADS, MB_DH), axis=0))
        ksum_rows = jnp.concatenate(
            [jnp.broadcast_to(ksum[h:h + 1, :], (SAMPLE_T, MB_DH)) for h in range(MB_HEADS)], axis=0)
        gate = jnp.sum(q * ksum_rows, axis=-1, keepdims=True) * (1.0 / MB_BLOCK)
        st_ref[n] = jnp.where(lane == ST_M, ms[n], jnp.where(lane == ST_L, ls[n], gate))


def _moba_past(layer, q_rows, cache_k, cache_v, page_table):
    db = q_rows.shape[0]
    rows = cache_k.shape[2]
    assert 2 * rows == MB_BLOCK * MB_HEADS
    nb = page_table.shape[1] // 2
    nbs = _pick(nb, (8, 4, 2, 1))
    key_head = np.arange(rows) % MB_HEADS
    row_head = np.arange(QROWS) // SAMPLE_T
    bias = jnp.asarray(np.where(key_head[None, :] == row_head[:, None], 0.0, NEG).astype(np.float32))
    pg = lambda i: pl.BlockSpec((None, None, rows, MB_DH),
                                lambda b, n, pt, i=i: (layer, pt[b, 2 * nbs * n + i], 0, 0))
    pages = [pg(i) for i in range(2 * nbs)]
    part = pl.BlockSpec((None, nbs, QROWS, MB_DH), lambda b, n, pt: (b, n, 0, 0))
    shape = jax.ShapeDtypeStruct((db, nb, QROWS, MB_DH), F32)
    grid_spec = pltpu.PrefetchScalarGridSpec(
        num_scalar_prefetch=1,
        grid=(db, nb // nbs),
        in_specs=[pl.BlockSpec((1, QROWS, MB_DH), lambda b, n, pt: (b, 0, 0)),
                  pl.BlockSpec((QROWS, rows), lambda b, n, pt: (0, 0))] + pages + pages,
        out_specs=[part, part],
    )
    return pl.pallas_call(
        functools.partial(_moba_past_kernel, nbs=nbs),
        grid_spec=grid_spec,
        out_shape=[shape, shape],
        compiler_params=_cparams(("parallel", "parallel")),
        name="moba_past",
    )(page_table, q_rows, bias, *([cache_k] * (2 * nbs)), *([cache_v] * (2 * nbs)))


def _moba_merge_kernel(op_ref, st_ref, q_ref, k_ref, v_ref, z_ref, o_ref, *, nb, t_valid):
    scale = MB_DH ** -0.5
    g = st_ref[0, :, :, ST_G:ST_G + 1]
    m = st_ref[0, :, :, ST_M:ST_M + 1]
    l = st_ref[0, :, :, ST_L:ST_L + 1]
    nidx = lax.broadcasted_iota(jnp.int32, g.shape, 0)
    sel = jnp.zeros(g.shape, jnp.bool_)
    gm = g
    for _ in range(min(MB_TOPK, nb)):
        mx = jnp.max(gm, axis=0, keepdims=True)
        first = jnp.min(jnp.where(gm == mx, nidx, nb), axis=0, keepdims=True)
        pick = nidx == first
        sel = sel | pick
        gm = jnp.where(pick, -jnp.inf, gm)

    rr = lax.broadcasted_iota(jnp.int32, (SAMPLE_T, SAMPLE_T), 0)
    cc = lax.broadcasted_iota(jnp.int32, (SAMPLE_T, SAMPLE_T), 1)
    mo, lo, oo = [], [], []
    for h in range(MB_HEADS):
        sl = slice(h * MB_DH, (h + 1) * MB_DH)
        qh = q_ref[0, h * SAMPLE_T:(h + 1) * SAMPLE_T, :]
        s = _dot_nt(qh.astype(BF16), k_ref[0, :, sl].astype(BF16)) * scale
        s = jnp.where((cc <= rr) & (cc < t_valid), s, NEG)
        mh = jnp.max(s, axis=-1, keepdims=True)
        p = jnp.exp(s - mh)
        mo.append(mh)
        lo.append(jnp.sum(p, axis=-1, keepdims=True))
        oo.append(_dot(p.astype(BF16), v_ref[0, :, sl].astype(BF16)))
    m_own, l_own, o_own = (jnp.concatenate(x, axis=0) for x in (mo, lo, oo))

    m_tot = jnp.maximum(jnp.max(jnp.where(sel, m, NEG), axis=0), m_own)
    w = jnp.where(sel, jnp.exp(m - m_tot[None]), 0.0)
    w_own = jnp.exp(m_own - m_tot)
    den = jnp.sum(w * l, axis=0) + w_own * l_own
    acc = w_own * o_own
    for n in range(nb):
        acc = acc + w[n] * op_ref[0, n]
    o = acc / den
    for h in range(MB_HEADS):
        sl = slice(h * MB_DH, (h + 1) * MB_DH)
        o_ref[0, :, sl] = o[h * SAMPLE_T:(h + 1) * SAMPLE_T, :] * _silu(z_ref[0, :, sl])


def _moba_merge(o_part, stats, q_rows, k_rope, proj, t_valid):
    db, nb = o_part.shape[:2]
    vcol, zcol = _MAIN_OFF["a_v"] // GROUP_W, _MAIN_OFF["a_z"] // GROUP_W
    part = pl.BlockSpec((1, nb, QROWS, MB_DH), lambda b: (b, 0, 0, 0))
    row = pl.BlockSpec((1, SAMPLE_T, GROUP_W), lambda b: (b, 0, 0))
    return pl.pallas_call(
        functools.partial(_moba_merge_kernel, nb=nb, t_valid=t_valid),
        grid=(db,),
        in_specs=[part, part, pl.BlockSpec((1, QROWS, MB_DH), lambda b: (b, 0, 0)), row,
                  pl.BlockSpec((1, SAMPLE_T, GROUP_W), lambda b: (b, 0, vcol)),
                  pl.BlockSpec((1, SAMPLE_T, GROUP_W), lambda b: (b, 0, zcol))],
        out_specs=row,
        out_shape=jax.ShapeDtypeStruct((db, SAMPLE_T, GROUP_W), F32),
        compiler_params=_cparams(("parallel",)),
        name="moba_merge",
    )(o_part, stats, q_rows, k_rope, proj, proj)


def _layer_weights(gla_w_gate_l, ml_b_i_l, ml_b_f_l):
    wg_pad = jnp.concatenate([gla_w_gate_l, jnp.zeros((LANES - GLA_GATE_RANK, gla_w_gate_l.shape[1]), F32)],
                             axis=0)
    bias_row = jnp.concatenate([jnp.zeros((SM_I,), F32), ml_b_i_l, ml_b_f_l,
                                jnp.zeros((LANES - SM_F - ML_HEADS,), F32)]).reshape(1, LANES)
    return wg_pad, bias_row


def _mixers(proj, small, bsz, t, t_valid, lw, rope_tables, gla_s0, ml_state, out_dtype, want_vn):
    (wg_pad, bias_row, gla_b_gate, gla_norm_w, ml_norm_w, gm_ln_w, gm_ln_b, gm_ws, gm_bs_t) = lw
    proj = proj.reshape(bsz, t, N_MAIN)
    small = small.reshape(bsz, t, LANES)

    out_a, gla_s = _gla(proj, small, wg_pad, gla_b_gate, gla_norm_w, gla_s0, t_valid, out_dtype)

    gates = _gates(small, bias_row, t_valid)
    out_b, ml_c, ml_n, ml_m = _mlstm(proj, gates, ml_norm_w, *ml_state, out_dtype)

    out_c, vn = _gmlp(proj, gm_ln_w, gm_ln_b, gm_ws, gm_bs_t, out_dtype, want_vn)

    q_rope = _rope(proj, _MAIN_OFF["a_q"], rope_tables)
    return proj, (out_a, out_b, out_c), q_rope, gla_s, (ml_c, ml_n, ml_m[..., :1]), vn


def kernel(x_prompt, x_sample, cache_k, cache_v, page_table, state_gla, state_mlstm_C, state_mlstm_n,
           state_mlstm_m, norm_w, w_in, gla_w_gate, gla_b_gate, gla_norm_w, ml_b_i, ml_b_f, ml_norm_w,
           gm_ln_w, gm_ln_b, gm_w_s, gm_b_s, w_out, final_norm_w):
    bp, tp, d = x_prompt.shape
    db, ts, _ = x_sample.shape
    depth = w_in.shape[0]
    page = cache_k.shape[2]
    past_len = page_table.shape[1] * page
    assert w_out.shape[1] == 4 * GROUP_W and ts <= SAMPLE_T
    assert past_len % MB_BLOCK == 0 and tp % MB_BLOCK == 0

    tables_p = _rope_tables(jnp.arange(tp, dtype=jnp.int32))
    tables_s = _rope_tables(past_len + jnp.arange(SAMPLE_T, dtype=jnp.int32))
    cache_k = cache_k.reshape(depth, cache_k.shape[1], page * MB_HEADS, MB_DH)
    cache_v = cache_v.reshape(depth, cache_v.shape[1], page * MB_HEADS, MB_DH)

    yp = x_prompt.reshape(bp * tp, d)
    ys = jnp.pad(x_sample, ((0, 0), (0, SAMPLE_T - ts), (0, 0))).reshape(db * SAMPLE_T, d)
    dt_p = BF16
    dt_s = F32

    zero_gla = jnp.zeros((bp, GLA_HEADS // 2, GLA_DV, LANES), F32)
    zero_ml = (jnp.zeros((bp, ML_HEADS, ML_DH, ML_DH), F32), jnp.zeros((bp, ML_HEADS, 1, ML_DH), F32),
               jnp.zeros((bp, ML_HEADS, 1, LANES), F32))

    outs = {n: [] for n in ("ks", "vs", "gp", "gs", "cp", "cs", "np", "ns", "mp", "ms", "vv")}
    lp = min(tp, GM_CHUNK)
    w_t = jnp.swapaxes(w_in, 1, 2)
    w_main = _wprep(w_t)
    w_small = _wsmall(w_t)
    w_out4 = w_out.reshape(depth, 4, GROUP_W, d)
    kv_stacks = None
    for l in range(depth):
        wg_pad, bias_row = _layer_weights(gla_w_gate[l], ml_b_i[l], ml_b_f[l])
        common = (wg_pad, bias_row, gla_b_gate[l], gla_norm_w[l], ml_norm_w[l], gm_ln_w[l], gm_ln_b[l])
        h_p, small_p = _rms_small(yp, norm_w[l], w_small, l)
        h_s, small_s = _rms_small(ys, norm_w[l], w_small, l)
        proj_p, proj_s = _in_proj(h_p, h_s, w_main, l)

        lw = common + (gm_w_s[l][:, :lp, :lp], gm_b_s[l][:, :lp].T)
        proj, mix, q_rope, gla_s, ml_s, _ = _mixers(
            proj_p, small_p, bp, tp, tp, lw, tables_p, zero_gla, zero_ml, dt_p, want_vn=False)
        kv_stacks = _kv_heads(proj, tables_p, l, depth, kv_stacks)
        out_d = _moba_prompt(q_rope, kv_stacks[0], kv_stacks[1], l, proj, dt_p)
        yp = _out_proj([a.reshape(bp * tp, GROUP_W) for a in mix + (out_d,)], w_out4, l, yp)
        outs["gp"].append(_gla_state_from_pairs(gla_s))
        outs["cp"].append(ml_s[0])
        outs["np"].append(ml_s[1][:, :, 0, :])
        outs["mp"].append(ml_s[2][:, :, 0, 0])

        lw = common + (gm_w_s[l][:, :SAMPLE_T, :SAMPLE_T], gm_b_s[l][:, :SAMPLE_T].T)
        ml_state = (state_mlstm_C[l], state_mlstm_n[l][:, :, None, :],
                    jnp.broadcast_to(state_mlstm_m[l][:, :, None, None], (db, ML_HEADS, 1, LANES)))
        proj, mix, q_rope, gla_s, ml_s, vn = _mixers(
            proj_s, small_s, db, SAMPLE_T, ts, lw, tables_s, _gla_state_to_pairs(state_gla[l]), ml_state,
            dt_s, want_vn=True)
        k_rope = _rope(proj, _MAIN_OFF["a_k"], tables_s)
        v_new = proj[:, :, _MAIN_OFF["a_v"]:_MAIN_OFF["a_v"] + GROUP_W]
        q_rows = (q_rope.reshape(db, SAMPLE_T, MB_HEADS, MB_DH).transpose(0, 2, 1, 3)
                  .reshape(db, QROWS, MB_DH))
        o_part, stats = _moba_past(l, q_rows, cache_k, cache_v, page_table)
        out_d = _moba_merge(o_part, stats, q_rows, k_rope, proj, ts)
        ys = _out_proj([a.reshape(db * SAMPLE_T, GROUP_W) for a in mix + (out_d,)], w_out4, l, ys)
        outs["ks"].append(k_rope[:, :ts].reshape(db, ts, MB_HEADS, MB_DH))
        outs["vs"].append(v_new[:, :ts].reshape(db, ts, MB_HEADS, MB_DH))
        outs["gs"].append(_gla_state_from_pairs(gla_s))
        outs["cs"].append(ml_s[0])
        outs["ns"].append(ml_s[1][:, :, 0, :])
        outs["ms"].append(ml_s[2][:, :, 0, 0])
        outs["vv"].append(vn[:, :ts])

    y_prompt = _rms(yp, final_norm_w).reshape(bp, tp, d)
    y_sample = _rms(ys, final_norm_w).reshape(db, SAMPLE_T, d)[:, :ts]
    st = jnp.stack
    k_prompt = kv_stacks[0].reshape(depth, bp, tp, MB_HEADS, MB_DH)
    v_prompt = kv_stacks[1].reshape(depth, bp, tp, MB_HEADS, MB_DH)
    return (y_prompt, y_sample, k_prompt, v_prompt, st(outs["ks"]), st(outs["vs"]),
            st(outs["gp"]), st(outs["gs"]), st(outs["cp"]), st(outs["cs"]), st(outs["np"]), st(outs["ns"]),
            st(outs["mp"]), st(outs["ms"]), st(outs["vv"]))
```

```python
import functools

import numpy as np
import jax
import jax.numpy as jnp
from jax import lax
from jax.experimental import pallas as pl
from jax.experimental.pallas import tpu as pltpu

F32 = jnp.float32
BF16 = jnp.bfloat16
HIGHEST = lax.Precision.HIGHEST

GROUP_W = 1024
GLA_HEADS, GLA_DK, GLA_DV = 8, 64, 128
GLA_GATE_RANK, GLA_GATE_NORM = 16, 16.0
ML_HEADS, ML_DH = 8, 128
GM_GROUPS, GM_CH, GM_CHUNK = 8, 128, 128
MB_HEADS, MB_DH, MB_BLOCK, MB_TOPK = 8, 128, 256, 3
ROT_DIM, ROPE_THETA = 32, 500000.0
EPS = 1e-6

LANES = 128
SUBLANES = 8
VMEM_LIMIT = 56 * 1024 * 1024
VMEM_LIMIT_TALL = 60 * 1024 * 1024

NEG = -1e30
SAMPLE_T = SUBLANES

_MAIN_ORDER = ("g_q", "g_k", "g_v", "g_z", "m_q", "m_k", "m_v", "m_o", "m_z",
               "c_u", "c_v", "c_z", "a_q", "a_k", "a_v", "a_z")
_SPLIT_NAMES = ("g_q", "g_k", "g_v", "g_lr", "g_z", "m_q", "m_k", "m_v", "m_i", "m_f", "m_o", "m_z",
                "c_u", "c_v", "c_z", "a_q", "a_k", "a_v", "a_z")
_SPLIT_W = (GLA_HEADS * GLA_DK, GLA_HEADS * GLA_DK, GLA_HEADS * GLA_DV, GLA_GATE_RANK, GROUP_W,
            GROUP_W, GROUP_W, GROUP_W, ML_HEADS, ML_HEADS, GROUP_W, GROUP_W,
            GROUP_W, GROUP_W, GROUP_W, GROUP_W, GROUP_W, GROUP_W, GROUP_W)
_SRC_OFF = dict(zip(_SPLIT_NAMES, np.concatenate([[0], np.cumsum(_SPLIT_W)[:-1]]).tolist()))
_SRC_W = dict(zip(_SPLIT_NAMES, _SPLIT_W))
_MAIN_OFF = {}
_off = 0
for _n in _MAIN_ORDER:
    _MAIN_OFF[_n] = _off
    _off += _SRC_W[_n]
N_MAIN = _off
SM_LR, SM_I, SM_F = 0, GLA_GATE_RANK, GLA_GATE_RANK + ML_HEADS


def _cparams(sem, vmem_limit=VMEM_LIMIT):
    return pltpu.CompilerParams(dimension_semantics=sem, vmem_limit_bytes=vmem_limit)


def _pick(n, cands):
    for c in cands:
        if n % c == 0:
            return c
    return n


def _silu(x):
    return x / (1.0 + jnp.exp(-x))


def _sigmoid(x):
    return 1.0 / (1.0 + jnp.exp(-x))


def _log_sigmoid(x):
    return jnp.minimum(x, 0.0) - jnp.log(1.0 + jnp.exp(-jnp.abs(x)))


def _gelu(x):
    c = np.sqrt(2.0 / np.pi).astype(np.float32)
    return 0.5 * x * (1.0 + jnp.tanh(c * (x + 0.044715 * (x * x * x))))


def _dot_nt(a, b, precision=None):
    return lax.dot_general(a, b, (((1,), (1,)), ((), ())), preferred_element_type=F32, precision=precision)


def _dot_tn(a, b, precision=None):
    return lax.dot_general(a, b, (((0,), (0,)), ((), ())), preferred_element_type=F32, precision=precision)


def _dot(a, b, precision=None):
    return jnp.dot(a, b, preferred_element_type=F32, precision=precision)


def _rms_small_kernel(x_ref, nw_ref, ws_ref, h_ref, sm_ref):
    x = x_ref[...]
    ms = jnp.mean(x * x, axis=-1, keepdims=True)
    hb = (x * lax.rsqrt(ms + EPS) * nw_ref[...]).astype(BF16)
    h_ref[...] = hb
    sm_ref[...] = _dot(hb, ws_ref[...])


def _rms_small(x, nw, w_small, layer):
    m, d = x.shape
    bm = _pick(m, (512, 256, 128, 64, 32, 16, 8))
    return pl.pallas_call(
        _rms_small_kernel,
        grid=(m // bm,),
        in_specs=[pl.BlockSpec((bm, d), lambda i: (i, 0)),
                  pl.BlockSpec((1, d), lambda i: (0, 0)),
                  pl.BlockSpec((None, d, LANES), lambda i: (layer, 0, 0))],
        out_specs=[pl.BlockSpec((bm, d), lambda i: (i, 0)),
                   pl.BlockSpec((bm, LANES), lambda i: (i, 0))],
        out_shape=[jax.ShapeDtypeStruct((m, d), BF16), jax.ShapeDtypeStruct((m, LANES), F32)],
        compiler_params=_cparams(("parallel",)),
        name="rms_small",
    )(x, nw.reshape(1, d), w_small)


def _rms_kernel(x_ref, nw_ref, o_ref):
    x = x_ref[...]
    ms = jnp.mean(x * x, axis=-1, keepdims=True)
    o_ref[...] = x * lax.rsqrt(ms + EPS) * nw_ref[...]


def _rms(x, nw):
    m, d = x.shape
    bm = _pick(m, (512, 256, 128, 64, 32, 16, 8))
    return pl.pallas_call(
        _rms_kernel,
        grid=(m // bm,),
        in_specs=[pl.BlockSpec((bm, d), lambda i: (i, 0)), pl.BlockSpec((1, d), lambda i: (0, 0))],
        out_specs=pl.BlockSpec((bm, d), lambda i: (i, 0)),
        out_shape=jax.ShapeDtypeStruct((m, d), F32),
        compiler_params=_cparams(("parallel",)),
        name="rms_final",
    )(x, nw.reshape(1, d))


WP_BN = 1024
WP_B1 = _MAIN_OFF["g_z"] // WP_BN
WP_B2 = _MAIN_OFF["m_o"] // WP_BN
WP_S1 = GLA_GATE_RANK
WP_S2 = GLA_GATE_RANK + 2 * ML_HEADS
assert _MAIN_OFF["g_z"] % WP_BN == 0 and _MAIN_OFF["m_o"] % WP_BN == 0 and N_MAIN % WP_BN == 0
assert WP_S1 % SUBLANES == 0 and WP_S2 % SUBLANES == 0 and sum(_SPLIT_W) % WP_S2 == 0


def _wprep_kernel(a_ref, t_ref, o_ref):
    j = pl.program_id(2)

    def emit(s):
        x = a_ref[...] if s == 0 else jnp.concatenate([a_ref[...], t_ref[...]], axis=0)[s:s + WP_BN]
        o_ref[...] = x.T.astype(BF16)

    @pl.when(j < WP_B1)
    def _():
        emit(0)

    @pl.when((j >= WP_B1) & (j < WP_B2))
    def _():
        emit(WP_S1)

    @pl.when(j >= WP_B2)
    def _():
        emit(WP_S2)


def _wprep(w_t):
    depth, _, d = w_t.shape
    kb = _pick(d, (2048, 1024, 512, 256, 128))
    return pl.pallas_call(
        _wprep_kernel,
        grid=(depth, d // kb, N_MAIN // WP_BN),
        in_specs=[pl.BlockSpec((None, WP_BN, kb), lambda l, i, j: (l, j, i)),
                  pl.BlockSpec((None, WP_S2, kb), lambda l, i, j: (l, (j + 1) * (WP_BN // WP_S2), i))],
        out_specs=pl.BlockSpec((None, kb, WP_BN), lambda l, i, j: (l, i, j)),
        out_shape=jax.ShapeDtypeStruct((depth, d, N_MAIN), BF16),
        compiler_params=_cparams(("parallel", "parallel", "parallel")),
        name="w_prep",
    )(w_t, w_t)


def _wsmall_kernel(lr_ref, if_ref, o_ref):
    kb = lr_ref.shape[1]
    x = jnp.concatenate([lr_ref[...], if_ref[...],
                         jnp.zeros((LANES - GLA_GATE_RANK - 2 * ML_HEADS, kb), F32)], axis=0)
    o_ref[...] = x.T.astype(BF16)


def _wsmall(w_t):
    depth, _, d = w_t.shape
    kb = _pick(d, (512, 256, 128))
    lr, gi = _SRC_OFF["g_lr"], _SRC_OFF["m_i"]
    assert _SRC_OFF["m_f"] == gi + ML_HEADS and lr % GLA_GATE_RANK == 0 and gi % (2 * ML_HEADS) == 0
    return pl.pallas_call(
        _wsmall_kernel,
        grid=(depth, d // kb),
        in_specs=[pl.BlockSpec((None, GLA_GATE_RANK, kb), lambda l, i: (l, lr // GLA_GATE_RANK, i)),
                  pl.BlockSpec((None, 2 * ML_HEADS, kb), lambda l, i: (l, gi // (2 * ML_HEADS), i))],
        out_specs=pl.BlockSpec((None, kb, LANES), lambda l, i: (l, i, 0)),
        out_shape=jax.ShapeDtypeStruct((depth, d, LANES), BF16),
        compiler_params=_cparams(("parallel", "parallel")),
        name="w_small",
    )(w_t, w_t)


def _mm_kernel(a_ref, as_ref, b_ref, o_ref, os_ref):
    w = b_ref[...]
    o_ref[...] = _dot(a_ref[...], w)

    @pl.when(pl.program_id(1) == 0)
    def _():
        os_ref[...] = _dot(as_ref[...], w)


def _in_proj(a, a_s, w_all, layer):
    m, k = a.shape
    ms = a_s.shape[0]
    n = w_all.shape[2]
    bm = _pick(m, (1024, 512, 256, 128, 64))
    bn = _pick(n, (1024, 512, 256, 128))
    return pl.pallas_call(
        _mm_kernel,
        grid=(n // bn, m // bm),
        in_specs=[pl.BlockSpec((bm, k), lambda j, i: (i, 0)),
                  pl.BlockSpec((ms, k), lambda j, i: (0, 0)),
                  pl.BlockSpec((None, k, bn), lambda j, i: (layer, 0, j))],
        out_specs=[pl.BlockSpec((bm, bn), lambda j, i: (i, j)),
                   pl.BlockSpec((ms, bn), lambda j, i: (0, j))],
        out_shape=[jax.ShapeDtypeStruct((m, n), F32), jax.ShapeDtypeStruct((ms, n), F32)],
        compiler_params=_cparams(("parallel", "arbitrary")),
        name="in_proj",
    )(a, a_s, w_all)


def _out_kernel(a0, a1, a2, a3, w_ref, x_ref, o_ref):
    acc = x_ref[...]
    for g, a in enumerate((a0, a1, a2, a3)):
        acc = acc + _dot(a[...].astype(BF16), w_ref[g].astype(BF16))
    o_ref[...] = acc


def _out_proj(mix, w_all, layer, x):
    m, d = x.shape
    bm = _pick(m, (2048, 1024, 512, 256, 128, 64))
    bn = _pick(d, (256, 128))
    a_spec = pl.BlockSpec((bm, GROUP_W), lambda i, j: (i, 0))
    return pl.pallas_call(
        _out_kernel,
        grid=(m // bm, d // bn),
        in_specs=[a_spec, a_spec, a_spec, a_spec,
                  pl.BlockSpec((None, 4, GROUP_W, bn), lambda i, j: (layer, 0, 0, j)),
                  pl.BlockSpec((bm, bn), lambda i, j: (i, j))],
        out_specs=pl.BlockSpec((bm, bn), lambda i, j: (i, j)),
        out_shape=jax.ShapeDtypeStruct((m, d), F32),
        compiler_params=_cparams(("parallel", "parallel"), VMEM_LIMIT_TALL),
        name="out_proj",
    )(*mix, w_all, x)


def _rope_kernel(x_ref, a_ref, b_ref, c_ref, o_ref):
    a, b, c = a_ref[...], b_ref[...], c_ref[...]
    for h in range(MB_HEADS):
        sl = slice(h * MB_DH, (h + 1) * MB_DH)
        x = x_ref[0, :, sl]
        o_ref[0, :, sl] = (x * a + pltpu.roll(x, MB_DH - ROT_DIM // 2, 1) * b
                           + pltpu.roll(x, ROT_DIM // 2, 1) * c)


def _rope_tables(pos):
    half = ROT_DIM // 2
    inv_freq = jnp.power(ROPE_THETA, -jnp.arange(0, ROT_DIM, 2, dtype=F32) / ROT_DIM)
    ang = pos.astype(F32)[:, None] * inv_freq[None, :]
    cos, sin = jnp.cos(ang), jnp.sin(ang)
    t = pos.shape[0]
    a = jnp.concatenate([cos, cos, jnp.ones((t, MB_DH - ROT_DIM), F32)], axis=1)
    b = jnp.concatenate([-sin, jnp.zeros((t, MB_DH - half), F32)], axis=1)
    c = jnp.concatenate([jnp.zeros((t, half), F32), sin, jnp.zeros((t, MB_DH - ROT_DIM), F32)], axis=1)
    return a, b, c


def _rope(proj, col, tables):
    bsz, t, _ = proj.shape
    bt = _pick(t, (1024, 512, 256, 128, 64, 32, 16, 8))
    tab = pl.BlockSpec((bt, MB_DH), lambda b, i: (i, 0))
    return pl.pallas_call(
        _rope_kernel,
        grid=(bsz, t // bt),
        in_specs=[pl.BlockSpec((1, bt, GROUP_W), lambda b, i: (b, i, col // GROUP_W)), tab, tab, tab],
        out_specs=pl.BlockSpec((1, bt, GROUP_W), lambda b, i: (b, i, 0)),
        out_shape=jax.ShapeDtypeStruct((bsz, t, GROUP_W), F32),
        compiler_params=_cparams(("parallel", "parallel")),
        name="rope",
    )(proj, *tables)


def _kv_heads_kernel(k_ref, v_ref, a_ref, b_ref, c_ref, *refs):
    k4_ref, v4_ref = refs[-2:]
    a, b, c = a_ref[...], b_ref[...], c_ref[...]
    bt = a.shape[0]
    for h in range(MB_HEADS):
        sl = slice(h * MB_DH, (h + 1) * MB_DH)
        x = k_ref[0, :, sl]
        rows = pl.ds(h, bt, stride=MB_HEADS)
        k4_ref[rows, :] = (x * a + pltpu.roll(x, MB_DH - ROT_DIM // 2, 1) * b
                           + pltpu.roll(x, ROT_DIM // 2, 1) * c)
        v4_ref[rows, :] = v_ref[0, :, sl]


def _kv_heads(proj, tables, layer, depth, stacks):
    bsz, t, _ = proj.shape
    bt = _pick(t, (512, 256, 128, 64, 32, 16, 8))
    tab = pl.BlockSpec((bt, MB_DH), lambda b, i: (i, 0))
    kcol, vcol = _MAIN_OFF["a_k"] // GROUP_W, _MAIN_OFF["a_v"] // GROUP_W
    out = pl.BlockSpec((None, None, bt * MB_HEADS, MB_DH), lambda b, i: (layer, b, i, 0))
    shape = jax.ShapeDtypeStruct((depth, bsz, t * MB_HEADS, MB_DH), F32)
    in_specs = [pl.BlockSpec((1, bt, GROUP_W), lambda b, i: (b, i, kcol)),
                pl.BlockSpec((1, bt, GROUP_W), lambda b, i: (b, i, vcol)), tab, tab, tab]
    aliases = {}
    if stacks is not None:
        in_specs += [pl.BlockSpec(memory_space=pl.ANY), pl.BlockSpec(memory_space=pl.ANY)]
        aliases = {5: 0, 6: 1}
    return pl.pallas_call(
        _kv_heads_kernel,
        grid=(bsz, t // bt),
        in_specs=in_specs,
        out_specs=[out, out],
        out_shape=[shape, shape],
        input_output_aliases=aliases,
        compiler_params=_cparams(("parallel", "parallel")),
        name="kv_heads",
    )(proj, proj, *tables, *(stacks or ()))


GLA_GROUP = 2
assert (GLA_HEADS // 2) % GLA_GROUP == 0


def _gla_levels(c):
    return [c >> i for i in range(1, c.bit_length())]


def _gla_sum_matrices(c, nch):
    sup = c * nch
    t = np.arange(sup)[:, None]
    r = np.arange(sup)[None, :]
    cb = (t // c) * c
    mats = [(r >= cb) & (r <= t), (r > t) & (r <= cb + c - 1)]
    for hs in _gla_levels(c):
        base = (t // (2 * hs)) * (2 * hs)
        ref = base + hs - 1
        mats.append(np.where(t - base >= hs, (r > ref) & (r <= t), (r > t) & (r <= ref)))
    return np.concatenate(mats, axis=0).astype(np.float32)


def _gla_kernel(q_ref, k_ref, v_ref, z_ref, sm_ref, wg_ref, bg_ref, nw_ref, s0_ref, sum_ref,
                o_ref, s_ref, g_scr, *, t, c, nch, t_valid):
    dk = GLA_DK
    sup = c * nch
    levels = _gla_levels(c)
    split = sum_ref.dtype == BF16
    x = _dot(sm_ref[0], wg_ref[...], precision=HIGHEST) + bg_ref[...]
    g_all = _log_sigmoid(x) * (1.0 / GLA_GATE_NORM)
    if t_valid < t:
        rows_t = lax.broadcasted_iota(jnp.int32, g_all.shape, 0)
        g_all = jnp.where(rows_t < t_valid, g_all, 0.0)
    g_scr[...] = g_all

    lane = lax.broadcasted_iota(jnp.int32, (sup, LANES), 1)
    row = lax.broadcasted_iota(jnp.int32, (sup, LANES), 0)
    head0 = lane < dk
    rr = lax.broadcasted_iota(jnp.int32, (sup, sup), 0)
    cc = lax.broadcasted_iota(jnp.int32, (sup, sup), 1)
    mm = BF16 if split else F32
    prec = None if split else HIGHEST
    nw = nw_ref[...]

    def pair_trip(p, r0, s2t, result):
        ql = slice(p * LANES, (p + 1) * LANES)
        vl = slice(p * 2 * GLA_DV, (p + 1) * 2 * GLA_DV)
        q = q_ref[0, pl.ds(r0, sup), ql] * (dk ** -0.5)
        k = k_ref[0, pl.ds(r0, sup), ql]
        if t_valid < t:
            k = jnp.where(row < t_valid, k, 0.0)
        g = g_scr[pl.ds(r0, sup), ql]
        v = v_ref[0, pl.ds(r0, sup), vl]
        z = z_ref[0, pl.ds(r0, sup), vl]
        if split:
            g1 = g.astype(BF16)
            r1 = g - g1.astype(F32)
            g2 = r1.astype(BF16)
            g3 = (r1 - g2.astype(F32)).astype(BF16)
            xs = _dot(sum_ref[...], jnp.concatenate([g1, g2, g3], axis=1))
            xs = xs[:, :LANES] + xs[:, LANES:2 * LANES] + xs[:, 2 * LANES:]
        else:
            xs = _dot(sum_ref[...], g, precision=HIGHEST)
        yield
        b = xs[0:sup]
        to_end = xs[sup:2 * sup]

        qk = q * k
        att0 = jnp.where(rr == cc, jnp.sum(jnp.where(head0, qk, 0.0), axis=-1, keepdims=True), 0.0)
        att1 = jnp.where(rr == cc, jnp.sum(jnp.where(head0, 0.0, qk), axis=-1, keepdims=True), 0.0)
        for i, hs in enumerate(levels):
            e = jnp.exp(xs[(2 + i) * sup:(3 + i) * sup])
            upper = (row & hs) != 0
            qt = jnp.where(upper, q * e, 0.0)
            kl = jnp.where(upper, 0.0, k * e).astype(mm)
            sh = (2 * hs).bit_length() - 1
            same = (rr >> sh) == (cc >> sh)
            att0 = att0 + jnp.where(same, _dot_nt(jnp.where(head0, qt, 0.0).astype(mm), kl, prec), 0.0)
            att1 = att1 + jnp.where(same, _dot_nt(jnp.where(head0, 0.0, qt).astype(mm), kl, prec), 0.0)
            yield

        kt = k * jnp.exp(to_end)
        qe = q * jnp.exp(b)
        qe0, qe1 = jnp.where(head0, qe, 0.0), jnp.where(head0, 0.0, qe)
        kt0, kt1 = jnp.where(head0, kt, 0.0), jnp.where(head0, 0.0, kt)
        v0, v1 = v[:, :GLA_DV], v[:, GLA_DV:]
        rows = [slice(ch * c, (ch + 1) * c) for ch in range(nch)]
        upds = [_dot_tn(jnp.concatenate([v0[r], v1[r]], axis=0).astype(mm),
                        jnp.concatenate([kt0[r], kt1[r]], axis=0).astype(mm), prec) for r in rows]
        yield
        o_int = []
        for ch, r in enumerate(rows):
            lhs = jnp.concatenate([qe0[r], qe1[r]], axis=0).astype(mm)
            o_int.append(_dot_nt(lhs, s2t.astype(mm), prec))
            s2t = jnp.exp(b[ch * c + c - 1:ch * c + c, :]) * s2t + upds[ch]
        result[p] = s2t
        yield

        for h, (att, vh) in enumerate(((att0, v0), (att1, v1))):
            o = (jnp.concatenate([oi[h * c:(h + 1) * c, :] for oi in o_int], axis=0)
                 + _dot(att.astype(mm), vh.astype(mm), prec))
            yield
            on = o * lax.rsqrt(jnp.mean(o * o, axis=-1, keepdims=True) + EPS) * nw
            zh = z[:, h * GLA_DV:(h + 1) * GLA_DV]
            cols = slice(p * 2 * GLA_DV + h * GLA_DV, p * 2 * GLA_DV + (h + 1) * GLA_DV)
            o_ref[0, pl.ds(r0, sup), cols] = (on * _silu(zh)).astype(o_ref.dtype)

    def body(si, states):
        r0 = pl.multiple_of(si * sup, sup)
        result = [None] * GLA_GROUP
        trips = [pair_trip(p, r0, states[p], result) for p in range(GLA_GROUP)]
        while trips:
            trips = [g for g in trips if next(g, True) is None]
        return tuple(result)

    states = lax.fori_loop(0, t // sup, body, tuple(s0_ref[0, p] for p in range(GLA_GROUP)))
    for p in range(GLA_GROUP):
        s_ref[0, p] = states[p]


def _gla(proj, small, wg_pad, bg, nw, s0t, t_valid, out_dtype):
    bsz, t, _ = proj.shape
    c = min(LANES, t)
    nch = 1
    pairs = GLA_HEADS // 2
    gw = GLA_GROUP * LANES
    vw = GLA_GROUP * 2 * GLA_DV
    qb, kb = _MAIN_OFF["g_q"] // gw, _MAIN_OFF["g_k"] // gw
    vb, zb = _MAIN_OFF["g_v"] // vw, _MAIN_OFF["g_z"] // vw
    kern = functools.partial(_gla_kernel, t=t, c=c, nch=nch, t_valid=t_valid)
    sums = _gla_sum_matrices(c, nch)
    sums = jnp.asarray(sums, BF16 if (c * nch) % 16 == 0 else F32)
    return pl.pallas_call(
        kern,
        grid=(bsz, pairs // GLA_GROUP),
        in_specs=[pl.BlockSpec((1, t, gw), lambda b, p: (b, 0, qb + p)),
                  pl.BlockSpec((1, t, gw), lambda b, p: (b, 0, kb + p)),
                  pl.BlockSpec((1, t, vw), lambda b, p: (b, 0, vb + p)),
                  pl.BlockSpec((1, t, vw), lambda b, p: (b, 0, zb + p)),
                  pl.BlockSpec((1, t, LANES), lambda b, p: (b, 0, 0)),
                  pl.BlockSpec((LANES, gw), lambda b, p: (0, p)),
                  pl.BlockSpec((1, gw), lambda b, p: (0, p)),
                  pl.BlockSpec((1, GLA_DV), lambda b, p: (0, 0)),
                  pl.BlockSpec((1, GLA_GROUP, GLA_DV, LANES), lambda b, p: (b, p, 0, 0)),
                  pl.BlockSpec(sums.shape, lambda b, p: (0, 0))],
        out_specs=[pl.BlockSpec((1, t, vw), lambda b, p: (b, 0, p)),
                   pl.BlockSpec((1, GLA_GROUP, GLA_DV, LANES), lambda b, p: (b, p, 0, 0))],
        out_shape=[jax.ShapeDtypeStruct((bsz, t, GROUP_W), out_dtype),
                   jax.ShapeDtypeStruct((bsz, pairs, GLA_DV, LANES), F32)],
        scratch_shapes=[pltpu.VMEM((t, gw), F32)],
        compiler_params=_cparams(("parallel", "parallel")),
        name="gla",
    )(proj, proj, proj, proj, small, wg_pad, bg.reshape(1, -1), nw.reshape(1, -1), s0t, sums)


def _gla_state_to_pairs(s):
    bsz = s.shape[0]
    s = s.reshape(bsz, GLA_HEADS // 2, 2, GLA_DK, GLA_DV)
    return s.transpose(0, 1, 4, 2, 3).reshape(bsz, GLA_HEADS // 2, GLA_DV, 2 * GLA_DK)


def _gla_state_from_pairs(s):
    bsz = s.shape[0]
    s = s.reshape(bsz, GLA_HEADS // 2, GLA_DV, 2, GLA_DK)
    return s.transpose(0, 1, 3, 4, 2).reshape(bsz, GLA_HEADS, GLA_DK, GLA_DV)


def _gates_kernel(sm_ref, bias_ref, o_ref, *, t_valid):
    x = sm_ref[0] + bias_ref[...]
    lane = lax.broadcasted_iota(jnp.int32, x.shape, 1)
    is_f = (lane >= SM_F) & (lane < SM_F + ML_HEADS)
    out = jnp.where(is_f, _log_sigmoid(x), x)
    if t_valid is not None:
        row = lax.broadcasted_iota(jnp.int32, x.shape, 0)
        out = jnp.where(row < t_valid, out, jnp.where(is_f, 0.0, NEG))
    o_ref[0] = out


def _gates(small, bias_row, t_valid):
    bsz, t, _ = small.shape
    bt = t if t_valid < t else _pick(t, (1024, 512, 256, 128, 64, 32, 16, 8))
    return pl.pallas_call(
        functools.partial(_gates_kernel, t_valid=t_valid if t_valid < t else None),
        grid=(bsz, t // bt),
        in_specs=[pl.BlockSpec((1, bt, LANES), lambda b, i: (b, i, 0)),
                  pl.BlockSpec((1, LANES), lambda b, i: (0, 0))],
        out_specs=pl.BlockSpec((1, bt, LANES), lambda b, i: (b, i, 0)),
        out_shape=jax.ShapeDtypeStruct((bsz, t, LANES), F32),
        compiler_params=_cparams(("parallel", "parallel")),
        name="ml_gates",
    )(small, bias_row)


ML_PAIR = 2


def _mlstm_kernel(q_ref, k_ref, v_ref, og_ref, z_ref, gt_ref, nw_ref,
                  c0_ref, n0_ref, m0_ref, o_ref, c_ref, n_ref, m_ref, *, t, c):
    h0 = pl.program_id(1) * ML_PAIR
    hs = range(ML_PAIR)
    lane = lax.broadcasted_iota(jnp.int32, (c, LANES), 1)
    rr = lax.broadcasted_iota(jnp.int32, (c, c), 0)
    cc = lax.broadcasted_iota(jnp.int32, (c, c), 1)
    causal = rr >= cc
    nw = nw_ref[...]
    hsl = lambda j: slice(j * ML_DH, (j + 1) * ML_DH)

    def body(ci, carry):
        cms, ns, m_prevs = carry
        r0 = pl.multiple_of(ci * c, c)
        gt = gt_ref[0, pl.ds(r0, c), :]
        qs = [q_ref[0, pl.ds(r0, c), hsl(j)] * (ML_DH ** -0.5) for j in hs]
        ks = [k_ref[0, pl.ds(r0, c), hsl(j)] for j in hs]
        vs = [v_ref[0, pl.ds(r0, c), hsl(j)] for j in hs]
        s_qk = [_dot_nt(qs[j], ks[j], precision=HIGHEST) for j in hs]
        s_qc = [_dot_nt(qs[j], cms[j], precision=HIGHEST) for j in hs]

        d, inter, m_t, i_cols, fc_cols = [], [], [], [], []
        for j in hs:
            i_col = jnp.sum(jnp.where(lane == SM_I + h0 + j, gt, 0.0), axis=-1, keepdims=True)
            f_col = jnp.sum(jnp.where(lane == SM_F + h0 + j, gt, 0.0), axis=-1, keepdims=True)
            i_row = jnp.sum(jnp.where(rr == cc, i_col, 0.0), axis=0, keepdims=True)
            f_row = jnp.sum(jnp.where(rr == cc, f_col, 0.0), axis=0, keepdims=True)
            fc_col = jnp.sum(jnp.where(causal, f_row, 0.0), axis=-1, keepdims=True)
            fc_row = jnp.sum(jnp.where(rr <= cc, f_col, 0.0), axis=0, keepdims=True)
            dj = jnp.where(causal, fc_col - fc_row + i_row, NEG)
            d.append(dj)
            inter.append(fc_col + m_prevs[j])
            m_t.append(jnp.maximum(inter[j], jnp.max(dj, axis=-1, keepdims=True)))
            i_cols.append(i_col)
            fc_cols.append(fc_col)

        w_state = [jnp.exp(inter[j] - m_t[j]) for j in hs]
        qk = [s_qk[j] * jnp.exp(d[j] - m_t[j]) for j in hs]
        pv = [_dot(qk[j], vs[j], precision=HIGHEST) for j in hs]
        w_end, dec, m_new = [], [], []
        for j in hs:
            m_new.append(m_t[j][c - 1:c, :])
            f_end = fc_cols[j][c - 1:c, :]
            w_end.append(jnp.exp(f_end - fc_cols[j] + i_cols[j] - m_new[j]))
            dec.append(jnp.exp(f_end + m_prevs[j] - m_new[j]))
        upd = [_dot_tn(vs[j] * w_end[j], ks[j], precision=HIGHEST) for j in hs]

        c_new, n_new = [], []
        for j in hs:
            num = w_state[j] * s_qc[j] + pv[j]
            den = (w_state[j] * jnp.sum(qs[j] * ns[j], axis=-1, keepdims=True)
                   + jnp.sum(qk[j], axis=-1, keepdims=True))
            hout = num / jnp.maximum(jnp.abs(den), jnp.exp(-m_t[j]))
            c_new.append(dec[j] * cms[j] + upd[j])
            n_new.append(dec[j] * ns[j] + jnp.sum(ks[j] * w_end[j], axis=0, keepdims=True))
            o = _sigmoid(og_ref[0, pl.ds(r0, c), hsl(j)]) * hout
            on = o * lax.rsqrt(jnp.mean(o * o, axis=-1, keepdims=True) + EPS) * nw
            o_ref[0, pl.ds(r0, c), hsl(j)] = (on * _silu(z_ref[0, pl.ds(r0, c), hsl(j)])).astype(o_ref.dtype)
        return tuple(c_new), tuple(n_new), tuple(m_new)

    init = (tuple(c0_ref[0, j] for j in hs), tuple(n0_ref[0, j] for j in hs),
            tuple(m0_ref[0, j][:, :1] for j in hs))
    cms, ns, ms = lax.fori_loop(0, t // c, body, init)
    for j in hs:
        c_ref[0, j] = cms[j]
        n_ref[0, j] = ns[j]
        m_ref[0, j] = jnp.broadcast_to(ms[j], (1, LANES))


def _split3(x):
    a = x.astype(BF16)
    r = x - a.astype(F32)
    b = r.astype(BF16)
    return a, b, (r - b.astype(F32)).astype(BF16)


def _mlstm_wide_kernel(q_ref, k_ref, v_ref, og_ref, z_ref, gt_ref, nw_ref, c0_ref, n0_ref, m0_ref,
                       o_ref, c_ref, n_ref, m_ref, t_scr, g_scr, i_scr, fc_scr, *, t):
    c = LANES
    nc = t // c
    h0 = pl.program_id(1) * ML_PAIR
    hs = range(ML_PAIR)
    rr = lax.broadcasted_iota(jnp.int32, (c, c), 0)
    cc = lax.broadcasted_iota(jnp.int32, (c, c), 1)
    causal = rr >= cc
    nw = nw_ref[...]
    hsl = lambda j: slice(j * ML_DH, (j + 1) * ML_DH)
    ones_b = jnp.ones((c, ML_DH), BF16)
    tril_b = jnp.where(causal, 1.0, 0.0).astype(BF16)

    lane_t = lax.broadcasted_iota(jnp.int32, (c, LANES), 1)
    for ci in range(nc):
        rows = slice(ci * c, (ci + 1) * c)
        g = gt_ref[0, rows, :]
        g_scr[rows, :] = jnp.where(lane_t < SM_F, g, sum(_dot(tril_b, x) for x in _split3(g)))
    sel_r = lax.broadcasted_iota(jnp.int32, (LANES, 2 * LANES), 0)
    sel_c = lax.broadcasted_iota(jnp.int32, (LANES, 2 * LANES), 1)
    g_terms = _split3(g_scr[...])
    for j in hs:
        src = jnp.where(sel_c < LANES, SM_I + h0 + j, SM_F + h0 + j)
        sel = jnp.where(sel_r == src, 1.0, 0.0).astype(BF16)
        both = sum(_dot(g, sel) for g in g_terms)
        i_scr[j] = both[:, :LANES]
        fc_scr[j] = both[:, LANES:]

    def body(ci, carry):
        cms, ns, m_prevs = carry
        r0 = pl.multiple_of(ci * c, c)
        t_scr[...] = g_scr[pl.ds(r0, c), :].T
        qbs = [(q_ref[0, pl.ds(r0, c), hsl(j)] * (ML_DH ** -0.5)).astype(BF16) for j in hs]
        kbs = [k_ref[0, pl.ds(r0, c), hsl(j)].astype(BF16) for j in hs]
        vs = [v_ref[0, pl.ds(r0, c), hsl(j)] for j in hs]
        s_qk = [_dot_nt(qbs[j], kbs[j]) for j in hs]
        s_qc = [_dot_nt(qbs[j], jnp.concatenate([cms[j], jnp.broadcast_to(ns[j], (c, ML_DH))],
                                                axis=0).astype(BF16)) for j in hs]

        d, inter, m_t, fcs = [], [], [], []
        for j in hs:
            fc = fc_scr[j, pl.ds(r0, c), :]
            i_row = t_scr[pl.ds(SM_I + h0 + j, 1), :]
            fc_row = t_scr[pl.ds(SM_F + h0 + j, 1), :]
            dj = jnp.where(causal, fc - fc_row + i_row, NEG)
            d.append(dj)
            inter.append(fc + m_prevs[j])
            m_t.append(jnp.maximum(inter[j], jnp.max(dj, axis=-1, keepdims=True)))
            fcs.append(fc)

        w_state = [jnp.exp(inter[j] - m_t[j]) for j in hs]
        qk = [(s_qk[j] * jnp.exp(d[j] - m_t[j])).astype(BF16) for j in hs]
        pv = [_dot(qk[j], jnp.concatenate([vs[j].astype(BF16), ones_b], axis=1)) for j in hs]
        w_end, dec, m_new = [], [], []
        for j in hs:
            m_new.append(m_t[j][c - 1:c, :1])
            f_end = fcs[j][c - 1:c, :]
            w_end.append(jnp.exp(f_end - fcs[j] + i_scr[j, pl.ds(r0, c), :] - m_new[j]))
            dec.append(jnp.exp(f_end[:, :1] + m_prevs[j] - m_new[j]))
        upd = [_dot_tn(jnp.concatenate([vs[j] * w_end[j], w_end[j]], axis=1).astype(BF16), kbs[j])
               for j in hs]

        c_new, n_new = [], []
        hout = []
        for j in hs:
            num = w_state[j] * s_qc[j][:, :ML_DH] + pv[j][:, :ML_DH]
            den = w_state[j] * s_qc[j][:, ML_DH:] + pv[j][:, ML_DH:]
            hout.append(_sigmoid(og_ref[0, pl.ds(r0, c), hsl(j)]) * num
                        / jnp.maximum(jnp.abs(den), jnp.exp(-m_t[j])))
            c_new.append(dec[j] * cms[j] + upd[j][:ML_DH, :])
            n_new.append(dec[j] * ns[j] + upd[j][ML_DH:ML_DH + 1, :])
        msq = [_dot((o * o).astype(BF16), ones_b) * (1.0 / ML_DH) for o in hout]
        for j in hs:
            on = hout[j] * lax.rsqrt(msq[j] + EPS) * nw
            o_ref[0, pl.ds(r0, c), hsl(j)] = (on * _silu(z_ref[0, pl.ds(r0, c), hsl(j)])).astype(o_ref.dtype)
        return tuple(c_new), tuple(n_new), tuple(m_new)

    init = (tuple(c0_ref[0, j] for j in hs), tuple(n0_ref[0, j] for j in hs),
            tuple(m0_ref[0, j][:, :1] for j in hs))
    cms, ns, ms = lax.fori_loop(0, nc, body, init)
    for j in hs:
        c_ref[0, j] = cms[j]
        n_ref[0, j] = ns[j]
        m_ref[0, j] = jnp.broadcast_to(ms[j], (1, LANES))


def _mlstm(proj, gates, nw, c0, n0, m0, out_dtype):
    bsz, t, _ = proj.shape
    c = min(128, t)
    w = ML_PAIR * ML_DH
    col = lambda name: _MAIN_OFF[name] // w
    spec = lambda name: pl.BlockSpec((1, t, w), lambda b, p, o=col(name): (b, 0, o + p))
    st = lambda rows: pl.BlockSpec((1, ML_PAIR, rows, ML_DH), lambda b, p: (b, p, 0, 0))
    if c == LANES:
        kern = functools.partial(_mlstm_wide_kernel, t=t)
        scratch = ([pltpu.VMEM((LANES, LANES), F32), pltpu.VMEM((t, LANES), F32)]
                   + [pltpu.VMEM((ML_PAIR, t, LANES), F32)] * 2)
    else:
        kern = functools.partial(_mlstm_kernel, t=t, c=c)
        scratch = []
    return pl.pallas_call(
        kern,
        grid=(bsz, ML_HEADS // ML_PAIR),
        in_specs=[spec("m_q"), spec("m_k"), spec("m_v"), spec("m_o"), spec("m_z"),
                  pl.BlockSpec((1, t, LANES), lambda b, p: (b, 0, 0)),
                  pl.BlockSpec((1, ML_DH), lambda b, p: (0, 0)),
                  st(ML_DH), st(1), st(1)],
        out_specs=[pl.BlockSpec((1, t, w), lambda b, p: (b, 0, p)), st(ML_DH), st(1), st(1)],
        out_shape=[jax.ShapeDtypeStruct((bsz, t, GROUP_W), out_dtype),
                   jax.ShapeDtypeStruct((bsz, ML_HEADS, ML_DH, ML_DH), F32),
                   jax.ShapeDtypeStruct((bsz, ML_HEADS, 1, ML_DH), F32),
                   jax.ShapeDtypeStruct((bsz, ML_HEADS, 1, LANES), F32)],
        scratch_shapes=scratch,
        compiler_params=_cparams(("parallel", "parallel")),
        name="mlstm",
    )(proj, proj, proj, proj, proj, gates, nw.reshape(1, -1), c0, n0, m0)


def _gmlp_kernel(u_ref, v_ref, z_ref, lw_ref, lb_ref, ws_ref, bs_ref, o_ref, *vn_refs, l, nl):
    rr = lax.broadcasted_iota(jnp.int32, (l, l), 0)
    cc = lax.broadcasted_iota(jnp.int32, (l, l), 1)
    wts = [jnp.where(rr >= cc, ws_ref[g], 0.0) for g in range(GM_GROUPS)]
    if l >= GM_CHUNK:
        wts = [w.astype(BF16) for w in wts]
    for ch in range(nl):
        rows = slice(ch * l, (ch + 1) * l)
        gv = _gelu(v_ref[0, rows, :])
        mu = jnp.mean(gv, axis=-1, keepdims=True)
        xc = gv - mu
        vn = xc * lax.rsqrt(jnp.mean(xc * xc, axis=-1, keepdims=True) + EPS) * lw_ref[...] + lb_ref[...]
        if vn_refs:
            vn_refs[0][0, rows, :] = vn
        for g in range(GM_GROUPS):
            sl = slice(g * GM_CH, (g + 1) * GM_CH)
            vg = vn[:, sl]
            if l >= GM_CHUNK:
                s = _dot(wts[g], vg.astype(BF16))
            else:
                s = jnp.zeros((l, GM_CH), F32)
                for r in range(l):
                    s = s + wts[g][:, r:r + 1] * vg[r:r + 1, :]
            s = s + bs_ref[:, g:g + 1]
            o_ref[0, rows, sl] = (_gelu(u_ref[0, rows, sl]) * s * _silu(z_ref[0, rows, sl])).astype(o_ref.dtype)


def _gmlp(proj, lw, lb, ws, bs_t, out_dtype, want_vn):
    bsz, t, _ = proj.shape
    l = min(t, GM_CHUNK)
    nl = _pick(t // l, (4, 2, 1))
    col = lambda name: _MAIN_OFF[name] // GROUP_W
    spec = lambda name: pl.BlockSpec((1, l * nl, GROUP_W), lambda b, i, o=col(name): (b, i, o))
    out = pl.BlockSpec((1, l * nl, GROUP_W), lambda b, i: (b, i, 0))
    out_specs = [out] + ([out] if want_vn else [])
    out_shape = ([jax.ShapeDtypeStruct((bsz, t, GROUP_W), out_dtype)]
                 + ([jax.ShapeDtypeStruct((bsz, t, GROUP_W), F32)] if want_vn else []))
    res = pl.pallas_call(
        functools.partial(_gmlp_kernel, l=l, nl=nl),
        grid=(bsz, t // (l * nl)),
        in_specs=[spec("c_u"), spec("c_v"), spec("c_z"),
                  pl.BlockSpec((1, GROUP_W), lambda b, i: (0, 0)),
                  pl.BlockSpec((1, GROUP_W), lambda b, i: (0, 0)),
                  pl.BlockSpec((GM_GROUPS, l, l), lambda b, i: (0, 0, 0)),
                  pl.BlockSpec((l, GM_GROUPS), lambda b, i: (0, 0))],
        out_specs=out_specs,
        out_shape=out_shape,
        compiler_params=_cparams(("parallel", "parallel")),
        name="gmlp",
    )(proj, proj, proj, lw.reshape(1, -1), lb.reshape(1, -1), ws, bs_t)
    return (res[0], res[1]) if want_vn else (res[0], None)


MB_GROUP = 2
assert MB_HEADS % MB_GROUP == 0


def _moba_prompt_kernel(q_ref, k_ref, v_ref, z_ref, o_ref, kb_scr, vb_scr, km_scr, *, t):
    blk = MB_BLOCK
    nb = t // blk
    scale = MB_DH ** -0.5
    nbp = -(-nb // SUBLANES) * SUBLANES
    blk_row = lax.broadcasted_iota(jnp.int32, (nbp, blk), 0)
    rr = lax.broadcasted_iota(jnp.int32, (blk, blk), 0)
    cc = lax.broadcasted_iota(jnp.int32, (blk, blk), 1)
    log2_scale = float(scale * np.log2(np.e))
    lane_tiles = lambda xs: [x[:, i:i + LANES] for x in xs for i in range(0, blk, LANES)]

    def head_steps(hh):
        h = pl.program_id(1) * MB_GROUP + hh
        cols = slice(hh * MB_DH, (hh + 1) * MB_DH)
        km_scr[hh] = jnp.zeros((LANES, MB_DH), F32)
        for n in range(nb):
            rows = pl.ds(n * blk * MB_HEADS + h, blk, stride=MB_HEADS)
            kn = k_ref[rows, :]
            kb_scr[hh, n * blk:(n + 1) * blk, :] = kn.astype(BF16)
            vb_scr[hh, n * blk:(n + 1) * blk, :] = v_ref[rows, :].astype(BF16)
            km_scr[hh, n:n + 1, :] = jnp.mean(kn, axis=0, keepdims=True)
        kmean = km_scr[hh]
        yield
        for qi in range(nb):
            rows = slice(qi * blk, (qi + 1) * blk)
            q = q_ref[0, rows, cols]
            qb = q.astype(BF16)
            bias = None
            if qi > MB_TOPK:
                gate = _dot_nt(kmean, q, precision=HIGHEST)[0:nbp, :]
                yield
                cnt = jnp.zeros((nbp, blk), F32)
                for m in range(qi):
                    gm = gate[m:m + 1, :]
                    beats = (gm > gate) | ((gm == gate) & (blk_row > m))
                    cnt = cnt + jnp.where(beats, 1.0, 0.0)
                bias_t = jnp.where(cnt < MB_TOPK, 0.0, NEG)
                bias = jnp.concatenate([bias_t, jnp.zeros((LANES - nbp, blk), F32)], axis=0).T
            ss = []
            for j in range(qi + 1):
                s = _dot_nt(qb, kb_scr[hh, j * blk:(j + 1) * blk, :]) * log2_scale
                if j == qi:
                    s = jnp.where(cc <= rr, s, NEG)
                elif bias is not None:
                    s = s + bias[:, j:j + 1]
                ss.append(s)
            yield
            m_i = jnp.max(functools.reduce(jnp.maximum, lane_tiles(ss)), axis=-1, keepdims=True)
            ps = [jnp.exp2(s - m_i) for s in ss]
            l_i = jnp.sum(functools.reduce(jnp.add, lane_tiles(ps)), axis=-1, keepdims=True)
            acc = _dot(ps[0].astype(BF16), vb_scr[hh, 0:blk, :])
            for j in range(1, qi + 1):
                acc = acc + _dot(ps[j].astype(BF16), vb_scr[hh, j * blk:(j + 1) * blk, :])
            yield
            o_ref[0, rows, cols] = ((acc / l_i) * _silu(z_ref[0, rows, cols])).astype(o_ref.dtype)

    heads = [head_steps(hh) for hh in range(MB_GROUP)]
    while heads:
        heads = [g for g in heads if next(g, True) is None]


def _moba_prompt(q_rope, k4, v4, layer, proj, out_dtype):
    bsz, t, _ = proj.shape
    assert t % MB_BLOCK == 0 and t // MB_BLOCK <= LANES
    gw = MB_GROUP * MB_DH
    zcol = _MAIN_OFF["a_z"] // gw
    hd = pl.BlockSpec((1, t, gw), lambda b, h: (b, 0, h))
    kv = pl.BlockSpec((None, None, t * MB_HEADS, MB_DH), lambda b, h: (layer, b, 0, 0))
    return pl.pallas_call(
        functools.partial(_moba_prompt_kernel, t=t),
        grid=(bsz, MB_HEADS // MB_GROUP),
        in_specs=[hd, kv, kv, pl.BlockSpec((1, t, gw), lambda b, h: (b, 0, zcol + h))],
        out_specs=hd,
        out_shape=jax.ShapeDtypeStruct((bsz, t, GROUP_W), out_dtype),
        scratch_shapes=[pltpu.VMEM((MB_GROUP, t, MB_DH), BF16), pltpu.VMEM((MB_GROUP, t, MB_DH), BF16),
                        pltpu.VMEM((MB_GROUP, LANES, MB_DH), F32)],
        compiler_params=_cparams(("parallel", "arbitrary"), VMEM_LIMIT_TALL),
        name="moba_prompt",
    )(q_rope, k4, v4, proj)


QROWS = MB_HEADS * SAMPLE_T
ST_M, ST_L, ST_G = 0, 1, 2


def _moba_past_kernel(pt_ref, q_ref, bias_ref, *refs, nbs):
    del pt_ref
    npg = 2 * nbs
    k_refs, v_refs = refs[:npg], refs[npg:2 * npg]
    o_ref, st_ref = refs[2 * npg:]
    scale = MB_DH ** -0.5
    page = k_refs[0].shape[0] // MB_HEADS
    q = q_ref[0]
    qb = q.astype(BF16)
    bias = bias_ref[...]
    lane = lax.broadcasted_iota(jnp.int32, (QROWS, LANES), 1)
    ss = [_dot_nt(qb, r[...].astype(BF16)) * scale + bias for r in k_refs]
    ps, ms, ls = [], [], []
    for n in range(nbs):
        s0, s1 = ss[2 * n], ss[2 * n + 1]
        m = jnp.maximum(jnp.max(s0, axis=-1, keepdims=True), jnp.max(s1, axis=-1, keepdims=True))
        p0, p1 = jnp.exp(s0 - m), jnp.exp(s1 - m)
        ps += [p0.astype(BF16), p1.astype(BF16)]
        ms.append(m)
        ls.append(jnp.sum(p0, axis=-1, keepdims=True) + jnp.sum(p1, axis=-1, keepdims=True))
    for n in range(nbs):
        o_ref[n] = (_dot(ps[2 * n], v_refs[2 * n][...].astype(BF16))
                    + _dot(ps[2 * n + 1], v_refs[2 * n + 1][...].astype(BF16)))
        ksum = (jnp.sum(k_refs[2 * n][...].reshape(page, MB_HEADS, MB_DH), axis=0)
                + jnp.sum(k_refs[2 * n + 1][...].reshape(page, MB_HEADS, MB_DH), axis=0))
        ksum_rows = jnp.concatenate(
            [jnp.broadcast_to(ksum[h:h + 1, :], (SAMPLE_T, MB_DH)) for h in range(MB_HEADS)], axis=0)
        gate = jnp.sum(q * ksum_rows, axis=-1, keepdims=True) * (1.0 / MB_BLOCK)
        st_ref[n] = jnp.where(lane == ST_M, ms[n], jnp.where(lane == ST_L, ls[n], gate))


def _moba_past(layer, q_rows, cache_k, cache_v, page_table):
    db = q_rows.shape[0]
    rows = cache_k.shape[2]
    assert 2 * rows == MB_BLOCK * MB_HEADS
    nb = page_table.shape[1] // 2
    nbs = _pick(nb, (8, 4, 2, 1))
    key_head = np.arange(rows) % MB_HEADS
    row_head = np.arange(QROWS) // SAMPLE_T
    bias = jnp.asarray(np.where(key_head[None, :] == row_head[:, None], 0.0, NEG).astype(np.float32))
    pg = lambda i: pl.BlockSpec((None, None, rows, MB_DH),
                                lambda b, n, pt, i=i: (layer, pt[b, 2 * nbs * n + i], 0, 0))
    pages = [pg(i) for i in range(2 * nbs)]
    part = pl.BlockSpec((None, nbs, QROWS, MB_DH), lambda b, n, pt: (b, n, 0, 0))
    shape = jax.ShapeDtypeStruct((db, nb, QROWS, MB_DH), F32)
    grid_spec = pltpu.PrefetchScalarGridSpec(
        num_scalar_prefetch=1,
        grid=(db, nb // nbs),
        in_specs=[pl.BlockSpec((1, QROWS, MB_DH), lambda b, n, pt: (b, 0, 0)),
                  pl.BlockSpec((QROWS, rows), lambda b, n, pt: (0, 0))] + pages + pages,
        out_specs=[part, part],
    )
    return pl.pallas_call(
        functools.partial(_moba_past_kernel, nbs=nbs),
        grid_spec=grid_spec,
        out_shape=[shape, shape],
        compiler_params=_cparams(("parallel", "parallel")),
        name="moba_past",
    )(page_table, q_rows, bias, *([cache_k] * (2 * nbs)), *([cache_v] * (2 * nbs)))


def _moba_merge_kernel(op_ref, st_ref, q_ref, k_ref, v_ref, z_ref, o_ref, *, nb, t_valid):
    scale = MB_DH ** -0.5
    g = st_ref[0, :, :, ST_G:ST_G + 1]
    m = st_ref[0, :, :, ST_M:ST_M + 1]
    l = st_ref[0, :, :, ST_L:ST_L + 1]
    nidx = lax.broadcasted_iota(jnp.int32, g.shape, 0)
    sel = jnp.zeros(g.shape, jnp.bool_)
    gm = g
    for _ in range(min(MB_TOPK, nb)):
        mx = jnp.max(gm, axis=0, keepdims=True)
        first = jnp.min(jnp.where(gm == mx, nidx, nb), axis=0, keepdims=True)
        pick = nidx == first
        sel = sel | pick
        gm = jnp.where(pick, -jnp.inf, gm)

    rr = lax.broadcasted_iota(jnp.int32, (SAMPLE_T, SAMPLE_T), 0)
    cc = lax.broadcasted_iota(jnp.int32, (SAMPLE_T, SAMPLE_T), 1)
    mo, lo, oo = [], [], []
    for h in range(MB_HEADS):
        sl = slice(h * MB_DH, (h + 1) * MB_DH)
        qh = q_ref[0, h * SAMPLE_T:(h + 1) * SAMPLE_T, :]
        s = _dot_nt(qh.astype(BF16), k_ref[0, :, sl].astype(BF16)) * scale
        s = jnp.where((cc <= rr) & (cc < t_valid), s, NEG)
        mh = jnp.max(s, axis=-1, keepdims=True)
        p = jnp.exp(s - mh)
        mo.append(mh)
        lo.append(jnp.sum(p, axis=-1, keepdims=True))
        oo.append(_dot(p.astype(BF16), v_ref[0, :, sl].astype(BF16)))
    m_own, l_own, o_own = (jnp.concatenate(x, axis=0) for x in (mo, lo, oo))

    m_tot = jnp.maximum(jnp.max(jnp.where(sel, m, NEG), axis=0), m_own)
    w = jnp.where(sel, jnp.exp(m - m_tot[None]), 0.0)
    w_own = jnp.exp(m_own - m_tot)
    den = jnp.sum(w * l, axis=0) + w_own * l_own
    acc = w_own * o_own
    for n in range(nb):
        acc = acc + w[n] * op_ref[0, n]
    o = acc / den
    for h in range(MB_HEADS):
        sl = slice(h * MB_DH, (h + 1) * MB_DH)
        o_ref[0, :, sl] = o[h * SAMPLE_T:(h + 1) * SAMPLE_T, :] * _silu(z_ref[0, :, sl])


def _moba_merge(o_part, stats, q_rows, k_rope, proj, t_valid):
    db, nb = o_part.shape[:2]
    vcol, zcol = _MAIN_OFF["a_v"] // GROUP_W, _MAIN_OFF["a_z"] // GROUP_W
    part = pl.BlockSpec((1, nb, QROWS, MB_DH), lambda b: (b, 0, 0, 0))
    row = pl.BlockSpec((1, SAMPLE_T, GROUP_W), lambda b: (b, 0, 0))
    return pl.pallas_call(
        functools.partial(_moba_merge_kernel, nb=nb, t_valid=t_valid),
        grid=(db,),
        in_specs=[part, part, pl.BlockSpec((1, QROWS, MB_DH), lambda b: (b, 0, 0)), row,
                  pl.BlockSpec((1, SAMPLE_T, GROUP_W), lambda b: (b, 0, vcol)),
                  pl.BlockSpec((1, SAMPLE_T, GROUP_W), lambda b: (b, 0, zcol))],
        out_specs=row,
        out_shape=jax.ShapeDtypeStruct((db, SAMPLE_T, GROUP_W), F32),
        compiler_params=_cparams(("parallel",)),
        name="moba_merge",
    )(o_part, stats, q_rows, k_rope, proj, proj)


def _layer_weights(gla_w_gate_l, ml_b_i_l, ml_b_f_l):
    wg_pad = jnp.concatenate([gla_w_gate_l, jnp.zeros((LANES - GLA_GATE_RANK, gla_w_gate_l.shape[1]), F32)],
                             axis=0)
    bias_row = jnp.concatenate([jnp.zeros((SM_I,), F32), ml_b_i_l, ml_b_f_l,
                                jnp.zeros((LANES - SM_F - ML_HEADS,), F32)]).reshape(1, LANES)
    return wg_pad, bias_row


def _mixers(proj, small, bsz, t, t_valid, lw, rope_tables, gla_s0, ml_state, out_dtype, want_vn):
    (wg_pad, bias_row, gla_b_gate, gla_norm_w, ml_norm_w, gm_ln_w, gm_ln_b, gm_ws, gm_bs_t) = lw
    proj = proj.reshape(bsz, t, N_MAIN)
    small = small.reshape(bsz, t, LANES)

    out_a, gla_s = _gla(proj, small, wg_pad, gla_b_gate, gla_norm_w, gla_s0, t_valid, out_dtype)

    gates = _gates(small, bias_row, t_valid)
    out_b, ml_c, ml_n, ml_m = _mlstm(proj, gates, ml_norm_w, *ml_state, out_dtype)

    out_c, vn = _gmlp(proj, gm_ln_w, gm_ln_b, gm_ws, gm_bs_t, out_dtype, want_vn)

    q_rope = _rope(proj, _MAIN_OFF["a_q"], rope_tables)
    return proj, (out_a, out_b, out_c), q_rope, gla_s, (ml_c, ml_n, ml_m[..., :1]), vn


def kernel(x_prompt, x_sample, cache_k, cache_v, page_table, state_gla, state_mlstm_C, state_mlstm_n,
           state_mlstm_m, norm_w, w_in, gla_w_gate, gla_b_gate, gla_norm_w, ml_b_i, ml_b_f, ml_norm_w,
           gm_ln_w, gm_ln_b, gm_w_s, gm_b_s, w_out, final_norm_w):
    bp, tp, d = x_prompt.shape
    db, ts, _ = x_sample.shape
    depth = w_in.shape[0]
    page = cache_k.shape[2]
    past_len = page_table.shape[1] * page
    assert w_out.shape[1] == 4 * GROUP_W and ts <= SAMPLE_T
    assert past_len % MB_BLOCK == 0 and tp % MB_BLOCK == 0

    tables_p = _rope_tables(jnp.arange(tp, dtype=jnp.int32))
    tables_s = _rope_tables(past_len + jnp.arange(SAMPLE_T, dtype=jnp.int32))
    cache_k = cache_k.reshape(depth, cache_k.shape[1], page * MB_HEADS, MB_DH)
    cache_v = cache_v.reshape(depth, cache_v.shape[1], page * MB_HEADS, MB_DH)

    yp = x_prompt.reshape(bp * tp, d)
    ys = jnp.pad(x_sample, ((0, 0), (0, SAMPLE_T - ts), (0, 0))).reshape(db * SAMPLE_T, d)
    dt_p = BF16
    dt_s = F32

    zero_gla = jnp.zeros((bp, GLA_HEADS // 2, GLA_DV, LANES), F32)
    zero_ml = (jnp.zeros((bp, ML_HEADS, ML_DH, ML_DH), F32), jnp.zeros((bp, ML_HEADS, 1, ML_DH), F32),
               jnp.zeros((bp, ML_HEADS, 1, LANES), F32))

    outs = {n: [] for n in ("ks", "vs", "gp", "gs", "cp", "cs", "np", "ns", "mp", "ms", "vv")}
    lp = min(tp, GM_CHUNK)
    w_t = jnp.swapaxes(w_in, 1, 2)
    w_main = _wprep(w_t)
    w_small = _wsmall(w_t)
    w_out4 = w_out.reshape(depth, 4, GROUP_W, d)
    kv_stacks = None
    for l in range(depth):
        wg_pad, bias_row = _layer_weights(gla_w_gate[l], ml_b_i[l], ml_b_f[l])
        common = (wg_pad, bias_row, gla_b_gate[l], gla_norm_w[l], ml_norm_w[l], gm_ln_w[l], gm_ln_b[l])
        h_p, small_p = _rms_small(yp, norm_w[l], w_small, l)
        h_s, small_s = _rms_small(ys, norm_w[l], w_small, l)
        proj_p, proj_s = _in_proj(h_p, h_s, w_main, l)

        lw = common + (gm_w_s[l][:, :lp, :lp], gm_b_s[l][:, :lp].T)
        proj, mix, q_rope, gla_s, ml_s, _ = _mixers(
            proj_p, small_p, bp, tp, tp, lw, tables_p, zero_gla, zero_ml, dt_p, want_vn=False)
        kv_stacks = _kv_heads(proj, tables_p, l, depth, kv_stacks)
        out_d = _moba_prompt(q_rope, kv_stacks[0], kv_stacks[1], l, proj, dt_p)
        yp = _out_proj([a.reshape(bp * tp, GROUP_W) for a in mix + (out_d,)], w_out4, l, yp)
        outs["gp"].append(_gla_state_from_pairs(gla_s))
        outs["cp"].append(ml_s[0])
        outs["np"].append(ml_s[1][:, :, 0, :])
        outs["mp"].append(ml_s[2][:, :, 0, 0])

        lw = common + (gm_w_s[l][:, :SAMPLE_T, :SAMPLE_T], gm_b_s[l][:, :SAMPLE_T].T)
        ml_state = (state_mlstm_C[l], state_mlstm_n[l][:, :, None, :],
                    jnp.broadcast_to(state_mlstm_m[l][:, :, None, None], (db, ML_HEADS, 1, LANES)))
        proj, mix, q_rope, gla_s, ml_s, vn = _mixers(
            proj_s, small_s, db, SAMPLE_T, ts, lw, tables_s, _gla_state_to_pairs(state_gla[l]), ml_state,
            dt_s, want_vn=True)
        k_rope = _rope(proj, _MAIN_OFF["a_k"], tables_s)
        v_new = proj[:, :, _MAIN_OFF["a_v"]:_MAIN_OFF["a_v"] + GROUP_W]
        q_rows = (q_rope.reshape(db, SAMPLE_T, MB_HEADS, MB_DH).transpose(0, 2, 1, 3)
                  .reshape(db, QROWS, MB_DH))
        o_part, stats = _moba_past(l, q_rows, cache_k, cache_v, page_table)
        out_d = _moba_merge(o_part, stats, q_rows, k_rope, proj, ts)
        ys = _out_proj([a.reshape(db * SAMPLE_T, GROUP_W) for a in mix + (out_d,)], w_out4, l, ys)
        outs["ks"].append(k_rope[:, :ts].reshape(db, ts, MB_HEADS, MB_DH))
        outs["vs"].append(v_new[:, :ts].reshape(db, ts, MB_HEADS, MB_DH))
        outs["gs"].append(_gla_state_from_pairs(gla_s))
        outs["cs"].append(ml_s[0])
        outs["ns"].append(ml_s[1][:, :, 0, :])
        outs["ms"].append(ml_s[2][:, :, 0, 0])
        outs["vv"].append(vn[:, :ts])

    y_prompt = _rms(yp, final_norm_w).reshape(bp, tp, d)
    y_sample = _rms(ys, final_norm_w).reshape(db, SAMPLE_T, d)[:, :ts]
    st = jnp.stack
    k_prompt = kv_stacks[0].reshape(depth, bp, tp, MB_HEADS, MB_DH)
    v_prompt = kv_stacks[1].reshape(depth, bp, tp, MB_HEADS, MB_DH)
    return (y_prompt, y_sample, k_prompt, v_prompt, st(outs["ks"]), st(outs["vs"]),
            st(outs["gp"]), st(outs["gs"]), st(outs["cp"]), st(outs["cs"]), st(outs["np"]), st(outs["ns"]),
            st(outs["mp"]), st(outs["ms"]), st(outs["vv"]))
```

```python
import functools

import numpy as np
import jax
import jax.numpy as jnp
from jax import lax
from jax.experimental import pallas as pl
from jax.experimental.pallas import tpu as pltpu

F32 = jnp.float32
BF16 = jnp.bfloat16
HIGHEST = lax.Precision.HIGHEST

GROUP_W = 1024
GLA_HEADS, GLA_DK, GLA_DV = 8, 64, 128
GLA_GATE_RANK, GLA_GATE_NORM = 16, 16.0
ML_HEADS, ML_DH = 8, 128
GM_GROUPS, GM_CH, GM_CHUNK = 8, 128, 128
MB_HEADS, MB_DH, MB_BLOCK, MB_TOPK = 8, 128, 256, 3
ROT_DIM, ROPE_THETA = 32, 500000.0
EPS = 1e-6

LANES = 128
SUBLANES = 8
VMEM_LIMIT = 56 * 1024 * 1024
VMEM_LIMIT_TALL = 60 * 1024 * 1024

NEG = -1e30
SAMPLE_T = SUBLANES

_MAIN_ORDER = ("g_q", "g_k", "g_v", "g_z", "m_q", "m_k", "m_v", "m_o", "m_z",
               "c_u", "c_v", "c_z", "a_q", "a_k", "a_v", "a_z")
_SPLIT_NAMES = ("g_q", "g_k", "g_v", "g_lr", "g_z", "m_q", "m_k", "m_v", "m_i", "m_f", "m_o", "m_z",
                "c_u", "c_v", "c_z", "a_q", "a_k", "a_v", "a_z")
_SPLIT_W = (GLA_HEADS * GLA_DK, GLA_HEADS * GLA_DK, GLA_HEADS * GLA_DV, GLA_GATE_RANK, GROUP_W,
            GROUP_W, GROUP_W, GROUP_W, ML_HEADS, ML_HEADS, GROUP_W, GROUP_W,
            GROUP_W, GROUP_W, GROUP_W, GROUP_W, GROUP_W, GROUP_W, GROUP_W)
_SRC_OFF = dict(zip(_SPLIT_NAMES, np.concatenate([[0], np.cumsum(_SPLIT_W)[:-1]]).tolist()))
_SRC_W = dict(zip(_SPLIT_NAMES, _SPLIT_W))
_MAIN_OFF = {}
_off = 0
for _n in _MAIN_ORDER:
    _MAIN_OFF[_n] = _off
    _off += _SRC_W[_n]
N_MAIN = _off
SM_LR, SM_I, SM_F = 0, GLA_GATE_RANK, GLA_GATE_RANK + ML_HEADS


def _cparams(sem, vmem_limit=VMEM_LIMIT):
    return pltpu.CompilerParams(dimension_semantics=sem, vmem_limit_bytes=vmem_limit)


def _pick(n, cands):
    for c in cands:
        if n % c == 0:
            return c
    return n


def _silu(x):
    return x / (1.0 + jnp.exp(-x))


def _sigmoid(x):
    return 1.0 / (1.0 + jnp.exp(-x))


def _log_sigmoid(x):
    return jnp.minimum(x, 0.0) - jnp.log(1.0 + jnp.exp(-jnp.abs(x)))


def _gelu(x):
    c = np.sqrt(2.0 / np.pi).astype(np.float32)
    return 0.5 * x * (1.0 + jnp.tanh(c * (x + 0.044715 * (x * x * x))))


def _dot_nt(a, b, precision=None):
    return lax.dot_general(a, b, (((1,), (1,)), ((), ())), preferred_element_type=F32, precision=precision)


def _dot_tn(a, b, precision=None):
    return lax.dot_general(a, b, (((0,), (0,)), ((), ())), preferred_element_type=F32, precision=precision)


def _dot(a, b, precision=None):
    return jnp.dot(a, b, preferred_element_type=F32, precision=precision)


def _rms_small_kernel(x_ref, nw_ref, ws_ref, h_ref, sm_ref):
    x = x_ref[...]
    ms = jnp.mean(x * x, axis=-1, keepdims=True)
    hb = (x * lax.rsqrt(ms + EPS) * nw_ref[...]).astype(BF16)
    h_ref[...] = hb
    sm_ref[...] = _dot(hb, ws_ref[...])


def _rms_small(x, nw, w_small, layer):
    m, d = x.shape
    bm = _pick(m, (512, 256, 128, 64, 32, 16, 8))
    return pl.pallas_call(
        _rms_small_kernel,
        grid=(m // bm,),
        in_specs=[pl.BlockSpec((bm, d), lambda i: (i, 0)),
                  pl.BlockSpec((1, d), lambda i: (0, 0)),
                  pl.BlockSpec((None, d, LANES), lambda i: (layer, 0, 0))],
        out_specs=[pl.BlockSpec((bm, d), lambda i: (i, 0)),
                   pl.BlockSpec((bm, LANES), lambda i: (i, 0))],
        out_shape=[jax.ShapeDtypeStruct((m, d), BF16), jax.ShapeDtypeStruct((m, LANES), F32)],
        compiler_params=_cparams(("parallel",)),
        name="rms_small",
    )(x, nw.reshape(1, d), w_small)


def _rms_kernel(x_ref, nw_ref, o_ref):
    x = x_ref[...]
    ms = jnp.mean(x * x, axis=-1, keepdims=True)
    o_ref[...] = x * lax.rsqrt(ms + EPS) * nw_ref[...]


def _rms(x, nw):
    m, d = x.shape
    bm = _pick(m, (512, 256, 128, 64, 32, 16, 8))
    return pl.pallas_call(
        _rms_kernel,
        grid=(m // bm,),
        in_specs=[pl.BlockSpec((bm, d), lambda i: (i, 0)), pl.BlockSpec((1, d), lambda i: (0, 0))],
        out_specs=pl.BlockSpec((bm, d), lambda i: (i, 0)),
        out_shape=jax.ShapeDtypeStruct((m, d), F32),
        compiler_params=_cparams(("parallel",)),
        name="rms_final",
    )(x, nw.reshape(1, d))


WP_BN = 1024
WP_B1 = _MAIN_OFF["g_z"] // WP_BN
WP_B2 = _MAIN_OFF["m_o"] // WP_BN
WP_S1 = GLA_GATE_RANK
WP_S2 = GLA_GATE_RANK + 2 * ML_HEADS
assert _MAIN_OFF["g_z"] % WP_BN == 0 and _MAIN_OFF["m_o"] % WP_BN == 0 and N_MAIN % WP_BN == 0
assert WP_S1 % SUBLANES == 0 and WP_S2 % SUBLANES == 0 and sum(_SPLIT_W) % WP_S2 == 0


def _wprep_kernel(a_ref, t_ref, o_ref):
    j = pl.program_id(2)

    def emit(s):
        x = a_ref[...] if s == 0 else jnp.concatenate([a_ref[...], t_ref[...]], axis=0)[s:s + WP_BN]
        o_ref[...] = x.T.astype(BF16)

    @pl.when(j < WP_B1)
    def _():
        emit(0)

    @pl.when((j >= WP_B1) & (j < WP_B2))
    def _():
        emit(WP_S1)

    @pl.when(j >= WP_B2)
    def _():
        emit(WP_S2)


def _wprep(w_t):
    depth, _, d = w_t.shape
    kb = _pick(d, (2048, 1024, 512, 256, 128))
    return pl.pallas_call(
        _wprep_kernel,
        grid=(depth, d // kb, N_MAIN // WP_BN),
        in_specs=[pl.BlockSpec((None, WP_BN, kb), lambda l, i, j: (l, j, i)),
                  pl.BlockSpec((None, WP_S2, kb), lambda l, i, j: (l, (j + 1) * (WP_BN // WP_S2), i))],
        out_specs=pl.BlockSpec((None, kb, WP_BN), lambda l, i, j: (l, i, j)),
        out_shape=jax.ShapeDtypeStruct((depth, d, N_MAIN), BF16),
        compiler_params=_cparams(("parallel", "parallel", "parallel")),
        name="w_prep",
    )(w_t, w_t)


def _wsmall_kernel(lr_ref, if_ref, o_ref):
    kb = lr_ref.shape[1]
    x = jnp.concatenate([lr_ref[...], if_ref[...],
                         jnp.zeros((LANES - GLA_GATE_RANK - 2 * ML_HEADS, kb), F32)], axis=0)
    o_ref[...] = x.T.astype(BF16)


def _wsmall(w_t):
    depth, _, d = w_t.shape
    kb = _pick(d, (512, 256, 128))
    lr, gi = _SRC_OFF["g_lr"], _SRC_OFF["m_i"]
    assert _SRC_OFF["m_f"] == gi + ML_HEADS and lr % GLA_GATE_RANK == 0 and gi % (2 * ML_HEADS) == 0
    return pl.pallas_call(
        _wsmall_kernel,
        grid=(depth, d // kb),
        in_specs=[pl.BlockSpec((None, GLA_GATE_RANK, kb), lambda l, i: (l, lr // GLA_GATE_RANK, i)),
                  pl.BlockSpec((None, 2 * ML_HEADS, kb), lambda l, i: (l, gi // (2 * ML_HEADS), i))],
        out_specs=pl.BlockSpec((None, kb, LANES), lambda l, i: (l, i, 0)),
        out_shape=jax.ShapeDtypeStruct((depth, d, LANES), BF16),
        compiler_params=_cparams(("parallel", "parallel")),
        name="w_small",
    )(w_t, w_t)


def _mm_kernel(a_ref, as_ref, b_ref, o_ref, os_ref):
    w = b_ref[...]
    o_ref[...] = _dot(a_ref[...], w)

    @pl.when(pl.program_id(1) == 0)
    def _():
        os_ref[...] = _dot(as_ref[...], w)


def _in_proj(a, a_s, w_all, layer):
    m, k = a.shape
    ms = a_s.shape[0]
    n = w_all.shape[2]
    bm = _pick(m, (1024, 512, 256, 128, 64))
    bn = _pick(n, (1024, 512, 256, 128))
    return pl.pallas_call(
        _mm_kernel,
        grid=(n // bn, m // bm),
        in_specs=[pl.BlockSpec((bm, k), lambda j, i: (i, 0)),
                  pl.BlockSpec((ms, k), lambda j, i: (0, 0)),
                  pl.BlockSpec((None, k, bn), lambda j, i: (layer, 0, j))],
        out_specs=[pl.BlockSpec((bm, bn), lambda j, i: (i, j)),
                   pl.BlockSpec((ms, bn), lambda j, i: (0, j))],
        out_shape=[jax.ShapeDtypeStruct((m, n), F32), jax.ShapeDtypeStruct((ms, n), F32)],
        compiler_params=_cparams(("parallel", "arbitrary")),
        name="in_proj",
    )(a, a_s, w_all)


def _out_kernel(a0, a1, a2, a3, w_ref, x_ref, o_ref):
    acc = x_ref[...]
    for g, a in enumerate((a0, a1, a2, a3)):
        acc = acc + _dot(a[...].astype(BF16), w_ref[g].astype(BF16))
    o_ref[...] = acc


def _out_proj(mix, w_all, layer, x):
    m, d = x.shape
    bm = _pick(m, (2048, 1024, 512, 256, 128, 64))
    bn = _pick(d, (256, 128))
    a_spec = pl.BlockSpec((bm, GROUP_W), lambda i, j: (i, 0))
    return pl.pallas_call(
        _out_kernel,
        grid=(m // bm, d // bn),
        in_specs=[a_spec, a_spec, a_spec, a_spec,
                  pl.BlockSpec((None, 4, GROUP_W, bn), lambda i, j: (layer, 0, 0, j)),
                  pl.BlockSpec((bm, bn), lambda i, j: (i, j))],
        out_specs=pl.BlockSpec((bm, bn), lambda i, j: (i, j)),
        out_shape=jax.ShapeDtypeStruct((m, d), F32),
        compiler_params=_cparams(("parallel", "parallel"), VMEM_LIMIT_TALL),
        name="out_proj",
    )(*mix, w_all, x)


def _rope_kernel(x_ref, a_ref, b_ref, c_ref, o_ref):
    a, b, c = a_ref[...], b_ref[...], c_ref[...]
    for h in range(MB_HEADS):
        sl = slice(h * MB_DH, (h + 1) * MB_DH)
        x = x_ref[0, :, sl]
        o_ref[0, :, sl] = (x * a + pltpu.roll(x, MB_DH - ROT_DIM // 2, 1) * b
                           + pltpu.roll(x, ROT_DIM // 2, 1) * c)


def _rope_tables(pos):
    half = ROT_DIM // 2
    inv_freq = jnp.power(ROPE_THETA, -jnp.arange(0, ROT_DIM, 2, dtype=F32) / ROT_DIM)
    ang = pos.astype(F32)[:, None] * inv_freq[None, :]
    cos, sin = jnp.cos(ang), jnp.sin(ang)
    t = pos.shape[0]
    a = jnp.concatenate([cos, cos, jnp.ones((t, MB_DH - ROT_DIM), F32)], axis=1)
    b = jnp.concatenate([-sin, jnp.zeros((t, MB_DH - half), F32)], axis=1)
    c = jnp.concatenate([jnp.zeros((t, half), F32), sin, jnp.zeros((t, MB_DH - ROT_DIM), F32)], axis=1)
    return a, b, c


def _rope(proj, col, tables):
    bsz, t, _ = proj.shape
    bt = _pick(t, (1024, 512, 256, 128, 64, 32, 16, 8))
    tab = pl.BlockSpec((bt, MB_DH), lambda b, i: (i, 0))
    return pl.pallas_call(
        _rope_kernel,
        grid=(bsz, t // bt),
        in_specs=[pl.BlockSpec((1, bt, GROUP_W), lambda b, i: (b, i, col // GROUP_W)), tab, tab, tab],
        out_specs=pl.BlockSpec((1, bt, GROUP_W), lambda b, i: (b, i, 0)),
        out_shape=jax.ShapeDtypeStruct((bsz, t, GROUP_W), F32),
        compiler_params=_cparams(("parallel", "parallel")),
        name="rope",
    )(proj, *tables)


def _kv_heads_kernel(k_ref, v_ref, a_ref, b_ref, c_ref, *refs):
    k4_ref, v4_ref = refs[-2:]
    a, b, c = a_ref[...], b_ref[...], c_ref[...]
    bt = a.shape[0]
    for h in range(MB_HEADS):
        sl = slice(h * MB_DH, (h + 1) * MB_DH)
        x = k_ref[0, :, sl]
        rows = pl.ds(h, bt, stride=MB_HEADS)
        k4_ref[rows, :] = (x * a + pltpu.roll(x, MB_DH - ROT_DIM // 2, 1) * b
                           + pltpu.roll(x, ROT_DIM // 2, 1) * c)
        v4_ref[rows, :] = v_ref[0, :, sl]


def _kv_heads(proj, tables, layer, depth, stacks):
    bsz, t, _ = proj.shape
    bt = _pick(t, (512, 256, 128, 64, 32, 16, 8))
    tab = pl.BlockSpec((bt, MB_DH), lambda b, i: (i, 0))
    kcol, vcol = _MAIN_OFF["a_k"] // GROUP_W, _MAIN_OFF["a_v"] // GROUP_W
    out = pl.BlockSpec((None, None, bt * MB_HEADS, MB_DH), lambda b, i: (layer, b, i, 0))
    shape = jax.ShapeDtypeStruct((depth, bsz, t * MB_HEADS, MB_DH), F32)
    in_specs = [pl.BlockSpec((1, bt, GROUP_W), lambda b, i: (b, i, kcol)),
                pl.BlockSpec((1, bt, GROUP_W), lambda b, i: (b, i, vcol)), tab, tab, tab]
    aliases = {}
    if stacks is not None:
        in_specs += [pl.BlockSpec(memory_space=pl.ANY), pl.BlockSpec(memory_space=pl.ANY)]
        aliases = {5: 0, 6: 1}
    return pl.pallas_call(
        _kv_heads_kernel,
        grid=(bsz, t // bt),
        in_specs=in_specs,
        out_specs=[out, out],
        out_shape=[shape, shape],
        input_output_aliases=aliases,
        compiler_params=_cparams(("parallel", "parallel")),
        name="kv_heads",
    )(proj, proj, *tables, *(stacks or ()))


GLA_GROUP = 2
GLA_SPAN = 2
assert (GLA_HEADS // 2) % GLA_GROUP == 0


def _gla_levels(c):
    return [c >> i for i in range(1, c.bit_length())]


def _gla_sum_matrices(c, nch):
    sup = c * nch
    t = np.arange(sup)[:, None]
    r = np.arange(sup)[None, :]
    cb = (t // c) * c
    mats = [(r >= cb) & (r <= t), (r > t) & (r <= cb + c - 1)]
    for hs in _gla_levels(c):
        base = (t // (2 * hs)) * (2 * hs)
        ref = base + hs - 1
        mats.append(np.where(t - base >= hs, (r > ref) & (r <= t), (r > t) & (r <= ref)))
    return np.concatenate(mats, axis=0).astype(np.float32)


def _gla_kernel(q_ref, k_ref, v_ref, z_ref, sm_ref, wg_ref, bg_ref, nw_ref, s0_ref, sum_ref,
                o_ref, s_ref, g_scr, *, t, c, nch, t_valid):
    dk = GLA_DK
    sup = c * nch
    levels = _gla_levels(c)
    split = sum_ref.dtype == BF16
    x = _dot(sm_ref[0], wg_ref[...], precision=HIGHEST) + bg_ref[...]
    g_all = _log_sigmoid(x) * (1.0 / GLA_GATE_NORM)
    if t_valid < t:
        rows_t = lax.broadcasted_iota(jnp.int32, g_all.shape, 0)
        g_all = jnp.where(rows_t < t_valid, g_all, 0.0)
    g_scr[...] = g_all

    lane = lax.broadcasted_iota(jnp.int32, (sup, LANES), 1)
    row = lax.broadcasted_iota(jnp.int32, (sup, LANES), 0)
    head0 = lane < dk
    rr = lax.broadcasted_iota(jnp.int32, (sup, sup), 0)
    cc = lax.broadcasted_iota(jnp.int32, (sup, sup), 1)
    mm = BF16 if split else F32
    prec = None if split else HIGHEST
    nw = nw_ref[...]

    def pair_trip(p, r0, state):
        ql = slice(p * LANES, (p + 1) * LANES)
        vl = slice(p * 2 * GLA_DV, (p + 1) * 2 * GLA_DV)
        q = q_ref[0, pl.ds(r0, sup), ql] * (dk ** -0.5)
        k = k_ref[0, pl.ds(r0, sup), ql]
        if t_valid < t:
            k = jnp.where(row < t_valid, k, 0.0)
        g = g_scr[pl.ds(r0, sup), ql]
        v = v_ref[0, pl.ds(r0, sup), vl]
        z = z_ref[0, pl.ds(r0, sup), vl]
        if split:
            g1 = g.astype(BF16)
            r1 = g - g1.astype(F32)
            g2 = r1.astype(BF16)
            g3 = (r1 - g2.astype(F32)).astype(BF16)
            xs = _dot(sum_ref[...], jnp.concatenate([g1, g2, g3], axis=1))
            xs = xs[:, :LANES] + xs[:, LANES:2 * LANES] + xs[:, 2 * LANES:]
        else:
            xs = _dot(sum_ref[...], g, precision=HIGHEST)
        yield
        b = xs[0:sup]
        to_end = xs[sup:2 * sup]

        qk = q * k
        att0 = jnp.where(rr == cc, jnp.sum(jnp.where(head0, qk, 0.0), axis=-1, keepdims=True), 0.0)
        att1 = jnp.where(rr == cc, jnp.sum(jnp.where(head0, 0.0, qk), axis=-1, keepdims=True), 0.0)
        for i, hs in enumerate(levels):
            e = jnp.exp(xs[(2 + i) * sup:(3 + i) * sup])
            upper = (row & hs) != 0
            qt = jnp.where(upper, q * e, 0.0)
            kl = jnp.where(upper, 0.0, k * e).astype(mm)
            sh = (2 * hs).bit_length() - 1
            same = (rr >> sh) == (cc >> sh)
            att0 = att0 + jnp.where(same, _dot_nt(jnp.where(head0, qt, 0.0).astype(mm), kl, prec), 0.0)
            att1 = att1 + jnp.where(same, _dot_nt(jnp.where(head0, 0.0, qt).astype(mm), kl, prec), 0.0)
            yield

        kt = k * jnp.exp(to_end)
        qe = q * jnp.exp(b)
        qe0, qe1 = jnp.where(head0, qe, 0.0), jnp.where(head0, 0.0, qe)
        kt0, kt1 = jnp.where(head0, kt, 0.0), jnp.where(head0, 0.0, kt)
        v0, v1 = v[:, :GLA_DV], v[:, GLA_DV:]
        rows = [slice(ch * c, (ch + 1) * c) for ch in range(nch)]
        upds = [_dot_tn(jnp.concatenate([v0[r], v1[r]], axis=0).astype(mm),
                        jnp.concatenate([kt0[r], kt1[r]], axis=0).astype(mm), prec) for r in rows]
        yield
        o_int = []
        s2t = state[p]
        for ch, r in enumerate(rows):
            lhs = jnp.concatenate([qe0[r], qe1[r]], axis=0).astype(mm)
            o_int.append(_dot_nt(lhs, s2t.astype(mm), prec))
            s2t = jnp.exp(b[ch * c + c - 1:ch * c + c, :]) * s2t + upds[ch]
        state[p] = s2t
        yield

        for h, (att, vh) in enumerate(((att0, v0), (att1, v1))):
            o = (jnp.concatenate([oi[h * c:(h + 1) * c, :] for oi in o_int], axis=0)
                 + _dot(att.astype(mm), vh.astype(mm), prec))
            yield
            on = o * lax.rsqrt(jnp.mean(o * o, axis=-1, keepdims=True) + EPS) * nw
            zh = z[:, h * GLA_DV:(h + 1) * GLA_DV]
            cols = slice(p * 2 * GLA_DV + h * GLA_DV, p * 2 * GLA_DV + (h + 1) * GLA_DV)
            o_ref[0, pl.ds(r0, sup), cols] = (on * _silu(zh)).astype(o_ref.dtype)

    span = GLA_SPAN if (t // sup) % GLA_SPAN == 0 else 1

    def body(si, states):
        r0 = pl.multiple_of(si * (sup * span), sup * span)
        state = list(states)
        trips = [pair_trip(p, pl.multiple_of(r0 + j * sup, sup), state)
                 for j in range(span) for p in range(GLA_GROUP)]
        while trips:
            trips = [g for g in trips if next(g, True) is None]
        return tuple(state)

    states = lax.fori_loop(0, t // (sup * span), body, tuple(s0_ref[0, p] for p in range(GLA_GROUP)))
    for p in range(GLA_GROUP):
        s_ref[0, p] = states[p]


def _gla(proj, small, wg_pad, bg, nw, s0t, t_valid, out_dtype):
    bsz, t, _ = proj.shape
    c = min(LANES, t)
    nch = 1
    pairs = GLA_HEADS // 2
    gw = GLA_GROUP * LANES
    vw = GLA_GROUP * 2 * GLA_DV
    qb, kb = _MAIN_OFF["g_q"] // gw, _MAIN_OFF["g_k"] // gw
    vb, zb = _MAIN_OFF["g_v"] // vw, _MAIN_OFF["g_z"] // vw
    kern = functools.partial(_gla_kernel, t=t, c=c, nch=nch, t_valid=t_valid)
    sums = _gla_sum_matrices(c, nch)
    sums = jnp.asarray(sums, BF16 if (c * nch) % 16 == 0 else F32)
    return pl.pallas_call(
        kern,
        grid=(bsz, pairs // GLA_GROUP),
        in_specs=[pl.BlockSpec((1, t, gw), lambda b, p: (b, 0, qb + p)),
                  pl.BlockSpec((1, t, gw), lambda b, p: (b, 0, kb + p)),
                  pl.BlockSpec((1, t, vw), lambda b, p: (b, 0, vb + p)),
                  pl.BlockSpec((1, t, vw), lambda b, p: (b, 0, zb + p)),
                  pl.BlockSpec((1, t, LANES), lambda b, p: (b, 0, 0)),
                  pl.BlockSpec((LANES, gw), lambda b, p: (0, p)),
                  pl.BlockSpec((1, gw), lambda b, p: (0, p)),
                  pl.BlockSpec((1, GLA_DV), lambda b, p: (0, 0)),
                  pl.BlockSpec((1, GLA_GROUP, GLA_DV, LANES), lambda b, p: (b, p, 0, 0)),
                  pl.BlockSpec(sums.shape, lambda b, p: (0, 0))],
        out_specs=[pl.BlockSpec((1, t, vw), lambda b, p: (b, 0, p)),
                   pl.BlockSpec((1, GLA_GROUP, GLA_DV, LANES), lambda b, p: (b, p, 0, 0))],
        out_shape=[jax.ShapeDtypeStruct((bsz, t, GROUP_W), out_dtype),
                   jax.ShapeDtypeStruct((bsz, pairs, GLA_DV, LANES), F32)],
        scratch_shapes=[pltpu.VMEM((t, gw), F32)],
        compiler_params=_cparams(("parallel", "parallel")),
        name="gla",
    )(proj, proj, proj, proj, small, wg_pad, bg.reshape(1, -1), nw.reshape(1, -1), s0t, sums)


def _gla_state_to_pairs(s):
    bsz = s.shape[0]
    s = s.reshape(bsz, GLA_HEADS // 2, 2, GLA_DK, GLA_DV)
    return s.transpose(0, 1, 4, 2, 3).reshape(bsz, GLA_HEADS // 2, GLA_DV, 2 * GLA_DK)


def _gla_state_from_pairs(s):
    bsz = s.shape[0]
    s = s.reshape(bsz, GLA_HEADS // 2, GLA_DV, 2, GLA_DK)
    return s.transpose(0, 1, 3, 4, 2).reshape(bsz, GLA_HEADS, GLA_DK, GLA_DV)


def _gates_kernel(sm_ref, bias_ref, o_ref, *, t_valid):
    x = sm_ref[0] + bias_ref[...]
    lane = lax.broadcasted_iota(jnp.int32, x.shape, 1)
    is_f = (lane >= SM_F) & (lane < SM_F + ML_HEADS)
    out = jnp.where(is_f, _log_sigmoid(x), x)
    if t_valid is not None:
        row = lax.broadcasted_iota(jnp.int32, x.shape, 0)
        out = jnp.where(row < t_valid, out, jnp.where(is_f, 0.0, NEG))
    o_ref[0] = out


def _gates(small, bias_row, t_valid):
    bsz, t, _ = small.shape
    bt = t if t_valid < t else _pick(t, (1024, 512, 256, 128, 64, 32, 16, 8))
    return pl.pallas_call(
        functools.partial(_gates_kernel, t_valid=t_valid if t_valid < t else None),
        grid=(bsz, t // bt),
        in_specs=[pl.BlockSpec((1, bt, LANES), lambda b, i: (b, i, 0)),
                  pl.BlockSpec((1, LANES), lambda b, i: (0, 0))],
        out_specs=pl.BlockSpec((1, bt, LANES), lambda b, i: (b, i, 0)),
        out_shape=jax.ShapeDtypeStruct((bsz, t, LANES), F32),
        compiler_params=_cparams(("parallel", "parallel")),
        name="ml_gates",
    )(small, bias_row)


ML_PAIR = 2
ML_SPAN = 2


def _mlstm_kernel(q_ref, k_ref, v_ref, og_ref, z_ref, gt_ref, nw_ref,
                  c0_ref, n0_ref, m0_ref, o_ref, c_ref, n_ref, m_ref, *, t, c):
    h0 = pl.program_id(1) * ML_PAIR
    hs = range(ML_PAIR)
    lane = lax.broadcasted_iota(jnp.int32, (c, LANES), 1)
    rr = lax.broadcasted_iota(jnp.int32, (c, c), 0)
    cc = lax.broadcasted_iota(jnp.int32, (c, c), 1)
    causal = rr >= cc
    nw = nw_ref[...]
    hsl = lambda j: slice(j * ML_DH, (j + 1) * ML_DH)

    def body(ci, carry):
        cms, ns, m_prevs = carry
        r0 = pl.multiple_of(ci * c, c)
        gt = gt_ref[0, pl.ds(r0, c), :]
        qs = [q_ref[0, pl.ds(r0, c), hsl(j)] * (ML_DH ** -0.5) for j in hs]
        ks = [k_ref[0, pl.ds(r0, c), hsl(j)] for j in hs]
        vs = [v_ref[0, pl.ds(r0, c), hsl(j)] for j in hs]
        s_qk = [_dot_nt(qs[j], ks[j], precision=HIGHEST) for j in hs]
        s_qc = [_dot_nt(qs[j], cms[j], precision=HIGHEST) for j in hs]

        d, inter, m_t, i_cols, fc_cols = [], [], [], [], []
        for j in hs:
            i_col = jnp.sum(jnp.where(lane == SM_I + h0 + j, gt, 0.0), axis=-1, keepdims=True)
            f_col = jnp.sum(jnp.where(lane == SM_F + h0 + j, gt, 0.0), axis=-1, keepdims=True)
            i_row = jnp.sum(jnp.where(rr == cc, i_col, 0.0), axis=0, keepdims=True)
            f_row = jnp.sum(jnp.where(rr == cc, f_col, 0.0), axis=0, keepdims=True)
            fc_col = jnp.sum(jnp.where(causal, f_row, 0.0), axis=-1, keepdims=True)
            fc_row = jnp.sum(jnp.where(rr <= cc, f_col, 0.0), axis=0, keepdims=True)
            dj = jnp.where(causal, fc_col - fc_row + i_row, NEG)
            d.append(dj)
            inter.append(fc_col + m_prevs[j])
            m_t.append(jnp.maximum(inter[j], jnp.max(dj, axis=-1, keepdims=True)))
            i_cols.append(i_col)
            fc_cols.append(fc_col)

        w_state = [jnp.exp(inter[j] - m_t[j]) for j in hs]
        qk = [s_qk[j] * jnp.exp(d[j] - m_t[j]) for j in hs]
        pv = [_dot(qk[j], vs[j], precision=HIGHEST) for j in hs]
        w_end, dec, m_new = [], [], []
        for j in hs:
            m_new.append(m_t[j][c - 1:c, :])
            f_end = fc_cols[j][c - 1:c, :]
            w_end.append(jnp.exp(f_end - fc_cols[j] + i_cols[j] - m_new[j]))
            dec.append(jnp.exp(f_end + m_prevs[j] - m_new[j]))
        upd = [_dot_tn(vs[j] * w_end[j], ks[j], precision=HIGHEST) for j in hs]

        c_new, n_new = [], []
        for j in hs:
            num = w_state[j] * s_qc[j] + pv[j]
            den = (w_state[j] * jnp.sum(qs[j] * ns[j], axis=-1, keepdims=True)
                   + jnp.sum(qk[j], axis=-1, keepdims=True))
            hout = num / jnp.maximum(jnp.abs(den), jnp.exp(-m_t[j]))
            c_new.append(dec[j] * cms[j] + upd[j])
            n_new.append(dec[j] * ns[j] + jnp.sum(ks[j] * w_end[j], axis=0, keepdims=True))
            o = _sigmoid(og_ref[0, pl.ds(r0, c), hsl(j)]) * hout
            on = o * lax.rsqrt(jnp.mean(o * o, axis=-1, keepdims=True) + EPS) * nw
            o_ref[0, pl.ds(r0, c), hsl(j)] = (on * _silu(z_ref[0, pl.ds(r0, c), hsl(j)])).astype(o_ref.dtype)
        return tuple(c_new), tuple(n_new), tuple(m_new)

    init = (tuple(c0_ref[0, j] for j in hs), tuple(n0_ref[0, j] for j in hs),
            tuple(m0_ref[0, j][:, :1] for j in hs))
    cms, ns, ms = lax.fori_loop(0, t // c, body, init)
    for j in hs:
        c_ref[0, j] = cms[j]
        n_ref[0, j] = ns[j]
        m_ref[0, j] = jnp.broadcast_to(ms[j], (1, LANES))


def _split3(x):
    a = x.astype(BF16)
    r = x - a.astype(F32)
    b = r.astype(BF16)
    return a, b, (r - b.astype(F32)).astype(BF16)


def _mlstm_wide_kernel(q_ref, k_ref, v_ref, og_ref, z_ref, gt_ref, nw_ref, c0_ref, n0_ref, m0_ref,
                       o_ref, c_ref, n_ref, m_ref, t_scr, g_scr, i_scr, fc_scr, *, t):
    c = LANES
    nc = t // c
    h0 = pl.program_id(1) * ML_PAIR
    hs = range(ML_PAIR)
    rr = lax.broadcasted_iota(jnp.int32, (c, c), 0)
    cc = lax.broadcasted_iota(jnp.int32, (c, c), 1)
    causal = rr >= cc
    nw = nw_ref[...]
    hsl = lambda j: slice(j * ML_DH, (j + 1) * ML_DH)
    ones_b = jnp.ones((c, ML_DH), BF16)
    tril_b = jnp.where(causal, 1.0, 0.0).astype(BF16)

    lane_t = lax.broadcasted_iota(jnp.int32, (c, LANES), 1)
    for ci in range(nc):
        rows = slice(ci * c, (ci + 1) * c)
        g = gt_ref[0, rows, :]
        g_scr[rows, :] = jnp.where(lane_t < SM_F, g, sum(_dot(tril_b, x) for x in _split3(g)))
    sel_r = lax.broadcasted_iota(jnp.int32, (LANES, 2 * LANES), 0)
    sel_c = lax.broadcasted_iota(jnp.int32, (LANES, 2 * LANES), 1)
    g_terms = _split3(g_scr[...])
    for j in hs:
        src = jnp.where(sel_c < LANES, SM_I + h0 + j, SM_F + h0 + j)
        sel = jnp.where(sel_r == src, 1.0, 0.0).astype(BF16)
        both = sum(_dot(g, sel) for g in g_terms)
        i_scr[j] = both[:, :LANES]
        fc_scr[j] = both[:, LANES:]

    def head_trip(j, u, r0, cms, ns, ms):
        rows = pl.ds(r0, c)
        qb = (q_ref[0, rows, hsl(j)] * (ML_DH ** -0.5)).astype(BF16)
        kb = k_ref[0, rows, hsl(j)].astype(BF16)
        v = v_ref[0, rows, hsl(j)]
        s_qk = _dot_nt(qb, kb)
        yield
        fc = fc_scr[j, rows, :]
        i_row = t_scr[u, pl.ds(SM_I + h0 + j, 1), :]
        fc_row = t_scr[u, pl.ds(SM_F + h0 + j, 1), :]
        d = jnp.where(causal, fc - fc_row + i_row, NEG)
        m_prev = ms[j]
        inter = fc + m_prev
        m_t = jnp.maximum(inter, jnp.max(d, axis=-1, keepdims=True))
        m_new = m_t[c - 1:c, :1]
        ms[j] = m_new
        yield
        w_state = jnp.exp(inter - m_t)
        qk = (s_qk * jnp.exp(d - m_t)).astype(BF16)
        pv = _dot(qk, jnp.concatenate([v.astype(BF16), ones_b], axis=1))
        f_end = fc[c - 1:c, :]
        w_end = jnp.exp(f_end - fc + i_scr[j, rows, :] - m_new)
        dec = jnp.exp(f_end[:, :1] + m_prev - m_new)
        yield
        s_qc = _dot_nt(qb, jnp.concatenate([cms[j], jnp.broadcast_to(ns[j], (c, ML_DH))],
                                           axis=0).astype(BF16))
        upd = _dot_tn(jnp.concatenate([v * w_end, w_end], axis=1).astype(BF16), kb)
        cms[j] = dec * cms[j] + upd[:ML_DH, :]
        ns[j] = dec * ns[j] + upd[ML_DH:ML_DH + 1, :]
        yield
        num = w_state * s_qc[:, :ML_DH] + pv[:, :ML_DH]
        den = w_state * s_qc[:, ML_DH:] + pv[:, ML_DH:]
        hout = _sigmoid(og_ref[0, rows, hsl(j)]) * num / jnp.maximum(jnp.abs(den), jnp.exp(-m_t))
        msq = _dot((hout * hout).astype(BF16), ones_b) * (1.0 / ML_DH)
        yield
        on = hout * lax.rsqrt(msq + EPS) * nw
        o_ref[0, rows, hsl(j)] = (on * _silu(z_ref[0, rows, hsl(j)])).astype(o_ref.dtype)

    span = ML_SPAN if nc % ML_SPAN == 0 else 1

    def body(ci, carry):
        cms, ns, ms = (list(x) for x in carry)
        r0 = pl.multiple_of(ci * (c * span), c * span)
        trips = []
        for u in range(span):
            ru = pl.multiple_of(r0 + u * c, c)
            t_scr[u] = g_scr[pl.ds(ru, c), :].T
            trips += [head_trip(j, u, ru, cms, ns, ms) for j in hs]
        while trips:
            trips = [g for g in trips if next(g, True) is None]
        return tuple(cms), tuple(ns), tuple(ms)

    init = (tuple(c0_ref[0, j] for j in hs), tuple(n0_ref[0, j] for j in hs),
            tuple(m0_ref[0, j][:, :1] for j in hs))
    cms, ns, ms = lax.fori_loop(0, nc // span, body, init)
    for j in hs:
        c_ref[0, j] = cms[j]
        n_ref[0, j] = ns[j]
        m_ref[0, j] = jnp.broadcast_to(ms[j], (1, LANES))


def _mlstm(proj, gates, nw, c0, n0, m0, out_dtype):
    bsz, t, _ = proj.shape
    c = min(128, t)
    w = ML_PAIR * ML_DH
    col = lambda name: _MAIN_OFF[name] // w
    spec = lambda name: pl.BlockSpec((1, t, w), lambda b, p, o=col(name): (b, 0, o + p))
    st = lambda rows: pl.BlockSpec((1, ML_PAIR, rows, ML_DH), lambda b, p: (b, p, 0, 0))
    if c == LANES:
        kern = functools.partial(_mlstm_wide_kernel, t=t)
        scratch = ([pltpu.VMEM((ML_SPAN, LANES, LANES), F32), pltpu.VMEM((t, LANES), F32)]
                   + [pltpu.VMEM((ML_PAIR, t, LANES), F32)] * 2)
    else:
        kern = functools.partial(_mlstm_kernel, t=t, c=c)
        scratch = []
    return pl.pallas_call(
        kern,
        grid=(bsz, ML_HEADS // ML_PAIR),
        in_specs=[spec("m_q"), spec("m_k"), spec("m_v"), spec("m_o"), spec("m_z"),
                  pl.BlockSpec((1, t, LANES), lambda b, p: (b, 0, 0)),
                  pl.BlockSpec((1, ML_DH), lambda b, p: (0, 0)),
                  st(ML_DH), st(1), st(1)],
        out_specs=[pl.BlockSpec((1, t, w), lambda b, p: (b, 0, p)), st(ML_DH), st(1), st(1)],
        out_shape=[jax.ShapeDtypeStruct((bsz, t, GROUP_W), out_dtype),
                   jax.ShapeDtypeStruct((bsz, ML_HEADS, ML_DH, ML_DH), F32),
                   jax.ShapeDtypeStruct((bsz, ML_HEADS, 1, ML_DH), F32),
                   jax.ShapeDtypeStruct((bsz, ML_HEADS, 1, LANES), F32)],
        scratch_shapes=scratch,
        compiler_params=_cparams(("parallel", "parallel")),
        name="mlstm",
    )(proj, proj, proj, proj, proj, gates, nw.reshape(1, -1), c0, n0, m0)


def _gmlp_kernel(u_ref, v_ref, z_ref, lw_ref, lb_ref, ws_ref, bs_ref, o_ref, *vn_refs, l, nl):
    rr = lax.broadcasted_iota(jnp.int32, (l, l), 0)
    cc = lax.broadcasted_iota(jnp.int32, (l, l), 1)
    wts = [jnp.where(rr >= cc, ws_ref[g], 0.0) for g in range(GM_GROUPS)]
    if l >= GM_CHUNK:
        wts = [w.astype(BF16) for w in wts]
    for ch in range(nl):
        rows = slice(ch * l, (ch + 1) * l)
        gv = _gelu(v_ref[0, rows, :])
        mu = jnp.mean(gv, axis=-1, keepdims=True)
        xc = gv - mu
        vn = xc * lax.rsqrt(jnp.mean(xc * xc, axis=-1, keepdims=True) + EPS) * lw_ref[...] + lb_ref[...]
        if vn_refs:
            vn_refs[0][0, rows, :] = vn
        for g in range(GM_GROUPS):
            sl = slice(g * GM_CH, (g + 1) * GM_CH)
            vg = vn[:, sl]
            if l >= GM_CHUNK:
                s = _dot(wts[g], vg.astype(BF16))
            else:
                s = jnp.zeros((l, GM_CH), F32)
                for r in range(l):
                    s = s + wts[g][:, r:r + 1] * vg[r:r + 1, :]
            s = s + bs_ref[:, g:g + 1]
            o_ref[0, rows, sl] = (_gelu(u_ref[0, rows, sl]) * s * _silu(z_ref[0, rows, sl])).astype(o_ref.dtype)


def _gmlp(proj, lw, lb, ws, bs_t, out_dtype, want_vn):
    bsz, t, _ = proj.shape
    l = min(t, GM_CHUNK)
    nl = _pick(t // l, (4, 2, 1))
    col = lambda name: _MAIN_OFF[name] // GROUP_W
    spec = lambda name: pl.BlockSpec((1, l * nl, GROUP_W), lambda b, i, o=col(name): (b, i, o))
    out = pl.BlockSpec((1, l * nl, GROUP_W), lambda b, i: (b, i, 0))
    out_specs = [out] + ([out] if want_vn else [])
    out_shape = ([jax.ShapeDtypeStruct((bsz, t, GROUP_W), out_dtype)]
                 + ([jax.ShapeDtypeStruct((bsz, t, GROUP_W), F32)] if want_vn else []))
    res = pl.pallas_call(
        functools.partial(_gmlp_kernel, l=l, nl=nl),
        grid=(bsz, t // (l * nl)),
        in_specs=[spec("c_u"), spec("c_v"), spec("c_z"),
                  pl.BlockSpec((1, GROUP_W), lambda b, i: (0, 0)),
                  pl.BlockSpec((1, GROUP_W), lambda b, i: (0, 0)),
                  pl.BlockSpec((GM_GROUPS, l, l), lambda b, i: (0, 0, 0)),
                  pl.BlockSpec((l, GM_GROUPS), lambda b, i: (0, 0))],
        out_specs=out_specs,
        out_shape=out_shape,
        compiler_params=_cparams(("parallel", "parallel")),
        name="gmlp",
    )(proj, proj, proj, lw.reshape(1, -1), lb.reshape(1, -1), ws, bs_t)
    return (res[0], res[1]) if want_vn else (res[0], None)


MB_GROUP = 2
assert MB_HEADS % MB_GROUP == 0


def _moba_prompt_kernel(q_ref, k_ref, v_ref, z_ref, o_ref, kb_scr, vb_scr, km_scr, *, t):
    blk = MB_BLOCK
    nb = t // blk
    scale = MB_DH ** -0.5
    nbp = -(-nb // SUBLANES) * SUBLANES
    blk_row = lax.broadcasted_iota(jnp.int32, (nbp, blk), 0)
    rr = lax.broadcasted_iota(jnp.int32, (blk, blk), 0)
    cc = lax.broadcasted_iota(jnp.int32, (blk, blk), 1)
    log2_scale = float(scale * np.log2(np.e))
    lane_tiles = lambda xs: [x[:, i:i + LANES] for x in xs for i in range(0, blk, LANES)]

    def head_steps(hh):
        h = pl.program_id(1) * MB_GROUP + hh
        cols = slice(hh * MB_DH, (hh + 1) * MB_DH)
        km_scr[hh] = jnp.zeros((LANES, MB_DH), F32)
        for n in range(nb):
            rows = pl.ds(n * blk * MB_HEADS + h, blk, stride=MB_HEADS)
            kn = k_ref[rows, :]
            kb_scr[hh, n * blk:(n + 1) * blk, :] = kn.astype(BF16)
            vb_scr[hh, n * blk:(n + 1) * blk, :] = v_ref[rows, :].astype(BF16)
            km_scr[hh, n:n + 1, :] = jnp.mean(kn, axis=0, keepdims=True)
        kmean = km_scr[hh]
        yield
        for qi in range(nb):
            rows = slice(qi * blk, (qi + 1) * blk)
            q = q_ref[0, rows, cols]
            qb = q.astype(BF16)
            bias = None
            if qi > MB_TOPK:
                gate = _dot_nt(kmean, q, precision=HIGHEST)[0:nbp, :]
                yield
                cnt = jnp.zeros((nbp, blk), F32)
                for m in range(qi):
                    gm = gate[m:m + 1, :]
                    beats = (gm > gate) | ((gm == gate) & (blk_row > m))
                    cnt = cnt + jnp.where(beats, 1.0, 0.0)
                bias_t = jnp.where(cnt < MB_TOPK, 0.0, NEG)
                bias = jnp.concatenate([bias_t, jnp.zeros((LANES - nbp, blk), F32)], axis=0).T
            ss = []
            for j in range(qi + 1):
                s = _dot_nt(qb, kb_scr[hh, j * blk:(j + 1) * blk, :]) * log2_scale
                if j == qi:
                    s = jnp.where(cc <= rr, s, NEG)
                elif bias is not None:
                    s = s + bias[:, j:j + 1]
                ss.append(s)
            yield
            m_i = jnp.max(functools.reduce(jnp.maximum, lane_tiles(ss)), axis=-1, keepdims=True)
            ps = [jnp.exp2(s - m_i) for s in ss]
            l_i = jnp.sum(functools.reduce(jnp.add, lane_tiles(ps)), axis=-1, keepdims=True)
            acc = _dot(ps[0].astype(BF16), vb_scr[hh, 0:blk, :])
            for j in range(1, qi + 1):
                acc = acc + _dot(ps[j].astype(BF16), vb_scr[hh, j * blk:(j + 1) * blk, :])
            yield
            o_ref[0, rows, cols] = ((acc / l_i) * _silu(z_ref[0, rows, cols])).astype(o_ref.dtype)

    heads = [head_steps(hh) for hh in range(MB_GROUP)]
    while heads:
        heads = [g for g in heads if next(g, True) is None]


def _moba_prompt(q_rope, k4, v4, layer, proj, out_dtype):
    bsz, t, _ = proj.shape
    assert t % MB_BLOCK == 0 and t // MB_BLOCK <= LANES
    gw = MB_GROUP * MB_DH
    zcol = _MAIN_OFF["a_z"] // gw
    hd = pl.BlockSpec((1, t, gw), lambda b, h: (b, 0, h))
    kv = pl.BlockSpec((None, None, t * MB_HEADS, MB_DH), lambda b, h: (layer, b, 0, 0))
    return pl.pallas_call(
        functools.partial(_moba_prompt_kernel, t=t),
        grid=(bsz, MB_HEADS // MB_GROUP),
        in_specs=[hd, kv, kv, pl.BlockSpec((1, t, gw), lambda b, h: (b, 0, zcol + h))],
        out_specs=hd,
        out_shape=jax.ShapeDtypeStruct((bsz, t, GROUP_W), out_dtype),
        scratch_shapes=[pltpu.VMEM((MB_GROUP, t, MB_DH), BF16), pltpu.VMEM((MB_GROUP, t, MB_DH), BF16),
                        pltpu.VMEM((MB_GROUP, LANES, MB_DH), F32)],
        compiler_params=_cparams(("parallel", "arbitrary"), VMEM_LIMIT_TALL),
        name="moba_prompt",
    )(q_rope, k4, v4, proj)


QROWS = MB_HEADS * SAMPLE_T
ST_M, ST_L, ST_G = 0, 1, 2


def _moba_past_kernel(pt_ref, q_ref, bias_ref, *refs, nbs):
    del pt_ref
    npg = 2 * nbs
    k_refs, v_refs = refs[:npg], refs[npg:2 * npg]
    o_ref, st_ref = refs[2 * npg:]
    scale = MB_DH ** -0.5
    page = k_refs[0].shape[0] // MB_HEADS
    q = q_ref[0]
    qb = q.astype(BF16)
    bias = bias_ref[...]
    lane = lax.broadcasted_iota(jnp.int32, (QROWS, LANES), 1)
    ss = [_dot_nt(qb, r[...].astype(BF16)) * scale + bias for r in k_refs]
    ps, ms, ls = [], [], []
    for n in range(nbs):
        s0, s1 = ss[2 * n], ss[2 * n + 1]
        m = jnp.maximum(jnp.max(s0, axis=-1, keepdims=True), jnp.max(s1, axis=-1, keepdims=True))
        p0, p1 = jnp.exp(s0 - m), jnp.exp(s1 - m)
        ps += [p0.astype(BF16), p1.astype(BF16)]
        ms.append(m)
        ls.append(jnp.sum(p0, axis=-1, keepdims=True) + jnp.sum(p1, axis=-1, keepdims=True))
    for n in range(nbs):
        o_ref[n] = (_dot(ps[2 * n], v_refs[2 * n][...].astype(BF16))
                    + _dot(ps[2 * n + 1], v_refs[2 * n + 1][...].astype(BF16)))
        ksum = (jnp.sum(k_refs[2 * n][...].reshape(page, MB_HEADS, MB_DH), axis=0)
                + jnp.sum(k_refs[2 * n + 1][...].reshape(page, MB_HEADS, MB_DH), axis=0))
        ksum_rows = jnp.concatenate(
            [jnp.broadcast_to(ksum[h:h + 1, :], (SAMPLE_T, MB_DH)) for h in range(MB_HEADS)], axis=0)
        gate = jnp.sum(q * ksum_rows, axis=-1, keepdims=True) * (1.0 / MB_BLOCK)
        st_ref[n] = jnp.where(lane == ST_M, ms[n], jnp.where(lane == ST_L, ls[n], gate))


def _moba_past(layer, q_rows, cache_k, cache_v, page_table):
    db = q_rows.shape[0]
    rows = cache_k.shape[2]
    assert 2 * rows == MB_BLOCK * MB_HEADS
    nb = page_table.shape[1] // 2
    nbs = _pick(nb, (8, 4, 2, 1))
    key_head = np.arange(rows) % MB_HEADS
    row_head = np.arange(QROWS) // SAMPLE_T
    bias = jnp.asarray(np.where(key_head[None, :] == row_head[:, None], 0.0, NEG).astype(np.float32))
    pg = lambda i: pl.BlockSpec((None, None, rows, MB_DH),
                                lambda b, n, pt, i=i: (layer, pt[b, 2 * nbs * n + i], 0, 0))
    pages = [pg(i) for i in range(2 * nbs)]
    part = pl.BlockSpec((None, nbs, QROWS, MB_DH), lambda b, n, pt: (b, n, 0, 0))
    shape = jax.ShapeDtypeStruct((db, nb, QROWS, MB_DH), F32)
    grid_spec = pltpu.PrefetchScalarGridSpec(
        num_scalar_prefetch=1,
        grid=(db, nb // nbs),
        in_specs=[pl.BlockSpec((1, QROWS, MB_DH), lambda b, n, pt: (b, 0, 0)),
                  pl.BlockSpec((QROWS, rows), lambda b, n, pt: (0, 0))] + pages + pages,
        out_specs=[part, part],
    )
    return pl.pallas_call(
        functools.partial(_moba_past_kernel, nbs=nbs),
        grid_spec=grid_spec,
        out_shape=[shape, shape],
        compiler_params=_cparams(("parallel", "parallel")),
        name="moba_past",
    )(page_table, q_rows, bias, *([cache_k] * (2 * nbs)), *([cache_v] * (2 * nbs)))


def _moba_merge_kernel(op_ref, st_ref, q_ref, k_ref, v_ref, z_ref, o_ref, *, nb, t_valid):
    scale = MB_DH ** -0.5
    g = st_ref[0, :, :, ST_G:ST_G + 1]
    m = st_ref[0, :, :, ST_M:ST_M + 1]
    l = st_ref[0, :, :, ST_L:ST_L + 1]
    nidx = lax.broadcasted_iota(jnp.int32, g.shape, 0)
    sel = jnp.zeros(g.shape, jnp.bool_)
    gm = g
    for _ in range(min(MB_TOPK, nb)):
        mx = jnp.max(gm, axis=0, keepdims=True)
        first = jnp.min(jnp.where(gm == mx, nidx, nb), axis=0, keepdims=True)
        pick = nidx == first
        sel = sel | pick
        gm = jnp.where(pick, -jnp.inf, gm)

    rr = lax.broadcasted_iota(jnp.int32, (SAMPLE_T, SAMPLE_T), 0)
    cc = lax.broadcasted_iota(jnp.int32, (SAMPLE_T, SAMPLE_T), 1)
    mo, lo, oo = [], [], []
    for h in range(MB_HEADS):
        sl = slice(h * MB_DH, (h + 1) * MB_DH)
        qh = q_ref[0, h * SAMPLE_T:(h + 1) * SAMPLE_T, :]
        s = _dot_nt(qh.astype(BF16), k_ref[0, :, sl].astype(BF16)) * scale
        s = jnp.where((cc <= rr) & (cc < t_valid), s, NEG)
        mh = jnp.max(s, axis=-1, keepdims=True)
        p = jnp.exp(s - mh)
        mo.append(mh)
        lo.append(jnp.sum(p, axis=-1, keepdims=True))
        oo.append(_dot(p.astype(BF16), v_ref[0, :, sl].astype(BF16)))
    m_own, l_own, o_own = (jnp.concatenate(x, axis=0) for x in (mo, lo, oo))

    m_tot = jnp.maximum(jnp.max(jnp.where(sel, m, NEG), axis=0), m_own)
    w = jnp.where(sel, jnp.exp(m - m_tot[None]), 0.0)
    w_own = jnp.exp(m_own - m_tot)
    den = jnp.sum(w * l, axis=0) + w_own * l_own
    acc = w_own * o_own
    for n in range(nb):
        acc = acc + w[n] * op_ref[0, n]
    o = acc / den
    for h in range(MB_HEADS):
        sl = slice(h * MB_DH, (h + 1) * MB_DH)
        o_ref[0, :, sl] = o[h * SAMPLE_T:(h + 1) * SAMPLE_T, :] * _silu(z_ref[0, :, sl])


def _moba_merge(o_part, stats, q_rows, k_rope, proj, t_valid):
    db, nb = o_part.shape[:2]
    vcol, zcol = _MAIN_OFF["a_v"] // GROUP_W, _MAIN_OFF["a_z"] // GROUP_W
    part = pl.BlockSpec((1, nb, QROWS, MB_DH), lambda b: (b, 0, 0, 0))
    row = pl.BlockSpec((1, SAMPLE_T, GROUP_W), lambda b: (b, 0, 0))
    return pl.pallas_call(
        functools.partial(_moba_merge_kernel, nb=nb, t_valid=t_valid),
        grid=(db,),
        in_specs=[part, part, pl.BlockSpec((1, QROWS, MB_DH), lambda b: (b, 0, 0)), row,
                  pl.BlockSpec((1, SAMPLE_T, GROUP_W), lambda b: (b, 0, vcol)),
                  pl.BlockSpec((1, SAMPLE_T, GROUP_W), lambda b: (b, 0, zcol))],
        out_specs=row,
        out_shape=jax.ShapeDtypeStruct((db, SAMPLE_T, GROUP_W), F32),
        compiler_params=_cparams(("parallel",)),
        name="moba_merge",
    )(o_part, stats, q_rows, k_rope, proj, proj)


def _layer_weights(gla_w_gate_l, ml_b_i_l, ml_b_f_l):
    wg_pad = jnp.concatenate([gla_w_gate_l, jnp.zeros((LANES - GLA_GATE_RANK, gla_w_gate_l.shape[1]), F32)],
                             axis=0)
    bias_row = jnp.concatenate([jnp.zeros((SM_I,), F32), ml_b_i_l, ml_b_f_l,
                                jnp.zeros((LANES - SM_F - ML_HEADS,), F32)]).reshape(1, LANES)
    return wg_pad, bias_row


def _mixers(proj, small, bsz, t, t_valid, lw, rope_tables, gla_s0, ml_state, out_dtype, want_vn):
    (wg_pad, bias_row, gla_b_gate, gla_norm_w, ml_norm_w, gm_ln_w, gm_ln_b, gm_ws, gm_bs_t) = lw
    proj = proj.reshape(bsz, t, N_MAIN)
    small = small.reshape(bsz, t, LANES)

    out_a, gla_s = _gla(proj, small, wg_pad, gla_b_gate, gla_norm_w, gla_s0, t_valid, out_dtype)

    gates = _gates(small, bias_row, t_valid)
    out_b, ml_c, ml_n, ml_m = _mlstm(proj, gates, ml_norm_w, *ml_state, out_dtype)

    out_c, vn = _gmlp(proj, gm_ln_w, gm_ln_b, gm_ws, gm_bs_t, out_dtype, want_vn)

    q_rope = _rope(proj, _MAIN_OFF["a_q"], rope_tables)
    return proj, (out_a, out_b, out_c), q_rope, gla_s, (ml_c, ml_n, ml_m[..., :1]), vn


def kernel(x_prompt, x_sample, cache_k, cache_v, page_table, state_gla, state_mlstm_C, state_mlstm_n,
           state_mlstm_m, norm_w, w_in, gla_w_gate, gla_b_gate, gla_norm_w, ml_b_i, ml_b_f, ml_norm_w,
           gm_ln_w, gm_ln_b, gm_w_s, gm_b_s, w_out, final_norm_w):
    bp, tp, d = x_prompt.shape
    db, ts, _ = x_sample.shape
    depth = w_in.shape[0]
    page = cache_k.shape[2]
    past_len = page_table.shape[1] * page
    assert w_out.shape[1] == 4 * GROUP_W and ts <= SAMPLE_T
    assert past_len % MB_BLOCK == 0 and tp % MB_BLOCK == 0

    tables_p = _rope_tables(jnp.arange(tp, dtype=jnp.int32))
    tables_s = _rope_tables(past_len + jnp.arange(SAMPLE_T, dtype=jnp.int32))
    cache_k = cache_k.reshape(depth, cache_k.shape[1], page * MB_HEADS, MB_DH)
    cache_v = cache_v.reshape(depth, cache_v.shape[1], page * MB_HEADS, MB_DH)

    yp = x_prompt.reshape(bp * tp, d)
    ys = jnp.pad(x_sample, ((0, 0), (0, SAMPLE_T - ts), (0, 0))).reshape(db * SAMPLE_T, d)
    dt_p = BF16
    dt_s = F32

    zero_gla = jnp.zeros((bp, GLA_HEADS // 2, GLA_DV, LANES), F32)
    zero_ml = (jnp.zeros((bp, ML_HEADS, ML_DH, ML_DH), F32), jnp.zeros((bp, ML_HEADS, 1, ML_DH), F32),
               jnp.zeros((bp, ML_HEADS, 1, LANES), F32))

    outs = {n: [] for n in ("ks", "vs", "gp", "gs", "cp", "cs", "np", "ns", "mp", "ms", "vv")}
    lp = min(tp, GM_CHUNK)
    w_t = jnp.swapaxes(w_in, 1, 2)
    w_main = _wprep(w_t)
    w_small = _wsmall(w_t)
    w_out4 = w_out.reshape(depth, 4, GROUP_W, d)
    kv_stacks = None
    for l in range(depth):
        wg_pad, bias_row = _layer_weights(gla_w_gate[l], ml_b_i[l], ml_b_f[l])
        common = (wg_pad, bias_row, gla_b_gate[l], gla_norm_w[l], ml_norm_w[l], gm_ln_w[l], gm_ln_b[l])
        h_p, small_p = _rms_small(yp, norm_w[l], w_small, l)
        h_s, small_s = _rms_small(ys, norm_w[l], w_small, l)
        proj_p, proj_s = _in_proj(h_p, h_s, w_main, l)

        lw = common + (gm_w_s[l][:, :lp, :lp], gm_b_s[l][:, :lp].T)
        proj, mix, q_rope, gla_s, ml_s, _ = _mixers(
            proj_p, small_p, bp, tp, tp, lw, tables_p, zero_gla, zero_ml, dt_p, want_vn=False)
        kv_stacks = _kv_heads(proj, tables_p, l, depth, kv_stacks)
        out_d = _moba_prompt(q_rope, kv_stacks[0], kv_stacks[1], l, proj, dt_p)
        yp = _out_proj([a.reshape(bp * tp, GROUP_W) for a in mix + (out_d,)], w_out4, l, yp)
        outs["gp"].append(_gla_state_from_pairs(gla_s))
        outs["cp"].append(ml_s[0])
        outs["np"].append(ml_s[1][:, :, 0, :])
        outs["mp"].append(ml_s[2][:, :, 0, 0])

        lw = common + (gm_w_s[l][:, :SAMPLE_T, :SAMPLE_T], gm_b_s[l][:, :SAMPLE_T].T)
        ml_state = (state_mlstm_C[l], state_mlstm_n[l][:, :, None, :],
                    jnp.broadcast_to(state_mlstm_m[l][:, :, None, None], (db, ML_HEADS, 1, LANES)))
        proj, mix, q_rope, gla_s, ml_s, vn = _mixers(
            proj_s, small_s, db, SAMPLE_T, ts, lw, tables_s, _gla_state_to_pairs(state_gla[l]), ml_state,
            dt_s, want_vn=True)
        k_rope = _rope(proj, _MAIN_OFF["a_k"], tables_s)
        v_new = proj[:, :, _MAIN_OFF["a_v"]:_MAIN_OFF["a_v"] + GROUP_W]
        q_rows = (q_rope.reshape(db, SAMPLE_T, MB_HEADS, MB_DH).transpose(0, 2, 1, 3)
                  .reshape(db, QROWS, MB_DH))
        o_part, stats = _moba_past(l, q_rows, cache_k, cache_v, page_table)
        out_d = _moba_merge(o_part, stats, q_rows, k_rope, proj, ts)
        ys = _out_proj([a.reshape(db * SAMPLE_T, GROUP_W) for a in mix + (out_d,)], w_out4, l, ys)
        outs["ks"].append(k_rope[:, :ts].reshape(db, ts, MB_HEADS, MB_DH))
        outs["vs"].append(v_new[:, :ts].reshape(db, ts, MB_HEADS, MB_DH))
        outs["gs"].append(_gla_state_from_pairs(gla_s))
        outs["cs"].append(ml_s[0])
        outs["ns"].append(ml_s[1][:, :, 0, :])
        outs["ms"].append(ml_s[2][:, :, 0, 0])
        outs["vv"].append(vn[:, :ts])

    y_prompt = _rms(yp, final_norm_w).reshape(bp, tp, d)
    y_sample = _rms(ys, final_norm_w).reshape(db, SAMPLE_T, d)[:, :ts]
    st = jnp.stack
    k_prompt = kv_stacks[0].reshape(depth, bp, tp, MB_HEADS, MB_DH)
    v_prompt = kv_stacks[1].reshape(depth, bp, tp, MB_HEADS, MB_DH)
    return (y_prompt, y_sample, k_prompt, v_prompt, st(outs["ks"]), st(outs["vs"]),
            st(outs["gp"]), st(outs["gs"]), st(outs["cp"]), st(outs["cs"]), st(outs["np"]), st(outs["ns"]),
            st(outs["mp"]), st(outs["ms"]), st(outs["vv"]))
```

```python
import functools

import numpy as np
import jax
import jax.numpy as jnp
from jax import lax
from jax.experimental import pallas as pl
from jax.experimental.pallas import tpu as pltpu

F32 = jnp.float32
BF16 = jnp.bfloat16
HIGHEST = lax.Precision.HIGHEST

GROUP_W = 1024
GLA_HEADS, GLA_DK, GLA_DV = 8, 64, 128
GLA_GATE_RANK, GLA_GATE_NORM = 16, 16.0
ML_HEADS, ML_DH = 8, 128
GM_GROUPS, GM_CH, GM_CHUNK = 8, 128, 128
MB_HEADS, MB_DH, MB_BLOCK, MB_TOPK = 8, 128, 256, 3
ROT_DIM, ROPE_THETA = 32, 500000.0
EPS = 1e-6

LANES = 128
SUBLANES = 8
VMEM_LIMIT = 56 * 1024 * 1024
VMEM_LIMIT_TALL = 60 * 1024 * 1024

NEG = -1e30
SAMPLE_T = SUBLANES

_MAIN_ORDER = ("g_q", "g_k", "g_v", "g_z", "m_q", "m_k", "m_v", "m_o", "m_z",
               "c_u", "c_v", "c_z", "a_q", "a_k", "a_v", "a_z")
_SPLIT_NAMES = ("g_q", "g_k", "g_v", "g_lr", "g_z", "m_q", "m_k", "m_v", "m_i", "m_f", "m_o", "m_z",
                "c_u", "c_v", "c_z", "a_q", "a_k", "a_v", "a_z")
_SPLIT_W = (GLA_HEADS * GLA_DK, GLA_HEADS * GLA_DK, GLA_HEADS * GLA_DV, GLA_GATE_RANK, GROUP_W,
            GROUP_W, GROUP_W, GROUP_W, ML_HEADS, ML_HEADS, GROUP_W, GROUP_W,
            GROUP_W, GROUP_W, GROUP_W, GROUP_W, GROUP_W, GROUP_W, GROUP_W)
_SRC_OFF = dict(zip(_SPLIT_NAMES, np.concatenate([[0], np.cumsum(_SPLIT_W)[:-1]]).tolist()))
_SRC_W = dict(zip(_SPLIT_NAMES, _SPLIT_W))
_MAIN_OFF = {}
_off = 0
for _n in _MAIN_ORDER:
    _MAIN_OFF[_n] = _off
    _off += _SRC_W[_n]
N_MAIN = _off
SM_LR, SM_I, SM_F = 0, GLA_GATE_RANK, GLA_GATE_RANK + ML_HEADS


def _cparams(sem, vmem_limit=VMEM_LIMIT):
    return pltpu.CompilerParams(dimension_semantics=sem, vmem_limit_bytes=vmem_limit)


def _pick(n, cands):
    for c in cands:
        if n % c == 0:
            return c
    return n


def _silu(x):
    return x / (1.0 + jnp.exp(-x))


def _sigmoid(x):
    return 1.0 / (1.0 + jnp.exp(-x))


def _log_sigmoid(x):
    return jnp.minimum(x, 0.0) - jnp.log(1.0 + jnp.exp(-jnp.abs(x)))


def _gelu(x):
    c = np.sqrt(2.0 / np.pi).astype(np.float32)
    return 0.5 * x * (1.0 + jnp.tanh(c * (x + 0.044715 * (x * x * x))))


def _dot_nt(a, b, precision=None):
    return lax.dot_general(a, b, (((1,), (1,)), ((), ())), preferred_element_type=F32, precision=precision)


def _dot_tn(a, b, precision=None):
    return lax.dot_general(a, b, (((0,), (0,)), ((), ())), preferred_element_type=F32, precision=precision)


def _dot(a, b, precision=None):
    return jnp.dot(a, b, preferred_element_type=F32, precision=precision)


def _rms_small_kernel(x_ref, nw_ref, ws_ref, h_ref, sm_ref):
    x = x_ref[...]
    ms = jnp.mean(x * x, axis=-1, keepdims=True)
    hb = (x * lax.rsqrt(ms + EPS) * nw_ref[...]).astype(BF16)
    h_ref[...] = hb
    sm_ref[...] = _dot(hb, ws_ref[...])


def _rms_small(x, nw, w_small, layer):
    m, d = x.shape
    bm = _pick(m, (512, 256, 128, 64, 32, 16, 8))
    return pl.pallas_call(
        _rms_small_kernel,
        grid=(m // bm,),
        in_specs=[pl.BlockSpec((bm, d), lambda i: (i, 0)),
                  pl.BlockSpec((1, d), lambda i: (0, 0)),
                  pl.BlockSpec((None, d, LANES), lambda i: (layer, 0, 0))],
        out_specs=[pl.BlockSpec((bm, d), lambda i: (i, 0)),
                   pl.BlockSpec((bm, LANES), lambda i: (i, 0))],
        out_shape=[jax.ShapeDtypeStruct((m, d), BF16), jax.ShapeDtypeStruct((m, LANES), F32)],
        compiler_params=_cparams(("parallel",)),
        name="rms_small",
    )(x, nw.reshape(1, d), w_small)


def _rms_kernel(x_ref, nw_ref, o_ref):
    x = x_ref[...]
    ms = jnp.mean(x * x, axis=-1, keepdims=True)
    o_ref[...] = x * lax.rsqrt(ms + EPS) * nw_ref[...]


def _rms(x, nw):
    m, d = x.shape
    bm = _pick(m, (512, 256, 128, 64, 32, 16, 8))
    return pl.pallas_call(
        _rms_kernel,
        grid=(m // bm,),
        in_specs=[pl.BlockSpec((bm, d), lambda i: (i, 0)), pl.BlockSpec((1, d), lambda i: (0, 0))],
        out_specs=pl.BlockSpec((bm, d), lambda i: (i, 0)),
        out_shape=jax.ShapeDtypeStruct((m, d), F32),
        compiler_params=_cparams(("parallel",)),
        name="rms_final",
    )(x, nw.reshape(1, d))


WP_BN = 1024
WP_B1 = _MAIN_OFF["g_z"] // WP_BN
WP_B2 = _MAIN_OFF["m_o"] // WP_BN
WP_S1 = GLA_GATE_RANK
WP_S2 = GLA_GATE_RANK + 2 * ML_HEADS
assert _MAIN_OFF["g_z"] % WP_BN == 0 and _MAIN_OFF["m_o"] % WP_BN == 0 and N_MAIN % WP_BN == 0
assert WP_S1 % SUBLANES == 0 and WP_S2 % SUBLANES == 0 and sum(_SPLIT_W) % WP_S2 == 0


def _wprep_kernel(a_ref, t_ref, o_ref):
    j = pl.program_id(2)

    def emit(s):
        x = a_ref[...] if s == 0 else jnp.concatenate([a_ref[...], t_ref[...]], axis=0)[s:s + WP_BN]
        o_ref[...] = x.T.astype(BF16)

    @pl.when(j < WP_B1)
    def _():
        emit(0)

    @pl.when((j >= WP_B1) & (j < WP_B2))
    def _():
        emit(WP_S1)

    @pl.when(j >= WP_B2)
    def _():
        emit(WP_S2)


def _wprep(w_t):
    depth, _, d = w_t.shape
    kb = _pick(d, (2048, 1024, 512, 256, 128))
    return pl.pallas_call(
        _wprep_kernel,
        grid=(depth, d // kb, N_MAIN // WP_BN),
        in_specs=[pl.BlockSpec((None, WP_BN, kb), lambda l, i, j: (l, j, i)),
                  pl.BlockSpec((None, WP_S2, kb), lambda l, i, j: (l, (j + 1) * (WP_BN // WP_S2), i))],
        out_specs=pl.BlockSpec((None, kb, WP_BN), lambda l, i, j: (l, i, j)),
        out_shape=jax.ShapeDtypeStruct((depth, d, N_MAIN), BF16),
        compiler_params=_cparams(("parallel", "parallel", "parallel")),
        name="w_prep",
    )(w_t, w_t)


def _wsmall_kernel(lr_ref, if_ref, o_ref):
    kb = lr_ref.shape[1]
    x = jnp.concatenate([lr_ref[...], if_ref[...],
                         jnp.zeros((LANES - GLA_GATE_RANK - 2 * ML_HEADS, kb), F32)], axis=0)
    o_ref[...] = x.T.astype(BF16)


def _wsmall(w_t):
    depth, _, d = w_t.shape
    kb = _pick(d, (512, 256, 128))
    lr, gi = _SRC_OFF["g_lr"], _SRC_OFF["m_i"]
    assert _SRC_OFF["m_f"] == gi + ML_HEADS and lr % GLA_GATE_RANK == 0 and gi % (2 * ML_HEADS) == 0
    return pl.pallas_call(
        _wsmall_kernel,
        grid=(depth, d // kb),
        in_specs=[pl.BlockSpec((None, GLA_GATE_RANK, kb), lambda l, i: (l, lr // GLA_GATE_RANK, i)),
                  pl.BlockSpec((None, 2 * ML_HEADS, kb), lambda l, i: (l, gi // (2 * ML_HEADS), i))],
        out_specs=pl.BlockSpec((None, kb, LANES), lambda l, i: (l, i, 0)),
        out_shape=jax.ShapeDtypeStruct((depth, d, LANES), BF16),
        compiler_params=_cparams(("parallel", "parallel")),
        name="w_small",
    )(w_t, w_t)


def _mm_kernel(a_ref, as_ref, b_ref, o_ref, os_ref):
    w = b_ref[...]
    o_ref[...] = _dot(a_ref[...], w)

    @pl.when(pl.program_id(1) == 0)
    def _():
        os_ref[...] = _dot(as_ref[...], w)


def _in_proj(a, a_s, w_all, layer):
    m, k = a.shape
    ms = a_s.shape[0]
    n = w_all.shape[2]
    bm = _pick(m, (1024, 512, 256, 128, 64))
    bn = _pick(n, (1024, 512, 256, 128))
    return pl.pallas_call(
        _mm_kernel,
        grid=(n // bn, m // bm),
        in_specs=[pl.BlockSpec((bm, k), lambda j, i: (i, 0)),
                  pl.BlockSpec((ms, k), lambda j, i: (0, 0)),
                  pl.BlockSpec((None, k, bn), lambda j, i: (layer, 0, j))],
        out_specs=[pl.BlockSpec((bm, bn), lambda j, i: (i, j)),
                   pl.BlockSpec((ms, bn), lambda j, i: (0, j))],
        out_shape=[jax.ShapeDtypeStruct((m, n), F32), jax.ShapeDtypeStruct((ms, n), F32)],
        compiler_params=_cparams(("parallel", "arbitrary")),
        name="in_proj",
    )(a, a_s, w_all)


def _out_kernel(a0, a1, a2, a3, w_ref, x_ref, o_ref):
    acc = x_ref[...]
    for g, a in enumerate((a0, a1, a2, a3)):
        acc = acc + _dot(a[...].astype(BF16), w_ref[g].astype(BF16))
    o_ref[...] = acc


def _out_proj(mix, w_all, layer, x):
    m, d = x.shape
    bm = _pick(m, (2048, 1024, 512, 256, 128, 64))
    bn = _pick(d, (256, 128))
    a_spec = pl.BlockSpec((bm, GROUP_W), lambda i, j: (i, 0))
    return pl.pallas_call(
        _out_kernel,
        grid=(m // bm, d // bn),
        in_specs=[a_spec, a_spec, a_spec, a_spec,
                  pl.BlockSpec((None, 4, GROUP_W, bn), lambda i, j: (layer, 0, 0, j)),
                  pl.BlockSpec((bm, bn), lambda i, j: (i, j))],
        out_specs=pl.BlockSpec((bm, bn), lambda i, j: (i, j)),
        out_shape=jax.ShapeDtypeStruct((m, d), F32),
        compiler_params=_cparams(("parallel", "parallel"), VMEM_LIMIT_TALL),
        name="out_proj",
    )(*mix, w_all, x)


def _rope_kernel(x_ref, a_ref, b_ref, c_ref, o_ref):
    a, b, c = a_ref[...], b_ref[...], c_ref[...]
    for h in range(MB_HEADS):
        sl = slice(h * MB_DH, (h + 1) * MB_DH)
        x = x_ref[0, :, sl]
        o_ref[0, :, sl] = (x * a + pltpu.roll(x, MB_DH - ROT_DIM // 2, 1) * b
                           + pltpu.roll(x, ROT_DIM // 2, 1) * c)


def _rope_tables(pos):
    half = ROT_DIM // 2
    inv_freq = jnp.power(ROPE_THETA, -jnp.arange(0, ROT_DIM, 2, dtype=F32) / ROT_DIM)
    ang = pos.astype(F32)[:, None] * inv_freq[None, :]
    cos, sin = jnp.cos(ang), jnp.sin(ang)
    t = pos.shape[0]
    a = jnp.concatenate([cos, cos, jnp.ones((t, MB_DH - ROT_DIM), F32)], axis=1)
    b = jnp.concatenate([-sin, jnp.zeros((t, MB_DH - half), F32)], axis=1)
    c = jnp.concatenate([jnp.zeros((t, half), F32), sin, jnp.zeros((t, MB_DH - ROT_DIM), F32)], axis=1)
    return a, b, c


def _rope(proj, col, tables):
    bsz, t, _ = proj.shape
    bt = _pick(t, (1024, 512, 256, 128, 64, 32, 16, 8))
    tab = pl.BlockSpec((bt, MB_DH), lambda b, i: (i, 0))
    return pl.pallas_call(
        _rope_kernel,
        grid=(bsz, t // bt),
        in_specs=[pl.BlockSpec((1, bt, GROUP_W), lambda b, i: (b, i, col // GROUP_W)), tab, tab, tab],
        out_specs=pl.BlockSpec((1, bt, GROUP_W), lambda b, i: (b, i, 0)),
        out_shape=jax.ShapeDtypeStruct((bsz, t, GROUP_W), F32),
        compiler_params=_cparams(("parallel", "parallel")),
        name="rope",
    )(proj, *tables)


def _kv_heads_kernel(k_ref, v_ref, a_ref, b_ref, c_ref, *refs):
    k4_ref, v4_ref = refs[-2:]
    a, b, c = a_ref[...], b_ref[...], c_ref[...]
    bt = a.shape[0]
    for h in range(MB_HEADS):
        sl = slice(h * MB_DH, (h + 1) * MB_DH)
        x = k_ref[0, :, sl]
        rows = pl.ds(h, bt, stride=MB_HEADS)
        k4_ref[rows, :] = (x * a + pltpu.roll(x, MB_DH - ROT_DIM // 2, 1) * b
                           + pltpu.roll(x, ROT_DIM // 2, 1) * c)
        v4_ref[rows, :] = v_ref[0, :, sl]


def _kv_heads(proj, tables, layer, depth, stacks):
    bsz, t, _ = proj.shape
    bt = _pick(t, (512, 256, 128, 64, 32, 16, 8))
    tab = pl.BlockSpec((bt, MB_DH), lambda b, i: (i, 0))
    kcol, vcol = _MAIN_OFF["a_k"] // GROUP_W, _MAIN_OFF["a_v"] // GROUP_W
    out = pl.BlockSpec((None, None, bt * MB_HEADS, MB_DH), lambda b, i: (layer, b, i, 0))
    shape = jax.ShapeDtypeStruct((depth, bsz, t * MB_HEADS, MB_DH), F32)
    in_specs = [pl.BlockSpec((1, bt, GROUP_W), lambda b, i: (b, i, kcol)),
                pl.BlockSpec((1, bt, GROUP_W), lambda b, i: (b, i, vcol)), tab, tab, tab]
    aliases = {}
    if stacks is not None:
        in_specs += [pl.BlockSpec(memory_space=pl.ANY), pl.BlockSpec(memory_space=pl.ANY)]
        aliases = {5: 0, 6: 1}
    return pl.pallas_call(
        _kv_heads_kernel,
        grid=(bsz, t // bt),
        in_specs=in_specs,
        out_specs=[out, out],
        out_shape=[shape, shape],
        input_output_aliases=aliases,
        compiler_params=_cparams(("parallel", "parallel")),
        name="kv_heads",
    )(proj, proj, *tables, *(stacks or ()))


GLA_GROUP = 2
GLA_SPAN = 4
assert (GLA_HEADS // 2) % GLA_GROUP == 0


def _gla_levels(c):
    return [c >> i for i in range(1, c.bit_length())]


def _gla_sum_matrices(c, nch):
    sup = c * nch
    t = np.arange(sup)[:, None]
    r = np.arange(sup)[None, :]
    cb = (t // c) * c
    mats = [(r >= cb) & (r <= t), (r > t) & (r <= cb + c - 1)]
    for hs in _gla_levels(c):
        base = (t // (2 * hs)) * (2 * hs)
        ref = base + hs - 1
        mats.append(np.where(t - base >= hs, (r > ref) & (r <= t), (r > t) & (r <= ref)))
    return np.concatenate(mats, axis=0).astype(np.float32)


def _gla_kernel(q_ref, k_ref, v_ref, z_ref, sm_ref, wg_ref, bg_ref, nw_ref, s0_ref, sum_ref,
                o_ref, s_ref, g_scr, *, t, c, nch, t_valid):
    dk = GLA_DK
    sup = c * nch
    levels = _gla_levels(c)
    split = sum_ref.dtype == BF16
    x = _dot(sm_ref[0], wg_ref[...], precision=HIGHEST) + bg_ref[...]
    g_all = _log_sigmoid(x) * (1.0 / GLA_GATE_NORM)
    if t_valid < t:
        rows_t = lax.broadcasted_iota(jnp.int32, g_all.shape, 0)
        g_all = jnp.where(rows_t < t_valid, g_all, 0.0)
    g_scr[...] = g_all

    lane = lax.broadcasted_iota(jnp.int32, (sup, LANES), 1)
    row = lax.broadcasted_iota(jnp.int32, (sup, LANES), 0)
    head0 = lane < dk
    rr = lax.broadcasted_iota(jnp.int32, (sup, sup), 0)
    cc = lax.broadcasted_iota(jnp.int32, (sup, sup), 1)
    mm = BF16 if split else F32
    prec = None if split else HIGHEST
    nw = nw_ref[...]

    def pair_trip(p, r0, state):
        ql = slice(p * LANES, (p + 1) * LANES)
        vl = slice(p * 2 * GLA_DV, (p + 1) * 2 * GLA_DV)
        q = q_ref[0, pl.ds(r0, sup), ql] * (dk ** -0.5)
        k = k_ref[0, pl.ds(r0, sup), ql]
        if t_valid < t:
            k = jnp.where(row < t_valid, k, 0.0)
        g = g_scr[pl.ds(r0, sup), ql]
        v = v_ref[0, pl.ds(r0, sup), vl]
        z = z_ref[0, pl.ds(r0, sup), vl]
        if split:
            g1 = g.astype(BF16)
            r1 = g - g1.astype(F32)
            g2 = r1.astype(BF16)
            g3 = (r1 - g2.astype(F32)).astype(BF16)
            xs = _dot(sum_ref[...], jnp.concatenate([g1, g2, g3], axis=1))
            xs = xs[:, :LANES] + xs[:, LANES:2 * LANES] + xs[:, 2 * LANES:]
        else:
            xs = _dot(sum_ref[...], g, precision=HIGHEST)
        yield
        b = xs[0:sup]
        to_end = xs[sup:2 * sup]

        qk = q * k
        att0 = jnp.where(rr == cc, jnp.sum(jnp.where(head0, qk, 0.0), axis=-1, keepdims=True), 0.0)
        att1 = jnp.where(rr == cc, jnp.sum(jnp.where(head0, 0.0, qk), axis=-1, keepdims=True), 0.0)
        for i, hs in enumerate(levels):
            e = jnp.exp(xs[(2 + i) * sup:(3 + i) * sup])
            upper = (row & hs) != 0
            qt = jnp.where(upper, q * e, 0.0)
            kl = jnp.where(upper, 0.0, k * e).astype(mm)
            sh = (2 * hs).bit_length() - 1
            same = (rr >> sh) == (cc >> sh)
            att0 = att0 + jnp.where(same, _dot_nt(jnp.where(head0, qt, 0.0).astype(mm), kl, prec), 0.0)
            att1 = att1 + jnp.where(same, _dot_nt(jnp.where(head0, 0.0, qt).astype(mm), kl, prec), 0.0)
            yield

        kt = k * jnp.exp(to_end)
        qe = q * jnp.exp(b)
        qe0, qe1 = jnp.where(head0, qe, 0.0), jnp.where(head0, 0.0, qe)
        kt0, kt1 = jnp.where(head0, kt, 0.0), jnp.where(head0, 0.0, kt)
        v0, v1 = v[:, :GLA_DV], v[:, GLA_DV:]
        rows = [slice(ch * c, (ch + 1) * c) for ch in range(nch)]
        upds = [_dot_tn(jnp.concatenate([v0[r], v1[r]], axis=0).astype(mm),
                        jnp.concatenate([kt0[r], kt1[r]], axis=0).astype(mm), prec) for r in rows]
        yield
        o_int = []
        s2t = state[p]
        for ch, r in enumerate(rows):
            lhs = jnp.concatenate([qe0[r], qe1[r]], axis=0).astype(mm)
            o_int.append(_dot_nt(lhs, s2t.astype(mm), prec))
            s2t = jnp.exp(b[ch * c + c - 1:ch * c + c, :]) * s2t + upds[ch]
        state[p] = s2t
        yield

        for h, (att, vh) in enumerate(((att0, v0), (att1, v1))):
            o = (jnp.concatenate([oi[h * c:(h + 1) * c, :] for oi in o_int], axis=0)
                 + _dot(att.astype(mm), vh.astype(mm), prec))
            yield
            on = o * lax.rsqrt(jnp.mean(o * o, axis=-1, keepdims=True) + EPS) * nw
            zh = z[:, h * GLA_DV:(h + 1) * GLA_DV]
            cols = slice(p * 2 * GLA_DV + h * GLA_DV, p * 2 * GLA_DV + (h + 1) * GLA_DV)
            o_ref[0, pl.ds(r0, sup), cols] = (on * _silu(zh)).astype(o_ref.dtype)

    span = GLA_SPAN if (t // sup) % GLA_SPAN == 0 else 1

    def body(si, states):
        r0 = pl.multiple_of(si * (sup * span), sup * span)
        state = list(states)
        trips = [pair_trip(p, pl.multiple_of(r0 + j * sup, sup), state)
                 for j in range(span) for p in range(GLA_GROUP)]
        while trips:
            trips = [g for g in trips if next(g, True) is None]
        return tuple(state)

    states = lax.fori_loop(0, t // (sup * span), body, tuple(s0_ref[0, p] for p in range(GLA_GROUP)))
    for p in range(GLA_GROUP):
        s_ref[0, p] = states[p]


def _gla(proj, small, wg_pad, bg, nw, s0t, t_valid, out_dtype):
    bsz, t, _ = proj.shape
    c = min(LANES, t)
    nch = 1
    pairs = GLA_HEADS // 2
    gw = GLA_GROUP * LANES
    vw = GLA_GROUP * 2 * GLA_DV
    qb, kb = _MAIN_OFF["g_q"] // gw, _MAIN_OFF["g_k"] // gw
    vb, zb = _MAIN_OFF["g_v"] // vw, _MAIN_OFF["g_z"] // vw
    kern = functools.partial(_gla_kernel, t=t, c=c, nch=nch, t_valid=t_valid)
    sums = _gla_sum_matrices(c, nch)
    sums = jnp.asarray(sums, BF16 if (c * nch) % 16 == 0 else F32)
    return pl.pallas_call(
        kern,
        grid=(bsz, pairs // GLA_GROUP),
        in_specs=[pl.BlockSpec((1, t, gw), lambda b, p: (b, 0, qb + p)),
                  pl.BlockSpec((1, t, gw), lambda b, p: (b, 0, kb + p)),
                  pl.BlockSpec((1, t, vw), lambda b, p: (b, 0, vb + p)),
                  pl.BlockSpec((1, t, vw), lambda b, p: (b, 0, zb + p)),
                  pl.BlockSpec((1, t, LANES), lambda b, p: (b, 0, 0)),
                  pl.BlockSpec((LANES, gw), lambda b, p: (0, p)),
                  pl.BlockSpec((1, gw), lambda b, p: (0, p)),
                  pl.BlockSpec((1, GLA_DV), lambda b, p: (0, 0)),
                  pl.BlockSpec((1, GLA_GROUP, GLA_DV, LANES), lambda b, p: (b, p, 0, 0)),
                  pl.BlockSpec(sums.shape, lambda b, p: (0, 0))],
        out_specs=[pl.BlockSpec((1, t, vw), lambda b, p: (b, 0, p)),
                   pl.BlockSpec((1, GLA_GROUP, GLA_DV, LANES), lambda b, p: (b, p, 0, 0))],
        out_shape=[jax.ShapeDtypeStruct((bsz, t, GROUP_W), out_dtype),
                   jax.ShapeDtypeStruct((bsz, pairs, GLA_DV, LANES), F32)],
        scratch_shapes=[pltpu.VMEM((t, gw), F32)],
        compiler_params=_cparams(("parallel", "parallel")),
        name="gla",
    )(proj, proj, proj, proj, small, wg_pad, bg.reshape(1, -1), nw.reshape(1, -1), s0t, sums)


def _gla_state_to_pairs(s):
    bsz = s.shape[0]
    s = s.reshape(bsz, GLA_HEADS // 2, 2, GLA_DK, GLA_DV)
    return s.transpose(0, 1, 4, 2, 3).reshape(bsz, GLA_HEADS // 2, GLA_DV, 2 * GLA_DK)


def _gla_state_from_pairs(s):
    bsz = s.shape[0]
    s = s.reshape(bsz, GLA_HEADS // 2, GLA_DV, 2, GLA_DK)
    return s.transpose(0, 1, 3, 4, 2).reshape(bsz, GLA_HEADS, GLA_DK, GLA_DV)


def _gates_kernel(sm_ref, bias_ref, o_ref, *, t_valid):
    x = sm_ref[0] + bias_ref[...]
    lane = lax.broadcasted_iota(jnp.int32, x.shape, 1)
    is_f = (lane >= SM_F) & (lane < SM_F + ML_HEADS)
    out = jnp.where(is_f, _log_sigmoid(x), x)
    if t_valid is not None:
        row = lax.broadcasted_iota(jnp.int32, x.shape, 0)
        out = jnp.where(row < t_valid, out, jnp.where(is_f, 0.0, NEG))
    o_ref[0] = out


def _gates(small, bias_row, t_valid):
    bsz, t, _ = small.shape
    bt = t if t_valid < t else _pick(t, (1024, 512, 256, 128, 64, 32, 16, 8))
    return pl.pallas_call(
        functools.partial(_gates_kernel, t_valid=t_valid if t_valid < t else None),
        grid=(bsz, t // bt),
        in_specs=[pl.BlockSpec((1, bt, LANES), lambda b, i: (b, i, 0)),
                  pl.BlockSpec((1, LANES), lambda b, i: (0, 0))],
        out_specs=pl.BlockSpec((1, bt, LANES), lambda b, i: (b, i, 0)),
        out_shape=jax.ShapeDtypeStruct((bsz, t, LANES), F32),
        compiler_params=_cparams(("parallel", "parallel")),
        name="ml_gates",
    )(small, bias_row)


ML_PAIR = 2
ML_SPAN = 4


def _mlstm_kernel(q_ref, k_ref, v_ref, og_ref, z_ref, gt_ref, nw_ref,
                  c0_ref, n0_ref, m0_ref, o_ref, c_ref, n_ref, m_ref, *, t, c):
    h0 = pl.program_id(1) * ML_PAIR
    hs = range(ML_PAIR)
    lane = lax.broadcasted_iota(jnp.int32, (c, LANES), 1)
    rr = lax.broadcasted_iota(jnp.int32, (c, c), 0)
    cc = lax.broadcasted_iota(jnp.int32, (c, c), 1)
    causal = rr >= cc
    nw = nw_ref[...]
    hsl = lambda j: slice(j * ML_DH, (j + 1) * ML_DH)

    def body(ci, carry):
        cms, ns, m_prevs = carry
        r0 = pl.multiple_of(ci * c, c)
        gt = gt_ref[0, pl.ds(r0, c), :]
        qs = [q_ref[0, pl.ds(r0, c), hsl(j)] * (ML_DH ** -0.5) for j in hs]
        ks = [k_ref[0, pl.ds(r0, c), hsl(j)] for j in hs]
        vs = [v_ref[0, pl.ds(r0, c), hsl(j)] for j in hs]
        s_qk = [_dot_nt(qs[j], ks[j], precision=HIGHEST) for j in hs]
        s_qc = [_dot_nt(qs[j], cms[j], precision=HIGHEST) for j in hs]

        d, inter, m_t, i_cols, fc_cols = [], [], [], [], []
        for j in hs:
            i_col = jnp.sum(jnp.where(lane == SM_I + h0 + j, gt, 0.0), axis=-1, keepdims=True)
            f_col = jnp.sum(jnp.where(lane == SM_F + h0 + j, gt, 0.0), axis=-1, keepdims=True)
            i_row = jnp.sum(jnp.where(rr == cc, i_col, 0.0), axis=0, keepdims=True)
            f_row = jnp.sum(jnp.where(rr == cc, f_col, 0.0), axis=0, keepdims=True)
            fc_col = jnp.sum(jnp.where(causal, f_row, 0.0), axis=-1, keepdims=True)
            fc_row = jnp.sum(jnp.where(rr <= cc, f_col, 0.0), axis=0, keepdims=True)
            dj = jnp.where(causal, fc_col - fc_row + i_row, NEG)
            d.append(dj)
            inter.append(fc_col + m_prevs[j])
            m_t.append(jnp.maximum(inter[j], jnp.max(dj, axis=-1, keepdims=True)))
            i_cols.append(i_col)
            fc_cols.append(fc_col)

        w_state = [jnp.exp(inter[j] - m_t[j]) for j in hs]
        qk = [s_qk[j] * jnp.exp(d[j] - m_t[j]) for j in hs]
        pv = [_dot(qk[j], vs[j], precision=HIGHEST) for j in hs]
        w_end, dec, m_new = [], [], []
        for j in hs:
            m_new.append(m_t[j][c - 1:c, :])
            f_end = fc_cols[j][c - 1:c, :]
            w_end.append(jnp.exp(f_end - fc_cols[j] + i_cols[j] - m_new[j]))
            dec.append(jnp.exp(f_end + m_prevs[j] - m_new[j]))
        upd = [_dot_tn(vs[j] * w_end[j], ks[j], precision=HIGHEST) for j in hs]

        c_new, n_new = [], []
        for j in hs:
            num = w_state[j] * s_qc[j] + pv[j]
            den = (w_state[j] * jnp.sum(qs[j] * ns[j], axis=-1, keepdims=True)
                   + jnp.sum(qk[j], axis=-1, keepdims=True))
            hout = num / jnp.maximum(jnp.abs(den), jnp.exp(-m_t[j]))
            c_new.append(dec[j] * cms[j] + upd[j])
            n_new.append(dec[j] * ns[j] + jnp.sum(ks[j] * w_end[j], axis=0, keepdims=True))
            o = _sigmoid(og_ref[0, pl.ds(r0, c), hsl(j)]) * hout
            on = o * lax.rsqrt(jnp.mean(o * o, axis=-1, keepdims=True) + EPS) * nw
            o_ref[0, pl.ds(r0, c), hsl(j)] = (on * _silu(z_ref[0, pl.ds(r0, c), hsl(j)])).astype(o_ref.dtype)
        return tuple(c_new), tuple(n_new), tuple(m_new)

    init = (tuple(c0_ref[0, j] for j in hs), tuple(n0_ref[0, j] for j in hs),
            tuple(m0_ref[0, j][:, :1] for j in hs))
    cms, ns, ms = lax.fori_loop(0, t // c, body, init)
    for j in hs:
        c_ref[0, j] = cms[j]
        n_ref[0, j] = ns[j]
        m_ref[0, j] = jnp.broadcast_to(ms[j], (1, LANES))


def _split3(x):
    a = x.astype(BF16)
    r = x - a.astype(F32)
    b = r.astype(BF16)
    return a, b, (r - b.astype(F32)).astype(BF16)


def _mlstm_wide_kernel(q_ref, k_ref, v_ref, og_ref, z_ref, gt_ref, nw_ref, c0_ref, n0_ref, m0_ref,
                       o_ref, c_ref, n_ref, m_ref, t_scr, g_scr, i_scr, fc_scr, *, t):
    c = LANES
    nc = t // c
    h0 = pl.program_id(1) * ML_PAIR
    hs = range(ML_PAIR)
    rr = lax.broadcasted_iota(jnp.int32, (c, c), 0)
    cc = lax.broadcasted_iota(jnp.int32, (c, c), 1)
    causal = rr >= cc
    nw = nw_ref[...]
    hsl = lambda j: slice(j * ML_DH, (j + 1) * ML_DH)
    ones_b = jnp.ones((c, ML_DH), BF16)
    tril_b = jnp.where(causal, 1.0, 0.0).astype(BF16)

    lane_t = lax.broadcasted_iota(jnp.int32, (c, LANES), 1)
    for ci in range(nc):
        rows = slice(ci * c, (ci + 1) * c)
        g = gt_ref[0, rows, :]
        g_scr[rows, :] = jnp.where(lane_t < SM_F, g, sum(_dot(tril_b, x) for x in _split3(g)))
    sel_r = lax.broadcasted_iota(jnp.int32, (LANES, 2 * LANES), 0)
    sel_c = lax.broadcasted_iota(jnp.int32, (LANES, 2 * LANES), 1)
    g_terms = _split3(g_scr[...])
    for j in hs:
        src = jnp.where(sel_c < LANES, SM_I + h0 + j, SM_F + h0 + j)
        sel = jnp.where(sel_r == src, 1.0, 0.0).astype(BF16)
        both = sum(_dot(g, sel) for g in g_terms)
        i_scr[j] = both[:, :LANES]
        fc_scr[j] = both[:, LANES:]

    def head_trip(j, u, r0, cms, ns, ms):
        rows = pl.ds(r0, c)
        qb = (q_ref[0, rows, hsl(j)] * (ML_DH ** -0.5)).astype(BF16)
        kb = k_ref[0, rows, hsl(j)].astype(BF16)
        v = v_ref[0, rows, hsl(j)]
        s_qk = _dot_nt(qb, kb)
        yield
        fc = fc_scr[j, rows, :]
        i_row = t_scr[u, pl.ds(SM_I + h0 + j, 1), :]
        fc_row = t_scr[u, pl.ds(SM_F + h0 + j, 1), :]
        d = jnp.where(causal, fc - fc_row + i_row, NEG)
        m_prev = ms[j]
        inter = fc + m_prev
        m_t = jnp.maximum(inter, jnp.max(d, axis=-1, keepdims=True))
        m_new = m_t[c - 1:c, :1]
        ms[j] = m_new
        yield
        w_state = jnp.exp(inter - m_t)
        qk = (s_qk * jnp.exp(d - m_t)).astype(BF16)
        pv = _dot(qk, jnp.concatenate([v.astype(BF16), ones_b], axis=1))
        f_end = fc[c - 1:c, :]
        w_end = jnp.exp(f_end - fc + i_scr[j, rows, :] - m_new)
        dec = jnp.exp(f_end[:, :1] + m_prev - m_new)
        yield
        s_qc = _dot_nt(qb, jnp.concatenate([cms[j], jnp.broadcast_to(ns[j], (c, ML_DH))],
                                           axis=0).astype(BF16))
        upd = _dot_tn(jnp.concatenate([v * w_end, w_end], axis=1).astype(BF16), kb)
        cms[j] = dec * cms[j] + upd[:ML_DH, :]
        ns[j] = dec * ns[j] + upd[ML_DH:ML_DH + 1, :]
        yield
        num = w_state * s_qc[:, :ML_DH] + pv[:, :ML_DH]
        den = w_state * s_qc[:, ML_DH:] + pv[:, ML_DH:]
        hout = _sigmoid(og_ref[0, rows, hsl(j)]) * num / jnp.maximum(jnp.abs(den), jnp.exp(-m_t))
        msq = _dot((hout * hout).astype(BF16), ones_b) * (1.0 / ML_DH)
        yield
        on = hout * lax.rsqrt(msq + EPS) * nw
        o_ref[0, rows, hsl(j)] = (on * _silu(z_ref[0, rows, hsl(j)])).astype(o_ref.dtype)

    span = ML_SPAN if nc % ML_SPAN == 0 else 1

    def body(ci, carry):
        cms, ns, ms = (list(x) for x in carry)
        r0 = pl.multiple_of(ci * (c * span), c * span)
        trips = []
        for u in range(span):
            ru = pl.multiple_of(r0 + u * c, c)
            t_scr[u] = g_scr[pl.ds(ru, c), :].T
            trips += [head_trip(j, u, ru, cms, ns, ms) for j in hs]
        while trips:
            trips = [g for g in trips if next(g, True) is None]
        return tuple(cms), tuple(ns), tuple(ms)

    init = (tuple(c0_ref[0, j] for j in hs), tuple(n0_ref[0, j] for j in hs),
            tuple(m0_ref[0, j][:, :1] for j in hs))
    cms, ns, ms = lax.fori_loop(0, nc // span, body, init)
    for j in hs:
        c_ref[0, j] = cms[j]
        n_ref[0, j] = ns[j]
        m_ref[0, j] = jnp.broadcast_to(ms[j], (1, LANES))


def _mlstm(proj, gates, nw, c0, n0, m0, out_dtype):
    bsz, t, _ = proj.shape
    c = min(128, t)
    w = ML_PAIR * ML_DH
    col = lambda name: _MAIN_OFF[name] // w
    spec = lambda name: pl.BlockSpec((1, t, w), lambda b, p, o=col(name): (b, 0, o + p))
    st = lambda rows: pl.BlockSpec((1, ML_PAIR, rows, ML_DH), lambda b, p: (b, p, 0, 0))
    if c == LANES:
        kern = functools.partial(_mlstm_wide_kernel, t=t)
        scratch = ([pltpu.VMEM((ML_SPAN, LANES, LANES), F32), pltpu.VMEM((t, LANES), F32)]
                   + [pltpu.VMEM((ML_PAIR, t, LANES), F32)] * 2)
    else:
        kern = functools.partial(_mlstm_kernel, t=t, c=c)
        scratch = []
    return pl.pallas_call(
        kern,
        grid=(bsz, ML_HEADS // ML_PAIR),
        in_specs=[spec("m_q"), spec("m_k"), spec("m_v"), spec("m_o"), spec("m_z"),
                  pl.BlockSpec((1, t, LANES), lambda b, p: (b, 0, 0)),
                  pl.BlockSpec((1, ML_DH), lambda b, p: (0, 0)),
                  st(ML_DH), st(1), st(1)],
        out_specs=[pl.BlockSpec((1, t, w), lambda b, p: (b, 0, p)), st(ML_DH), st(1), st(1)],
        out_shape=[jax.ShapeDtypeStruct((bsz, t, GROUP_W), out_dtype),
                   jax.ShapeDtypeStruct((bsz, ML_HEADS, ML_DH, ML_DH), F32),
                   jax.ShapeDtypeStruct((bsz, ML_HEADS, 1, ML_DH), F32),
                   jax.ShapeDtypeStruct((bsz, ML_HEADS, 1, LANES), F32)],
        scratch_shapes=scratch,
        compiler_params=_cparams(("parallel", "parallel")),
        name="mlstm",
    )(proj, proj, proj, proj, proj, gates, nw.reshape(1, -1), c0, n0, m0)


def _gmlp_kernel(u_ref, v_ref, z_ref, lw_ref, lb_ref, ws_ref, bs_ref, o_ref, *vn_refs, l, nl):
    rr = lax.broadcasted_iota(jnp.int32, (l, l), 0)
    cc = lax.broadcasted_iota(jnp.int32, (l, l), 1)
    wts = [jnp.where(rr >= cc, ws_ref[g], 0.0) for g in range(GM_GROUPS)]
    if l >= GM_CHUNK:
        wts = [w.astype(BF16) for w in wts]
    for ch in range(nl):
        rows = slice(ch * l, (ch + 1) * l)
        gv = _gelu(v_ref[0, rows, :])
        mu = jnp.mean(gv, axis=-1, keepdims=True)
        xc = gv - mu
        vn = xc * lax.rsqrt(jnp.mean(xc * xc, axis=-1, keepdims=True) + EPS) * lw_ref[...] + lb_ref[...]
        if vn_refs:
            vn_refs[0][0, rows, :] = vn
        for g in range(GM_GROUPS):
            sl = slice(g * GM_CH, (g + 1) * GM_CH)
            vg = vn[:, sl]
            if l >= GM_CHUNK:
                s = _dot(wts[g], vg.astype(BF16))
            else:
                s = jnp.zeros((l, GM_CH), F32)
                for r in range(l):
                    s = s + wts[g][:, r:r + 1] * vg[r:r + 1, :]
            s = s + bs_ref[:, g:g + 1]
            o_ref[0, rows, sl] = (_gelu(u_ref[0, rows, sl]) * s * _silu(z_ref[0, rows, sl])).astype(o_ref.dtype)


def _gmlp(proj, lw, lb, ws, bs_t, out_dtype, want_vn):
    bsz, t, _ = proj.shape
    l = min(t, GM_CHUNK)
    nl = _pick(t // l, (4, 2, 1))
    col = lambda name: _MAIN_OFF[name] // GROUP_W
    spec = lambda name: pl.BlockSpec((1, l * nl, GROUP_W), lambda b, i, o=col(name): (b, i, o))
    out = pl.BlockSpec((1, l * nl, GROUP_W), lambda b, i: (b, i, 0))
    out_specs = [out] + ([out] if want_vn else [])
    out_shape = ([jax.ShapeDtypeStruct((bsz, t, GROUP_W), out_dtype)]
                 + ([jax.ShapeDtypeStruct((bsz, t, GROUP_W), F32)] if want_vn else []))
    res = pl.pallas_call(
        functools.partial(_gmlp_kernel, l=l, nl=nl),
        grid=(bsz, t // (l * nl)),
        in_specs=[spec("c_u"), spec("c_v"), spec("c_z"),
                  pl.BlockSpec((1, GROUP_W), lambda b, i: (0, 0)),
                  pl.BlockSpec((1, GROUP_W), lambda b, i: (0, 0)),
                  pl.BlockSpec((GM_GROUPS, l, l), lambda b, i: (0, 0, 0)),
                  pl.BlockSpec((l, GM_GROUPS), lambda b, i: (0, 0))],
        out_specs=out_specs,
        out_shape=out_shape,
        compiler_params=_cparams(("parallel", "parallel")),
        name="gmlp",
    )(proj, proj, proj, lw.reshape(1, -1), lb.reshape(1, -1), ws, bs_t)
    return (res[0], res[1]) if want_vn else (res[0], None)


MB_GROUP = 2
assert MB_HEADS % MB_GROUP == 0


def _moba_prompt_kernel(q_ref, k_ref, v_ref, z_ref, o_ref, kb_scr, vb_scr, km_scr, *, t):
    blk = MB_BLOCK
    nb = t // blk
    scale = MB_DH ** -0.5
    nbp = -(-nb // SUBLANES) * SUBLANES
    blk_row = lax.broadcasted_iota(jnp.int32, (nbp, blk), 0)
    rr = lax.broadcasted_iota(jnp.int32, (blk, blk), 0)
    cc = lax.broadcasted_iota(jnp.int32, (blk, blk), 1)
    log2_scale = float(scale * np.log2(np.e))
    lane_tiles = lambda xs: [x[:, i:i + LANES] for x in xs for i in range(0, blk, LANES)]

    def head_steps(hh):
        h = pl.program_id(1) * MB_GROUP + hh
        cols = slice(hh * MB_DH, (hh + 1) * MB_DH)
        km_scr[hh] = jnp.zeros((LANES, MB_DH), F32)
        for n in range(nb):
            rows = pl.ds(n * blk * MB_HEADS + h, blk, stride=MB_HEADS)
            kn = k_ref[rows, :]
            kb_scr[hh, n * blk:(n + 1) * blk, :] = kn.astype(BF16)
            vb_scr[hh, n * blk:(n + 1) * blk, :] = v_ref[rows, :].astype(BF16)
            km_scr[hh, n:n + 1, :] = jnp.mean(kn, axis=0, keepdims=True)
        kmean = km_scr[hh]
        yield
        for qi in range(nb):
            rows = slice(qi * blk, (qi + 1) * blk)
            q = q_ref[0, rows, cols]
            qb = q.astype(BF16)
            bias = None
            if qi > MB_TOPK:
                gate = _dot_nt(kmean, q, precision=HIGHEST)[0:nbp, :]
                yield
                cnt = jnp.zeros((nbp, blk), F32)
                for m in range(qi):
                    gm = gate[m:m + 1, :]
                    beats = (gm > gate) | ((gm == gate) & (blk_row > m))
                    cnt = cnt + jnp.where(beats, 1.0, 0.0)
                bias_t = jnp.where(cnt < MB_TOPK, 0.0, NEG)
                bias = jnp.concatenate([bias_t, jnp.zeros((LANES - nbp, blk), F32)], axis=0).T
            ss = []
            for j in range(qi + 1):
                s = _dot_nt(qb, kb_scr[hh, j * blk:(j + 1) * blk, :]) * log2_scale
                if j == qi:
                    s = jnp.where(cc <= rr, s, NEG)
                elif bias is not None:
                    s = s + bias[:, j:j + 1]
                ss.append(s)
            yield
            m_i = jnp.max(functools.reduce(jnp.maximum, lane_tiles(ss)), axis=-1, keepdims=True)
            ps = [jnp.exp2(s - m_i) for s in ss]
            l_i = jnp.sum(functools.reduce(jnp.add, lane_tiles(ps)), axis=-1, keepdims=True)
            acc = _dot(ps[0].astype(BF16), vb_scr[hh, 0:blk, :])
            for j in range(1, qi + 1):
                acc = acc + _dot(ps[j].astype(BF16), vb_scr[hh, j * blk:(j + 1) * blk, :])
            yield
            o_ref[0, rows, cols] = ((acc / l_i) * _silu(z_ref[0, rows, cols])).astype(o_ref.dtype)

    heads = [head_steps(hh) for hh in range(MB_GROUP)]
    while heads:
        heads = [g for g in heads if next(g, True) is None]


def _moba_prompt(q_rope, k4, v4, layer, proj, out_dtype):
    bsz, t, _ = proj.shape
    assert t % MB_BLOCK == 0 and t // MB_BLOCK <= LANES
    gw = MB_GROUP * MB_DH
    zcol = _MAIN_OFF["a_z"] // gw
    hd = pl.BlockSpec((1, t, gw), lambda b, h: (b, 0, h))
    kv = pl.BlockSpec((None, None, t * MB_HEADS, MB_DH), lambda b, h: (layer, b, 0, 0))
    return pl.pallas_call(
        functools.partial(_moba_prompt_kernel, t=t),
        grid=(bsz, MB_HEADS // MB_GROUP),
        in_specs=[hd, kv, kv, pl.BlockSpec((1, t, gw), lambda b, h: (b, 0, zcol + h))],
        out_specs=hd,
        out_shape=jax.ShapeDtypeStruct((bsz, t, GROUP_W), out_dtype),
        scratch_shapes=[pltpu.VMEM((MB_GROUP, t, MB_DH), BF16), pltpu.VMEM((MB_GROUP, t, MB_DH), BF16),
                        pltpu.VMEM((MB_GROUP, LANES, MB_DH), F32)],
        compiler_params=_cparams(("parallel", "arbitrary"), VMEM_LIMIT_TALL),
        name="moba_prompt",
    )(q_rope, k4, v4, proj)


QROWS = MB_HEADS * SAMPLE_T
ST_M, ST_L, ST_G = 0, 1, 2


def _moba_past_kernel(pt_ref, q_ref, bias_ref, *refs, nbs):
    del pt_ref
    npg = 2 * nbs
    k_refs, v_refs = refs[:npg], refs[npg:2 * npg]
    o_ref, st_ref = refs[2 * npg:]
    scale = MB_DH ** -0.5
    page = k_refs[0].shape[0] // MB_HEADS
    q = q_ref[0]
    qb = q.astype(BF16)
    bias = bias_ref[...]
    lane = lax.broadcasted_iota(jnp.int32, (QROWS, LANES), 1)
    ss = [_dot_nt(qb, r[...].astype(BF16)) * scale + bias for r in k_refs]
    ps, ms, ls = [], [], []
    for n in range(nbs):
        s0, s1 = ss[2 * n], ss[2 * n + 1]
        m = jnp.maximum(jnp.max(s0, axis=-1, keepdims=True), jnp.max(s1, axis=-1, keepdims=True))
        p0, p1 = jnp.exp(s0 - m), jnp.exp(s1 - m)
        ps += [p0.astype(BF16), p1.astype(BF16)]
        ms.append(m)
        ls.append(jnp.sum(p0, axis=-1, keepdims=True) + jnp.sum(p1, axis=-1, keepdims=True))
    for n in range(nbs):
        o_ref[n] = (_dot(ps[2 * n], v_refs[2 * n][...].astype(BF16))
                    + _dot(ps[2 * n + 1], v_refs[2 * n + 1][...].astype(BF16)))
        ksum = (jnp.sum(k_refs[2 * n][...].reshape(page, MB_HEADS, MB_DH), axis=0)
                + jnp.sum(k_refs[2 * n + 1][...].reshape(page, MB_HEADS, MB_DH), axis=0))
        ksum_rows = jnp.concatenate(
            [jnp.broadcast_to(ksum[h:h + 1, :], (SAMPLE_T, MB_DH)) for h in range(MB_HEADS)], axis=0)
        gate = jnp.sum(q * ksum_rows, axis=-1, keepdims=True) * (1.0 / MB_BLOCK)
        st_ref[n] = jnp.where(lane == ST_M, ms[n], jnp.where(lane == ST_L, ls[n], gate))


def _moba_past(layer, q_rows, cache_k, cache_v, page_table):
    db = q_rows.shape[0]
    rows = cache_k.shape[2]
    assert 2 * rows == MB_BLOCK * MB_HEADS
    nb = page_table.shape[1] // 2
    nbs = _pick(nb, (8, 4, 2, 1))
    key_head = np.arange(rows) % MB_HEADS
    row_head = np.arange(QROWS) // SAMPLE_T
    bias = jnp.asarray(np.where(key_head[None, :] == row_head[:, None], 0.0, NEG).astype(np.float32))
    pg = lambda i: pl.BlockSpec((None, None, rows, MB_DH),
                                lambda b, n, pt, i=i: (layer, pt[b, 2 * nbs * n + i], 0, 0))
    pages = [pg(i) for i in range(2 * nbs)]
    part = pl.BlockSpec((None, nbs, QROWS, MB_DH), lambda b, n, pt: (b, n, 0, 0))
    shape = jax.ShapeDtypeStruct((db, nb, QROWS, MB_DH), F32)
    grid_spec = pltpu.PrefetchScalarGridSpec(
        num_scalar_prefetch=1,
        grid=(db, nb // nbs),
        in_specs=[pl.BlockSpec((1, QROWS, MB_DH), lambda b, n, pt: (b, 0, 0)),
                  pl.BlockSpec((QROWS, rows), lambda b, n, pt: (0, 0))] + pages + pages,
        out_specs=[part, part],
    )
    return pl.pallas_call(
        functools.partial(_moba_past_kernel, nbs=nbs),
        grid_spec=grid_spec,
        out_shape=[shape, shape],
        compiler_params=_cparams(("parallel", "parallel")),
        name="moba_past",
    )(page_table, q_rows, bias, *([cache_k] * (2 * nbs)), *([cache_v] * (2 * nbs)))


def _moba_merge_kernel(op_ref, st_ref, q_ref, k_ref, v_ref, z_ref, o_ref, *, nb, t_valid):
    scale = MB_DH ** -0.5
    g = st_ref[0, :, :, ST_G:ST_G + 1]
    m = st_ref[0, :, :, ST_M:ST_M + 1]
    l = st_ref[0, :, :, ST_L:ST_L + 1]
    nidx = lax.broadcasted_iota(jnp.int32, g.shape, 0)
    sel = jnp.zeros(g.shape, jnp.bool_)
    gm = g
    for _ in range(min(MB_TOPK, nb)):
        mx = jnp.max(gm, axis=0, keepdims=True)
        first = jnp.min(jnp.where(gm == mx, nidx, nb), axis=0, keepdims=True)
        pick = nidx == first
        sel = sel | pick
        gm = jnp.where(pick, -jnp.inf, gm)

    rr = lax.broadcasted_iota(jnp.int32, (SAMPLE_T, SAMPLE_T), 0)
    cc = lax.broadcasted_iota(jnp.int32, (SAMPLE_T, SAMPLE_T), 1)
    mo, lo, oo = [], [], []
    for h in range(MB_HEADS):
        sl = slice(h * MB_DH, (h + 1) * MB_DH)
        qh = q_ref[0, h * SAMPLE_T:(h + 1) * SAMPLE_T, :]
        s = _dot_nt(qh.astype(BF16), k_ref[0, :, sl].astype(BF16)) * scale
        s = jnp.where((cc <= rr) & (cc < t_valid), s, NEG)
        mh = jnp.max(s, axis=-1, keepdims=True)
        p = jnp.exp(s - mh)
        mo.append(mh)
        lo.append(jnp.sum(p, axis=-1, keepdims=True))
        oo.append(_dot(p.astype(BF16), v_ref[0, :, sl].astype(BF16)))
    m_own, l_own, o_own = (jnp.concatenate(x, axis=0) for x in (mo, lo, oo))

    m_tot = jnp.maximum(jnp.max(jnp.where(sel, m, NEG), axis=0), m_own)
    w = jnp.where(sel, jnp.exp(m - m_tot[None]), 0.0)
    w_own = jnp.exp(m_own - m_tot)
    den = jnp.sum(w * l, axis=0) + w_own * l_own
    acc = w_own * o_own
    for n in range(nb):
        acc = acc + w[n] * op_ref[0, n]
    o = acc / den
    for h in range(MB_HEADS):
        sl = slice(h * MB_DH, (h + 1) * MB_DH)
        o_ref[0, :, sl] = o[h * SAMPLE_T:(h + 1) * SAMPLE_T, :] * _silu(z_ref[0, :, sl])


def _moba_merge(o_part, stats, q_rows, k_rope, proj, t_valid):
    db, nb = o_part.shape[:2]
    vcol, zcol = _MAIN_OFF["a_v"] // GROUP_W, _MAIN_OFF["a_z"] // GROUP_W
    part = pl.BlockSpec((1, nb, QROWS, MB_DH), lambda b: (b, 0, 0, 0))
    row = pl.BlockSpec((1, SAMPLE_T, GROUP_W), lambda b: (b, 0, 0))
    return pl.pallas_call(
        functools.partial(_moba_merge_kernel, nb=nb, t_valid=t_valid),
        grid=(db,),
        in_specs=[part, part, pl.BlockSpec((1, QROWS, MB_DH), lambda b: (b, 0, 0)), row,
                  pl.BlockSpec((1, SAMPLE_T, GROUP_W), lambda b: (b, 0, vcol)),
                  pl.BlockSpec((1, SAMPLE_T, GROUP_W), lambda b: (b, 0, zcol))],
        out_specs=row,
        out_shape=jax.ShapeDtypeStruct((db, SAMPLE_T, GROUP_W), F32),
        compiler_params=_cparams(("parallel",)),
        name="moba_merge",
    )(o_part, stats, q_rows, k_rope, proj, proj)


def _layer_weights(gla_w_gate_l, ml_b_i_l, ml_b_f_l):
    wg_pad = jnp.concatenate([gla_w_gate_l, jnp.zeros((LANES - GLA_GATE_RANK, gla_w_gate_l.shape[1]), F32)],
                             axis=0)
    bias_row = jnp.concatenate([jnp.zeros((SM_I,), F32), ml_b_i_l, ml_b_f_l,
                                jnp.zeros((LANES - SM_F - ML_HEADS,), F32)]).reshape(1, LANES)
    return wg_pad, bias_row


def _mixers(proj, small, bsz, t, t_valid, lw, rope_tables, gla_s0, ml_state, out_dtype, want_vn):
    (wg_pad, bias_row, gla_b_gate, gla_norm_w, ml_norm_w, gm_ln_w, gm_ln_b, gm_ws, gm_bs_t) = lw
    proj = proj.reshape(bsz, t, N_MAIN)
    small = small.reshape(bsz, t, LANES)

    out_a, gla_s = _gla(proj, small, wg_pad, gla_b_gate, gla_norm_w, gla_s0, t_valid, out_dtype)

    gates = _gates(small, bias_row, t_valid)
    out_b, ml_c, ml_n, ml_m = _mlstm(proj, gates, ml_norm_w, *ml_state, out_dtype)

    out_c, vn = _gmlp(proj, gm_ln_w, gm_ln_b, gm_ws, gm_bs_t, out_dtype, want_vn)

    q_rope = _rope(proj, _MAIN_OFF["a_q"], rope_tables)
    return proj, (out_a, out_b, out_c), q_rope, gla_s, (ml_c, ml_n, ml_m[..., :1]), vn


def kernel(x_prompt, x_sample, cache_k, cache_v, page_table, state_gla, state_mlstm_C, state_mlstm_n,
           state_mlstm_m, norm_w, w_in, gla_w_gate, gla_b_gate, gla_norm_w, ml_b_i, ml_b_f, ml_norm_w,
           gm_ln_w, gm_ln_b, gm_w_s, gm_b_s, w_out, final_norm_w):
    bp, tp, d = x_prompt.shape
    db, ts, _ = x_sample.shape
    depth = w_in.shape[0]
    page = cache_k.shape[2]
    past_len = page_table.shape[1] * page
    assert w_out.shape[1] == 4 * GROUP_W and ts <= SAMPLE_T
    assert past_len % MB_BLOCK == 0 and tp % MB_BLOCK == 0

    tables_p = _rope_tables(jnp.arange(tp, dtype=jnp.int32))
    tables_s = _rope_tables(past_len + jnp.arange(SAMPLE_T, dtype=jnp.int32))
    cache_k = cache_k.reshape(depth, cache_k.shape[1], page * MB_HEADS, MB_DH)
    cache_v = cache_v.reshape(depth, cache_v.shape[1], page * MB_HEADS, MB_DH)

    yp = x_prompt.reshape(bp * tp, d)
    ys = jnp.pad(x_sample, ((0, 0), (0, SAMPLE_T - ts), (0, 0))).reshape(db * SAMPLE_T, d)
    dt_p = BF16
    dt_s = F32

    zero_gla = jnp.zeros((bp, GLA_HEADS // 2, GLA_DV, LANES), F32)
    zero_ml = (jnp.zeros((bp, ML_HEADS, ML_DH, ML_DH), F32), jnp.zeros((bp, ML_HEADS, 1, ML_DH), F32),
               jnp.zeros((bp, ML_HEADS, 1, LANES), F32))

    outs = {n: [] for n in ("ks", "vs", "gp", "gs", "cp", "cs", "np", "ns", "mp", "ms", "vv")}
    lp = min(tp, GM_CHUNK)
    w_t = jnp.swapaxes(w_in, 1, 2)
    w_main = _wprep(w_t)
    w_small = _wsmall(w_t)
    w_out4 = w_out.reshape(depth, 4, GROUP_W, d)
    kv_stacks = None
    for l in range(depth):
        wg_pad, bias_row = _layer_weights(gla_w_gate[l], ml_b_i[l], ml_b_f[l])
        common = (wg_pad, bias_row, gla_b_gate[l], gla_norm_w[l], ml_norm_w[l], gm_ln_w[l], gm_ln_b[l])
        h_p, small_p = _rms_small(yp, norm_w[l], w_small, l)
        h_s, small_s = _rms_small(ys, norm_w[l], w_small, l)
        proj_p, proj_s = _in_proj(h_p, h_s, w_main, l)

        lw = common + (gm_w_s[l][:, :lp, :lp], gm_b_s[l][:, :lp].T)
        proj, mix, q_rope, gla_s, ml_s, _ = _mixers(
            proj_p, small_p, bp, tp, tp, lw, tables_p, zero_gla, zero_ml, dt_p, want_vn=False)
        kv_stacks = _kv_heads(proj, tables_p, l, depth, kv_stacks)
        out_d = _moba_prompt(q_rope, kv_stacks[0], kv_stacks[1], l, proj, dt_p)
        yp = _out_proj([a.reshape(bp * tp, GROUP_W) for a in mix + (out_d,)], w_out4, l, yp)
        outs["gp"].append(_gla_state_from_pairs(gla_s))
        outs["cp"].append(ml_s[0])
        outs["np"].append(ml_s[1][:, :, 0, :])
        outs["mp"].append(ml_s[2][:, :, 0, 0])

        lw = common + (gm_w_s[l][:, :SAMPLE_T, :SAMPLE_T], gm_b_s[l][:, :SAMPLE_T].T)
        ml_state = (state_mlstm_C[l], state_mlstm_n[l][:, :, None, :],
                    jnp.broadcast_to(state_mlstm_m[l][:, :, None, None], (db, ML_HEADS, 1, LANES)))
        proj, mix, q_rope, gla_s, ml_s, vn = _mixers(
            proj_s, small_s, db, SAMPLE_T, ts, lw, tables_s, _gla_state_to_pairs(state_gla[l]), ml_state,
            dt_s, want_vn=True)
        k_rope = _rope(proj, _MAIN_OFF["a_k"], tables_s)
        v_new = proj[:, :, _MAIN_OFF["a_v"]:_MAIN_OFF["a_v"] + GROUP_W]
        q_rows = (q_rope.reshape(db, SAMPLE_T, MB_HEADS, MB_DH).transpose(0, 2, 1, 3)
                  .reshape(db, QROWS, MB_DH))
        o_part, stats = _moba_past(l, q_rows, cache_k, cache_v, page_table)
        out_d = _moba_merge(o_part, stats, q_rows, k_rope, proj, ts)
        ys = _out_proj([a.reshape(db * SAMPLE_T, GROUP_W) for a in mix + (out_d,)], w_out4, l, ys)
        outs["ks"].append(k_rope[:, :ts].reshape(db, ts, MB_HEADS, MB_DH))
        outs["vs"].append(v_new[:, :ts].reshape(db, ts, MB_HEADS, MB_DH))
        outs["gs"].append(_gla_state_from_pairs(gla_s))
        outs["cs"].append(ml_s[0])
        outs["ns"].append(ml_s[1][:, :, 0, :])
        outs["ms"].append(ml_s[2][:, :, 0, 0])
        outs["vv"].append(vn[:, :ts])

    y_prompt = _rms(yp, final_norm_w).reshape(bp, tp, d)
    y_sample = _rms(ys, final_norm_w).reshape(db, SAMPLE_T, d)[:, :ts]
    st = jnp.stack
    k_prompt = kv_stacks[0].reshape(depth, bp, tp, MB_HEADS, MB_DH)
    v_prompt = kv_stacks[1].reshape(depth, bp, tp, MB_HEADS, MB_DH)
    return (y_prompt, y_sample, k_prompt, v_prompt, st(outs["ks"]), st(outs["vs"]),
            st(outs["gp"]), st(outs["gs"]), st(outs["cp"]), st(outs["cs"]), st(outs["np"]), st(outs["ns"]),
            st(outs["mp"]), st(outs["ms"]), st(outs["vv"]))
```
